```python
import jax, jax.numpy as jnp
from jax import lax
import numpy as np

D_MODEL = 2048
BATCH = 8
SEQ = 8192
DEPTH = 2

GRID_W = 64
CTX_LEN = 256
D_LRU = 1024
N_LRU_HEADS = 4
LRU_HEAD_DIM = D_LRU // N_LRU_HEADS
RG_C = 8.0
CONV_SHORT = 4
SHORT_PAD_L = 2
SHORT_PAD_R = 1
D_CONV = 1024
CONV_K = 31
D_MIX = D_LRU + D_CONV
D_IN = 2 * D_LRU + 2 * D_CONV
D_FF = 5632
N_MOD = 9
ALPHA = (2 * DEPTH) ** 0.25
BETA = (8 * DEPTH) ** -0.25
ADA_SCALE = 0.5
EPS = 1e-6

kernel_name = 'hybrid_rglru_conformer_dit_block'


def _layernorm(x, g, b):
    xf = x.astype(jnp.float32)
    mu = jnp.mean(xf, axis=-1, keepdims=True)
    var = jnp.mean(jnp.square(xf - mu), axis=-1, keepdims=True)
    y = (xf - mu) * lax.rsqrt(var + EPS) * g.astype(jnp.float32) + b.astype(jnp.float32)
    return y.astype(x.dtype)


def _modulate(s, shift, scale):
    return s * (1 + scale) + shift


def _swiglu(h, w_in, w_out):
    gate, up = jnp.split(h @ w_in, 2, axis=-1)
    return (jax.nn.silu(gate) * up) @ w_out


def _half_ffn(s, shift, scale, gate, w1, w2, g, b):
    return _layernorm(ALPHA * s + 0.5 * gate * _swiglu(_modulate(s, shift, scale), w1, w2), g, b)


def _dwconv(x, w, b, pad_left, pad_right):
    y = lax.conv_general_dilated(
        x, w.astype(x.dtype)[:, None, :], window_strides=(1,),
        padding=[(pad_left, pad_right)], dimension_numbers=('NWC', 'WIO', 'NWC'),
        feature_group_count=x.shape[-1])
    return y + b


def _lin_combine(left, right):
    a1, b1 = left
    a2, b2 = right
    return a1 * a2, a2 * b1 + b2


def _rglru(xc, w_r, b_r, w_i, b_i, lam, h0, reverse):
    bsz, t, c = xc.shape
    xf = xc.astype(jnp.float32)
    xh = xf.reshape(bsz, t, N_LRU_HEADS, LRU_HEAD_DIM)
    r = jax.nn.sigmoid(jnp.einsum('bthi,hij->bthj', xh, w_r.astype(jnp.float32)).reshape(bsz, t, c)
                       + b_r.astype(jnp.float32))
    i = jax.nn.sigmoid(jnp.einsum('bthi,hij->bthj', xh, w_i.astype(jnp.float32)).reshape(bsz, t, c)
                       + b_i.astype(jnp.float32))
    log_a = -RG_C * r * jax.nn.softplus(-lam.astype(jnp.float32))
    a = jnp.exp(log_a)
    b = jnp.sqrt(-jnp.expm1(2.0 * log_a)) * (i * xf)
    if h0 is not None:
        idx = t - 1 if reverse else 0
        b = b.at[:, idx].add(a[:, idx] * h0)
    _, h = lax.associative_scan(_lin_combine, (a, b), axis=1, reverse=reverse)
    return h


def _conv_module(v, gate, w, b, g, bb, n_seg):
    bsz, t, c = v.shape
    u = (v * jax.nn.sigmoid(gate)).reshape(bsz * n_seg, t // n_seg, c)
    u = _dwconv(u, w, b, CONV_K // 2, CONV_K // 2).reshape(bsz, t, c)
    return jax.nn.silu(_layernorm(u, g, bb))


def _mixer(h_lat, h_ctx, rows, w_in, conv4_w, conv4_b, w_rg, b_rg, w_ig, b_ig, lam,
           conv31_w, conv31_b, cln_g, cln_b, w_out, b_out, ctx_out):
    splits = [D_LRU, 2 * D_LRU, 2 * D_LRU + D_CONV]
    xr_l, gr_l, cv_l, cg_l = jnp.split(h_lat @ w_in, splits, axis=-1)
    if ctx_out:
        xr_c, gr_c, cv_c, cg_c = jnp.split(h_ctx @ w_in, splits, axis=-1)
    else:
        xr_c = h_ctx @ w_in[:, :D_LRU]
    xr_l = _dwconv(xr_l, conv4_w, conv4_b, SHORT_PAD_L, SHORT_PAD_R)
    xr_c = _dwconv(xr_c, conv4_w, conv4_b, SHORT_PAD_L, SHORT_PAD_R)
    rec_l = []
    rec_c = []
    for d, rev in ((0, False), (1, True)):
        h_c = _rglru(xr_c, w_rg[d], b_rg[d], w_ig[d], b_ig[d], lam[d], None, rev)
        h0 = h_c[:, 0] if rev else h_c[:, -1]
        h_l = _rglru(xr_l, w_rg[d], b_rg[d], w_ig[d], b_ig[d], lam[d], h0, rev)
        rec_l.append(h_l)
        rec_c.append(h_c)
    y_rec_l = (rec_l[0] + rec_l[1]).astype(h_lat.dtype) * jax.nn.gelu(gr_l)
    y_conv_l = _conv_module(cv_l, cg_l, conv31_w, conv31_b, cln_g, cln_b, rows)
    y_lat = jnp.concatenate([y_rec_l, y_conv_l], axis=-1) @ w_out + b_out
    if not ctx_out:
        return y_lat, None
    y_rec_c = (rec_c[0] + rec_c[1]).astype(h_ctx.dtype) * jax.nn.gelu(gr_c)
    y_conv_c = _conv_module(cv_c, cg_c, conv31_w, conv31_b, cln_g, cln_b, 1)
    y_ctx = jnp.concatenate([y_rec_c, y_conv_c], axis=-1) @ w_out + b_out
    return y_lat, y_ctx


def _fwd_setup_inputs(seed: int = 0) -> dict:
    key = jax.random.key(seed)
    ks = jax.random.split(key, 32)

    def nrm(k, shape, scale):
        return jax.random.normal(k, shape, jnp.float32) * scale

    x = nrm(ks[0], (BATCH, SEQ, D_MODEL), 1.0)
    c = nrm(ks[1], (BATCH, D_MODEL), 1.0)
    ctx = nrm(ks[2], (BATCH, CTX_LEN, D_MODEL), 1.0)
    c_ctx = nrm(ks[3], (D_MODEL,), 1.0)
    w_ada = nrm(ks[4], (DEPTH, D_MODEL, N_MOD * D_MODEL), ADA_SCALE * D_MODEL ** -0.5)
    b_ada = nrm(ks[5], (DEPTH, N_MOD * D_MODEL), 0.02)
    ln_g = 1.0 + nrm(ks[6], (DEPTH, 3, D_MODEL), 0.02)
    ln_b = nrm(ks[7], (DEPTH, 3, D_MODEL), 0.02)
    ff1_in = nrm(ks[8], (DEPTH, D_MODEL, 2 * D_FF), D_MODEL ** -0.5)
    ff1_out = nrm(ks[9], (DEPTH, D_FF, D_MODEL), BETA * D_FF ** -0.5)
    ff2_in = nrm(ks[10], (DEPTH, D_MODEL, 2 * D_FF), D_MODEL ** -0.5)
    ff2_out = nrm(ks[11], (DEPTH, D_FF, D_MODEL), BETA * D_FF ** -0.5)
    w_in = nrm(ks[12], (DEPTH, D_MODEL, D_IN), D_MODEL ** -0.5)
    conv4_w = nrm(ks[13], (DEPTH, CONV_SHORT, D_LRU), CONV_SHORT ** -0.5)
    conv4_b = nrm(ks[14], (DEPTH, D_LRU), 0.02)
    w_rg = nrm(ks[15], (DEPTH, 2, N_LRU_HEADS, LRU_HEAD_DIM, LRU_HEAD_DIM), LRU_HEAD_DIM ** -0.5)
    b_rg = nrm(ks[16], (DEPTH, 2, D_LRU), 0.02)
    w_ig = nrm(ks[17], (DEPTH, 2, N_LRU_HEADS, LRU_HEAD_DIM, LRU_HEAD_DIM), LRU_HEAD_DIM ** -0.5)
    b_ig = nrm(ks[18], (DEPTH, 2, D_LRU), 0.02)
    a_c = jax.random.uniform(ks[19], (DEPTH, 2, D_LRU), jnp.float32, minval=0.9, maxval=0.999)
    a_base = a_c ** (1.0 / RG_C)
    lam = jnp.log(a_base) - jnp.log1p(-a_base)
    conv31_w = nrm(ks[20], (DEPTH, CONV_K, D_CONV), CONV_K ** -0.5)
    conv31_b = nrm(ks[21], (DEPTH, D_CONV), 0.02)
    cln_g = 1.0 + nrm(ks[22], (DEPTH, D_CONV), 0.02)
    cln_b = nrm(ks[23], (DEPTH, D_CONV), 0.02)
    w_out = nrm(ks[24], (DEPTH, D_MIX, D_MODEL), BETA * D_MIX ** -0.5)
    b_out = nrm(ks[25], (DEPTH, D_MODEL), 0.02)
    return {'x': x, 'c': c, 'ctx': ctx, 'c_ctx': c_ctx, 'w_ada': w_ada, 'b_ada': b_ada,
            'ln_g': ln_g, 'ln_b': ln_b, 'ff1_in': ff1_in, 'ff1_out': ff1_out,
            'ff2_in': ff2_in, 'ff2_out': ff2_out, 'w_in': w_in, 'conv4_w': conv4_w,
            'conv4_b': conv4_b, 'w_rg': w_rg, 'b_rg': b_rg, 'w_ig': w_ig, 'b_ig': b_ig,
            'lam': lam, 'conv31_w': conv31_w, 'conv31_b': conv31_b, 'cln_g': cln_g,
            'cln_b': cln_b, 'w_out': w_out, 'b_out': b_out}


def _fwd_reference(x, c, ctx, c_ctx, w_ada, b_ada, ln_g, ln_b, ff1_in, ff1_out, ff2_in, ff2_out,
              w_in, conv4_w, conv4_b, w_rg, b_rg, w_ig, b_ig, lam, conv31_w, conv31_b,
              cln_g, cln_b, w_out, b_out):
    rows = x.shape[1] // GRID_W
    for l in range(DEPTH):
        last = l == DEPTH - 1
        m = jnp.split((jax.nn.silu(c) @ w_ada[l] + b_ada[l])[:, None, :], N_MOD, axis=-1)
        mc = jnp.split((jax.nn.silu(c_ctx) @ w_ada[l] + b_ada[l])[None, None, :], N_MOD, axis=-1)
        x = _half_ffn(x, m[0], m[1], m[2], ff1_in[l], ff1_out[l], ln_g[l, 0], ln_b[l, 0])
        ctx = _half_ffn(ctx, mc[0], mc[1], mc[2], ff1_in[l], ff1_out[l], ln_g[l, 0], ln_b[l, 0])
        y_lat, y_ctx = _mixer(_modulate(x, m[3], m[4]), _modulate(ctx, mc[3], mc[4]), rows,
                              w_in[l], conv4_w[l], conv4_b[l], w_rg[l], b_rg[l], w_ig[l], b_ig[l],
                              lam[l], conv31_w[l], conv31_b[l], cln_g[l], cln_b[l], w_out[l],
                              b_out[l], not last)
        x = _layernorm(ALPHA * x + m[5] * y_lat, ln_g[l, 1], ln_b[l, 1])
        if not last:
            ctx = _layernorm(ALPHA * ctx + mc[5] * y_ctx, ln_g[l, 1], ln_b[l, 1])
            ctx = _half_ffn(ctx, mc[6], mc[7], mc[8], ff2_in[l], ff2_out[l], ln_g[l, 2], ln_b[l, 2])
        x = _half_ffn(x, m[6], m[7], m[8], ff2_in[l], ff2_out[l], ln_g[l, 2], ln_b[l, 2])
    return x


import jax as _jax
import jax.numpy as _jnp

TWIN_FORMAT = 'train_step'
FWD_PARAMS = ['x', 'c', 'ctx', 'c_ctx', 'w_ada', 'b_ada', 'ln_g', 'ln_b', 'ff1_in', 'ff1_out', 'ff2_in', 'ff2_out', 'w_in', 'conv4_w', 'conv4_b', 'w_rg', 'b_rg', 'w_ig', 'b_ig', 'lam', 'conv31_w', 'conv31_b', 'cln_g', 'cln_b', 'w_out', 'b_out']
TWIN_WEIGHTS = ['c_ctx', 'w_ada', 'b_ada', 'ln_g', 'ln_b', 'ff1_in', 'ff1_out', 'ff2_in', 'ff2_out', 'w_in', 'conv4_w', 'conv4_b', 'w_rg', 'b_rg', 'w_ig', 'b_ig', 'lam', 'conv31_w', 'conv31_b', 'cln_g', 'cln_b', 'w_out', 'b_out']
TWIN_DIFF_INPUT = 'x'
TWIN_INPUTS = ['x', 'c', 'ctx', 'c_ctx', 'w_ada', 'b_ada', 'ln_g', 'ln_b', 'ff1_in', 'ff1_out', 'ff2_in', 'ff2_out', 'w_in', 'conv4_w', 'conv4_b', 'w_rg', 'b_rg', 'w_ig', 'b_ig', 'lam', 'conv31_w', 'conv31_b', 'cln_g', 'cln_b', 'w_out', 'b_out', 'loss_target', 'm_c_ctx', 'm_w_ada', 'm_b_ada', 'm_ln_g', 'm_ln_b', 'm_ff1_in', 'm_ff1_out', 'm_ff2_in', 'm_ff2_out', 'm_w_in', 'm_conv4_w', 'm_conv4_b', 'm_w_rg', 'm_b_rg', 'm_w_ig', 'm_b_ig', 'm_lam', 'm_conv31_w', 'm_conv31_b', 'm_cln_g', 'm_cln_b', 'm_w_out', 'm_b_out', 'v_c_ctx', 'v_w_ada', 'v_b_ada', 'v_ln_g', 'v_ln_b', 'v_ff1_in', 'v_ff1_out', 'v_ff2_in', 'v_ff2_out', 'v_w_in', 'v_conv4_w', 'v_conv4_b', 'v_w_rg', 'v_b_rg', 'v_w_ig', 'v_b_ig', 'v_lam', 'v_conv31_w', 'v_conv31_b', 'v_cln_g', 'v_cln_b', 'v_w_out', 'v_b_out']
TWIN_OUTPUTS = ['loss', 'grad_x', 'grad_c_ctx', 'grad_w_ada', 'grad_b_ada', 'grad_ln_g', 'grad_ln_b', 'grad_ff1_in', 'grad_ff1_out', 'grad_ff2_in', 'grad_ff2_out', 'grad_w_in', 'grad_conv4_w', 'grad_conv4_b', 'grad_w_rg', 'grad_b_rg', 'grad_w_ig', 'grad_b_ig', 'grad_lam', 'grad_conv31_w', 'grad_conv31_b', 'grad_cln_g', 'grad_cln_b', 'grad_w_out', 'grad_b_out', 'delta_c_ctx', 'delta_w_ada', 'delta_b_ada', 'delta_ln_g', 'delta_ln_b', 'delta_ff1_in', 'delta_ff1_out', 'delta_ff2_in', 'delta_ff2_out', 'delta_w_in', 'delta_conv4_w', 'delta_conv4_b', 'delta_w_rg', 'delta_b_rg', 'delta_w_ig', 'delta_b_ig', 'delta_lam', 'delta_conv31_w', 'delta_conv31_b', 'delta_cln_g', 'delta_cln_b', 'delta_w_out', 'delta_b_out', 'new_m_c_ctx', 'new_m_w_ada', 'new_m_b_ada', 'new_m_ln_g', 'new_m_ln_b', 'new_m_ff1_in', 'new_m_ff1_out', 'new_m_ff2_in', 'new_m_ff2_out', 'new_m_w_in', 'new_m_conv4_w', 'new_m_conv4_b', 'new_m_w_rg', 'new_m_b_rg', 'new_m_w_ig', 'new_m_b_ig', 'new_m_lam', 'new_m_conv31_w', 'new_m_conv31_b', 'new_m_cln_g', 'new_m_cln_b', 'new_m_w_out', 'new_m_b_out', 'new_v_c_ctx', 'new_v_w_ada', 'new_v_b_ada', 'new_v_ln_g', 'new_v_ln_b', 'new_v_ff1_in', 'new_v_ff1_out', 'new_v_ff2_in', 'new_v_ff2_out', 'new_v_w_in', 'new_v_conv4_w', 'new_v_conv4_b', 'new_v_w_rg', 'new_v_b_rg', 'new_v_w_ig', 'new_v_b_ig', 'new_v_lam', 'new_v_conv31_w', 'new_v_conv31_b', 'new_v_cln_g', 'new_v_cln_b', 'new_v_w_out', 'new_v_b_out']
TWIN_LEAF_KINDS = {'loss': 'loss', 'grad_x': 'grad_x', 'grad_c_ctx': 'grad_w', 'grad_w_ada': 'grad_w', 'grad_b_ada': 'grad_w', 'grad_ln_g': 'grad_w', 'grad_ln_b': 'grad_w', 'grad_ff1_in': 'grad_w', 'grad_ff1_out': 'grad_w', 'grad_ff2_in': 'grad_w', 'grad_ff2_out': 'grad_w', 'grad_w_in': 'grad_w', 'grad_conv4_w': 'grad_w', 'grad_conv4_b': 'grad_w', 'grad_w_rg': 'grad_w', 'grad_b_rg': 'grad_w', 'grad_w_ig': 'grad_w', 'grad_b_ig': 'grad_w', 'grad_lam': 'grad_w', 'grad_conv31_w': 'grad_w', 'grad_conv31_b': 'grad_w', 'grad_cln_g': 'grad_w', 'grad_cln_b': 'grad_w', 'grad_w_out': 'grad_w', 'grad_b_out': 'grad_w', 'delta_c_ctx': 'delta_w', 'delta_w_ada': 'delta_w', 'delta_b_ada': 'delta_w', 'delta_ln_g': 'delta_w', 'delta_ln_b': 'delta_w', 'delta_ff1_in': 'delta_w', 'delta_ff1_out': 'delta_w', 'delta_ff2_in': 'delta_w', 'delta_ff2_out': 'delta_w', 'delta_w_in': 'delta_w', 'delta_conv4_w': 'delta_w', 'delta_conv4_b': 'delta_w', 'delta_w_rg': 'delta_w', 'delta_b_rg': 'delta_w', 'delta_w_ig': 'delta_w', 'delta_b_ig': 'delta_w', 'delta_lam': 'delta_w', 'delta_conv31_w': 'delta_w', 'delta_conv31_b': 'delta_w', 'delta_cln_g': 'delta_w', 'delta_cln_b': 'delta_w', 'delta_w_out': 'delta_w', 'delta_b_out': 'delta_w', 'new_m_c_ctx': 'new_m', 'new_m_w_ada': 'new_m', 'new_m_b_ada': 'new_m', 'new_m_ln_g': 'new_m', 'new_m_ln_b': 'new_m', 'new_m_ff1_in': 'new_m', 'new_m_ff1_out': 'new_m', 'new_m_ff2_in': 'new_m', 'new_m_ff2_out': 'new_m', 'new_m_w_in': 'new_m', 'new_m_conv4_w': 'new_m', 'new_m_conv4_b': 'new_m', 'new_m_w_rg': 'new_m', 'new_m_b_rg': 'new_m', 'new_m_w_ig': 'new_m', 'new_m_b_ig': 'new_m', 'new_m_lam': 'new_m', 'new_m_conv31_w': 'new_m', 'new_m_conv31_b': 'new_m', 'new_m_cln_g': 'new_m', 'new_m_cln_b': 'new_m', 'new_m_w_out': 'new_m', 'new_m_b_out': 'new_m', 'new_v_c_ctx': 'new_v', 'new_v_w_ada': 'new_v', 'new_v_b_ada': 'new_v', 'new_v_ln_g': 'new_v', 'new_v_ln_b': 'new_v', 'new_v_ff1_in': 'new_v', 'new_v_ff1_out': 'new_v', 'new_v_ff2_in': 'new_v', 'new_v_ff2_out': 'new_v', 'new_v_w_in': 'new_v', 'new_v_conv4_w': 'new_v', 'new_v_conv4_b': 'new_v', 'new_v_w_rg': 'new_v', 'new_v_b_rg': 'new_v', 'new_v_w_ig': 'new_v', 'new_v_b_ig': 'new_v', 'new_v_lam': 'new_v', 'new_v_conv31_w': 'new_v', 'new_v_conv31_b': 'new_v', 'new_v_cln_g': 'new_v', 'new_v_cln_b': 'new_v', 'new_v_w_out': 'new_v', 'new_v_b_out': 'new_v'}


def _forward(args):
    return _fwd_reference(*[args[k] for k in FWD_PARAMS])


def _output_shape():
    def fwd():
        inp = _fwd_setup_inputs(0)
        return _fwd_reference(*[inp[k] for k in FWD_PARAMS])
    out = _jax.eval_shape(fwd)
    return out.shape, out.dtype

N_MICROBATCH = 1
ADAM_LR = 0.001
ADAM_B1 = 0.9
ADAM_B2 = 0.999
ADAM_EPS = 1e-08
ADAM_WD = 0.01
ADAM_STEP = 10
PER_EXAMPLE_BATCH_AXIS = {'x': 0, 'c': 0, 'ctx': 0, 'loss_target': 0}
SHARED_INPUTS = []
_WEIGHT_DTYPES = {'c_ctx': _jnp.float32, 'w_ada': _jnp.float32, 'b_ada': _jnp.float32, 'ln_g': _jnp.float32, 'ln_b': _jnp.float32, 'ff1_in': _jnp.float32, 'ff1_out': _jnp.float32, 'ff2_in': _jnp.float32, 'ff2_out': _jnp.float32, 'w_in': _jnp.float32, 'conv4_w': _jnp.float32, 'conv4_b': _jnp.float32, 'w_rg': _jnp.float32, 'b_rg': _jnp.float32, 'w_ig': _jnp.float32, 'b_ig': _jnp.float32, 'lam': _jnp.float32, 'conv31_w': _jnp.float32, 'conv31_b': _jnp.float32, 'cln_g': _jnp.float32, 'cln_b': _jnp.float32, 'w_out': _jnp.float32, 'b_out': _jnp.float32}
MOMENT_SCALE = {'c_ctx': 1.062134e-02, 'w_ada': 3.370231e-02, 'b_ada': 5.599142e-02, 'ln_g': 1.312258e+01, 'ln_b': 9.103241e-01, 'ff1_in': 2.938855e-03, 'ff1_out': 9.617194e-03, 'ff2_in': 2.921745e-03, 'ff2_out': 9.557552e-03, 'w_in': 3.192157e-02, 'conv4_w': 4.459740e-02, 'conv4_b': 1.303132e-01, 'w_rg': 2.236751e-03, 'b_rg': 3.559320e-03, 'w_ig': 4.477846e-03, 'b_ig': 8.735804e-03, 'lam': 9.021209e-03, 'conv31_w': 8.223191e-03, 'conv31_b': 2.303414e-02, 'cln_g': 1.157975e-02, 'cln_b': 1.257019e-02, 'w_out': 6.892845e-02, 'b_out': 4.783651e-02}


def _to_microbatches(a, axis):
    t = _jnp.moveaxis(a, axis, 0)
    t = t.reshape((N_MICROBATCH, t.shape[0] // N_MICROBATCH) + t.shape[1:])
    return _jnp.moveaxis(t, 1, axis + 1)


def setup_inputs(seed: int = 0) -> dict:
    inp = _fwd_setup_inputs(seed)
    key = _jax.random.fold_in(_jax.random.key(seed), 7919)
    shape, _ = _output_shape()
    out = dict(inp)
    out["loss_target"] = _jax.random.normal(_jax.random.fold_in(key, 0), shape, _jnp.float32)
    for i, name in enumerate(TWIN_WEIGHTS):
        w = inp[name].astype(_jnp.float32)
        if MOMENT_SCALE is None:
            s = _jnp.sqrt(_jnp.mean(_jnp.square(w)) + 1e-30)
        else:
            s = MOMENT_SCALE[name]
        km, kv = _jax.random.split(_jax.random.fold_in(key, i + 1))
        out[name] = w
        out["m_" + name] = s * _jax.random.normal(km, w.shape, _jnp.float32)
        out["v_" + name] = (s * s) * _jax.random.uniform(kv, w.shape, _jnp.float32, 0.5, 1.5)
    if N_MICROBATCH > 1:
        for name, axis in PER_EXAMPLE_BATCH_AXIS.items():
            out[name] = _to_microbatches(out[name], axis)
    return {'x': out['x'], 'c': out['c'], 'ctx': out['ctx'], 'c_ctx': out['c_ctx'], 'w_ada': out['w_ada'], 'b_ada': out['b_ada'], 'ln_g': out['ln_g'], 'ln_b': out['ln_b'], 'ff1_in': out['ff1_in'], 'ff1_out': out['ff1_out'], 'ff2_in': out['ff2_in'], 'ff2_out': out['ff2_out'], 'w_in': out['w_in'], 'conv4_w': out['conv4_w'], 'conv4_b': out['conv4_b'], 'w_rg': out['w_rg'], 'b_rg': out['b_rg'], 'w_ig': out['w_ig'], 'b_ig': out['b_ig'], 'lam': out['lam'], 'conv31_w': out['conv31_w'], 'conv31_b': out['conv31_b'], 'cln_g': out['cln_g'], 'cln_b': out['cln_b'], 'w_out': out['w_out'], 'b_out': out['b_out'], 'loss_target': out['loss_target'], 'm_c_ctx': out['m_c_ctx'], 'm_w_ada': out['m_w_ada'], 'm_b_ada': out['m_b_ada'], 'm_ln_g': out['m_ln_g'], 'm_ln_b': out['m_ln_b'], 'm_ff1_in': out['m_ff1_in'], 'm_ff1_out': out['m_ff1_out'], 'm_ff2_in': out['m_ff2_in'], 'm_ff2_out': out['m_ff2_out'], 'm_w_in': out['m_w_in'], 'm_conv4_w': out['m_conv4_w'], 'm_conv4_b': out['m_conv4_b'], 'm_w_rg': out['m_w_rg'], 'm_b_rg': out['m_b_rg'], 'm_w_ig': out['m_w_ig'], 'm_b_ig': out['m_b_ig'], 'm_lam': out['m_lam'], 'm_conv31_w': out['m_conv31_w'], 'm_conv31_b': out['m_conv31_b'], 'm_cln_g': out['m_cln_g'], 'm_cln_b': out['m_cln_b'], 'm_w_out': out['m_w_out'], 'm_b_out': out['m_b_out'], 'v_c_ctx': out['v_c_ctx'], 'v_w_ada': out['v_w_ada'], 'v_b_ada': out['v_b_ada'], 'v_ln_g': out['v_ln_g'], 'v_ln_b': out['v_ln_b'], 'v_ff1_in': out['v_ff1_in'], 'v_ff1_out': out['v_ff1_out'], 'v_ff2_in': out['v_ff2_in'], 'v_ff2_out': out['v_ff2_out'], 'v_w_in': out['v_w_in'], 'v_conv4_w': out['v_conv4_w'], 'v_conv4_b': out['v_conv4_b'], 'v_w_rg': out['v_w_rg'], 'v_b_rg': out['v_b_rg'], 'v_w_ig': out['v_w_ig'], 'v_b_ig': out['v_b_ig'], 'v_lam': out['v_lam'], 'v_conv31_w': out['v_conv31_w'], 'v_conv31_b': out['v_conv31_b'], 'v_cln_g': out['v_cln_g'], 'v_cln_b': out['v_cln_b'], 'v_w_out': out['v_w_out'], 'v_b_out': out['v_b_out']}


def _loss(weights, diff, rest, loss_target):
    with _jax.named_scope("forward"):
        args = {**rest, TWIN_DIFF_INPUT: diff, **{k: w.astype(_WEIGHT_DTYPES[k]) for k, w in weights.items()}}
        y = _forward(args)
    with _jax.named_scope("loss_head"):
        err = _jnp.square(y.astype(_jnp.float32) - loss_target)
        return 0.5 * _jnp.sum(_jnp.mean(err, axis=-1)) if err.ndim else 0.5 * err


def _adamw(w, g, m, v):
    m = ADAM_B1 * m + (1.0 - ADAM_B1) * g
    v = ADAM_B2 * v + (1.0 - ADAM_B2) * _jnp.square(g)
    m_hat = m / (1.0 - ADAM_B1 ** ADAM_STEP)
    v_hat = v / (1.0 - ADAM_B2 ** ADAM_STEP)
    delta = -ADAM_LR * (m_hat / (_jnp.sqrt(v_hat) + ADAM_EPS) + ADAM_WD * w)
    return delta, m, v


def reference(x, c, ctx, c_ctx, w_ada, b_ada, ln_g, ln_b, ff1_in, ff1_out, ff2_in, ff2_out, w_in, conv4_w, conv4_b, w_rg, b_rg, w_ig, b_ig, lam, conv31_w, conv31_b, cln_g, cln_b, w_out, b_out, loss_target, m_c_ctx, m_w_ada, m_b_ada, m_ln_g, m_ln_b, m_ff1_in, m_ff1_out, m_ff2_in, m_ff2_out, m_w_in, m_conv4_w, m_conv4_b, m_w_rg, m_b_rg, m_w_ig, m_b_ig, m_lam, m_conv31_w, m_conv31_b, m_cln_g, m_cln_b, m_w_out, m_b_out, v_c_ctx, v_w_ada, v_b_ada, v_ln_g, v_ln_b, v_ff1_in, v_ff1_out, v_ff2_in, v_ff2_out, v_w_in, v_conv4_w, v_conv4_b, v_w_rg, v_b_rg, v_w_ig, v_b_ig, v_lam, v_conv31_w, v_conv31_b, v_cln_g, v_cln_b, v_w_out, v_b_out):
    given = dict(x=x, c=c, ctx=ctx, c_ctx=c_ctx, w_ada=w_ada, b_ada=b_ada, ln_g=ln_g, ln_b=ln_b, ff1_in=ff1_in, ff1_out=ff1_out, ff2_in=ff2_in, ff2_out=ff2_out, w_in=w_in, conv4_w=conv4_w, conv4_b=conv4_b, w_rg=w_rg, b_rg=b_rg, w_ig=w_ig, b_ig=b_ig, lam=lam, conv31_w=conv31_w, conv31_b=conv31_b, cln_g=cln_g, cln_b=cln_b, w_out=w_out, b_out=b_out, loss_target=loss_target, m_c_ctx=m_c_ctx, m_w_ada=m_w_ada, m_b_ada=m_b_ada, m_ln_g=m_ln_g, m_ln_b=m_ln_b, m_ff1_in=m_ff1_in, m_ff1_out=m_ff1_out, m_ff2_in=m_ff2_in, m_ff2_out=m_ff2_out, m_w_in=m_w_in, m_conv4_w=m_conv4_w, m_conv4_b=m_conv4_b, m_w_rg=m_w_rg, m_b_rg=m_b_rg, m_w_ig=m_w_ig, m_b_ig=m_b_ig, m_lam=m_lam, m_conv31_w=m_conv31_w, m_conv31_b=m_conv31_b, m_cln_g=m_cln_g, m_cln_b=m_cln_b, m_w_out=m_w_out, m_b_out=m_b_out, v_c_ctx=v_c_ctx, v_w_ada=v_w_ada, v_b_ada=v_b_ada, v_ln_g=v_ln_g, v_ln_b=v_ln_b, v_ff1_in=v_ff1_in, v_ff1_out=v_ff1_out, v_ff2_in=v_ff2_in, v_ff2_out=v_ff2_out, v_w_in=v_w_in, v_conv4_w=v_conv4_w, v_conv4_b=v_conv4_b, v_w_rg=v_w_rg, v_b_rg=v_b_rg, v_w_ig=v_w_ig, v_b_ig=v_b_ig, v_lam=v_lam, v_conv31_w=v_conv31_w, v_conv31_b=v_conv31_b, v_cln_g=v_cln_g, v_cln_b=v_cln_b, v_w_out=v_w_out, v_b_out=v_b_out)
    weights = {n: given[n] for n in TWIN_WEIGHTS}
    shared = {n: given[n] for n in SHARED_INPUTS}
    per_example = {n: given[n] for n in ['x', 'c', 'ctx']}
    grad_fn = _jax.value_and_grad(_loss, argnums=(0, 1))

    def one_microbatch(ex, loss_target):
        ex = dict(ex)
        diff = ex.pop(TWIN_DIFF_INPUT)
        return grad_fn(weights, diff, {**shared, **ex}, loss_target)

    if N_MICROBATCH == 1:
        loss, (grad_w, grad_x) = one_microbatch(per_example, given["loss_target"])
    else:
        def body(carry, xs):
            loss_sum, grad_sum = carry
            l_k, (gw_k, gx_k) = one_microbatch(xs[0], xs[1])
            with _jax.named_scope("update"):
                return (loss_sum + l_k, _jax.tree.map(_jnp.add, grad_sum, gw_k)), gx_k

        init = (_jnp.zeros((), _jnp.float32), _jax.tree.map(_jnp.zeros_like, weights))
        (loss, grad_w), grad_x = _jax.lax.scan(body, init, (per_example, given["loss_target"]))
    with _jax.named_scope("update"):
        delta_w, new_m, new_v = {}, {}, {}
        for n in TWIN_WEIGHTS:
            delta_w[n], new_m[n], new_v[n] = _adamw(weights[n], grad_w[n], given["m_" + n], given["v_" + n])
    return (loss, grad_x, *[grad_w[n] for n in TWIN_WEIGHTS], *[delta_w[n] for n in TWIN_WEIGHTS],
            *[new_m[n] for n in TWIN_WEIGHTS], *[new_v[n] for n in TWIN_WEIGHTS])
```

```python
import math

import jax
import jax.numpy as jnp
from jax import lax
from jax.experimental import pallas as pl
from jax.experimental.pallas import tpu as pltpu

F32 = jnp.float32
BF16 = jnp.bfloat16
N_DEV = 8
MESH = pl.DeviceIdType.MESH
LN_EPS = 1e-6
RG_C = 8.0
SEG = 64
CONV_PAD = 16
ADAM_LR, ADAM_B1, ADAM_B2, ADAM_EPS, ADAM_WD, ADAM_STEP = 0.001, 0.9, 0.999, 1e-08, 0.01, 10
LANE = 128
PACK_QUANTUM = 16 * LANE
MOD_ROWS = 16
GELU_K = 0.7978845608028654
GELU_C = 0.044715

_TM = (768, 640, 512, 384, 256, 128)
_TN = (1024, 512, 256, 128)
_TK = (1408, 1024, 512, 256, 128)


def _cp(sem=None, vmem_mb=40):
    return pltpu.CompilerParams(dimension_semantics=sem, vmem_limit_bytes=vmem_mb * 2 ** 20)


def _pick(n, cands):
    for cand in cands:
        if n % cand == 0:
            return cand
    return n


def _sig(v):
    return 1.0 / (1.0 + jnp.exp(-v))


def _dot(a, b, ca, cb):
    return lax.dot_general(a, b, (((ca,), (cb,)), ((), ())), preferred_element_type=F32)


def _sum0(v):
    return jnp.sum(v, axis=0, keepdims=True)


def _gelu_and_grad(v):
    inner = GELU_K * (v + GELU_C * v * v * v)
    t = jnp.tanh(inner)
    gel = 0.5 * v * (1.0 + t)
    dgel = 0.5 * (1.0 + t) + 0.5 * v * (1.0 - t * t) * GELU_K * (1.0 + 3.0 * GELU_C * v * v)
    return gel, dgel


def _softplus_neg(lam):
    y = jnp.exp(-jnp.abs(lam))
    u = 1.0 + y
    log1p = jnp.where(u == 1.0, y, jnp.log(u) * (y / (u - 1.0)))
    return jnp.maximum(-lam, 0.0) + log1p


def _neg_expm1(z):
    ser = -z * (1.0 + z * (1 / 2 + z * (1 / 6 + z * (1 / 24 + z * (1 / 120 + z * (1 / 720 + z * (1 / 5040)))))))
    return jnp.where(z > -0.25, ser, 1.0 - jnp.exp(z))


def _row_spec(tt, w, col=0):
    return pl.BlockSpec((tt, w), lambda i: (i, col))


def _full_spec(shape):
    nd = len(shape)
    return pl.BlockSpec(shape, lambda *_: (0,) * nd)


def _mod_spec(d):
    return pl.BlockSpec((None, MOD_ROWS, d), lambda i: (jnp.minimum(i, 1), 0, 0))


def _stream_sum_spec(w):
    return pl.BlockSpec((8, w), lambda i: (jnp.minimum(i, 1), 0))


def _acc_rows(ref, init, rows):
    @pl.when(init)
    def _():
        ref[...] = jnp.zeros(ref.shape, F32)

    for k, row in enumerate(rows):
        ref[pl.ds(k, 1), :] += row


def _all_gather(v):
    def body(x_ref, out_ref, send_sems, recv_sems, local_sem):
        x, y, c = lax.axis_index("x"), lax.axis_index("y"), lax.axis_index("c")
        me, sibling = (x, y, c), (x, y, 1 - c)
        chips = [(1 - x, y), (x, 1 - y), (1 - x, 1 - y)]

        def slot(px, py, pc):
            return out_ref.at[4 * px + 2 * py + pc]

        def copy(k, block, to, src=None):
            return pltpu.make_async_remote_copy(
                src_ref=slot(*block) if src is None else src, dst_ref=slot(*block),
                send_sem=send_sems.at[k], recv_sem=recv_sems.at[k], device_id=to, device_id_type=MESH)

        mine = pltpu.make_async_copy(x_ref, slot(*me), local_sem)
        mine.start()
        first = [copy(0, me, sibling, src=x_ref)]
        first += [copy(1 + j, me, (*chip, c), src=x_ref) for j, chip in enumerate(chips)]
        for cp in first:
            cp.start()
        passed = [copy(4 + j, (*chip, c), sibling) for j, chip in enumerate(chips)]
        for j, chip in enumerate(chips):
            copy(1 + j, (*chip, c), me).wait_recv()
            passed[j].start()
        copy(0, sibling, me).wait_recv()
        for j, chip in enumerate(chips):
            copy(4 + j, (*chip, 1 - c), me).wait_recv()
        for cp in first + passed:
            cp.wait_send()
        mine.wait()

    return pl.pallas_call(
        body, name="all_gather",
        out_shape=jax.ShapeDtypeStruct((N_DEV,) + v.shape, v.dtype),
        in_specs=[pl.BlockSpec(memory_space=pltpu.HBM)],
        out_specs=pl.BlockSpec(memory_space=pltpu.HBM),
        scratch_shapes=[pltpu.SemaphoreType.DMA((7,)), pltpu.SemaphoreType.DMA((7,)), pltpu.SemaphoreType.DMA],
    )(v)


_PEER_FLIPS = [(0, 0, 1), (1, 0, 0), (0, 1, 0), (1, 1, 0), (1, 0, 1), (0, 1, 1), (1, 1, 1)]


def _all_to_all(parts):
    n_parts = len(parts)

    def body(*refs):
        x_refs, out_ref = refs[:n_parts], refs[n_parts]
        send_sems, recv_sems, local_sems = refs[n_parts + 1:]
        x, y, c = lax.axis_index("x"), lax.axis_index("y"), lax.axis_index("c")
        me = 4 * x + 2 * y + c
        local, remote = [], []
        for l in range(n_parts):
            cp = pltpu.make_async_copy(x_refs[l].at[me], out_ref.at[l, me], local_sems.at[l])
            cp.start()
            local.append(cp)
            for k, (fx, fy, fc) in enumerate(_PEER_FLIPS):
                px = 1 - x if fx else x
                py = 1 - y if fy else y
                pc = 1 - c if fc else c
                cp = pltpu.make_async_remote_copy(
                    src_ref=x_refs[l].at[4 * px + 2 * py + pc], dst_ref=out_ref.at[l, me],
                    send_sem=send_sems.at[7 * l + k], recv_sem=recv_sems.at[7 * l + k],
                    device_id=(px, py, pc), device_id_type=MESH)
                cp.start()
                remote.append(cp)
        for cp in remote:
            cp.wait()
        for cp in local:
            cp.wait()

    shape = parts[0].shape
    return pl.pallas_call(
        body, name="all_to_all",
        out_shape=jax.ShapeDtypeStruct((n_parts,) + shape, parts[0].dtype),
        in_specs=[pl.BlockSpec(memory_space=pltpu.HBM)] * n_parts,
        out_specs=pl.BlockSpec(memory_space=pltpu.HBM),
        scratch_shapes=[pltpu.SemaphoreType.DMA((7 * n_parts,)), pltpu.SemaphoreType.DMA((7 * n_parts,)),
                        pltpu.SemaphoreType.DMA((n_parts,))],
    )(*parts)


def _pack(arrs, dtype):
    pieces = []
    for a in arrs:
        flat = a.astype(dtype).reshape(-1)
        flat = jnp.pad(flat, (0, (-flat.shape[0]) % PACK_QUANTUM))
        pieces.append(flat.reshape(-1, LANE))
    return jnp.concatenate(pieces, axis=0) if len(pieces) > 1 else pieces[0]


def _unpack(packed, shapes, lead=()):
    flat = packed.reshape(lead + (-1,))
    outs, off = [], 0
    for sh in shapes:
        n = math.prod(sh)
        outs.append(flat[..., off:off + n].reshape(lead + tuple(sh)))
        off += n + (-n) % PACK_QUANTUM
    return outs


def _gather_list(arrs, dtype):
    gathered = _all_gather(_pack(arrs, dtype))
    return _unpack(gathered, [a.shape for a in arrs], lead=(N_DEV,))


def _unshard(g, axis):
    y = jnp.moveaxis(g, 0, axis)
    sh = y.shape
    return y.reshape(sh[:axis] + (sh[axis] * sh[axis + 1],) + sh[axis + 2:])


def _my_block(full, me, axis):
    size = full.shape[axis] // N_DEV
    return lax.dynamic_slice_in_dim(full, me * size, size, axis=axis)


def _sum_slabs(v):
    n_slab, rows, width = v.shape
    tr = _pick(rows, (1024, 512, 256, 128, 64, 32, 16, 8))

    def body(v_ref, o_ref):
        acc = v_ref[0]
        for k in range(1, n_slab):
            acc = acc + v_ref[k]
        o_ref[...] = acc

    return pl.pallas_call(
        body, name="sum_slabs", grid=(rows // tr,),
        in_specs=[pl.BlockSpec((n_slab, tr, width), lambda i: (0, i, 0))],
        out_specs=pl.BlockSpec((tr, width), lambda i: (i, 0)),
        out_shape=jax.ShapeDtypeStruct((rows, width), F32),
        compiler_params=_cp(("parallel",)),
    )(v)


def _mm(name, grid, nk, ins, in_specs, out_shape, out_specs, acc_shape, step, fin):
    n_in, n_out = len(ins), len(out_shape)

    def body(*refs):
        in_refs, out_refs, acc = refs[:n_in], refs[n_in:n_in + n_out], refs[n_in + n_out]
        k = pl.program_id(2)

        @pl.when(k == 0)
        def _():
            acc[...] = jnp.zeros(acc.shape, F32)

        step(acc, *in_refs)

        @pl.when(k == nk - 1)
        def _():
            fin(acc, in_refs, out_refs)

    return pl.pallas_call(
        body, name=name, grid=grid, in_specs=in_specs, out_specs=out_specs, out_shape=out_shape,
        scratch_shapes=[pltpu.VMEM(acc_shape, F32)],
        compiler_params=_cp(("parallel", "parallel", "arbitrary"), 48),
    )(*ins)


def _mm_nn(name, a, b, bias=None):
    m, kk = a.shape
    nn = b.shape[1]
    tm, tn = _pick(m, _TM), _pick(nn, _TN)
    tk = kk if kk <= 2048 else _pick(kk, _TK)
    nk = kk // tk
    ins = [a, b]
    specs = [pl.BlockSpec((tm, tk), lambda i, j, k: (i, k)), pl.BlockSpec((tk, tn), lambda i, j, k: (k, j))]
    if bias is not None:
        ins.append(bias)
        specs.append(pl.BlockSpec((1, tn), lambda i, j, k: (0, j)))

    def step(acc, a_ref, b_ref, *_):
        acc[...] += _dot(a_ref[...], b_ref[...], 1, 0)

    def fin(acc, in_refs, out_refs):
        r = acc[...]
        if bias is not None:
            r = r + in_refs[2][...]
        out_refs[0][...] = r

    return _mm(name, (m // tm, nn // tn, nk), nk, ins, specs,
               (jax.ShapeDtypeStruct((m, nn), F32),), (pl.BlockSpec((tm, tn), lambda i, j, k: (i, j)),),
               (tm, tn), step, fin)[0]


def _mm_swiglu(h, w1):
    m, kk = h.shape
    f = w1.shape[1] // 2
    tm, tn = _pick(m, _TM), _pick(f, (512, 256, 128))
    nj = f // tn

    def step(acc, h_ref, wg_ref, wu_ref):
        acc[0] += _dot(h_ref[...], wg_ref[...], 1, 0)
        acc[1] += _dot(h_ref[...], wu_ref[...], 1, 0)

    def fin(acc, in_refs, out_refs):
        zg, zu = acc[0], acc[1]
        out_refs[0][0] = zg.astype(BF16)
        out_refs[0][1] = zu.astype(BF16)
        out_refs[1][...] = (zg * _sig(zg) * zu).astype(BF16)

    return _mm("ffn_in", (m // tm, nj, 1), 1, [h, w1, w1],
               [pl.BlockSpec((tm, kk), lambda i, j, k: (i, 0)),
                pl.BlockSpec((kk, tn), lambda i, j, k: (0, j)),
                pl.BlockSpec((kk, tn), lambda i, j, k: (0, j + nj))],
               (jax.ShapeDtypeStruct((2, m, f), BF16), jax.ShapeDtypeStruct((m, f), BF16)),
               (pl.BlockSpec((2, tm, tn), lambda i, j, k: (0, i, j)), pl.BlockSpec((tm, tn), lambda i, j, k: (i, j))),
               (2, tm, tn), step, fin)


def _mm_nt_swiglu_bwd(dfb, w2, z):
    m, kk = dfb.shape
    f = w2.shape[0]
    tm, tn = _pick(m, _TM), _pick(f, (512, 256, 128))

    def step(acc, a_ref, b_ref, z_ref):
        acc[...] += _dot(a_ref[...], b_ref[...], 1, 1)

    def fin(acc, in_refs, out_refs):
        da = acc[...]
        zg, zu = in_refs[2][0].astype(F32), in_refs[2][1].astype(F32)
        sg = _sig(zg)
        out_refs[0][0] = (da * zu * (sg * (1.0 + zg * (1.0 - sg)))).astype(BF16)
        out_refs[0][1] = (da * (zg * sg)).astype(BF16)

    return _mm("ffn_out_dx", (m // tm, f // tn, 1), 1, [dfb, w2, z],
               [pl.BlockSpec((tm, kk), lambda i, j, k: (i, 0)),
                pl.BlockSpec((tn, kk), lambda i, j, k: (j, 0)),
                pl.BlockSpec((2, tm, tn), lambda i, j, k: (0, i, j))],
               (jax.ShapeDtypeStruct((2, m, f), BF16),),
               (pl.BlockSpec((2, tm, tn), lambda i, j, k: (0, i, j)),),
               (tm, tn), step, fin)[0]


def _mm_nt(name, a, b, a_halves=False):
    nn, kk = b.shape
    m = a.shape[1] if a_halves else a.shape[0]
    kh = kk // 2 if a_halves else kk
    tm, tn = _pick(m, _TM), _pick(nn, _TN)
    tk = kh if (kh <= 2048 and not a_halves) else _pick(kh, _TK)
    nk, nkh = kk // tk, kh // tk
    if a_halves:
        a_spec = pl.BlockSpec((None, tm, tk), lambda i, j, k: (k // nkh, i, k % nkh))
    else:
        a_spec = pl.BlockSpec((tm, tk), lambda i, j, k: (i, k))

    def step(acc, a_ref, b_ref):
        acc[...] += _dot(a_ref[...], b_ref[...], 1, 1)

    def fin(acc, in_refs, out_refs):
        out_refs[0][...] = acc[...]

    return _mm(name, (m // tm, nn // tn, nk), nk, [a, b],
               [a_spec, pl.BlockSpec((tn, tk), lambda i, j, k: (j, k))],
               (jax.ShapeDtypeStruct((m, nn), F32),), (pl.BlockSpec((tm, tn), lambda i, j, k: (i, j)),),
               (tm, tn), step, fin)[0]


def _mm_tn(name, a, b, b_halves=False, shard_cols=False):
    kt, m = a.shape
    nn = 2 * b.shape[2] if b_halves else b.shape[1]
    tk, tm = _pick(kt, _TM), _pick(m, (1408, 1024, 512, 256, 128))
    tn = nn // N_DEV if shard_cols else _pick(nn, _TN)
    njh = (nn // 2) // tn if b_halves else 0
    if b_halves:
        b_spec = pl.BlockSpec((None, tk, tn), lambda i, j, k: (j // njh, k, j % njh))
    else:
        b_spec = pl.BlockSpec((tk, tn), lambda i, j, k: (k, j))
    if shard_cols:
        out_shape = jax.ShapeDtypeStruct((N_DEV, m, tn), BF16)
        out_spec = pl.BlockSpec((None, tm, tn), lambda i, j, k: (j, i, 0))
    else:
        out_shape = jax.ShapeDtypeStruct((m, nn), BF16)
        out_spec = pl.BlockSpec((tm, tn), lambda i, j, k: (i, j))

    def step(acc, a_ref, b_ref):
        acc[...] += _dot(a_ref[...], b_ref[...], 0, 0)

    def fin(acc, in_refs, out_refs):
        out_refs[0][...] = acc[...].astype(BF16)

    return _mm(name, (m // tm, nn // tn, kt // tk), kt // tk, [a, b],
               [pl.BlockSpec((tk, tm), lambda i, j, k: (k, i)), b_spec],
               (out_shape,), (out_spec,), (tm, tn), step, fin)[0]


def _modulate(s, modtab, i_shift, i_scale, tt):
    n, d = s.shape

    def body(s_ref, m_ref, o_ref):
        shift, scale = m_ref[pl.ds(i_shift, 1), :], m_ref[pl.ds(i_scale, 1), :]
        o_ref[...] = (s_ref[...] * (1.0 + scale) + shift).astype(BF16)

    return pl.pallas_call(
        body, name="modulate", grid=(n // tt,),
        in_specs=[_row_spec(tt, d), _mod_spec(d)], out_specs=_row_spec(tt, d),
        out_shape=jax.ShapeDtypeStruct((n, d), BF16), compiler_params=_cp(("parallel",)),
    )(s, modtab)


def _res_ln(s, f, modtab, i_gate, coef, gam, bet, alpha, tt):
    n, d = s.shape

    def body(s_ref, f_ref, m_ref, g_ref, b_ref, r_ref, o_ref):
        gate = m_ref[pl.ds(i_gate, 1), :]
        r = alpha * s_ref[...] + (coef * gate) * f_ref[...]
        r_ref[...] = r
        mu = jnp.mean(r, axis=-1, keepdims=True)
        xc = r - mu
        var = jnp.mean(xc * xc, axis=-1, keepdims=True)
        o_ref[...] = xc * lax.rsqrt(var + LN_EPS) * g_ref[...] + b_ref[...]

    return pl.pallas_call(
        body, name="res_ln", grid=(n // tt,),
        in_specs=[_row_spec(tt, d), _row_spec(tt, d), _mod_spec(d), _full_spec((1, d)), _full_spec((1, d))],
        out_specs=(_row_spec(tt, d), _row_spec(tt, d)),
        out_shape=(jax.ShapeDtypeStruct((n, d), F32), jax.ShapeDtypeStruct((n, d), F32)),
        compiler_params=_cp(("parallel",)),
    )(s, f, modtab, gam, bet)


def _ln_bwd(do, r, f, modtab, i_gate, coef, gam, tt):
    n, d = do.shape

    def body(do_ref, r_ref, f_ref, m_ref, g_ref, dr_ref, dfb_ref, tot_ref, str_ref):
        i = pl.program_id(0)
        r = r_ref[...]
        mu = jnp.mean(r, axis=-1, keepdims=True)
        xc = r - mu
        rstd = lax.rsqrt(jnp.mean(xc * xc, axis=-1, keepdims=True) + LN_EPS)
        xh = xc * rstd
        dout = do_ref[...]
        dxh = dout * g_ref[...]
        m1 = jnp.mean(dxh, axis=-1, keepdims=True)
        m2 = jnp.mean(dxh * xh, axis=-1, keepdims=True)
        dr = rstd * (dxh - m1 - xh * m2)
        dr_ref[...] = dr
        dfb = (coef * m_ref[pl.ds(i_gate, 1), :]) * dr
        dfb_ref[...] = dfb.astype(BF16)
        _acc_rows(tot_ref, i == 0, [_sum0(dout * xh), _sum0(dout), _sum0(dfb)])
        _acc_rows(str_ref, i <= 1, [_sum0(coef * f_ref[...] * dr)])

    return pl.pallas_call(
        body, name="ln_bwd", grid=(n // tt,),
        in_specs=[_row_spec(tt, d), _row_spec(tt, d), _row_spec(tt, d), _mod_spec(d), _full_spec((1, d))],
        out_specs=(_row_spec(tt, d), _row_spec(tt, d), _full_spec((8, d)), _stream_sum_spec(d)),
        out_shape=(jax.ShapeDtypeStruct((n, d), F32), jax.ShapeDtypeStruct((n, d), BF16),
                   jax.ShapeDtypeStruct((8, d), F32), jax.ShapeDtypeStruct((16, d), F32)),
        compiler_params=_cp(("arbitrary",)),
    )(do, r, f, modtab, gam)


def _mod_bwd(dr, dh, s, modtab, i_scale, alpha, tt):
    n, d = dr.shape

    def body(dr_ref, dh_ref, s_ref, m_ref, o_ref, str_ref):
        i = pl.program_id(0)
        dh_v = dh_ref[...]
        o_ref[...] = alpha * dr_ref[...] + dh_v * (1.0 + m_ref[pl.ds(i_scale, 1), :])
        _acc_rows(str_ref, i <= 1, [_sum0(dh_v * s_ref[...]), _sum0(dh_v)])

    return pl.pallas_call(
        body, name="mod_bwd", grid=(n // tt,),
        in_specs=[_row_spec(tt, d), _row_spec(tt, d), _row_spec(tt, d), _mod_spec(d)],
        out_specs=(_row_spec(tt, d), _stream_sum_spec(d)),
        out_shape=(jax.ShapeDtypeStruct((n, d), F32), jax.ShapeDtypeStruct((16, d), F32)),
        compiler_params=_cp(("arbitrary",)),
    )(dr, dh, s, modtab)


def _loss_head(s, target, tt):
    n, d = s.shape
    nt = n // tt

    def body(s_ref, t_ref, do_ref, loss_ref, acc):
        i = pl.program_id(0)

        @pl.when(i == 0)
        def _():
            acc[...] = jnp.zeros(acc.shape, F32)
            do_ref[...] = jnp.zeros(do_ref.shape, F32)

        @pl.when(i > 0)
        def _():
            err = s_ref[...] - t_ref[...]
            do_ref[...] = err / d
            acc[...] += jnp.sum((err * err).reshape(tt // 8, 8, d), axis=0)

        @pl.when(i == nt - 1)
        def _():
            loss_ref[...] = jnp.full(loss_ref.shape, jnp.sum(acc[...]) * (0.5 / d), F32)

    return pl.pallas_call(
        body, name="loss_head", grid=(nt,),
        in_specs=[_row_spec(tt, d), pl.BlockSpec((tt, d), lambda i: (jnp.maximum(i - 1, 0), 0))],
        out_specs=(_row_spec(tt, d), _full_spec((8, LANE))),
        out_shape=(jax.ShapeDtypeStruct((n, d), F32), jax.ShapeDtypeStruct((8, LANE), F32)),
        scratch_shapes=[pltpu.VMEM((8, d), F32)],
        compiler_params=_cp(("arbitrary",)),
    )(s, target)


def _halo_specs(tt, c, n):
    nb, last = tt // 8, n // 8 - 1
    return [pl.BlockSpec((tt, c), lambda i: (i, 0)),
            pl.BlockSpec((8, c), lambda i: (jnp.maximum(i * nb - 1, 0), 0)),
            pl.BlockSpec((8, c), lambda i: (jnp.minimum((i + 1) * nb, last), 0))]


def _fill_halo(scr, main_ref, prev_ref, next_ref, i, nt, tt):
    has_prev = i >= 2
    has_next = jnp.logical_and(i >= 1, i < nt - 1)
    scr[pl.ds(0, 8), :] = jnp.where(has_prev, prev_ref[...], 0.0)
    scr[pl.ds(8, tt), :] = main_ref[...]
    scr[pl.ds(8 + tt, 8), :] = jnp.where(has_next, next_ref[...], 0.0)


def _conv4(p, w, b, tt, c):
    n = p.shape[0]
    nt, taps = n // tt, w.shape[0]
    left = taps // 2

    def body(x_ref, xp_ref, xn_ref, w_ref, b_ref, o_ref, scr):
        _fill_halo(scr, x_ref, xp_ref, xn_ref, pl.program_id(0), nt, tt)
        acc = jnp.broadcast_to(b_ref[...], (tt, c))
        for k in range(taps):
            acc = acc + w_ref[pl.ds(k, 1), :] * scr[pl.ds(8 + k - left, tt), :]
        o_ref[...] = acc

    return pl.pallas_call(
        body, name="conv4", grid=(nt,),
        in_specs=_halo_specs(tt, c, n) + [_full_spec(w.shape), _full_spec((1, c))],
        out_specs=_row_spec(tt, c), out_shape=jax.ShapeDtypeStruct((n, c), F32),
        scratch_shapes=[pltpu.VMEM((tt + 16, c), F32)], compiler_params=_cp(("parallel",)),
    )(p, p, p, w, b)


def _conv4_bwd(dxr, p, w, tt, c):
    n = p.shape[0]
    nt, taps = n // tt, w.shape[0]
    left = taps // 2

    def body(d_ref, dp_ref, dn_ref, x_ref, xp_ref, xn_ref, w_ref, o_ref, sum_ref, dscr, xscr):
        i = pl.program_id(0)
        _fill_halo(dscr, d_ref, dp_ref, dn_ref, i, nt, tt)
        _fill_halo(xscr, x_ref, xp_ref, xn_ref, i, nt, tt)
        acc = jnp.zeros((tt, c), F32)
        for k in range(taps):
            acc = acc + w_ref[pl.ds(k, 1), :] * dscr[pl.ds(8 - (k - left), tt), :]
        o_ref[...] = acc.astype(BF16)
        d = d_ref[...]
        rows = [_sum0(d * xscr[pl.ds(8 + k - left, tt), :]) for k in range(taps)] + [_sum0(d)]
        _acc_rows(sum_ref, i == 0, rows)

    return pl.pallas_call(
        body, name="conv4_bwd", grid=(nt,),
        in_specs=_halo_specs(tt, c, n) + _halo_specs(tt, c, n) + [_full_spec(w.shape)],
        out_specs=(_row_spec(tt, c), _full_spec((8, c))),
        out_shape=(jax.ShapeDtypeStruct((n, c), BF16), jax.ShapeDtypeStruct((8, c), F32)),
        scratch_shapes=[pltpu.VMEM((tt + 16, c), F32), pltpu.VMEM((tt + 16, c), F32)],
        compiler_params=_cp(("arbitrary",)),
    )(dxr, dxr, dxr, p, p, p, w)


def _head_gates(xv, wr, wi, br, bi, lam):
    xb = xv.astype(BF16)
    r = _sig(_dot(xb, wr, 1, 0) + br)
    ig = _sig(_dot(xb, wi, 1, 0) + bi)
    sp = _softplus_neg(lam)
    log_a = (-RG_C) * r * sp
    return r, ig, sp, jnp.exp(log_a), jnp.sqrt(_neg_expm1(2.0 * log_a))


def _tile_scan(a, b, increasing, tt, rows):
    s = 1
    while s < tt:
        if increasing:
            a_sh, b_sh, ok = pltpu.roll(a, s, 0), pltpu.roll(b, s, 0), rows >= s
        else:
            a_sh, b_sh, ok = pltpu.roll(a, tt - s, 0), pltpu.roll(b, tt - s, 0), rows < tt - s
        b = jnp.where(ok, a * b_sh + b, b)
        a = jnp.where(ok, a * a_sh, a)
        s *= 2
    return a, b


def _scan_tile_index(step, nt, reverse):
    return jnp.where(step == 0, 0, nt - step) if reverse else step


def _scan(xr, wr, wi, br, bi, lam, reverse, tt):
    n, c = xr.shape
    nt = n // tt
    heads, hs = wr.shape[0], wr.shape[2]
    lc = min(c, LANE)

    def tile(i):
        return _scan_tile_index(i, nt, reverse)

    def body(x_ref, wr_ref, wi_ref, br_ref, bi_ref, lam_ref, h_ref, hin_ref, a_scr, b_scr, carry):
        i = pl.program_id(0)

        @pl.when(i == 0)
        def _():
            carry[...] = jnp.zeros(carry.shape, F32)

        for hd in range(heads):
            sl = slice(hd * hs, (hd + 1) * hs)
            xv = x_ref[:, sl]
            _, ig, _, a, sq = _head_gates(xv, wr_ref[hd], wi_ref[hd], br_ref[:, sl], bi_ref[:, sl], lam_ref[:, sl])
            a_scr[:, sl] = a
            b_scr[:, sl] = sq * (ig * xv)
        hin_ref[...] = jnp.broadcast_to(carry[...], (8, c))
        rows = lax.broadcasted_iota(jnp.int32, (tt, lc), 0)
        for ch in range(c // lc):
            sl = slice(ch * lc, (ch + 1) * lc)
            big_a, big_b = _tile_scan(a_scr[:, sl], b_scr[:, sl], not reverse, tt, rows)
            h_ref[:, sl] = big_a * carry[:, sl] + big_b
        carry[...] = h_ref[pl.ds(0 if reverse else tt - 1, 1), :]

    return pl.pallas_call(
        body, name="lru_scan", grid=(nt,),
        in_specs=[pl.BlockSpec((tt, c), lambda i: (tile(i), 0)), _full_spec(wr.shape), _full_spec(wi.shape),
                  _full_spec((1, c)), _full_spec((1, c)), _full_spec((1, c))],
        out_specs=(pl.BlockSpec((tt, c), lambda i: (tile(i), 0)), pl.BlockSpec((None, 8, c), lambda i: (tile(i), 0, 0))),
        out_shape=(jax.ShapeDtypeStruct((n, c), F32), jax.ShapeDtypeStruct((nt, 8, c), F32)),
        scratch_shapes=[pltpu.VMEM((tt, c), F32), pltpu.VMEM((tt, c), F32), pltpu.VMEM((1, c), F32)],
        compiler_params=_cp(("arbitrary",)),
    )(xr, wr, wi, br, bi, lam)


def _scan_bwd(xr, dh, h, hin, wr, wi, br, bi, lam, reverse, tt, dx_prev=None):
    n, c = xr.shape
    nt = n // tt
    heads, hs = wr.shape[0], wr.shape[2]
    lc = min(c, LANE)
    inc = not reverse
    first, last = (0, tt - 1) if inc else (tt - 1, 0)

    def tile(i):
        return _scan_tile_index(nt - 1 - i, nt, reverse)

    def body(*refs):
        x_ref, dh_ref, h_ref, hin_ref, wr_ref, wi_ref, br_ref, bi_ref, lam_ref = refs[:9]
        refs = refs[9:]
        if dx_prev is not None:
            dxp_ref, refs = refs[0], refs[1:]
        dx_ref, dwr_ref, dwi_ref, sum_ref, a_scr, r_scr, i_scr, sq_scr, g_scr, da_scr, u_scr, ucarry = refs
        i = pl.program_id(0)

        @pl.when(i == 0)
        def _():
            ucarry[...] = jnp.zeros(ucarry.shape, F32)
            dwr_ref[...] = jnp.zeros(dwr_ref.shape, F32)
            dwi_ref[...] = jnp.zeros(dwi_ref.shape, F32)
            sum_ref[...] = jnp.zeros(sum_ref.shape, F32)

        for hd in range(heads):
            sl = slice(hd * hs, (hd + 1) * hs)
            r, ig, _, a, sq = _head_gates(x_ref[:, sl], wr_ref[hd], wi_ref[hd], br_ref[:, sl], bi_ref[:, sl],
                                          lam_ref[:, sl])
            a_scr[:, sl], r_scr[:, sl], i_scr[:, sl], sq_scr[:, sl] = a, r, ig, sq

        rows = lax.broadcasted_iota(jnp.int32, (tt, lc), 0)
        to_prev = 1 if inc else tt - 1
        to_next = tt - 1 if inc else 1
        for ch in range(c // lc):
            sl = slice(ch * lc, (ch + 1) * lc)
            a, dhv = a_scr[:, sl], dh_ref[:, sl]
            big_a, big_b = _tile_scan(a, a * dhv, not inc, tt, rows)
            u_in = ucarry[:, sl]
            u = big_a * u_in + big_b
            u_scr[:, sl] = u
            g = dhv + jnp.where(rows == last, u_in, pltpu.roll(u, to_next, 0))
            g_scr[:, sl] = g
            h_prev = jnp.where(rows == first, hin_ref[pl.ds(0, 1), sl], pltpu.roll(h_ref[:, sl], to_prev, 0))
            da_scr[:, sl] = g * h_prev
        ucarry[...] = u_scr[pl.ds(first, 1), :]

        for hd in range(heads):
            sl = slice(hd * hs, (hd + 1) * hs)
            xv, a, r, ig, sq = x_ref[:, sl], a_scr[:, sl], r_scr[:, sl], i_scr[:, sl], sq_scr[:, sl]
            g, lam_v = g_scr[:, sl], lam_ref[:, sl]
            sp = _softplus_neg(lam_v)
            d_sq = g * ig * xv
            d_em = d_sq * 0.5 / sq
            d_log_a = (da_scr[:, sl] - 2.0 * a * d_em) * a
            dzr = (d_log_a * ((-RG_C) * sp)) * r * (1.0 - r)
            dzi = (g * sq * xv) * ig * (1.0 - ig)
            dzr_b, dzi_b, xb = dzr.astype(BF16), dzi.astype(BF16), xv.astype(BF16)
            dx = g * sq * ig + _dot(dzr_b, wr_ref[hd], 1, 1) + _dot(dzi_b, wi_ref[hd], 1, 1)
            if dx_prev is not None:
                dx = dx + dxp_ref[:, sl]
            dx_ref[:, sl] = dx
            dwr_ref[hd] += _dot(xb, dzr_b, 0, 0)
            dwi_ref[hd] += _dot(xb, dzi_b, 0, 0)
            sum_ref[pl.ds(0, 1), sl] += _sum0(dzr)
            sum_ref[pl.ds(1, 1), sl] += _sum0(dzi)
            sum_ref[pl.ds(2, 1), sl] += _sum0(d_log_a * ((-RG_C) * r)) * (-_sig(-lam_v))

    tile_spec = pl.BlockSpec((tt, c), lambda i: (tile(i), 0))
    ins = [xr, dh, h, hin, wr, wi, br, bi, lam]
    in_specs = [tile_spec, tile_spec, tile_spec, pl.BlockSpec((None, 8, c), lambda i: (tile(i), 0, 0)),
                _full_spec(wr.shape), _full_spec(wi.shape), _full_spec((1, c)), _full_spec((1, c)), _full_spec((1, c))]
    if dx_prev is not None:
        ins.append(dx_prev)
        in_specs.append(tile_spec)
    return pl.pallas_call(
        body, name="lru_scan_bwd", grid=(nt,), in_specs=in_specs,
        out_specs=(tile_spec, _full_spec(wr.shape), _full_spec(wi.shape), _full_spec((8, c))),
        out_shape=(jax.ShapeDtypeStruct((n, c), F32), jax.ShapeDtypeStruct(wr.shape, F32),
                   jax.ShapeDtypeStruct(wi.shape, F32), jax.ShapeDtypeStruct((8, c), F32)),
        scratch_shapes=[pltpu.VMEM((tt, c), F32)] * 7 + [pltpu.VMEM((1, c), F32)],
        compiler_params=_cp(("arbitrary",)),
    )(*ins)


def _seg_conv(scr, w_ref, taps, pos, seg_len, tt, sl, flip):
    half = taps // 2
    acc = jnp.zeros((tt, sl.stop - sl.start), F32)
    for off in range(-half, half + 1):
        k = half - off if flip else half + off
        ok = jnp.logical_and(pos + off >= 0, pos + off < seg_len)
        acc = acc + jnp.where(ok, scr[pl.ds(CONV_PAD + off, tt), sl], 0.0) * w_ref[pl.ds(k, 1), sl]
    return acc


def _zero_pads(scr, tt, c):
    scr[pl.ds(0, CONV_PAD), :] = jnp.zeros((CONV_PAD, c), F32)
    scr[pl.ds(CONV_PAD + tt, CONV_PAD), :] = jnp.zeros((CONV_PAD, c), F32)


def _ln_stats(v):
    mu = jnp.mean(v, axis=-1, keepdims=True)
    xc = v - mu
    rstd = lax.rsqrt(jnp.mean(xc * xc, axis=-1, keepdims=True) + LN_EPS)
    return xc * rstd, rstd


def _mix_mid(p, h_f, h_b, w31, b31, clg, clb, tt):
    n, c = h_f.shape
    taps = w31.shape[0]
    lc = min(c, LANE)

    def body(gr_ref, cv_ref, cg_ref, hf_ref, hb_ref, w_ref, b_ref, g_ref, bb_ref, y_ref, uc_ref, scr):
        i = pl.program_id(0)
        seg_len = jnp.where(i == 0, tt, SEG)
        gel, _ = _gelu_and_grad(gr_ref[...])
        y_ref[:, 0:c] = ((hf_ref[...] + hb_ref[...]) * gel).astype(BF16)
        _zero_pads(scr, tt, c)
        scr[pl.ds(CONV_PAD, tt), :] = cv_ref[...] * _sig(cg_ref[...])
        pos = lax.broadcasted_iota(jnp.int32, (tt, lc), 0) & (seg_len - 1)
        for ch in range(c // lc):
            sl = slice(ch * lc, (ch + 1) * lc)
            uc_ref[:, sl] = b_ref[:, sl] + _seg_conv(scr, w_ref, taps, pos, seg_len, tt, sl, False)
        xh, _ = _ln_stats(uc_ref[...])
        v = xh * g_ref[...] + bb_ref[...]
        y_ref[:, c:2 * c] = (v * _sig(v)).astype(BF16)

    return pl.pallas_call(
        body, name="mix_mid", grid=(n // tt,),
        in_specs=[_row_spec(tt, c, 1), _row_spec(tt, c, 2), _row_spec(tt, c, 3), _row_spec(tt, c), _row_spec(tt, c),
                  _full_spec(w31.shape), _full_spec((1, c)), _full_spec((1, c)), _full_spec((1, c))],
        out_specs=(_row_spec(tt, 2 * c), _row_spec(tt, c)),
        out_shape=(jax.ShapeDtypeStruct((n, 2 * c), BF16), jax.ShapeDtypeStruct((n, c), F32)),
        scratch_shapes=[pltpu.VMEM((tt + 2 * CONV_PAD, c), F32)],
        compiler_params=_cp(("parallel",)),
    )(p, p, p, h_f, h_b, w31, b31, clg, clb)


def _mix_mid_bwd(dymix, p, h_f, h_b, uc, w31, clg, clb, tt):
    n, c = h_f.shape
    taps = w31.shape[0]
    lc = min(c, LANE)
    sum_rows = 8 * ((taps + 3 + 7) // 8)

    def body(dyr_ref, dyc_ref, gr_ref, cv_ref, cg_ref, hf_ref, hb_ref, uc_ref, w_ref, g_ref, bb_ref,
             dh_ref, dp_ref, sum_ref, uscr, dscr):
        i = pl.program_id(0)

        @pl.when(i == 0)
        def _():
            sum_ref[...] = jnp.zeros(sum_ref.shape, F32)

        seg_len = jnp.where(i == 0, tt, SEG)
        dyr = dyr_ref[...]
        gel, dgel = _gelu_and_grad(gr_ref[...])
        dh_ref[...] = dyr * gel
        dp_ref[:, 0:c] = (dyr * (hf_ref[...] + hb_ref[...]) * dgel).astype(BF16)

        xh, rstd = _ln_stats(uc_ref[...])
        v = xh * g_ref[...] + bb_ref[...]
        sg = _sig(v)
        dv = dyc_ref[...] * (sg * (1.0 + v * (1.0 - sg)))
        dxh = dv * g_ref[...]
        m1 = jnp.mean(dxh, axis=-1, keepdims=True)
        m2 = jnp.mean(dxh * xh, axis=-1, keepdims=True)
        duc = rstd * (dxh - m1 - xh * m2)
        sum_ref[pl.ds(taps, 1), :] += _sum0(duc)
        sum_ref[pl.ds(taps + 1, 1), :] += _sum0(dv * xh)
        sum_ref[pl.ds(taps + 2, 1), :] += _sum0(dv)

        _zero_pads(dscr, tt, c)
        _zero_pads(uscr, tt, c)
        dscr[pl.ds(CONV_PAD, tt), :] = duc
        uscr[pl.ds(CONV_PAD, tt), :] = cv_ref[...] * _sig(cg_ref[...])
        pos = lax.broadcasted_iota(jnp.int32, (tt, lc), 0) & (seg_len - 1)
        half = taps // 2
        for ch in range(c // lc):
            sl = slice(ch * lc, (ch + 1) * lc)
            du = _seg_conv(dscr, w_ref, taps, pos, seg_len, tt, sl, True)
            sgc, cv = _sig(cg_ref[:, sl]), cv_ref[:, sl]
            dp_ref[:, c + sl.start:c + sl.stop] = (du * sgc).astype(BF16)
            dp_ref[:, 2 * c + sl.start:2 * c + sl.stop] = (du * cv * sgc * (1.0 - sgc)).astype(BF16)
            duc_c = dscr[pl.ds(CONV_PAD, tt), sl]
            for off in range(-half, half + 1):
                ok = jnp.logical_and(pos + off >= 0, pos + off < seg_len)
                sum_ref[pl.ds(half + off, 1), sl] += _sum0(
                    duc_c * jnp.where(ok, uscr[pl.ds(CONV_PAD + off, tt), sl], 0.0))

    return pl.pallas_call(
        body, name="mix_mid_bwd", grid=(n // tt,),
        in_specs=[_row_spec(tt, c, 0), _row_spec(tt, c, 1), _row_spec(tt, c, 1), _row_spec(tt, c, 2),
                  _row_spec(tt, c, 3), _row_spec(tt, c), _row_spec(tt, c), _row_spec(tt, c),
                  _full_spec(w31.shape), _full_spec((1, c)), _full_spec((1, c))],
        out_specs=(_row_spec(tt, c), _row_spec(tt, 3 * c), _full_spec((sum_rows, c))),
        out_shape=(jax.ShapeDtypeStruct((n, c), F32), jax.ShapeDtypeStruct((n, 3 * c), BF16),
                   jax.ShapeDtypeStruct((sum_rows, c), F32)),
        scratch_shapes=[pltpu.VMEM((tt + 2 * CONV_PAD, c), F32), pltpu.VMEM((tt + 2 * CONV_PAD, c), F32)],
        compiler_params=_cp(("arbitrary",)),
    )(dymix, dymix, p, p, p, h_f, h_b, uc, w31, clg, clb)


def _ada_fwd(cc, w_ada, b_cols):
    depth, d, wc = w_ada.shape
    tn = _pick(wc, (768, 512, 384, 256, 128))

    def body(c_ref, w_ref, b_ref, o_ref):
        cv = c_ref[...]
        o_ref[...] = _dot((cv * _sig(cv)).astype(BF16), w_ref[...].astype(BF16), 1, 0) + b_ref[...]

    return pl.pallas_call(
        body, name="ada_fwd", grid=(depth, wc // tn),
        in_specs=[_full_spec((MOD_ROWS, d)), pl.BlockSpec((None, d, tn), lambda l, j: (l, 0, j)),
                  pl.BlockSpec((None, 1, tn), lambda l, j: (l, 0, j))],
        out_specs=pl.BlockSpec((None, MOD_ROWS, tn), lambda l, j: (l, 0, j)),
        out_shape=jax.ShapeDtypeStruct((depth, MOD_ROWS, wc), F32),
        compiler_params=_cp(("parallel", "parallel")),
    )(cc, w_ada, b_cols)


def _ada_bwd(cc, dm, w_ada):
    depth, d, wc = w_ada.shape
    tn = _pick(wc, (768, 512, 384, 256, 128))

    def body(c_ref, dm_ref, w_ref, gw_ref, gc_ref):
        cv = c_ref[...]
        sg = _sig(cv)
        dmb = dm_ref[...].astype(BF16)
        gw_ref[...] = _dot((cv * sg).astype(BF16), dmb, 0, 0)

        @pl.when(jnp.logical_and(pl.program_id(0) == 0, pl.program_id(1) == 0))
        def _():
            gc_ref[...] = jnp.zeros(gc_ref.shape, F32)

        gc_ref[...] += _dot(dmb, w_ref[...].astype(BF16), 1, 1) * (sg * (1.0 + cv * (1.0 - sg)))

    return pl.pallas_call(
        body, name="ada_bwd", grid=(depth, wc // tn),
        in_specs=[_full_spec((MOD_ROWS, d)), pl.BlockSpec((None, MOD_ROWS, tn), lambda l, j: (l, 0, j)),
                  pl.BlockSpec((None, d, tn), lambda l, j: (l, 0, j))],
        out_specs=(pl.BlockSpec((None, d, tn), lambda l, j: (l, 0, j)), _full_spec((MOD_ROWS, d))),
        out_shape=(jax.ShapeDtypeStruct((depth, d, wc), F32), jax.ShapeDtypeStruct((MOD_ROWS, d), F32)),
        compiler_params=_cp(("arbitrary", "arbitrary")),
    )(cc, dm, w_ada)


def _adam(grads, w, m, v):
    depth, n_slab, rows, cols = grads.shape
    tr = _pick(rows, (128, 64, 32, 16, 8))
    c1, c2 = 1.0 - ADAM_B1 ** ADAM_STEP, 1.0 - ADAM_B2 ** ADAM_STEP

    def body(g_ref, w_ref, m_ref, v_ref, go_ref, do_ref, mo_ref, vo_ref):
        g = g_ref[0].astype(F32)
        for k in range(1, n_slab):
            g = g + g_ref[k].astype(F32)
        go_ref[...] = g
        m_new = ADAM_B1 * m_ref[...] + (1.0 - ADAM_B1) * g
        v_new = ADAM_B2 * v_ref[...] + (1.0 - ADAM_B2) * (g * g)
        mo_ref[...] = m_new
        vo_ref[...] = v_new
        do_ref[...] = -ADAM_LR * ((m_new / c1) / (jnp.sqrt(v_new / c2) + ADAM_EPS) + ADAM_WD * w_ref[...])

    blk = pl.BlockSpec((None, tr, cols), lambda l, i: (l, i, 0))
    out = jax.ShapeDtypeStruct((depth, rows, cols), F32)
    return pl.pallas_call(
        body, name="adamw", grid=(depth, rows // tr),
        in_specs=[pl.BlockSpec((None, n_slab, tr, cols), lambda l, i: (l, 0, i, 0)), blk, blk, blk],
        out_specs=(blk, blk, blk, blk), out_shape=(out, out, out, out),
        compiler_params=_cp(("parallel", "parallel"), 48),
    )(grads, w, m, v)


_WEIGHTS = ['c_ctx', 'w_ada', 'b_ada', 'ln_g', 'ln_b', 'ff1_in', 'ff1_out', 'ff2_in', 'ff2_out', 'w_in', 'conv4_w',
            'conv4_b', 'w_rg', 'b_rg', 'w_ig', 'b_ig', 'lam', 'conv31_w', 'conv31_b', 'cln_g', 'cln_b', 'w_out', 'b_out']
_SMALL_SHARDED = {'ln_g': 2, 'ln_b': 2, 'conv4_w': 2, 'w_rg': 3, 'b_rg': 2, 'w_ig': 3, 'b_ig': 2, 'lam': 2,
                  'conv31_w': 2}
_BIG_SHARDED = {'ff1_in': 2, 'ff1_out': 1, 'ff2_in': 2, 'ff2_out': 1, 'w_in': 2, 'w_out': 1}
_SMALL = [n for n in _WEIGHTS if n not in _BIG_SHARDED and n != 'w_ada']


def kernel(x, c, ctx, c_ctx, w_ada, b_ada, ln_g, ln_b, ff1_in, ff1_out, ff2_in, ff2_out, w_in, conv4_w, conv4_b, w_rg, b_rg, w_ig, b_ig, lam, conv31_w, conv31_b, cln_g, cln_b, w_out, b_out, loss_target, m_c_ctx, m_w_ada, m_b_ada, m_ln_g, m_ln_b, m_ff1_in, m_ff1_out, m_ff2_in, m_ff2_out, m_w_in, m_conv4_w, m_conv4_b, m_w_rg, m_b_rg, m_w_ig, m_b_ig, m_lam, m_conv31_w, m_conv31_b, m_cln_g, m_cln_b, m_w_out, m_b_out, v_c_ctx, v_w_ada, v_b_ada, v_ln_g, v_ln_b, v_ff1_in, v_ff1_out, v_ff2_in, v_ff2_out, v_w_in, v_conv4_w, v_conv4_b, v_w_rg, v_b_rg, v_w_ig, v_b_ig, v_lam, v_conv31_w, v_conv31_b, v_cln_g, v_cln_b, v_w_out, v_b_out):
    wts = dict(c_ctx=c_ctx, w_ada=w_ada, b_ada=b_ada, ln_g=ln_g, ln_b=ln_b, ff1_in=ff1_in, ff1_out=ff1_out,
               ff2_in=ff2_in, ff2_out=ff2_out, w_in=w_in, conv4_w=conv4_w, conv4_b=conv4_b, w_rg=w_rg, b_rg=b_rg,
               w_ig=w_ig, b_ig=b_ig, lam=lam, conv31_w=conv31_w, conv31_b=conv31_b, cln_g=cln_g, cln_b=cln_b,
               w_out=w_out, b_out=b_out)
    mom = dict(zip(_WEIGHTS, (m_c_ctx, m_w_ada, m_b_ada, m_ln_g, m_ln_b, m_ff1_in, m_ff1_out, m_ff2_in, m_ff2_out,
                              m_w_in, m_conv4_w, m_conv4_b, m_w_rg, m_b_rg, m_w_ig, m_b_ig, m_lam, m_conv31_w,
                              m_conv31_b, m_cln_g, m_cln_b, m_w_out, m_b_out)))
    var = dict(zip(_WEIGHTS, (v_c_ctx, v_w_ada, v_b_ada, v_ln_g, v_ln_b, v_ff1_in, v_ff1_out, v_ff2_in, v_ff2_out,
                              v_w_in, v_conv4_w, v_conv4_b, v_w_rg, v_b_rg, v_w_ig, v_b_ig, v_lam, v_conv31_w,
                              v_conv31_b, v_cln_g, v_cln_b, v_w_out, v_b_out)))

    depth, d, wc = w_ada.shape
    t_lat, t_ctx = x.shape[1], ctx.shape[1]
    tt, n_tok = t_ctx, t_ctx + x.shape[1]
    assert t_lat % tt == 0 and tt % SEG == 0 and tt & (tt - 1) == 0 and d % 2 == 0
    ch = d // 2
    alpha = (2.0 * depth) ** 0.25
    me = 4 * lax.axis_index("x") + 2 * lax.axis_index("y") + lax.axis_index("c")

    names = list(_SMALL_SHARDED)
    got = _gather_list([wts[k] for k in names] + [c], F32)
    full = {k: _unshard(g, _SMALL_SHARDED[k]) for k, g in zip(names, got[:-1])}
    c_all = got[-1].reshape(N_DEV, d)
    cc = jnp.concatenate([c_all, jnp.pad(c_ctx[None, :], ((0, MOD_ROWS - N_DEV - 1), (0, 0)))], axis=0)

    b_cols = _my_block(b_ada, me, 1)[:, None, :]
    (m_cols,) = _gather_list([_ada_fwd(cc, w_ada, b_cols)], F32)
    m_full = jnp.moveaxis(m_cols, 0, 2).reshape(depth, MOD_ROWS, N_DEV * wc)
    m_lat = lax.dynamic_index_in_dim(m_full, me, axis=1, keepdims=False).reshape(depth, 9, d)
    m_ctx = m_full[:, N_DEV].reshape(depth, 9, d)
    modtab = jnp.pad(jnp.stack([m_ctx, m_lat], axis=1), ((0, 0), (0, 0), (0, MOD_ROWS - 9), (0, 0)))

    names = list(_BIG_SHARDED)
    got = _gather_list([wts[k] for k in names], BF16)
    big = {k: _unshard(g, _BIG_SHARDED[k]) for k, g in zip(names, got)}
    wr_b, wi_b = full['w_rg'].astype(BF16), full['w_ig'].astype(BF16)

    def row(v):
        return v.reshape(1, -1)

    s = jnp.concatenate([ctx[0], x[0]], axis=0)
    saved = []
    for l in range(depth):
        mt, sv = modtab[l], {}
        sv['s0'] = s
        sv['h1'] = _modulate(s, mt, 0, 1, tt)
        sv['z1'], sv['a1'] = _mm_swiglu(sv['h1'], big['ff1_in'][l])
        sv['f1'] = _mm_nn("ffn_out", sv['a1'], big['ff1_out'][l])
        sv['r1'], sv['s1'] = _res_ln(s, sv['f1'], mt, 2, 0.5, row(full['ln_g'][l, 0]), row(full['ln_b'][l, 0]), alpha, tt)
        sv['h2'] = _modulate(sv['s1'], mt, 3, 4, tt)
        sv['p'] = _mm_nn("mix_in", sv['h2'], big['w_in'][l])
        sv['xr'] = _conv4(sv['p'], full['conv4_w'][l], row(conv4_b[l]), tt, ch)
        for dr_, rev in ((0, False), (1, True)):
            sv['h', dr_], sv['hin', dr_] = _scan(
                sv['xr'], wr_b[l, dr_], wi_b[l, dr_], row(full['b_rg'][l, dr_]), row(full['b_ig'][l, dr_]),
                row(full['lam'][l, dr_]), rev, tt)
        sv['ymix'], sv['uc'] = _mix_mid(sv['p'], sv['h', 0], sv['h', 1], full['conv31_w'][l], row(conv31_b[l]),
                                        row(cln_g[l]), row(cln_b[l]), tt)
        sv['y'] = _mm_nn("mix_out", sv['ymix'], big['w_out'][l], bias=row(b_out[l]))
        sv['r2'], sv['s2'] = _res_ln(sv['s1'], sv['y'], mt, 5, 1.0, row(full['ln_g'][l, 1]), row(full['ln_b'][l, 1]),
                                     alpha, tt)
        sv['h3'] = _modulate(sv['s2'], mt, 6, 7, tt)
        sv['z3'], sv['a3'] = _mm_swiglu(sv['h3'], big['ff2_in'][l])
        sv['f3'] = _mm_nn("ffn_out", sv['a3'], big['ff2_out'][l])
        sv['r3'], s = _res_ln(sv['s2'], sv['f3'], mt, 8, 0.5, row(full['ln_g'][l, 2]), row(full['ln_b'][l, 2]), alpha, tt)
        saved.append(sv)

    cot, loss_blk = _loss_head(s, loss_target[0], tt)
    loss = lax.psum(loss_blk[0, 0], ("x", "y", "c"))

    gbig = {k: [None] * depth for k in _BIG_SHARDED}
    gsm = {k: [None] * depth for k in ('ln_g', 'ln_b', 'conv4_w', 'conv4_b', 'w_rg', 'b_rg', 'w_ig', 'b_ig', 'lam',
                                       'conv31_w', 'conv31_b', 'cln_g', 'cln_b', 'b_out', 'dm')}
    taps4, taps31 = conv4_w.shape[1], conv31_w.shape[1]

    def ffn_bwd(cot, sv, l, mt, r, f, z, a, h, s_in, w_in_name, w_out_name, ln_idx, i_gate, i_scale):
        dr, dfb, tot, per = _ln_bwd(cot, r, f, mt, i_gate, 0.5, row(full['ln_g'][l, ln_idx]), tt)
        dz = _mm_nt_swiglu_bwd(dfb, big[w_out_name][l], z)
        gbig[w_out_name][l] = _mm_tn("ffn_out_dw", a, dfb).reshape(N_DEV, -1, d)
        dh = _mm_nt("ffn_in_dx", dz, big[w_in_name][l], a_halves=True)
        gbig[w_in_name][l] = _mm_tn("ffn_in_dw", h, dz, b_halves=True, shard_cols=True)
        cot, per2 = _mod_bwd(dr, dh, s_in, mt, i_scale, alpha, tt)
        return cot, tot, per, per2

    for l in reversed(range(depth)):
        mt, sv = modtab[l], saved[l]
        cot, tot3, per3, mod3 = ffn_bwd(cot, sv, l, mt, sv['r3'], sv['f3'], sv['z3'], sv['a3'], sv['h3'], sv['s2'],
                                        'ff2_in', 'ff2_out', 2, 8, 7)
        dr, dyb, tot2, per2 = _ln_bwd(cot, sv['r2'], sv['y'], mt, 5, 1.0, row(full['ln_g'][l, 1]), tt)
        dymix = _mm_nt("mix_out_dx", dyb, big['w_out'][l])
        gbig['w_out'][l] = _mm_tn("mix_out_dw", sv['ymix'], dyb).reshape(N_DEV, -1, d)
        dh_rec, dp_rest, csum = _mix_mid_bwd(dymix, sv['p'], sv['h', 0], sv['h', 1], sv['uc'], full['conv31_w'][l],
                                             row(cln_g[l]), row(cln_b[l]), tt)
        dxr, gw_r, gw_i, gsum = None, [], [], []
        for dr_, rev in ((0, False), (1, True)):
            dxr, gwr, gwi, gs = _scan_bwd(
                sv['xr'], dh_rec, sv['h', dr_], sv['hin', dr_], wr_b[l, dr_], wi_b[l, dr_],
                row(full['b_rg'][l, dr_]), row(full['b_ig'][l, dr_]), row(full['lam'][l, dr_]), rev, tt, dxr)
            gw_r.append(gwr)
            gw_i.append(gwi)
            gsum.append(gs)
        dp_xr, c4sum = _conv4_bwd(dxr, sv['p'], full['conv4_w'][l], tt, ch)
        dp = jnp.concatenate([dp_xr, dp_rest], axis=1)
        dh = _mm_nt("mix_in_dx", dp, big['w_in'][l])
        gbig['w_in'][l] = _mm_tn("mix_in_dw", sv['h2'], dp, shard_cols=True)
        cot, mod2 = _mod_bwd(dr, dh, sv['s1'], mt, 4, alpha, tt)
        cot, tot1, per1, mod1 = ffn_bwd(cot, sv, l, mt, sv['r1'], sv['f1'], sv['z1'], sv['a1'], sv['h1'], sv['s0'],
                                        'ff1_in', 'ff1_out', 0, 2, 1)

        gsm['ln_g'][l] = jnp.stack([tot1[0], tot2[0], tot3[0]])
        gsm['ln_b'][l] = jnp.stack([tot1[1], tot2[1], tot3[1]])
        gsm['b_out'][l] = tot2[2]
        gsm['conv4_w'][l], gsm['conv4_b'][l] = c4sum[:taps4], c4sum[taps4]
        gsm['w_rg'][l], gsm['w_ig'][l] = jnp.stack(gw_r), jnp.stack(gw_i)
        gsm['b_rg'][l] = jnp.stack([g[0] for g in gsum])
        gsm['b_ig'][l] = jnp.stack([g[1] for g in gsum])
        gsm['lam'][l] = jnp.stack([g[2] for g in gsum])
        gsm['conv31_w'][l], gsm['conv31_b'][l] = csum[:taps31], csum[taps31]
        gsm['cln_g'][l], gsm['cln_b'][l] = csum[taps31 + 1], csum[taps31 + 2]
        gsm['dm'][l] = jnp.stack([jnp.stack([mod1[8 * st + 1], mod1[8 * st], per1[8 * st],
                                             mod2[8 * st + 1], mod2[8 * st], per2[8 * st],
                                             mod3[8 * st + 1], mod3[8 * st], per3[8 * st]]) for st in (0, 1)])
    grad_x = cot[t_ctx:][None]

    keys = list(gsm)
    partial = [jnp.stack(gsm[k]) for k in keys]
    shapes = [a.shape for a in partial]
    gathered = _all_gather(_pack(partial, F32))
    totals = dict(zip(keys, _unpack(_sum_slabs(gathered), shapes)))
    dm_all = _unpack(gathered, shapes, lead=(N_DEV,))[keys.index('dm')]
    dm_tot = totals['dm'].reshape(depth, 2, 9 * d)
    dm_sum = _sum_slabs(jnp.stack([_pack([dm_tot[:, 0]], F32), _pack([dm_tot[:, 1]], F32)]))
    grad = {'b_ada': _unpack(dm_sum, [(depth, 9 * d)])[0]}
    for k in ('ln_g', 'ln_b', 'conv4_w', 'b_rg', 'b_ig', 'lam', 'conv31_w'):
        grad[k] = _my_block(totals[k], me, 2)
    for k in ('w_rg', 'w_ig'):
        grad[k] = _my_block(totals[k], me, 3)
    for k in ('conv4_b', 'conv31_b', 'cln_g', 'cln_b', 'b_out'):
        grad[k] = totals[k]

    dm_rows = jnp.concatenate([jnp.moveaxis(dm_all[:, :, 1].reshape(N_DEV, depth, 9 * d), 0, 1),
                               jnp.pad(dm_tot[:, 0][:, None, :], ((0, 0), (0, MOD_ROWS - N_DEV - 1), (0, 0)))], axis=1)
    dm_cols = lax.dynamic_slice_in_dim(dm_rows, me * wc, wc, axis=2)
    grad_w_ada, gcc = _ada_bwd(cc, dm_cols, w_ada)
    gcc_all = _all_gather(_pack([gcc], F32))
    grad['c_ctx'] = _unpack(_sum_slabs(gcc_all), [(MOD_ROWS, d)])[0][N_DEV]

    res = {}
    gs, ws, ms, vs = (_pack([src[k] for k in _SMALL], F32)[None] for src in (grad, wts, mom, var))
    outs = _adam(gs[:, None], ws, ms, vs)
    small_shapes = [wts[k].shape for k in _SMALL]
    for k, vals in zip(_SMALL, zip(*(_unpack(o[0], small_shapes) for o in outs))):
        res[k] = vals
    res['w_ada'] = _adam(grad_w_ada[:, None], w_ada, m_w_ada, v_w_ada)
    for k in _BIG_SHARDED:
        res[k] = _adam(_all_to_all(gbig[k]), wts[k], mom[k], var[k])

    out = [loss, grad_x]
    for j in range(4):
        out += [res[k][j] for k in _WEIGHTS]
    return tuple(out)
```

```python
import math

import jax
import jax.numpy as jnp
from jax import lax
from jax.experimental import pallas as pl
from jax.experimental.pallas import tpu as pltpu

F32 = jnp.float32
BF16 = jnp.bfloat16
N_DEV = 8
MESH = pl.DeviceIdType.MESH
LN_EPS = 1e-6
RG_C = 8.0
SEG = 64
CONV_PAD = 16
ADAM_LR, ADAM_B1, ADAM_B2, ADAM_EPS, ADAM_WD, ADAM_STEP = 0.001, 0.9, 0.999, 1e-08, 0.01, 10
LANE = 128
PACK_QUANTUM = 16 * LANE
PACK_ROWS = 512
MOD_ROWS = 16
GELU_K = 0.7978845608028654
GELU_C = 0.044715

_TM = (768, 640, 512, 384, 256, 128)
_TN = (1024, 512, 256, 128)
_TK = (2816, 1024, 512, 256, 128)


def _cp(sem=None, vmem_mb=40):
    return pltpu.CompilerParams(dimension_semantics=sem, vmem_limit_bytes=vmem_mb * 2 ** 20)


def _pick(n, cands):
    for cand in cands:
        if n % cand == 0:
            return cand
    return n


def _sig(v):
    return 1.0 / (1.0 + jnp.exp(-v))


def _dot(a, b, ca, cb):
    return lax.dot_general(a, b, (((ca,), (cb,)), ((), ())), preferred_element_type=F32)


def _sum0(v):
    return jnp.sum(v, axis=0, keepdims=True)


def _gelu_and_grad(v):
    inner = GELU_K * (v + GELU_C * v * v * v)
    t = jnp.tanh(inner)
    gel = 0.5 * v * (1.0 + t)
    dgel = 0.5 * (1.0 + t) + 0.5 * v * (1.0 - t * t) * GELU_K * (1.0 + 3.0 * GELU_C * v * v)
    return gel, dgel


def _softplus_neg(lam):
    y = jnp.exp(-jnp.abs(lam))
    u = 1.0 + y
    log1p = jnp.where(u == 1.0, y, jnp.log(u) * (y / (u - 1.0)))
    return jnp.maximum(-lam, 0.0) + log1p


def _neg_expm1(z):
    ser = -z * (1.0 + z * (1 / 2 + z * (1 / 6 + z * (1 / 24 + z * (1 / 120 + z * (1 / 720 + z * (1 / 5040)))))))
    return jnp.where(z > -0.25, ser, 1.0 - jnp.exp(z))


def _row_spec(tt, w, col=0):
    return pl.BlockSpec((tt, w), lambda i: (i, col))


def _full_spec(shape):
    nd = len(shape)
    return pl.BlockSpec(shape, lambda *_: (0,) * nd)


def _mod_spec(d):
    return pl.BlockSpec((None, MOD_ROWS, d), lambda i: (jnp.minimum(i, 1), 0, 0))


def _stream_sum_spec(w):
    return pl.BlockSpec((8, w), lambda i: (jnp.minimum(i, 1), 0))


def _acc_rows(ref, init, rows):
    @pl.when(init)
    def _():
        ref[...] = jnp.zeros(ref.shape, F32)

    for k, row in enumerate(rows):
        ref[pl.ds(k, 1), :] += row


def _all_gather(v):
    def body(x_ref, out_ref, send_sems, recv_sems, local_sem):
        x, y, c = lax.axis_index("x"), lax.axis_index("y"), lax.axis_index("c")
        me, sibling = (x, y, c), (x, y, 1 - c)
        chips = [(1 - x, y), (x, 1 - y), (1 - x, 1 - y)]

        def slot(px, py, pc):
            return out_ref.at[4 * px + 2 * py + pc]

        def copy(k, block, to, src=None):
            return pltpu.make_async_remote_copy(
                src_ref=slot(*block) if src is None else src, dst_ref=slot(*block),
                send_sem=send_sems.at[k], recv_sem=recv_sems.at[k], device_id=to, device_id_type=MESH)

        mine = pltpu.make_async_copy(x_ref, slot(*me), local_sem)
        mine.start()
        first = [copy(0, me, sibling, src=x_ref)]
        first += [copy(1 + j, me, (*chip, c), src=x_ref) for j, chip in enumerate(chips)]
        for cp in first:
            cp.start()
        passed = [copy(4 + j, (*chip, c), sibling) for j, chip in enumerate(chips)]
        for j, chip in enumerate(chips):
            copy(1 + j, (*chip, c), me).wait_recv()
            passed[j].start()
        copy(0, sibling, me).wait_recv()
        for j, chip in enumerate(chips):
            copy(4 + j, (*chip, 1 - c), me).wait_recv()
        for cp in first + passed:
            cp.wait_send()
        mine.wait()

    return pl.pallas_call(
        body, name="all_gather",
        out_shape=jax.ShapeDtypeStruct((N_DEV,) + v.shape, v.dtype),
        in_specs=[pl.BlockSpec(memory_space=pltpu.HBM)],
        out_specs=pl.BlockSpec(memory_space=pltpu.HBM),
        scratch_shapes=[pltpu.SemaphoreType.DMA((7,)), pltpu.SemaphoreType.DMA((7,)), pltpu.SemaphoreType.DMA],
    )(v)


_PEER_FLIPS = [(0, 0, 1), (1, 0, 0), (0, 1, 0), (1, 1, 0), (1, 0, 1), (0, 1, 1), (1, 1, 1)]


def _all_to_all(parts):
    n_parts = len(parts)

    def body(*refs):
        x_refs, out_ref = refs[:n_parts], refs[n_parts]
        send_sems, recv_sems, local_sems = refs[n_parts + 1:]
        x, y, c = lax.axis_index("x"), lax.axis_index("y"), lax.axis_index("c")
        me = 4 * x + 2 * y + c
        local, remote = [], []
        for l in range(n_parts):
            cp = pltpu.make_async_copy(x_refs[l].at[me], out_ref.at[l, me], local_sems.at[l])
            cp.start()
            local.append(cp)
            for k, (fx, fy, fc) in enumerate(_PEER_FLIPS):
                px = 1 - x if fx else x
                py = 1 - y if fy else y
                pc = 1 - c if fc else c
                cp = pltpu.make_async_remote_copy(
                    src_ref=x_refs[l].at[4 * px + 2 * py + pc], dst_ref=out_ref.at[l, me],
                    send_sem=send_sems.at[7 * l + k], recv_sem=recv_sems.at[7 * l + k],
                    device_id=(px, py, pc), device_id_type=MESH)
                cp.start()
                remote.append(cp)
        for cp in remote:
            cp.wait()
        for cp in local:
            cp.wait()

    shape = parts[0].shape
    return pl.pallas_call(
        body, name="all_to_all",
        out_shape=jax.ShapeDtypeStruct((n_parts,) + shape, parts[0].dtype),
        in_specs=[pl.BlockSpec(memory_space=pltpu.HBM)] * n_parts,
        out_specs=pl.BlockSpec(memory_space=pltpu.HBM),
        scratch_shapes=[pltpu.SemaphoreType.DMA((7 * n_parts,)), pltpu.SemaphoreType.DMA((7 * n_parts,)),
                        pltpu.SemaphoreType.DMA((n_parts,))],
    )(*parts)


def _pack(arrs, dtype):
    pieces = []
    for a in arrs:
        flat = a.astype(dtype).reshape(-1)
        flat = jnp.pad(flat, (0, (-flat.shape[0]) % PACK_QUANTUM))
        pieces.append(flat.reshape(-1, LANE))
    packed = jnp.concatenate(pieces, axis=0) if len(pieces) > 1 else pieces[0]
    return jnp.pad(packed, ((0, (-packed.shape[0]) % PACK_ROWS), (0, 0)))


def _unpack(packed, shapes, lead=()):
    flat = packed.reshape(lead + (-1,))
    outs, off = [], 0
    for sh in shapes:
        n = math.prod(sh)
        outs.append(flat[..., off:off + n].reshape(lead + tuple(sh)))
        off += n + (-n) % PACK_QUANTUM
    return outs


def _gather_list(arrs, dtype):
    gathered = _all_gather(_pack(arrs, dtype))
    return _unpack(gathered, [a.shape for a in arrs], lead=(N_DEV,))


def _unshard(g, axis):
    y = jnp.moveaxis(g, 0, axis)
    sh = y.shape
    return y.reshape(sh[:axis] + (sh[axis] * sh[axis + 1],) + sh[axis + 2:])


def _my_block(full, me, axis):
    size = full.shape[axis] // N_DEV
    return lax.dynamic_slice_in_dim(full, me * size, size, axis=axis)


def _sum_slabs(v):
    n_slab, rows, width = v.shape
    tr = _pick(rows, (1024, 512, 256, 128, 64, 32, 16, 8))

    def body(v_ref, o_ref):
        acc = v_ref[0]
        for k in range(1, n_slab):
            acc = acc + v_ref[k]
        o_ref[...] = acc

    return pl.pallas_call(
        body, name="sum_slabs", grid=(rows // tr,),
        in_specs=[pl.BlockSpec((n_slab, tr, width), lambda i: (0, i, 0))],
        out_specs=pl.BlockSpec((tr, width), lambda i: (i, 0)),
        out_shape=jax.ShapeDtypeStruct((rows, width), F32),
        compiler_params=_cp(("parallel",)),
    )(v)


def _comm_copies(kinds, src_refs, dst_refs, send_sems, recv_sems, local_sems):
    x, y, c = lax.axis_index("x"), lax.axis_index("y"), lax.axis_index("c")
    me = 4 * x + 2 * y + c
    copies = []
    for j, kind in enumerate(kinds):
        src, dst = src_refs[j], dst_refs[j]
        copies.append(pltpu.make_async_copy(src if kind == "gather" else src.at[me], dst.at[me], local_sems.at[j]))
        for k, (fx, fy, fc) in enumerate(_PEER_FLIPS):
            px = 1 - x if fx else x
            py = 1 - y if fy else y
            pc = 1 - c if fc else c
            copies.append(pltpu.make_async_remote_copy(
                src_ref=src if kind == "gather" else src.at[4 * px + 2 * py + pc], dst_ref=dst.at[me],
                send_sem=send_sems.at[7 * j + k], recv_sem=recv_sems.at[7 * j + k],
                device_id=(px, py, pc), device_id_type=MESH))
    return copies


def _mm(name, grid, nk, ins, in_specs, out_shape, out_specs, acc_shape, prod, fin, comm=()):
    n_in, n_out, n_c = len(ins), len(out_shape), len(comm)
    kinds = [kind for kind, _ in comm]
    srcs = [a for _, a in comm]
    dsts = [jax.ShapeDtypeStruct(((N_DEV,) + a.shape) if kind == "gather" else a.shape, a.dtype) for kind, a in comm]
    hbm = pl.BlockSpec(memory_space=pltpu.HBM)

    def body(*refs):
        in_refs, refs = refs[:n_in], refs[n_in:]
        src_refs, refs = refs[:n_c], refs[n_c:]
        out_refs, refs = refs[:n_out], refs[n_out:]
        dst_refs, scratch = refs[:n_c], refs[n_c:]
        if nk > 1:
            acc, scratch = scratch[0], scratch[1:]
        ids = [pl.program_id(a) for a in range(3)]
        if n_c:
            copies = _comm_copies(kinds, src_refs, dst_refs, *scratch)

            @pl.when(jnp.logical_and(ids[0] == 0, jnp.logical_and(ids[1] == 0, ids[2] == 0)))
            def _():
                for cp in copies:
                    cp.start()

        if nk == 1:
            fin(prod(*in_refs), in_refs, out_refs)
        else:
            @pl.when(ids[2] == 0)
            def _():
                acc[...] = jnp.zeros(acc.shape, F32)

            for idx, val in enumerate(prod(*in_refs)):
                acc[idx] += val

            @pl.when(ids[2] == nk - 1)
            def _():
                fin(tuple(acc[idx] for idx in range(acc_shape[0])), in_refs, out_refs)

        if n_c:
            @pl.when(jnp.logical_and(ids[0] == grid[0] - 1, jnp.logical_and(ids[1] == grid[1] - 1, ids[2] == nk - 1)))
            def _():
                for cp in copies:
                    cp.wait()

    scratch = [pltpu.VMEM(acc_shape, F32)] if nk > 1 else []
    if n_c:
        scratch += [pltpu.SemaphoreType.DMA((7 * n_c,)), pltpu.SemaphoreType.DMA((7 * n_c,)),
                    pltpu.SemaphoreType.DMA((n_c,))]
    sem = ("arbitrary",) * 3 if n_c else ("parallel", "parallel", "arbitrary")
    res = pl.pallas_call(
        body, name=name + (f"_x{n_c}" if n_c else ""), grid=grid,
        in_specs=list(in_specs) + [hbm] * n_c, out_specs=tuple(out_specs) + (hbm,) * n_c,
        out_shape=tuple(out_shape) + tuple(dsts), scratch_shapes=scratch,
        compiler_params=_cp(sem, 48),
    )(*ins, *srcs)
    return res[:n_out], list(res[n_out:])


def _mm_nn(name, a, b, bias=None, comm=()):
    m, kk = a.shape
    nn = b.shape[1]
    tm, tn = _pick(m, _TM), _pick(nn, _TN)
    tk = kk if kk <= 2048 else _pick(kk, _TK)
    nk = kk // tk
    ins = [a, b]
    specs = [pl.BlockSpec((tm, tk), lambda i, j, k: (i, k)), pl.BlockSpec((tk, tn), lambda i, j, k: (k, j))]
    if bias is not None:
        ins.append(bias)
        specs.append(pl.BlockSpec((1, tn), lambda i, j, k: (0, j)))

    def prod(a_ref, b_ref, *_):
        return (_dot(a_ref[...], b_ref[...], 1, 0),)

    def fin(vals, in_refs, out_refs):
        out_refs[0][...] = vals[0] if bias is None else vals[0] + in_refs[2][...]

    outs, got = _mm(name, (m // tm, nn // tn, nk), nk, ins, specs,
                    (jax.ShapeDtypeStruct((m, nn), F32),), (pl.BlockSpec((tm, tn), lambda i, j, k: (i, j)),),
                    (1, tm, tn), prod, fin, comm)
    return outs[0], got


def _mm_swiglu(h, w1, comm=()):
    m, kk = h.shape
    f = w1.shape[1] // 2
    tm, tn = _pick(m, _TM), _pick(f, (512, 256, 128))
    nj = f // tn

    def prod(h_ref, wg_ref, wu_ref):
        return _dot(h_ref[...], wg_ref[...], 1, 0), _dot(h_ref[...], wu_ref[...], 1, 0)

    def fin(vals, in_refs, out_refs):
        zg, zu = vals
        out_refs[0][0] = zg.astype(BF16)
        out_refs[0][1] = zu.astype(BF16)
        out_refs[1][...] = (zg * _sig(zg) * zu).astype(BF16)

    return _mm("ffn_in", (m // tm, nj, 1), 1, [h, w1, w1],
               [pl.BlockSpec((tm, kk), lambda i, j, k: (i, 0)),
                pl.BlockSpec((kk, tn), lambda i, j, k: (0, j)),
                pl.BlockSpec((kk, tn), lambda i, j, k: (0, j + nj))],
               (jax.ShapeDtypeStruct((2, m, f), BF16), jax.ShapeDtypeStruct((m, f), BF16)),
               (pl.BlockSpec((2, tm, tn), lambda i, j, k: (0, i, j)), pl.BlockSpec((tm, tn), lambda i, j, k: (i, j))),
               (2, tm, tn), prod, fin, comm)


def _mm_nt_swiglu_bwd(dfb, w2, z, comm=()):
    m, kk = dfb.shape
    f = w2.shape[0]
    tm, tn = _pick(m, _TM), _pick(f, (512, 256, 128))

    def prod(a_ref, b_ref, z_ref):
        return (_dot(a_ref[...], b_ref[...], 1, 1),)

    def fin(vals, in_refs, out_refs):
        da = vals[0]
        zg, zu = in_refs[2][0].astype(F32), in_refs[2][1].astype(F32)
        sg = _sig(zg)
        out_refs[0][0] = (da * zu * (sg * (1.0 + zg * (1.0 - sg)))).astype(BF16)
        out_refs[0][1] = (da * (zg * sg)).astype(BF16)

    outs, got = _mm("ffn_out_dx", (m // tm, f // tn, 1), 1, [dfb, w2, z],
                    [pl.BlockSpec((tm, kk), lambda i, j, k: (i, 0)),
                     pl.BlockSpec((tn, kk), lambda i, j, k: (j, 0)),
                     pl.BlockSpec((2, tm, tn), lambda i, j, k: (0, i, j))],
                    (jax.ShapeDtypeStruct((2, m, f), BF16),),
                    (pl.BlockSpec((2, tm, tn), lambda i, j, k: (0, i, j)),),
                    (1, tm, tn), prod, fin, comm)
    return outs[0], got


def _mm_nt(name, a, b, a_halves=False, comm=()):
    nn, kk = b.shape
    m = a.shape[1] if a_halves else a.shape[0]
    kh = kk // 2 if a_halves else kk
    tm, tn = _pick(m, _TM), _pick(nn, _TN)
    tk = kh if (kh <= 2048 and not a_halves) else _pick(kh, _TK)
    nk, nkh = kk // tk, kh // tk
    if a_halves:
        a_spec = pl.BlockSpec((None, tm, tk), lambda i, j, k: (k // nkh, i, k % nkh))
    else:
        a_spec = pl.BlockSpec((tm, tk), lambda i, j, k: (i, k))

    def prod(a_ref, b_ref):
        return (_dot(a_ref[...], b_ref[...], 1, 1),)

    def fin(vals, in_refs, out_refs):
        out_refs[0][...] = vals[0]

    outs, got = _mm(name, (m // tm, nn // tn, nk), nk, [a, b],
                    [a_spec, pl.BlockSpec((tn, tk), lambda i, j, k: (j, k))],
                    (jax.ShapeDtypeStruct((m, nn), F32),), (pl.BlockSpec((tm, tn), lambda i, j, k: (i, j)),),
                    (1, tm, tn), prod, fin, comm)
    return outs[0], got


def _mm_tn(name, a, b, b_halves=False, shard_cols=False, comm=()):
    kt, m = a.shape
    nn = 2 * b.shape[2] if b_halves else b.shape[1]
    tk, tm = _pick(kt, _TM), _pick(m, (1408, 1024, 512, 256, 128))
    shard = nn // N_DEV
    per = 2 if (shard_cols and shard % 256 and not (2 * shard) % 256) else 1
    tn = per * shard if shard_cols else _pick(nn, _TN)
    njh = (nn // 2) // tn if b_halves else 0
    if b_halves:
        b_spec = pl.BlockSpec((None, tk, tn), lambda i, j, k: (j // njh, k, j % njh))
    else:
        b_spec = pl.BlockSpec((tk, tn), lambda i, j, k: (k, j))
    if shard_cols:
        out_shape = jax.ShapeDtypeStruct((N_DEV, m, shard), BF16)
        out_spec = pl.BlockSpec((per, tm, shard), lambda i, j, k: (j, i, 0))
    else:
        out_shape = jax.ShapeDtypeStruct((m, nn), BF16)
        out_spec = pl.BlockSpec((tm, tn), lambda i, j, k: (i, j))

    def prod(a_ref, b_ref):
        return (_dot(a_ref[...], b_ref[...], 0, 0),)

    def fin(vals, in_refs, out_refs):
        r = vals[0].astype(BF16)
        if shard_cols:
            for q in range(per):
                out_refs[0][q] = r[:, q * shard:(q + 1) * shard]
        else:
            out_refs[0][...] = r

    outs, got = _mm(name, (m // tm, nn // tn, kt // tk), kt // tk, [a, b],
                    [pl.BlockSpec((tk, tm), lambda i, j, k: (k, i)), b_spec],
                    (out_shape,), (out_spec,), (1, tm, tn), prod, fin, comm)
    return outs[0], got


def _modulate(s, modtab, i_shift, i_scale, tt):
    n, d = s.shape

    def body(s_ref, m_ref, o_ref):
        shift, scale = m_ref[pl.ds(i_shift, 1), :], m_ref[pl.ds(i_scale, 1), :]
        o_ref[...] = (s_ref[...] * (1.0 + scale) + shift).astype(BF16)

    return pl.pallas_call(
        body, name="modulate", grid=(n // tt,),
        in_specs=[_row_spec(tt, d), _mod_spec(d)], out_specs=_row_spec(tt, d),
        out_shape=jax.ShapeDtypeStruct((n, d), BF16), compiler_params=_cp(("parallel",)),
    )(s, modtab)


def _res_ln(s, f, modtab, i_gate, coef, gam, bet, alpha, tt):
    n, d = s.shape

    def body(s_ref, f_ref, m_ref, g_ref, b_ref, r_ref, o_ref):
        gate = m_ref[pl.ds(i_gate, 1), :]
        r = alpha * s_ref[...] + (coef * gate) * f_ref[...]
        r_ref[...] = r
        mu = jnp.mean(r, axis=-1, keepdims=True)
        xc = r - mu
        var = jnp.mean(xc * xc, axis=-1, keepdims=True)
        o_ref[...] = xc * lax.rsqrt(var + LN_EPS) * g_ref[...] + b_ref[...]

    return pl.pallas_call(
        body, name="res_ln", grid=(n // tt,),
        in_specs=[_row_spec(tt, d), _row_spec(tt, d), _mod_spec(d), _full_spec((1, d)), _full_spec((1, d))],
        out_specs=(_row_spec(tt, d), _row_spec(tt, d)),
        out_shape=(jax.ShapeDtypeStruct((n, d), F32), jax.ShapeDtypeStruct((n, d), F32)),
        compiler_params=_cp(("parallel",)),
    )(s, f, modtab, gam, bet)


def _ln_bwd(do, r, f, modtab, i_gate, coef, gam, tt):
    n, d = do.shape

    def body(do_ref, r_ref, f_ref, m_ref, g_ref, dr_ref, dfb_ref, tot_ref, str_ref):
        i = pl.program_id(0)
        r = r_ref[...]
        mu = jnp.mean(r, axis=-1, keepdims=True)
        xc = r - mu
        rstd = lax.rsqrt(jnp.mean(xc * xc, axis=-1, keepdims=True) + LN_EPS)
        xh = xc * rstd
        dout = do_ref[...]
        dxh = dout * g_ref[...]
        m1 = jnp.mean(dxh, axis=-1, keepdims=True)
        m2 = jnp.mean(dxh * xh, axis=-1, keepdims=True)
        dr = rstd * (dxh - m1 - xh * m2)
        dr_ref[...] = dr
        dfb = (coef * m_ref[pl.ds(i_gate, 1), :]) * dr
        dfb_ref[...] = dfb.astype(BF16)
        _acc_rows(tot_ref, i == 0, [_sum0(dout * xh), _sum0(dout), _sum0(dfb)])
        _acc_rows(str_ref, i <= 1, [_sum0(coef * f_ref[...] * dr)])

    return pl.pallas_call(
        body, name="ln_bwd", grid=(n // tt,),
        in_specs=[_row_spec(tt, d), _row_spec(tt, d), _row_spec(tt, d), _mod_spec(d), _full_spec((1, d))],
        out_specs=(_row_spec(tt, d), _row_spec(tt, d), _full_spec((8, d)), _stream_sum_spec(d)),
        out_shape=(jax.ShapeDtypeStruct((n, d), F32), jax.ShapeDtypeStruct((n, d), BF16),
                   jax.ShapeDtypeStruct((8, d), F32), jax.ShapeDtypeStruct((16, d), F32)),
        compiler_params=_cp(("arbitrary",)),
    )(do, r, f, modtab, gam)


def _mod_bwd(dr, dh, s, modtab, i_scale, alpha, tt):
    n, d = dr.shape

    def body(dr_ref, dh_ref, s_ref, m_ref, o_ref, str_ref):
        i = pl.program_id(0)
        dh_v = dh_ref[...]
        o_ref[...] = alpha * dr_ref[...] + dh_v * (1.0 + m_ref[pl.ds(i_scale, 1), :])
        _acc_rows(str_ref, i <= 1, [_sum0(dh_v * s_ref[...]), _sum0(dh_v)])

    return pl.pallas_call(
        body, name="mod_bwd", grid=(n // tt,),
        in_specs=[_row_spec(tt, d), _row_spec(tt, d), _row_spec(tt, d), _mod_spec(d)],
        out_specs=(_row_spec(tt, d), _stream_sum_spec(d)),
        out_shape=(jax.ShapeDtypeStruct((n, d), F32), jax.ShapeDtypeStruct((16, d), F32)),
        compiler_params=_cp(("arbitrary",)),
    )(dr, dh, s, modtab)


def _loss_head(s, target, tt):
    n, d = s.shape
    nt = n // tt

    def body(s_ref, t_ref, do_ref, loss_ref, acc):
        i = pl.program_id(0)

        @pl.when(i == 0)
        def _():
            acc[...] = jnp.zeros(acc.shape, F32)
            do_ref[...] = jnp.zeros(do_ref.shape, F32)

        @pl.when(i > 0)
        def _():
            err = s_ref[...] - t_ref[...]
            do_ref[...] = err / d
            acc[...] += jnp.sum((err * err).reshape(tt // 8, 8, d), axis=0)

        @pl.when(i == nt - 1)
        def _():
            loss_ref[...] = jnp.full(loss_ref.shape, jnp.sum(acc[...]) * (0.5 / d), F32)

    return pl.pallas_call(
        body, name="loss_head", grid=(nt,),
        in_specs=[_row_spec(tt, d), pl.BlockSpec((tt, d), lambda i: (jnp.maximum(i - 1, 0), 0))],
        out_specs=(_row_spec(tt, d), _full_spec((8, LANE))),
        out_shape=(jax.ShapeDtypeStruct((n, d), F32), jax.ShapeDtypeStruct((8, LANE), F32)),
        scratch_shapes=[pltpu.VMEM((8, d), F32)],
        compiler_params=_cp(("arbitrary",)),
    )(s, target)


def _halo_specs(tt, c, n):
    nb, last = tt // 8, n // 8 - 1
    return [pl.BlockSpec((tt, c), lambda i: (i, 0)),
            pl.BlockSpec((8, c), lambda i: (jnp.maximum(i * nb - 1, 0), 0)),
            pl.BlockSpec((8, c), lambda i: (jnp.minimum((i + 1) * nb, last), 0))]


def _fill_halo(scr, main_ref, prev_ref, next_ref, i, nt, tt):
    has_prev = i >= 2
    has_next = jnp.logical_and(i >= 1, i < nt - 1)
    scr[pl.ds(0, 8), :] = jnp.where(has_prev, prev_ref[...], 0.0)
    scr[pl.ds(8, tt), :] = main_ref[...]
    scr[pl.ds(8 + tt, 8), :] = jnp.where(has_next, next_ref[...], 0.0)


def _conv4(p, w, b, tt, c):
    n = p.shape[0]
    nt, taps = n // tt, w.shape[0]
    left = taps // 2

    def body(x_ref, xp_ref, xn_ref, w_ref, b_ref, o_ref, scr):
        _fill_halo(scr, x_ref, xp_ref, xn_ref, pl.program_id(0), nt, tt)
        acc = jnp.broadcast_to(b_ref[...], (tt, c))
        for k in range(taps):
            acc = acc + w_ref[pl.ds(k, 1), :] * scr[pl.ds(8 + k - left, tt), :]
        o_ref[...] = acc

    return pl.pallas_call(
        body, name="conv4", grid=(nt,),
        in_specs=_halo_specs(tt, c, n) + [_full_spec(w.shape), _full_spec((1, c))],
        out_specs=_row_spec(tt, c), out_shape=jax.ShapeDtypeStruct((n, c), F32),
        scratch_shapes=[pltpu.VMEM((tt + 16, c), F32)], compiler_params=_cp(("parallel",)),
    )(p, p, p, w, b)


def _conv4_bwd(dxr, p, w, tt, c):
    n = p.shape[0]
    nt, taps = n // tt, w.shape[0]
    left = taps // 2

    def body(d_ref, dp_ref, dn_ref, x_ref, xp_ref, xn_ref, w_ref, o_ref, sum_ref, dscr, xscr):
        i = pl.program_id(0)
        _fill_halo(dscr, d_ref, dp_ref, dn_ref, i, nt, tt)
        _fill_halo(xscr, x_ref, xp_ref, xn_ref, i, nt, tt)
        acc = jnp.zeros((tt, c), F32)
        for k in range(taps):
            acc = acc + w_ref[pl.ds(k, 1), :] * dscr[pl.ds(8 - (k - left), tt), :]
        o_ref[...] = acc.astype(BF16)
        d = d_ref[...]
        rows = [_sum0(d * xscr[pl.ds(8 + k - left, tt), :]) for k in range(taps)] + [_sum0(d)]
        _acc_rows(sum_ref, i == 0, rows)

    return pl.pallas_call(
        body, name="conv4_bwd", grid=(nt,),
        in_specs=_halo_specs(tt, c, n) + _halo_specs(tt, c, n) + [_full_spec(w.shape)],
        out_specs=(_row_spec(tt, c), _full_spec((8, c))),
        out_shape=(jax.ShapeDtypeStruct((n, c), BF16), jax.ShapeDtypeStruct((8, c), F32)),
        scratch_shapes=[pltpu.VMEM((tt + 16, c), F32), pltpu.VMEM((tt + 16, c), F32)],
        compiler_params=_cp(("arbitrary",)),
    )(dxr, dxr, dxr, p, p, p, w)


def _head_gates(xv, wr, wi, br, bi, lam):
    xb = xv.astype(BF16)
    r = _sig(_dot(xb, wr, 1, 0) + br)
    ig = _sig(_dot(xb, wi, 1, 0) + bi)
    sp = _softplus_neg(lam)
    log_a = (-RG_C) * r * sp
    return r, ig, sp, jnp.exp(log_a), jnp.sqrt(_neg_expm1(2.0 * log_a))


def _tile_scan(a, b, increasing, tt, rows):
    s = 1
    while s < tt:
        if increasing:
            a_sh, b_sh, ok = pltpu.roll(a, s, 0), pltpu.roll(b, s, 0), rows >= s
        else:
            a_sh, b_sh, ok = pltpu.roll(a, tt - s, 0), pltpu.roll(b, tt - s, 0), rows < tt - s
        b = jnp.where(ok, a * b_sh + b, b)
        a = jnp.where(ok, a * a_sh, a)
        s *= 2
    return a, b


def _scan_tile_index(step, nt, reverse):
    return jnp.where(step == 0, 0, nt - step) if reverse else step


def _scan(xr, wr, wi, br, bi, lam, reverse, tt):
    n, c = xr.shape
    nt = n // tt
    heads, hs = wr.shape[0], wr.shape[2]
    lc = min(c, LANE)

    def tile(i):
        return _scan_tile_index(i, nt, reverse)

    def body(x_ref, wr_ref, wi_ref, br_ref, bi_ref, lam_ref, h_ref, hin_ref, a_scr, b_scr, carry):
        i = pl.program_id(0)

        @pl.when(i == 0)
        def _():
            carry[...] = jnp.zeros(carry.shape, F32)

        for hd in range(heads):
            sl = slice(hd * hs, (hd + 1) * hs)
            xv = x_ref[:, sl]
            _, ig, _, a, sq = _head_gates(xv, wr_ref[hd], wi_ref[hd], br_ref[:, sl], bi_ref[:, sl], lam_ref[:, sl])
            a_scr[:, sl] = a
            b_scr[:, sl] = sq * (ig * xv)
        hin_ref[...] = jnp.broadcast_to(carry[...], (8, c))
        rows = lax.broadcasted_iota(jnp.int32, (tt, lc), 0)
        for ch in range(c // lc):
            sl = slice(ch * lc, (ch + 1) * lc)
            big_a, big_b = _tile_scan(a_scr[:, sl], b_scr[:, sl], not reverse, tt, rows)
            h_ref[:, sl] = big_a * carry[:, sl] + big_b
        carry[...] = h_ref[pl.ds(0 if reverse else tt - 1, 1), :]

    return pl.pallas_call(
        body, name="lru_scan", grid=(nt,),
        in_specs=[pl.BlockSpec((tt, c), lambda i: (tile(i), 0)), _full_spec(wr.shape), _full_spec(wi.shape),
                  _full_spec((1, c)), _full_spec((1, c)), _full_spec((1, c))],
        out_specs=(pl.BlockSpec((tt, c), lambda i: (tile(i), 0)), pl.BlockSpec((None, 8, c), lambda i: (tile(i), 0, 0))),
        out_shape=(jax.ShapeDtypeStruct((n, c), F32), jax.ShapeDtypeStruct((nt, 8, c), F32)),
        scratch_shapes=[pltpu.VMEM((tt, c), F32), pltpu.VMEM((tt, c), F32), pltpu.VMEM((1, c), F32)],
        compiler_params=_cp(("arbitrary",)),
    )(xr, wr, wi, br, bi, lam)


def _scan_bwd(xr, dh, h, hin, wr, wi, br, bi, lam, reverse, tt, dx_prev=None):
    n, c = xr.shape
    nt = n // tt
    heads, hs = wr.shape[0], wr.shape[2]
    lc = min(c, LANE)
    inc = not reverse
    first, last = (0, tt - 1) if inc else (tt - 1, 0)

    def tile(i):
        return _scan_tile_index(nt - 1 - i, nt, reverse)

    def body(*refs):
        x_ref, dh_ref, h_ref, hin_ref, wr_ref, wi_ref, br_ref, bi_ref, lam_ref = refs[:9]
        refs = refs[9:]
        if dx_prev is not None:
            dxp_ref, refs = refs[0], refs[1:]
        dx_ref, dwr_ref, dwi_ref, sum_ref, a_scr, r_scr, i_scr, sq_scr, g_scr, da_scr, u_scr, ucarry = refs
        i = pl.program_id(0)

        @pl.when(i == 0)
        def _():
            ucarry[...] = jnp.zeros(ucarry.shape, F32)
            dwr_ref[...] = jnp.zeros(dwr_ref.shape, F32)
            dwi_ref[...] = jnp.zeros(dwi_ref.shape, F32)
            sum_ref[...] = jnp.zeros(sum_ref.shape, F32)

        for hd in range(heads):
            sl = slice(hd * hs, (hd + 1) * hs)
            r, ig, _, a, sq = _head_gates(x_ref[:, sl], wr_ref[hd], wi_ref[hd], br_ref[:, sl], bi_ref[:, sl],
                                          lam_ref[:, sl])
            a_scr[:, sl], r_scr[:, sl], i_scr[:, sl], sq_scr[:, sl] = a, r, ig, sq

        rows = lax.broadcasted_iota(jnp.int32, (tt, lc), 0)
        to_prev = 1 if inc else tt - 1
        to_next = tt - 1 if inc else 1
        for ch in range(c // lc):
            sl = slice(ch * lc, (ch + 1) * lc)
            a, dhv = a_scr[:, sl], dh_ref[:, sl]
            big_a, big_b = _tile_scan(a, a * dhv, not inc, tt, rows)
            u_in = ucarry[:, sl]
            u = big_a * u_in + big_b
            u_scr[:, sl] = u
            g = dhv + jnp.where(rows == last, u_in, pltpu.roll(u, to_next, 0))
            g_scr[:, sl] = g
            h_prev = jnp.where(rows == first, hin_ref[pl.ds(0, 1), sl], pltpu.roll(h_ref[:, sl], to_prev, 0))
            da_scr[:, sl] = g * h_prev
        ucarry[...] = u_scr[pl.ds(first, 1), :]

        for hd in range(heads):
            sl = slice(hd * hs, (hd + 1) * hs)
            xv, a, r, ig, sq = x_ref[:, sl], a_scr[:, sl], r_scr[:, sl], i_scr[:, sl], sq_scr[:, sl]
            g, lam_v = g_scr[:, sl], lam_ref[:, sl]
            sp = _softplus_neg(lam_v)
            d_sq = g * ig * xv
            d_em = d_sq * 0.5 / sq
            d_log_a = (da_scr[:, sl] - 2.0 * a * d_em) * a
            dzr = (d_log_a * ((-RG_C) * sp)) * r * (1.0 - r)
            dzi = (g * sq * xv) * ig * (1.0 - ig)
            dzr_b, dzi_b, xb = dzr.astype(BF16), dzi.astype(BF16), xv.astype(BF16)
            dx = g * sq * ig + _dot(dzr_b, wr_ref[hd], 1, 1) + _dot(dzi_b, wi_ref[hd], 1, 1)
            if dx_prev is not None:
                dx = dx + dxp_ref[:, sl]
            dx_ref[:, sl] = dx
            dwr_ref[hd] += _dot(xb, dzr_b, 0, 0)
            dwi_ref[hd] += _dot(xb, dzi_b, 0, 0)
            sum_ref[pl.ds(0, 1), sl] += _sum0(dzr)
            sum_ref[pl.ds(1, 1), sl] += _sum0(dzi)
            sum_ref[pl.ds(2, 1), sl] += _sum0(d_log_a * ((-RG_C) * r)) * (-_sig(-lam_v))

    tile_spec = pl.BlockSpec((tt, c), lambda i: (tile(i), 0))
    ins = [xr, dh, h, hin, wr, wi, br, bi, lam]
    in_specs = [tile_spec, tile_spec, tile_spec, pl.BlockSpec((None, 8, c), lambda i: (tile(i), 0, 0)),
                _full_spec(wr.shape), _full_spec(wi.shape), _full_spec((1, c)), _full_spec((1, c)), _full_spec((1, c))]
    if dx_prev is not None:
        ins.append(dx_prev)
        in_specs.append(tile_spec)
    return pl.pallas_call(
        body, name="lru_scan_bwd", grid=(nt,), in_specs=in_specs,
        out_specs=(tile_spec, _full_spec(wr.shape), _full_spec(wi.shape), _full_spec((8, c))),
        out_shape=(jax.ShapeDtypeStruct((n, c), F32), jax.ShapeDtypeStruct(wr.shape, F32),
                   jax.ShapeDtypeStruct(wi.shape, F32), jax.ShapeDtypeStruct((8, c), F32)),
        scratch_shapes=[pltpu.VMEM((tt, c), F32)] * 7 + [pltpu.VMEM((1, c), F32)],
        compiler_params=_cp(("arbitrary",)),
    )(*ins)


def _seg_conv(scr, w_ref, taps, pos, seg_len, tt, sl, flip):
    half = taps // 2
    acc = jnp.zeros((tt, sl.stop - sl.start), F32)
    for off in range(-half, half + 1):
        k = half - off if flip else half + off
        ok = jnp.logical_and(pos + off >= 0, pos + off < seg_len)
        acc = acc + jnp.where(ok, scr[pl.ds(CONV_PAD + off, tt), sl], 0.0) * w_ref[pl.ds(k, 1), sl]
    return acc


def _zero_pads(scr, tt, c):
    scr[pl.ds(0, CONV_PAD), :] = jnp.zeros((CONV_PAD, c), F32)
    scr[pl.ds(CONV_PAD + tt, CONV_PAD), :] = jnp.zeros((CONV_PAD, c), F32)


def _ln_stats(v):
    mu = jnp.mean(v, axis=-1, keepdims=True)
    xc = v - mu
    rstd = lax.rsqrt(jnp.mean(xc * xc, axis=-1, keepdims=True) + LN_EPS)
    return xc * rstd, rstd


def _mix_mid(p, h_f, h_b, w31, b31, clg, clb, tt):
    n, c = h_f.shape
    taps = w31.shape[0]
    lc = min(c, LANE)

    def body(gr_ref, cv_ref, cg_ref, hf_ref, hb_ref, w_ref, b_ref, g_ref, bb_ref, y_ref, uc_ref, scr):
        i = pl.program_id(0)
        seg_len = jnp.where(i == 0, tt, SEG)
        gel, _ = _gelu_and_grad(gr_ref[...])
        y_ref[:, 0:c] = ((hf_ref[...] + hb_ref[...]) * gel).astype(BF16)
        _zero_pads(scr, tt, c)
        scr[pl.ds(CONV_PAD, tt), :] = cv_ref[...] * _sig(cg_ref[...])
        pos = lax.broadcasted_iota(jnp.int32, (tt, lc), 0) & (seg_len - 1)
        for ch in range(c // lc):
            sl = slice(ch * lc, (ch + 1) * lc)
            uc_ref[:, sl] = b_ref[:, sl] + _seg_conv(scr, w_ref, taps, pos, seg_len, tt, sl, False)
        xh, _ = _ln_stats(uc_ref[...])
        v = xh * g_ref[...] + bb_ref[...]
        y_ref[:, c:2 * c] = (v * _sig(v)).astype(BF16)

    return pl.pallas_call(
        body, name="mix_mid", grid=(n // tt,),
        in_specs=[_row_spec(tt, c, 1), _row_spec(tt, c, 2), _row_spec(tt, c, 3), _row_spec(tt, c), _row_spec(tt, c),
                  _full_spec(w31.shape), _full_spec((1, c)), _full_spec((1, c)), _full_spec((1, c))],
        out_specs=(_row_spec(tt, 2 * c), _row_spec(tt, c)),
        out_shape=(jax.ShapeDtypeStruct((n, 2 * c), BF16), jax.ShapeDtypeStruct((n, c), F32)),
        scratch_shapes=[pltpu.VMEM((tt + 2 * CONV_PAD, c), F32)],
        compiler_params=_cp(("parallel",)),
    )(p, p, p, h_f, h_b, w31, b31, clg, clb)


def _mix_mid_bwd(dymix, p, h_f, h_b, uc, w31, clg, clb, tt):
    n, c = h_f.shape
    taps = w31.shape[0]
    lc = min(c, LANE)
    sum_rows = 8 * ((taps + 3 + 7) // 8)

    def body(dyr_ref, dyc_ref, gr_ref, cv_ref, cg_ref, hf_ref, hb_ref, uc_ref, w_ref, g_ref, bb_ref,
             dh_ref, dp_ref, sum_ref, uscr, dscr):
        i = pl.program_id(0)

        @pl.when(i == 0)
        def _():
            sum_ref[...] = jnp.zeros(sum_ref.shape, F32)

        seg_len = jnp.where(i == 0, tt, SEG)
        dyr = dyr_ref[...]
        gel, dgel = _gelu_and_grad(gr_ref[...])
        dh_ref[...] = dyr * gel
        dp_ref[:, 0:c] = (dyr * (hf_ref[...] + hb_ref[...]) * dgel).astype(BF16)

        xh, rstd = _ln_stats(uc_ref[...])
        v = xh * g_ref[...] + bb_ref[...]
        sg = _sig(v)
        dv = dyc_ref[...] * (sg * (1.0 + v * (1.0 - sg)))
        dxh = dv * g_ref[...]
        m1 = jnp.mean(dxh, axis=-1, keepdims=True)
        m2 = jnp.mean(dxh * xh, axis=-1, keepdims=True)
        duc = rstd * (dxh - m1 - xh * m2)
        sum_ref[pl.ds(taps, 1), :] += _sum0(duc)
        sum_ref[pl.ds(taps + 1, 1), :] += _sum0(dv * xh)
        sum_ref[pl.ds(taps + 2, 1), :] += _sum0(dv)

        _zero_pads(dscr, tt, c)
        _zero_pads(uscr, tt, c)
        dscr[pl.ds(CONV_PAD, tt), :] = duc
        uscr[pl.ds(CONV_PAD, tt), :] = cv_ref[...] * _sig(cg_ref[...])
        pos = lax.broadcasted_iota(jnp.int32, (tt, lc), 0) & (seg_len - 1)
        half = taps // 2
        for ch in range(c // lc):
            sl = slice(ch * lc, (ch + 1) * lc)
            du = _seg_conv(dscr, w_ref, taps, pos, seg_len, tt, sl, True)
            sgc, cv = _sig(cg_ref[:, sl]), cv_ref[:, sl]
            dp_ref[:, c + sl.start:c + sl.stop] = (du * sgc).astype(BF16)
            dp_ref[:, 2 * c + sl.start:2 * c + sl.stop] = (du * cv * sgc * (1.0 - sgc)).astype(BF16)
            duc_c = dscr[pl.ds(CONV_PAD, tt), sl]
            for off in range(-half, half + 1):
                ok = jnp.logical_and(pos + off >= 0, pos + off < seg_len)
                sum_ref[pl.ds(half + off, 1), sl] += _sum0(
                    duc_c * jnp.where(ok, uscr[pl.ds(CONV_PAD + off, tt), sl], 0.0))

    return pl.pallas_call(
        body, name="mix_mid_bwd", grid=(n // tt,),
        in_specs=[_row_spec(tt, c, 0), _row_spec(tt, c, 1), _row_spec(tt, c, 1), _row_spec(tt, c, 2),
                  _row_spec(tt, c, 3), _row_spec(tt, c), _row_spec(tt, c), _row_spec(tt, c),
                  _full_spec(w31.shape), _full_spec((1, c)), _full_spec((1, c))],
        out_specs=(_row_spec(tt, c), _row_spec(tt, 3 * c), _full_spec((sum_rows, c))),
        out_shape=(jax.ShapeDtypeStruct((n, c), F32), jax.ShapeDtypeStruct((n, 3 * c), BF16),
                   jax.ShapeDtypeStruct((sum_rows, c), F32)),
        scratch_shapes=[pltpu.VMEM((tt + 2 * CONV_PAD, c), F32), pltpu.VMEM((tt + 2 * CONV_PAD, c), F32)],
        compiler_params=_cp(("arbitrary",)),
    )(dymix, dymix, p, p, p, h_f, h_b, uc, w31, clg, clb)


def _ada_fwd(cc, w_ada, b_cols):
    depth, d, wc = w_ada.shape
    tn = _pick(wc, (768, 512, 384, 256, 128))

    def body(c_ref, w_ref, b_ref, o_ref):
        cv = c_ref[...]
        o_ref[...] = _dot((cv * _sig(cv)).astype(BF16), w_ref[...].astype(BF16), 1, 0) + b_ref[...]

    return pl.pallas_call(
        body, name="ada_fwd", grid=(depth, wc // tn),
        in_specs=[_full_spec((MOD_ROWS, d)), pl.BlockSpec((None, d, tn), lambda l, j: (l, 0, j)),
                  pl.BlockSpec((None, 1, tn), lambda l, j: (l, 0, j))],
        out_specs=pl.BlockSpec((None, MOD_ROWS, tn), lambda l, j: (l, 0, j)),
        out_shape=jax.ShapeDtypeStruct((depth, MOD_ROWS, wc), F32),
        compiler_params=_cp(("parallel", "parallel")),
    )(cc, w_ada, b_cols)


def _ada_bwd(cc, dm, w_ada):
    depth, d, wc = w_ada.shape
    tn = _pick(wc, (768, 512, 384, 256, 128))

    def body(c_ref, dm_ref, w_ref, gw_ref, gc_ref):
        cv = c_ref[...]
        sg = _sig(cv)
        dmb = dm_ref[...].astype(BF16)
        gw_ref[...] = _dot((cv * sg).astype(BF16), dmb, 0, 0)

        @pl.when(jnp.logical_and(pl.program_id(0) == 0, pl.program_id(1) == 0))
        def _():
            gc_ref[...] = jnp.zeros(gc_ref.shape, F32)

        gc_ref[...] += _dot(dmb, w_ref[...].astype(BF16), 1, 1) * (sg * (1.0 + cv * (1.0 - sg)))

    return pl.pallas_call(
        body, name="ada_bwd", grid=(depth, wc // tn),
        in_specs=[_full_spec((MOD_ROWS, d)), pl.BlockSpec((None, MOD_ROWS, tn), lambda l, j: (l, 0, j)),
                  pl.BlockSpec((None, d, tn), lambda l, j: (l, 0, j))],
        out_specs=(pl.BlockSpec((None, d, tn), lambda l, j: (l, 0, j)), _full_spec((MOD_ROWS, d))),
        out_shape=(jax.ShapeDtypeStruct((depth, d, wc), F32), jax.ShapeDtypeStruct((MOD_ROWS, d), F32)),
        compiler_params=_cp(("arbitrary", "arbitrary")),
    )(cc, dm, w_ada)


def _adam(grads, w, m, v):
    depth, rows, cols = w.shape
    assert len(grads) == depth
    n_slab = grads[0].shape[0]
    tr = _pick(rows, (128, 64, 32, 16, 8))
    c1, c2 = 1.0 - ADAM_B1 ** ADAM_STEP, 1.0 - ADAM_B2 ** ADAM_STEP

    def body(*refs):
        g_refs = refs[:depth]
        w_ref, m_ref, v_ref, go_ref, do_ref, mo_ref, vo_ref = refs[depth:]
        for q in range(depth):
            @pl.when(pl.program_id(0) == q)
            def _(q=q):
                g = g_refs[q][0].astype(F32)
                for k in range(1, n_slab):
                    g = g + g_refs[q][k].astype(F32)
                go_ref[...] = g

        g = go_ref[...]
        m_new = ADAM_B1 * m_ref[...] + (1.0 - ADAM_B1) * g
        v_new = ADAM_B2 * v_ref[...] + (1.0 - ADAM_B2) * (g * g)
        mo_ref[...] = m_new
        vo_ref[...] = v_new
        do_ref[...] = -ADAM_LR * ((m_new / c1) / (jnp.sqrt(v_new / c2) + ADAM_EPS) + ADAM_WD * w_ref[...])

    blk = pl.BlockSpec((None, tr, cols), lambda l, i: (l, i, 0))
    g_specs = [pl.BlockSpec((n_slab, tr, cols), lambda l, i, q=q: (0, jnp.where(l == q, i, 0), 0))
               for q in range(depth)]
    out = jax.ShapeDtypeStruct((depth, rows, cols), F32)
    return pl.pallas_call(
        body, name="adamw", grid=(depth, rows // tr),
        in_specs=g_specs + [blk, blk, blk],
        out_specs=(blk, blk, blk, blk), out_shape=(out, out, out, out),
        compiler_params=_cp(("parallel", "parallel"), 48),
    )(*grads, w, m, v)


_WEIGHTS = ['c_ctx', 'w_ada', 'b_ada', 'ln_g', 'ln_b', 'ff1_in', 'ff1_out', 'ff2_in', 'ff2_out', 'w_in', 'conv4_w',
            'conv4_b', 'w_rg', 'b_rg', 'w_ig', 'b_ig', 'lam', 'conv31_w', 'conv31_b', 'cln_g', 'cln_b', 'w_out', 'b_out']
_SMALL_SHARDED = {'ln_g': 2, 'ln_b': 2, 'conv4_w': 2, 'w_rg': 3, 'b_rg': 2, 'w_ig': 3, 'b_ig': 2, 'lam': 2,
                  'conv31_w': 2}
_BIG_SHARDED = {'ff1_in': 2, 'ff1_out': 1, 'ff2_in': 2, 'ff2_out': 1, 'w_in': 2, 'w_out': 1}
_SMALL = [n for n in _WEIGHTS if n not in _BIG_SHARDED and n != 'w_ada']


def kernel(x, c, ctx, c_ctx, w_ada, b_ada, ln_g, ln_b, ff1_in, ff1_out, ff2_in, ff2_out, w_in, conv4_w, conv4_b, w_rg, b_rg, w_ig, b_ig, lam, conv31_w, conv31_b, cln_g, cln_b, w_out, b_out, loss_target, m_c_ctx, m_w_ada, m_b_ada, m_ln_g, m_ln_b, m_ff1_in, m_ff1_out, m_ff2_in, m_ff2_out, m_w_in, m_conv4_w, m_conv4_b, m_w_rg, m_b_rg, m_w_ig, m_b_ig, m_lam, m_conv31_w, m_conv31_b, m_cln_g, m_cln_b, m_w_out, m_b_out, v_c_ctx, v_w_ada, v_b_ada, v_ln_g, v_ln_b, v_ff1_in, v_ff1_out, v_ff2_in, v_ff2_out, v_w_in, v_conv4_w, v_conv4_b, v_w_rg, v_b_rg, v_w_ig, v_b_ig, v_lam, v_conv31_w, v_conv31_b, v_cln_g, v_cln_b, v_w_out, v_b_out):
    wts = dict(c_ctx=c_ctx, w_ada=w_ada, b_ada=b_ada, ln_g=ln_g, ln_b=ln_b, ff1_in=ff1_in, ff1_out=ff1_out,
               ff2_in=ff2_in, ff2_out=ff2_out, w_in=w_in, conv4_w=conv4_w, conv4_b=conv4_b, w_rg=w_rg, b_rg=b_rg,
               w_ig=w_ig, b_ig=b_ig, lam=lam, conv31_w=conv31_w, conv31_b=conv31_b, cln_g=cln_g, cln_b=cln_b,
               w_out=w_out, b_out=b_out)
    mom = dict(zip(_WEIGHTS, (m_c_ctx, m_w_ada, m_b_ada, m_ln_g, m_ln_b, m_ff1_in, m_ff1_out, m_ff2_in, m_ff2_out,
                              m_w_in, m_conv4_w, m_conv4_b, m_w_rg, m_b_rg, m_w_ig, m_b_ig, m_lam, m_conv31_w,
                              m_conv31_b, m_cln_g, m_cln_b, m_w_out, m_b_out)))
    var = dict(zip(_WEIGHTS, (v_c_ctx, v_w_ada, v_b_ada, v_ln_g, v_ln_b, v_ff1_in, v_ff1_out, v_ff2_in, v_ff2_out,
                              v_w_in, v_conv4_w, v_conv4_b, v_w_rg, v_b_rg, v_w_ig, v_b_ig, v_lam, v_conv31_w,
                              v_conv31_b, v_cln_g, v_cln_b, v_w_out, v_b_out)))

    depth, d, wc = w_ada.shape
    t_lat, t_ctx = x.shape[1], ctx.shape[1]
    tt, n_tok = t_ctx, t_ctx + x.shape[1]
    assert t_lat % tt == 0 and tt % SEG == 0 and tt & (tt - 1) == 0 and d % 2 == 0
    ch = d // 2
    alpha = (2.0 * depth) ** 0.25
    me = 4 * lax.axis_index("x") + 2 * lax.axis_index("y") + lax.axis_index("c")

    names = list(_SMALL_SHARDED)
    got = _gather_list([wts[k] for k in names] + [c], F32)
    full = {k: _unshard(g, _SMALL_SHARDED[k]) for k, g in zip(names, got[:-1])}
    c_all = got[-1].reshape(N_DEV, d)
    cc = jnp.concatenate([c_all, jnp.pad(c_ctx[None, :], ((0, MOD_ROWS - N_DEV - 1), (0, 0)))], axis=0)

    b_cols = _my_block(b_ada, me, 1)[:, None, :]
    (m_cols,) = _gather_list([_ada_fwd(cc, w_ada, b_cols)], F32)
    m_full = jnp.moveaxis(m_cols, 0, 2).reshape(depth, MOD_ROWS, N_DEV * wc)
    m_lat = lax.dynamic_index_in_dim(m_full, me, axis=1, keepdims=False).reshape(depth, 9, d)
    m_ctx = m_full[:, N_DEV].reshape(depth, 9, d)
    modtab = jnp.pad(jnp.stack([m_ctx, m_lat], axis=1), ((0, 0), (0, 0), (0, MOD_ROWS - 9), (0, 0)))

    big = {k: [None] * depth for k in _BIG_SHARDED}

    def shard(k, l):
        return wts[k][l].astype(BF16)

    def landed(k, l, g):
        big[k][l] = _unshard(g, _BIG_SHARDED[k] - 1)

    def gather(*keys_layers):
        return [("gather", shard(k, l)) for k, l in keys_layers if l < depth]

    got = _gather_list([shard('ff1_in', 0), shard('ff1_out', 0)], BF16)
    landed('ff1_in', 0, got[0])
    landed('ff1_out', 0, got[1])
    wr_b, wi_b = full['w_rg'].astype(BF16), full['w_ig'].astype(BF16)

    def row(v):
        return v.reshape(1, -1)

    s = jnp.concatenate([ctx[0], x[0]], axis=0)
    saved = []
    for l in range(depth):
        mt, sv = modtab[l], {}
        sv['s0'] = s
        sv['h1'] = _modulate(s, mt, 0, 1, tt)
        (sv['z1'], sv['a1']), got = _mm_swiglu(sv['h1'], big['ff1_in'][l], comm=gather(('ff2_in', l)))
        landed('ff2_in', l, got[0])
        sv['f1'], got = _mm_nn("ffn_out", sv['a1'], big['ff1_out'][l], comm=gather(('w_in', l), ('w_out', l)))
        landed('w_in', l, got[0])
        landed('w_out', l, got[1])
        sv['r1'], sv['s1'] = _res_ln(s, sv['f1'], mt, 2, 0.5, row(full['ln_g'][l, 0]), row(full['ln_b'][l, 0]), alpha, tt)
        sv['h2'] = _modulate(sv['s1'], mt, 3, 4, tt)
        sv['p'], got = _mm_nn("mix_in", sv['h2'], big['w_in'][l], comm=gather(('ff2_out', l)))
        landed('ff2_out', l, got[0])
        sv['xr'] = _conv4(sv['p'], full['conv4_w'][l], row(conv4_b[l]), tt, ch)
        for dr_, rev in ((0, False), (1, True)):
            sv['h', dr_], sv['hin', dr_] = _scan(
                sv['xr'], wr_b[l, dr_], wi_b[l, dr_], row(full['b_rg'][l, dr_]), row(full['b_ig'][l, dr_]),
                row(full['lam'][l, dr_]), rev, tt)
        sv['ymix'], sv['uc'] = _mix_mid(sv['p'], sv['h', 0], sv['h', 1], full['conv31_w'][l], row(conv31_b[l]),
                                        row(cln_g[l]), row(cln_b[l]), tt)
        sv['y'], _ = _mm_nn("mix_out", sv['ymix'], big['w_out'][l], bias=row(b_out[l]))
        sv['r2'], sv['s2'] = _res_ln(sv['s1'], sv['y'], mt, 5, 1.0, row(full['ln_g'][l, 1]), row(full['ln_b'][l, 1]),
                                     alpha, tt)
        sv['h3'] = _modulate(sv['s2'], mt, 6, 7, tt)
        (sv['z3'], sv['a3']), got = _mm_swiglu(sv['h3'], big['ff2_in'][l], comm=gather(('ff1_in', l + 1)))
        if got:
            landed('ff1_in', l + 1, got[0])
        sv['f3'], got = _mm_nn("ffn_out", sv['a3'], big['ff2_out'][l], comm=gather(('ff1_out', l + 1)))
        if got:
            landed('ff1_out', l + 1, got[0])
        sv['r3'], s = _res_ln(sv['s2'], sv['f3'], mt, 8, 0.5, row(full['ln_g'][l, 2]), row(full['ln_b'][l, 2]), alpha, tt)
        saved.append(sv)

    cot, loss_blk = _loss_head(s, loss_target[0], tt)
    loss = lax.psum(loss_blk[0, 0], ("x", "y", "c"))

    gx = {k: [None] * depth for k in _BIG_SHARDED}
    gsm = {k: [None] * depth for k in ('ln_g', 'ln_b', 'conv4_w', 'conv4_b', 'b_rg', 'b_ig', 'lam',
                                       'conv31_w', 'conv31_b', 'cln_g', 'cln_b', 'b_out', 'dm')}
    g_gate = {'w_rg': [None] * depth, 'w_ig': [None] * depth}
    taps4, taps31 = conv4_w.shape[1], conv31_w.shape[1]
    heads, hs = w_rg.shape[2], w_rg.shape[4]

    def scatter(pend):
        return [("scatter", g) for _, _, g in pend]

    def exchanged(pend, got):
        for (k, l, _), g in zip(pend, got):
            gx[k][l] = g

    def ffn_bwd(cot, l, mt, r, f, z, a, h, s_in, w_in_name, w_out_name, ln_idx, i_gate, i_scale, pend):
        dr, dfb, tot, per = _ln_bwd(cot, r, f, mt, i_gate, 0.5, row(full['ln_g'][l, ln_idx]), tt)
        dz, got = _mm_nt_swiglu_bwd(dfb, big[w_out_name][l], z, comm=scatter(pend[:1]))
        exchanged(pend[:1], got)
        g_out, got = _mm_tn("ffn_out_dw", a, dfb, comm=scatter(pend[1:]))
        exchanged(pend[1:], got)
        pend = [(w_out_name, l, g_out.reshape(N_DEV, -1, d))]
        dh, got = _mm_nt("ffn_in_dx", dz, big[w_in_name][l], a_halves=True, comm=scatter(pend))
        exchanged(pend, got)
        g_in, _ = _mm_tn("ffn_in_dw", h, dz, b_halves=True, shard_cols=True)
        cot, per2 = _mod_bwd(dr, dh, s_in, mt, i_scale, alpha, tt)
        return cot, tot, per, per2, [(w_in_name, l, g_in)]

    pend = []
    for l in reversed(range(depth)):
        mt, sv = modtab[l], saved[l]
        cot, tot3, per3, mod3, pend_ff2_in = ffn_bwd(cot, l, mt, sv['r3'], sv['f3'], sv['z3'], sv['a3'], sv['h3'],
                                                     sv['s2'], 'ff2_in', 'ff2_out', 2, 8, 7, pend)
        dr, dyb, tot2, per2 = _ln_bwd(cot, sv['r2'], sv['y'], mt, 5, 1.0, row(full['ln_g'][l, 1]), tt)
        dymix, _ = _mm_nt("mix_out_dx", dyb, big['w_out'][l])
        g_w_out, _ = _mm_tn("mix_out_dw", sv['ymix'], dyb)
        pend = [('w_out', l, g_w_out.reshape(N_DEV, -1, d))]
        dh_rec, dp_rest, csum = _mix_mid_bwd(dymix, sv['p'], sv['h', 0], sv['h', 1], sv['uc'], full['conv31_w'][l],
                                             row(cln_g[l]), row(cln_b[l]), tt)
        dxr, gw_r, gw_i, gsum = None, [], [], []
        for dr_, rev in ((0, False), (1, True)):
            dxr, gwr, gwi, gs = _scan_bwd(
                sv['xr'], dh_rec, sv['h', dr_], sv['hin', dr_], wr_b[l, dr_], wi_b[l, dr_],
                row(full['b_rg'][l, dr_]), row(full['b_ig'][l, dr_]), row(full['lam'][l, dr_]), rev, tt, dxr)
            gw_r.append(gwr)
            gw_i.append(gwi)
            gsum.append(gs)
        dp_xr, c4sum = _conv4_bwd(dxr, sv['p'], full['conv4_w'][l], tt, ch)
        dp = jnp.concatenate([dp_xr, dp_rest], axis=1)
        dh, got = _mm_nt("mix_in_dx", dp, big['w_in'][l], comm=scatter(pend))
        exchanged(pend, got)
        g_w_in, _ = _mm_tn("mix_in_dw", sv['h2'], dp, shard_cols=True)
        cot, mod2 = _mod_bwd(dr, dh, sv['s1'], mt, 4, alpha, tt)
        cot, tot1, per1, mod1, pend = ffn_bwd(cot, l, mt, sv['r1'], sv['f1'], sv['z1'], sv['a1'], sv['h1'], sv['s0'],
                                              'ff1_in', 'ff1_out', 0, 2, 1, pend_ff2_in + [('w_in', l, g_w_in)])

        gsm['ln_g'][l] = jnp.stack([tot1[0], tot2[0], tot3[0]])
        gsm['ln_b'][l] = jnp.stack([tot1[1], tot2[1], tot3[1]])
        gsm['b_out'][l] = tot2[2]
        gsm['conv4_w'][l], gsm['conv4_b'][l] = c4sum[:taps4], c4sum[taps4]
        for k, gws in (('w_rg', gw_r), ('w_ig', gw_i)):
            slabs = jnp.stack(gws).reshape(2, heads, N_DEV, hs // N_DEV, hs)
            g_gate[k][l] = jnp.moveaxis(slabs, 2, 0).reshape(N_DEV, -1, hs).astype(BF16)
        gsm['b_rg'][l] = jnp.stack([g[0] for g in gsum])
        gsm['b_ig'][l] = jnp.stack([g[1] for g in gsum])
        gsm['lam'][l] = jnp.stack([g[2] for g in gsum])
        gsm['conv31_w'][l], gsm['conv31_b'][l] = csum[:taps31], csum[taps31]
        gsm['cln_g'][l], gsm['cln_b'][l] = csum[taps31 + 1], csum[taps31 + 2]
        gsm['dm'][l] = jnp.stack([jnp.stack([mod1[8 * st + 1], mod1[8 * st], per1[8 * st],
                                             mod2[8 * st + 1], mod2[8 * st], per2[8 * st],
                                             mod3[8 * st + 1], mod3[8 * st], per3[8 * st]]) for st in (0, 1)])
    grad_x = cot[t_ctx:][None]
    exchanged(pend, [g for g in _all_to_all([g for _, _, g in pend])])
    gate_x = _all_to_all(g_gate['w_rg'] + g_gate['w_ig'])

    keys = list(gsm)
    partial = [jnp.stack(gsm[k]) for k in keys]
    shapes = [a.shape for a in partial]
    gathered = _all_gather(_pack(partial, F32))
    totals = dict(zip(keys, _unpack(_sum_slabs(gathered), shapes)))
    dm_all = _unpack(gathered, shapes, lead=(N_DEV,))[keys.index('dm')]
    dm_tot = totals['dm'].reshape(depth, 2, 9 * d)
    dm_sum = _sum_slabs(jnp.stack([_pack([dm_tot[:, 0]], F32), _pack([dm_tot[:, 1]], F32)]))
    grad = {'b_ada': _unpack(dm_sum, [(depth, 9 * d)])[0]}
    for k in ('ln_g', 'ln_b', 'conv4_w', 'b_rg', 'b_ig', 'lam', 'conv31_w'):
        grad[k] = _my_block(totals[k], me, 2)
    for k in ('conv4_b', 'conv31_b', 'cln_g', 'cln_b', 'b_out'):
        grad[k] = totals[k]

    dm_rows = jnp.concatenate([jnp.moveaxis(dm_all[:, :, 1].reshape(N_DEV, depth, 9 * d), 0, 1),
                               jnp.pad(dm_tot[:, 0][:, None, :], ((0, 0), (0, MOD_ROWS - N_DEV - 1), (0, 0)))], axis=1)
    dm_cols = lax.dynamic_slice_in_dim(dm_rows, me * wc, wc, axis=2)
    grad_w_ada, gcc = _ada_bwd(cc, dm_cols, w_ada)
    gcc_all = _all_gather(_pack([gcc], F32))
    grad['c_ctx'] = _unpack(_sum_slabs(gcc_all), [(MOD_ROWS, d)])[0][N_DEV]

    res = {}
    small = [k for k in _SMALL if k not in g_gate]
    gs, ws, ms, vs = (_pack([src[k] for k in small], F32).reshape(1, -1, 8 * LANE) for src in (grad, wts, mom, var))
    outs = _adam([gs], ws, ms, vs)
    small_shapes = [wts[k].shape for k in small]
    for k, vals in zip(small, zip(*(_unpack(o[0], small_shapes) for o in outs))):
        res[k] = vals
    res['w_ada'] = _adam([grad_w_ada[l][None] for l in range(depth)], w_ada, m_w_ada, v_w_ada)
    for j, k in enumerate(g_gate):
        flat = (depth, -1, hs)
        outs = _adam([gate_x[j * depth + l] for l in range(depth)], wts[k].reshape(flat), mom[k].reshape(flat),
                     var[k].reshape(flat))
        res[k] = tuple(o.reshape(wts[k].shape) for o in outs)
    for k in _BIG_SHARDED:
        res[k] = _adam(gx[k], wts[k], mom[k], var[k])

    out = [loss, grad_x]
    for j in range(4):
        out += [res[k][j] for k in _WEIGHTS]
    return tuple(out)
```

```python
import math

import jax
import jax.numpy as jnp
from jax import lax
from jax.experimental import pallas as pl
from jax.experimental.pallas import tpu as pltpu

F32 = jnp.float32
BF16 = jnp.bfloat16
N_DEV = 8
MESH = pl.DeviceIdType.MESH
LN_EPS = 1e-6
RG_C = 8.0
SEG = 64
CONV_PAD = 16
ADAM_LR, ADAM_B1, ADAM_B2, ADAM_EPS, ADAM_WD, ADAM_STEP = 0.001, 0.9, 0.999, 1e-08, 0.01, 10
LANE = 128
PACK_QUANTUM = 16 * LANE
PACK_ROWS = 512
MOD_ROWS = 16
GELU_K = 0.7978845608028654
GELU_C = 0.044715

_TM = (768, 640, 512, 384, 256, 128)
_TN = (1024, 512, 256, 128)
_TK = (2816, 1024, 512, 256, 128)
_SUB = (256, 128)


def _cp(sem=None, vmem_mb=40):
    return pltpu.CompilerParams(dimension_semantics=sem, vmem_limit_bytes=vmem_mb * 2 ** 20)


def _pick(n, cands):
    for cand in cands:
        if n % cand == 0:
            return cand
    return n


def _sig(v):
    return 1.0 / (1.0 + jnp.exp(-v))


def _dot(a, b, ca, cb):
    return lax.dot_general(a, b, (((ca,), (cb,)), ((), ())), preferred_element_type=F32)


def _sum0(v):
    return jnp.sum(v, axis=0, keepdims=True)


def _gelu_and_grad(v):
    inner = GELU_K * (v + GELU_C * v * v * v)
    t = jnp.tanh(inner)
    gel = 0.5 * v * (1.0 + t)
    dgel = 0.5 * (1.0 + t) + 0.5 * v * (1.0 - t * t) * GELU_K * (1.0 + 3.0 * GELU_C * v * v)
    return gel, dgel


def _softplus_neg(lam):
    y = jnp.exp(-jnp.abs(lam))
    u = 1.0 + y
    log1p = jnp.where(u == 1.0, y, jnp.log(u) * (y / (u - 1.0)))
    return jnp.maximum(-lam, 0.0) + log1p


def _neg_expm1(z):
    ser = -z * (1.0 + z * (1 / 2 + z * (1 / 6 + z * (1 / 24 + z * (1 / 120 + z * (1 / 720 + z * (1 / 5040)))))))
    return jnp.where(z > -0.25, ser, 1.0 - jnp.exp(z))


def _row_spec(tt, w, col=0):
    return pl.BlockSpec((tt, w), lambda i: (i, col))


def _full_spec(shape):
    nd = len(shape)
    return pl.BlockSpec(shape, lambda *_: (0,) * nd)


def _mod_spec(d):
    return pl.BlockSpec((None, MOD_ROWS, d), lambda i: (jnp.minimum(i, 1), 0, 0))


def _stream_sum_spec(w):
    return pl.BlockSpec((8, w), lambda i: (jnp.minimum(i, 1), 0))


def _acc_rows(ref, init, rows):
    @pl.when(init)
    def _():
        ref[...] = jnp.zeros(ref.shape, F32)

    for k, row in enumerate(rows):
        ref[pl.ds(k, 1), :] += row


def _all_gather(v):
    def body(x_ref, out_ref, send_sems, recv_sems, local_sem):
        x, y, c = lax.axis_index("x"), lax.axis_index("y"), lax.axis_index("c")
        me, sibling = (x, y, c), (x, y, 1 - c)
        chips = [(1 - x, y), (x, 1 - y), (1 - x, 1 - y)]

        def slot(px, py, pc):
            return out_ref.at[4 * px + 2 * py + pc]

        def copy(k, block, to, src=None):
            return pltpu.make_async_remote_copy(
                src_ref=slot(*block) if src is None else src, dst_ref=slot(*block),
                send_sem=send_sems.at[k], recv_sem=recv_sems.at[k], device_id=to, device_id_type=MESH)

        mine = pltpu.make_async_copy(x_ref, slot(*me), local_sem)
        mine.start()
        first = [copy(0, me, sibling, src=x_ref)]
        first += [copy(1 + j, me, (*chip, c), src=x_ref) for j, chip in enumerate(chips)]
        for cp in first:
            cp.start()
        passed = [copy(4 + j, (*chip, c), sibling) for j, chip in enumerate(chips)]
        for j, chip in enumerate(chips):
            copy(1 + j, (*chip, c), me).wait_recv()
            passed[j].start()
        copy(0, sibling, me).wait_recv()
        for j, chip in enumerate(chips):
            copy(4 + j, (*chip, 1 - c), me).wait_recv()
        for cp in first + passed:
            cp.wait_send()
        mine.wait()

    return pl.pallas_call(
        body, name="all_gather",
        out_shape=jax.ShapeDtypeStruct((N_DEV,) + v.shape, v.dtype),
        in_specs=[pl.BlockSpec(memory_space=pltpu.HBM)],
        out_specs=pl.BlockSpec(memory_space=pltpu.HBM),
        scratch_shapes=[pltpu.SemaphoreType.DMA((7,)), pltpu.SemaphoreType.DMA((7,)), pltpu.SemaphoreType.DMA],
    )(v)


_PEER_FLIPS = [(0, 0, 1), (1, 0, 0), (0, 1, 0), (1, 1, 0), (1, 0, 1), (0, 1, 1), (1, 1, 1)]


def _all_to_all(parts):
    n_parts = len(parts)

    def body(*refs):
        x_refs, out_ref = refs[:n_parts], refs[n_parts]
        send_sems, recv_sems, local_sems = refs[n_parts + 1:]
        x, y, c = lax.axis_index("x"), lax.axis_index("y"), lax.axis_index("c")
        me = 4 * x + 2 * y + c
        local, remote = [], []
        for l in range(n_parts):
            cp = pltpu.make_async_copy(x_refs[l].at[me], out_ref.at[l, me], local_sems.at[l])
            cp.start()
            local.append(cp)
            for k, (fx, fy, fc) in enumerate(_PEER_FLIPS):
                px = 1 - x if fx else x
                py = 1 - y if fy else y
                pc = 1 - c if fc else c
                cp = pltpu.make_async_remote_copy(
                    src_ref=x_refs[l].at[4 * px + 2 * py + pc], dst_ref=out_ref.at[l, me],
                    send_sem=send_sems.at[7 * l + k], recv_sem=recv_sems.at[7 * l + k],
                    device_id=(px, py, pc), device_id_type=MESH)
                cp.start()
                remote.append(cp)
        for cp in remote:
            cp.wait()
        for cp in local:
            cp.wait()

    shape = parts[0].shape
    return pl.pallas_call(
        body, name="all_to_all",
        out_shape=jax.ShapeDtypeStruct((n_parts,) + shape, parts[0].dtype),
        in_specs=[pl.BlockSpec(memory_space=pltpu.HBM)] * n_parts,
        out_specs=pl.BlockSpec(memory_space=pltpu.HBM),
        scratch_shapes=[pltpu.SemaphoreType.DMA((7 * n_parts,)), pltpu.SemaphoreType.DMA((7 * n_parts,)),
                        pltpu.SemaphoreType.DMA((n_parts,))],
    )(*parts)


def _pack(arrs, dtype):
    pieces = []
    for a in arrs:
        flat = a.astype(dtype).reshape(-1)
        flat = jnp.pad(flat, (0, (-flat.shape[0]) % PACK_QUANTUM))
        pieces.append(flat.reshape(-1, LANE))
    packed = jnp.concatenate(pieces, axis=0) if len(pieces) > 1 else pieces[0]
    return jnp.pad(packed, ((0, (-packed.shape[0]) % PACK_ROWS), (0, 0)))


def _unpack(packed, shapes, lead=()):
    flat = packed.reshape(lead + (-1,))
    outs, off = [], 0
    for sh in shapes:
        n = math.prod(sh)
        outs.append(flat[..., off:off + n].reshape(lead + tuple(sh)))
        off += n + (-n) % PACK_QUANTUM
    return outs


def _gather_list(arrs, dtype):
    gathered = _all_gather(_pack(arrs, dtype))
    return _unpack(gathered, [a.shape for a in arrs], lead=(N_DEV,))


def _unshard(g, axis):
    y = jnp.moveaxis(g, 0, axis)
    sh = y.shape
    return y.reshape(sh[:axis] + (sh[axis] * sh[axis + 1],) + sh[axis + 2:])


def _my_block(full, me, axis):
    size = full.shape[axis] // N_DEV
    return lax.dynamic_slice_in_dim(full, me * size, size, axis=axis)


def _sum_slabs(v):
    n_slab, rows, width = v.shape
    tr = _pick(rows, (1024, 512, 256, 128, 64, 32, 16, 8))

    def body(v_ref, o_ref):
        acc = v_ref[0]
        for k in range(1, n_slab):
            acc = acc + v_ref[k]
        o_ref[...] = acc

    return pl.pallas_call(
        body, name="sum_slabs", grid=(rows // tr,),
        in_specs=[pl.BlockSpec((n_slab, tr, width), lambda i: (0, i, 0))],
        out_specs=pl.BlockSpec((tr, width), lambda i: (i, 0)),
        out_shape=jax.ShapeDtypeStruct((rows, width), F32),
        compiler_params=_cp(("parallel",)),
    )(v)


N_LOCAL = 5


def _comm_copies(kinds, src_refs, dst_refs, send_sems, recv_sems, local_sems):
    x, y, c = lax.axis_index("x"), lax.axis_index("y"), lax.axis_index("c")
    me = 4 * x + 2 * y + c
    copies = []

    def remote(j, k, src, dst, peer):
        copies.append(pltpu.make_async_remote_copy(
            src_ref=src, dst_ref=dst, send_sem=send_sems.at[7 * j + k], recv_sem=recv_sems.at[7 * j + k],
            device_id=peer, device_id_type=MESH))

    def local(j, k, src, dst):
        copies.append(pltpu.make_async_copy(src, dst, local_sems.at[N_LOCAL * j + k]))

    for j, kind in enumerate(kinds):
        src, dst = src_refs[j], dst_refs[j]
        peers = [(1 - x if fx else x, 1 - y if fy else y, 1 - c if fc else c) for fx, fy, fc in _PEER_FLIPS]
        if kind == "scatter":
            local(j, 0, src.at[me], dst.at[me])
            for k, (px, py, pc) in enumerate(peers):
                remote(j, k, src.at[4 * px + 2 * py + pc], dst.at[me], (px, py, pc))
        elif kind == "spread":
            local(j, 0, src, dst.at[me])
            for k, peer in enumerate(peers[:4]):
                remote(j, k, src, dst.at[me], peer)
        else:
            local(j, 0, src.at[me], dst.at[me])
            for k, (px, py, pc) in enumerate(peers[:4]):
                local(j, 1 + k, src.at[4 * px + 2 * py + pc], dst.at[4 * px + 2 * py + pc])
            for k, (px, py, pc) in enumerate(peers[1:4]):
                remote(j, 4 + k, src.at[4 * px + 2 * py + pc], dst.at[4 * px + 2 * py + pc], peers[0])
    return copies


def _mm(name, grid, nk, ins, in_specs, out_shape, out_specs, acc_shape, prod, fin, comm=()):
    n_in, n_out, n_c = len(ins), len(out_shape), len(comm)
    kinds = [kind for kind, _ in comm]
    srcs = [a for _, a in comm]
    dsts = [jax.ShapeDtypeStruct(((N_DEV,) + a.shape) if kind == "spread" else a.shape, a.dtype) for kind, a in comm]
    hbm = pl.BlockSpec(memory_space=pltpu.HBM)

    def body(*refs):
        in_refs, refs = refs[:n_in], refs[n_in:]
        src_refs, refs = refs[:n_c], refs[n_c:]
        out_refs, refs = refs[:n_out], refs[n_out:]
        dst_refs, scratch = refs[:n_c], refs[n_c:]
        if nk > 1:
            acc, scratch = scratch[0], scratch[1:]
        ids = [pl.program_id(a) for a in range(3)]
        if n_c:
            copies = _comm_copies(kinds, src_refs, dst_refs, *scratch)

            @pl.when(jnp.logical_and(ids[0] == 0, jnp.logical_and(ids[1] == 0, ids[2] == 0)))
            def _():
                for cp in copies:
                    cp.start()

        if nk == 1:
            fin(prod(*in_refs), in_refs, out_refs)
        else:
            @pl.when(ids[2] == 0)
            def _():
                acc[...] = jnp.zeros(acc.shape, F32)

            for idx, val in enumerate(prod(*in_refs)):
                acc[idx] += val

            @pl.when(ids[2] == nk - 1)
            def _():
                fin(tuple(acc[idx] for idx in range(acc_shape[0])), in_refs, out_refs)

        if n_c:
            @pl.when(jnp.logical_and(ids[0] == grid[0] - 1, jnp.logical_and(ids[1] == grid[1] - 1, ids[2] == nk - 1)))
            def _():
                for cp in copies:
                    cp.wait()

    scratch = [pltpu.VMEM(acc_shape, F32)] if nk > 1 else []
    if n_c:
        scratch += [pltpu.SemaphoreType.DMA((7 * n_c,)), pltpu.SemaphoreType.DMA((7 * n_c,)),
                    pltpu.SemaphoreType.DMA((N_LOCAL * n_c,))]
    sem = ("arbitrary",) * 3 if n_c else ("parallel", "parallel", "arbitrary")
    res = pl.pallas_call(
        body, name=name + (f"_x{n_c}" if n_c else ""), grid=grid,
        in_specs=list(in_specs) + [hbm] * n_c, out_specs=tuple(out_specs) + (hbm,) * n_c,
        out_shape=tuple(out_shape) + tuple(dsts), scratch_shapes=scratch,
        compiler_params=_cp(sem, 48),
    )(*ins, *srcs)
    return res[:n_out], list(res[n_out:])


def _mm_nn(name, a, b, bias=None, comm=()):
    m, kk = a.shape
    nn = b.shape[1]
    tm, tn = _pick(m, _TM), _pick(nn, _TN)
    tk = kk if kk <= 2048 else _pick(kk, _TK)
    nk = kk // tk
    ins = [a, b]
    specs = [pl.BlockSpec((tm, tk), lambda i, j, k: (i, k)), pl.BlockSpec((tk, tn), lambda i, j, k: (k, j))]
    if bias is not None:
        ins.append(bias)
        specs.append(pl.BlockSpec((1, tn), lambda i, j, k: (0, j)))

    def prod(a_ref, b_ref, *_):
        return (_dot(a_ref[...], b_ref[...], 1, 0),)

    def fin(vals, in_refs, out_refs):
        out_refs[0][...] = vals[0] if bias is None else vals[0] + in_refs[2][...]

    outs, got = _mm(name, (m // tm, nn // tn, nk), nk, ins, specs,
                    (jax.ShapeDtypeStruct((m, nn), F32),), (pl.BlockSpec((tm, tn), lambda i, j, k: (i, j)),),
                    (1, tm, tn), prod, fin, comm)
    return outs[0], got


def _mm_swiglu(h, w1, comm=()):
    m, kk = h.shape
    f = w1.shape[1] // 2
    tm, tn = _pick(m, _TM), _pick(f, (512, 256, 128))
    nj = f // tn
    sub = _pick(tm, _SUB)

    def prod(*_):
        return ()

    def fin(_, in_refs, out_refs):
        h_ref, wg_ref, wu_ref = in_refs
        for q in range(tm // sub):
            rows = pl.ds(q * sub, sub)
            hq = h_ref[rows, :]
            zg, zu = _dot(hq, wg_ref[...], 1, 0), _dot(hq, wu_ref[...], 1, 0)
            out_refs[0][0, rows, :] = zg.astype(BF16)
            out_refs[0][1, rows, :] = zu.astype(BF16)
            out_refs[1][rows, :] = (zg * _sig(zg) * zu).astype(BF16)

    return _mm("ffn_in", (m // tm, nj, 1), 1, [h, w1, w1],
               [pl.BlockSpec((tm, kk), lambda i, j, k: (i, 0)),
                pl.BlockSpec((kk, tn), lambda i, j, k: (0, j)),
                pl.BlockSpec((kk, tn), lambda i, j, k: (0, j + nj))],
               (jax.ShapeDtypeStruct((2, m, f), BF16), jax.ShapeDtypeStruct((m, f), BF16)),
               (pl.BlockSpec((2, tm, tn), lambda i, j, k: (0, i, j)), pl.BlockSpec((tm, tn), lambda i, j, k: (i, j))),
               (2, tm, tn), prod, fin, comm)


def _mm_nt_swiglu_bwd(dfb, w2, z, comm=()):
    m, kk = dfb.shape
    f = w2.shape[0]
    tm, tn = _pick(m, _TM), _pick(f, (512, 256, 128))
    sub = _pick(tm, _SUB)

    def prod(*_):
        return ()

    def fin(_, in_refs, out_refs):
        a_ref, b_ref, z_ref = in_refs
        for q in range(tm // sub):
            rows = pl.ds(q * sub, sub)
            da = _dot(a_ref[rows, :], b_ref[...], 1, 1)
            zg, zu = z_ref[0, rows, :].astype(F32), z_ref[1, rows, :].astype(F32)
            sg = _sig(zg)
            out_refs[0][0, rows, :] = (da * zu * (sg * (1.0 + zg * (1.0 - sg)))).astype(BF16)
            out_refs[0][1, rows, :] = (da * (zg * sg)).astype(BF16)

    outs, got = _mm("ffn_out_dx", (m // tm, f // tn, 1), 1, [dfb, w2, z],
                    [pl.BlockSpec((tm, kk), lambda i, j, k: (i, 0)),
                     pl.BlockSpec((tn, kk), lambda i, j, k: (j, 0)),
                     pl.BlockSpec((2, tm, tn), lambda i, j, k: (0, i, j))],
                    (jax.ShapeDtypeStruct((2, m, f), BF16),),
                    (pl.BlockSpec((2, tm, tn), lambda i, j, k: (0, i, j)),),
                    (1, tm, tn), prod, fin, comm)
    return outs[0], got


def _mm_nt(name, a, b, a_halves=False, comm=()):
    nn, kk = b.shape
    m = a.shape[1] if a_halves else a.shape[0]
    kh = kk // 2 if a_halves else kk
    tm, tn = _pick(m, _TM), _pick(nn, _TN)
    tk = kh if (kh <= 2048 and not a_halves) else _pick(kh, _TK)
    nk, nkh = kk // tk, kh // tk
    if a_halves:
        a_spec = pl.BlockSpec((None, tm, tk), lambda i, j, k: (k // nkh, i, k % nkh))
    else:
        a_spec = pl.BlockSpec((tm, tk), lambda i, j, k: (i, k))

    def prod(a_ref, b_ref):
        return (_dot(a_ref[...], b_ref[...], 1, 1),)

    def fin(vals, in_refs, out_refs):
        out_refs[0][...] = vals[0]

    outs, got = _mm(name, (m // tm, nn // tn, nk), nk, [a, b],
                    [a_spec, pl.BlockSpec((tn, tk), lambda i, j, k: (j, k))],
                    (jax.ShapeDtypeStruct((m, nn), F32),), (pl.BlockSpec((tm, tn), lambda i, j, k: (i, j)),),
                    (1, tm, tn), prod, fin, comm)
    return outs[0], got


def _mm_tn(name, a, b, b_halves=False, shard_cols=False, comm=()):
    kt, m = a.shape
    nn = 2 * b.shape[2] if b_halves else b.shape[1]
    tk, tm = _pick(kt, _TM), _pick(m, (1408, 1024, 512, 256, 128))
    shard = nn // N_DEV
    per = 1
    while shard_cols and 2 * per * shard <= nn // 2 and ((per * shard) % 256 or per * shard < 1024):
        per *= 2
    tn = per * shard if shard_cols else _pick(nn, _TN)
    njh = (nn // 2) // tn if b_halves else 0
    if b_halves:
        b_spec = pl.BlockSpec((None, tk, tn), lambda i, j, k: (j // njh, k, j % njh))
    else:
        b_spec = pl.BlockSpec((tk, tn), lambda i, j, k: (k, j))
    if shard_cols:
        out_shape = jax.ShapeDtypeStruct((N_DEV, m, shard), BF16)
        out_spec = pl.BlockSpec((per, tm, shard), lambda i, j, k: (j, i, 0))
    else:
        out_shape = jax.ShapeDtypeStruct((m, nn), BF16)
        out_spec = pl.BlockSpec((tm, tn), lambda i, j, k: (i, j))

    def prod(a_ref, b_ref):
        return (_dot(a_ref[...], b_ref[...], 0, 0),)

    def fin(vals, in_refs, out_refs):
        r = vals[0].astype(BF16)
        if shard_cols:
            for q in range(per):
                out_refs[0][q] = r[:, q * shard:(q + 1) * shard]
        else:
            out_refs[0][...] = r

    outs, got = _mm(name, (m // tm, nn // tn, kt // tk), kt // tk, [a, b],
                    [pl.BlockSpec((tk, tm), lambda i, j, k: (k, i)), b_spec],
                    (out_shape,), (out_spec,), (1, tm, tn), prod, fin, comm)
    return outs[0], got


def _modulate(s, modtab, i_shift, i_scale, tt):
    n, d = s.shape

    def body(s_ref, m_ref, o_ref):
        shift, scale = m_ref[pl.ds(i_shift, 1), :], m_ref[pl.ds(i_scale, 1), :]
        o_ref[...] = (s_ref[...] * (1.0 + scale) + shift).astype(BF16)

    return pl.pallas_call(
        body, name="modulate", grid=(n // tt,),
        in_specs=[_row_spec(tt, d), _mod_spec(d)], out_specs=_row_spec(tt, d),
        out_shape=jax.ShapeDtypeStruct((n, d), BF16), compiler_params=_cp(("parallel",)),
    )(s, modtab)


def _res_ln(s, f, modtab, i_gate, coef, gam, bet, alpha, tt):
    n, d = s.shape

    def body(s_ref, f_ref, m_ref, g_ref, b_ref, r_ref, o_ref):
        gate = m_ref[pl.ds(i_gate, 1), :]
        r = alpha * s_ref[...] + (coef * gate) * f_ref[...]
        r_ref[...] = r
        mu = jnp.mean(r, axis=-1, keepdims=True)
        xc = r - mu
        var = jnp.mean(xc * xc, axis=-1, keepdims=True)
        o_ref[...] = xc * lax.rsqrt(var + LN_EPS) * g_ref[...] + b_ref[...]

    return pl.pallas_call(
        body, name="res_ln", grid=(n // tt,),
        in_specs=[_row_spec(tt, d), _row_spec(tt, d), _mod_spec(d), _full_spec((1, d)), _full_spec((1, d))],
        out_specs=(_row_spec(tt, d), _row_spec(tt, d)),
        out_shape=(jax.ShapeDtypeStruct((n, d), F32), jax.ShapeDtypeStruct((n, d), F32)),
        compiler_params=_cp(("parallel",)),
    )(s, f, modtab, gam, bet)


def _ln_bwd(do, r, f, modtab, i_gate, coef, gam, tt):
    n, d = do.shape

    def body(do_ref, r_ref, f_ref, m_ref, g_ref, dr_ref, dfb_ref, tot_ref, str_ref):
        i = pl.program_id(0)
        r = r_ref[...]
        mu = jnp.mean(r, axis=-1, keepdims=True)
        xc = r - mu
        rstd = lax.rsqrt(jnp.mean(xc * xc, axis=-1, keepdims=True) + LN_EPS)
        xh = xc * rstd
        dout = do_ref[...]
        dxh = dout * g_ref[...]
        m1 = jnp.mean(dxh, axis=-1, keepdims=True)
        m2 = jnp.mean(dxh * xh, axis=-1, keepdims=True)
        dr = rstd * (dxh - m1 - xh * m2)
        dr_ref[...] = dr
        dfb = (coef * m_ref[pl.ds(i_gate, 1), :]) * dr
        dfb_ref[...] = dfb.astype(BF16)
        _acc_rows(tot_ref, i == 0, [_sum0(dout * xh), _sum0(dout), _sum0(dfb)])
        _acc_rows(str_ref, i <= 1, [_sum0(coef * f_ref[...] * dr)])

    return pl.pallas_call(
        body, name="ln_bwd", grid=(n // tt,),
        in_specs=[_row_spec(tt, d), _row_spec(tt, d), _row_spec(tt, d), _mod_spec(d), _full_spec((1, d))],
        out_specs=(_row_spec(tt, d), _row_spec(tt, d), _full_spec((8, d)), _stream_sum_spec(d)),
        out_shape=(jax.ShapeDtypeStruct((n, d), F32), jax.ShapeDtypeStruct((n, d), BF16),
                   jax.ShapeDtypeStruct((8, d), F32), jax.ShapeDtypeStruct((16, d), F32)),
        compiler_params=_cp(("arbitrary",)),
    )(do, r, f, modtab, gam)


def _mod_bwd(dr, dh, s, modtab, i_scale, alpha, tt):
    n, d = dr.shape

    def body(dr_ref, dh_ref, s_ref, m_ref, o_ref, str_ref):
        i = pl.program_id(0)
        dh_v = dh_ref[...]
        o_ref[...] = alpha * dr_ref[...] + dh_v * (1.0 + m_ref[pl.ds(i_scale, 1), :])
        _acc_rows(str_ref, i <= 1, [_sum0(dh_v * s_ref[...]), _sum0(dh_v)])

    return pl.pallas_call(
        body, name="mod_bwd", grid=(n // tt,),
        in_specs=[_row_spec(tt, d), _row_spec(tt, d), _row_spec(tt, d), _mod_spec(d)],
        out_specs=(_row_spec(tt, d), _stream_sum_spec(d)),
        out_shape=(jax.ShapeDtypeStruct((n, d), F32), jax.ShapeDtypeStruct((16, d), F32)),
        compiler_params=_cp(("arbitrary",)),
    )(dr, dh, s, modtab)


def _loss_head(s, target, tt):
    n, d = s.shape
    nt = n // tt

    def body(s_ref, t_ref, do_ref, loss_ref, acc):
        i = pl.program_id(0)

        @pl.when(i == 0)
        def _():
            acc[...] = jnp.zeros(acc.shape, F32)
            do_ref[...] = jnp.zeros(do_ref.shape, F32)

        @pl.when(i > 0)
        def _():
            err = s_ref[...] - t_ref[...]
            do_ref[...] = err / d
            acc[...] += jnp.sum((err * err).reshape(tt // 8, 8, d), axis=0)

        @pl.when(i == nt - 1)
        def _():
            loss_ref[...] = jnp.full(loss_ref.shape, jnp.sum(acc[...]) * (0.5 / d), F32)

    return pl.pallas_call(
        body, name="loss_head", grid=(nt,),
        in_specs=[_row_spec(tt, d), pl.BlockSpec((tt, d), lambda i: (jnp.maximum(i - 1, 0), 0))],
        out_specs=(_row_spec(tt, d), _full_spec((8, LANE))),
        out_shape=(jax.ShapeDtypeStruct((n, d), F32), jax.ShapeDtypeStruct((8, LANE), F32)),
        scratch_shapes=[pltpu.VMEM((8, d), F32)],
        compiler_params=_cp(("arbitrary",)),
    )(s, target)


def _halo_specs(tt, c, n):
    nb, last = tt // 8, n // 8 - 1
    return [pl.BlockSpec((tt, c), lambda i: (i, 0)),
            pl.BlockSpec((8, c), lambda i: (jnp.maximum(i * nb - 1, 0), 0)),
            pl.BlockSpec((8, c), lambda i: (jnp.minimum((i + 1) * nb, last), 0))]


def _fill_halo(scr, main_ref, prev_ref, next_ref, i, nt, tt):
    has_prev = i >= 2
    has_next = jnp.logical_and(i >= 1, i < nt - 1)
    scr[pl.ds(0, 8), :] = jnp.where(has_prev, prev_ref[...], 0.0)
    scr[pl.ds(8, tt), :] = main_ref[...]
    scr[pl.ds(8 + tt, 8), :] = jnp.where(has_next, next_ref[...], 0.0)


def _conv4(p, w, b, tt, c):
    n = p.shape[0]
    nt, taps = n // tt, w.shape[0]
    left = taps // 2

    def body(x_ref, xp_ref, xn_ref, w_ref, b_ref, o_ref, scr):
        _fill_halo(scr, x_ref, xp_ref, xn_ref, pl.program_id(0), nt, tt)
        acc = jnp.broadcast_to(b_ref[...], (tt, c))
        for k in range(taps):
            acc = acc + w_ref[pl.ds(k, 1), :] * scr[pl.ds(8 + k - left, tt), :]
        o_ref[...] = acc

    return pl.pallas_call(
        body, name="conv4", grid=(nt,),
        in_specs=_halo_specs(tt, c, n) + [_full_spec(w.shape), _full_spec((1, c))],
        out_specs=_row_spec(tt, c), out_shape=jax.ShapeDtypeStruct((n, c), F32),
        scratch_shapes=[pltpu.VMEM((tt + 16, c), F32)], compiler_params=_cp(("parallel",)),
    )(p, p, p, w, b)


def _conv4_bwd(dxr, p, w, tt, c):
    n = p.shape[0]
    nt, taps = n // tt, w.shape[0]
    left = taps // 2

    def body(d_ref, dp_ref, dn_ref, x_ref, xp_ref, xn_ref, w_ref, o_ref, sum_ref, dscr, xscr):
        i = pl.program_id(0)
        _fill_halo(dscr, d_ref, dp_ref, dn_ref, i, nt, tt)
        _fill_halo(xscr, x_ref, xp_ref, xn_ref, i, nt, tt)
        acc = jnp.zeros((tt, c), F32)
        for k in range(taps):
            acc = acc + w_ref[pl.ds(k, 1), :] * dscr[pl.ds(8 - (k - left), tt), :]
        o_ref[...] = acc.astype(BF16)
        d = d_ref[...]
        rows = [_sum0(d * xscr[pl.ds(8 + k - left, tt), :]) for k in range(taps)] + [_sum0(d)]
        _acc_rows(sum_ref, i == 0, rows)

    return pl.pallas_call(
        body, name="conv4_bwd", grid=(nt,),
        in_specs=_halo_specs(tt, c, n) + _halo_specs(tt, c, n) + [_full_spec(w.shape)],
        out_specs=(_row_spec(tt, c), _full_spec((8, c))),
        out_shape=(jax.ShapeDtypeStruct((n, c), BF16), jax.ShapeDtypeStruct((8, c), F32)),
        scratch_shapes=[pltpu.VMEM((tt + 16, c), F32), pltpu.VMEM((tt + 16, c), F32)],
        compiler_params=_cp(("arbitrary",)),
    )(dxr, dxr, dxr, p, p, p, w)


def _head_gates(xv, wr, wi, br, bi, lam):
    xb = xv.astype(BF16)
    r = _sig(_dot(xb, wr, 1, 0) + br)
    ig = _sig(_dot(xb, wi, 1, 0) + bi)
    sp = _softplus_neg(lam)
    log_a = (-RG_C) * r * sp
    return r, ig, sp, jnp.exp(log_a), jnp.sqrt(_neg_expm1(2.0 * log_a))


def _tile_scan(a, b, increasing, tt, rows):
    s = 1
    while s < tt:
        if increasing:
            a_sh, b_sh, ok = pltpu.roll(a, s, 0), pltpu.roll(b, s, 0), rows >= s
        else:
            a_sh, b_sh, ok = pltpu.roll(a, tt - s, 0), pltpu.roll(b, tt - s, 0), rows < tt - s
        b = jnp.where(ok, a * b_sh + b, b)
        a = jnp.where(ok, a * a_sh, a)
        s *= 2
    return a, b


def _scan_tile_index(step, nt, reverse):
    return jnp.where(step == 0, 0, nt - step) if reverse else step


def _scan(xr, wr, wi, br, bi, lam, reverse, tt):
    n, c = xr.shape
    nt = n // tt
    heads, hs = wr.shape[0], wr.shape[2]
    lc = min(c, LANE)

    def tile(i):
        return _scan_tile_index(i, nt, reverse)

    def body(x_ref, wr_ref, wi_ref, br_ref, bi_ref, lam_ref, h_ref, hin_ref, a_scr, b_scr, carry):
        i = pl.program_id(0)

        @pl.when(i == 0)
        def _():
            carry[...] = jnp.zeros(carry.shape, F32)

        for hd in range(heads):
            sl = slice(hd * hs, (hd + 1) * hs)
            xv = x_ref[:, sl]
            _, ig, _, a, sq = _head_gates(xv, wr_ref[hd], wi_ref[hd], br_ref[:, sl], bi_ref[:, sl], lam_ref[:, sl])
            a_scr[:, sl] = a
            b_scr[:, sl] = sq * (ig * xv)
        hin_ref[...] = jnp.broadcast_to(carry[...], (8, c))
        rows = lax.broadcasted_iota(jnp.int32, (tt, lc), 0)
        for ch in range(c // lc):
            sl = slice(ch * lc, (ch + 1) * lc)
            big_a, big_b = _tile_scan(a_scr[:, sl], b_scr[:, sl], not reverse, tt, rows)
            h_ref[:, sl] = big_a * carry[:, sl] + big_b
        carry[...] = h_ref[pl.ds(0 if reverse else tt - 1, 1), :]

    return pl.pallas_call(
        body, name="lru_scan", grid=(nt,),
        in_specs=[pl.BlockSpec((tt, c), lambda i: (tile(i), 0)), _full_spec(wr.shape), _full_spec(wi.shape),
                  _full_spec((1, c)), _full_spec((1, c)), _full_spec((1, c))],
        out_specs=(pl.BlockSpec((tt, c), lambda i: (tile(i), 0)), pl.BlockSpec((None, 8, c), lambda i: (tile(i), 0, 0))),
        out_shape=(jax.ShapeDtypeStruct((n, c), F32), jax.ShapeDtypeStruct((nt, 8, c), F32)),
        scratch_shapes=[pltpu.VMEM((tt, c), F32), pltpu.VMEM((tt, c), F32), pltpu.VMEM((1, c), F32)],
        compiler_params=_cp(("arbitrary",)),
    )(xr, wr, wi, br, bi, lam)


def _scan_bwd(xr, dh, h, hin, wr, wi, br, bi, lam, reverse, tt, dx_prev=None):
    n, c = xr.shape
    nt = n // tt
    heads, hs = wr.shape[0], wr.shape[2]
    lc = min(c, LANE)
    inc = not reverse
    first, last = (0, tt - 1) if inc else (tt - 1, 0)

    def tile(i):
        return _scan_tile_index(nt - 1 - i, nt, reverse)

    def body(*refs):
        x_ref, dh_ref, h_ref, hin_ref, wr_ref, wi_ref, br_ref, bi_ref, lam_ref = refs[:9]
        refs = refs[9:]
        if dx_prev is not None:
            dxp_ref, refs = refs[0], refs[1:]
        dx_ref, dwr_ref, dwi_ref, sum_ref, a_scr, r_scr, i_scr, sq_scr, g_scr, da_scr, u_scr, ucarry = refs
        i = pl.program_id(0)

        @pl.when(i == 0)
        def _():
            ucarry[...] = jnp.zeros(ucarry.shape, F32)
            dwr_ref[...] = jnp.zeros(dwr_ref.shape, F32)
            dwi_ref[...] = jnp.zeros(dwi_ref.shape, F32)
            sum_ref[...] = jnp.zeros(sum_ref.shape, F32)

        for hd in range(heads):
            sl = slice(hd * hs, (hd + 1) * hs)
            r, ig, _, a, sq = _head_gates(x_ref[:, sl], wr_ref[hd], wi_ref[hd], br_ref[:, sl], bi_ref[:, sl],
                                          lam_ref[:, sl])
            a_scr[:, sl], r_scr[:, sl], i_scr[:, sl], sq_scr[:, sl] = a, r, ig, sq

        rows = lax.broadcasted_iota(jnp.int32, (tt, lc), 0)
        to_prev = 1 if inc else tt - 1
        to_next = tt - 1 if inc else 1
        for ch in range(c // lc):
            sl = slice(ch * lc, (ch + 1) * lc)
            a, dhv = a_scr[:, sl], dh_ref[:, sl]
            big_a, big_b = _tile_scan(a, a * dhv, not inc, tt, rows)
            u_in = ucarry[:, sl]
            u = big_a * u_in + big_b
            u_scr[:, sl] = u
            g = dhv + jnp.where(rows == last, u_in, pltpu.roll(u, to_next, 0))
            g_scr[:, sl] = g
            h_prev = jnp.where(rows == first, hin_ref[pl.ds(0, 1), sl], pltpu.roll(h_ref[:, sl], to_prev, 0))
            da_scr[:, sl] = g * h_prev
        ucarry[...] = u_scr[pl.ds(first, 1), :]

        for hd in range(heads):
            sl = slice(hd * hs, (hd + 1) * hs)
            xv, a, r, ig, sq = x_ref[:, sl], a_scr[:, sl], r_scr[:, sl], i_scr[:, sl], sq_scr[:, sl]
            g, lam_v = g_scr[:, sl], lam_ref[:, sl]
            sp = _softplus_neg(lam_v)
            d_sq = g * ig * xv
            d_em = d_sq * 0.5 / sq
            d_log_a = (da_scr[:, sl] - 2.0 * a * d_em) * a
            dzr = (d_log_a * ((-RG_C) * sp)) * r * (1.0 - r)
            dzi = (g * sq * xv) * ig * (1.0 - ig)
            dzr_b, dzi_b, xb = dzr.astype(BF16), dzi.astype(BF16), xv.astype(BF16)
            dx = g * sq * ig + _dot(dzr_b, wr_ref[hd], 1, 1) + _dot(dzi_b, wi_ref[hd], 1, 1)
            if dx_prev is not None:
                dx = dx + dxp_ref[:, sl]
            dx_ref[:, sl] = dx
            dwr_ref[hd] += _dot(xb, dzr_b, 0, 0)
            dwi_ref[hd] += _dot(xb, dzi_b, 0, 0)
            sum_ref[pl.ds(0, 1), sl] += _sum0(dzr)
            sum_ref[pl.ds(1, 1), sl] += _sum0(dzi)
            sum_ref[pl.ds(2, 1), sl] += _sum0(d_log_a * ((-RG_C) * r)) * (-_sig(-lam_v))

    tile_spec = pl.BlockSpec((tt, c), lambda i: (tile(i), 0))
    ins = [xr, dh, h, hin, wr, wi, br, bi, lam]
    in_specs = [tile_spec, tile_spec, tile_spec, pl.BlockSpec((None, 8, c), lambda i: (tile(i), 0, 0)),
                _full_spec(wr.shape), _full_spec(wi.shape), _full_spec((1, c)), _full_spec((1, c)), _full_spec((1, c))]
    if dx_prev is not None:
        ins.append(dx_prev)
        in_specs.append(tile_spec)
    return pl.pallas_call(
        body, name="lru_scan_bwd", grid=(nt,), in_specs=in_specs,
        out_specs=(tile_spec, _full_spec(wr.shape), _full_spec(wi.shape), _full_spec((8, c))),
        out_shape=(jax.ShapeDtypeStruct((n, c), F32), jax.ShapeDtypeStruct(wr.shape, F32),
                   jax.ShapeDtypeStruct(wi.shape, F32), jax.ShapeDtypeStruct((8, c), F32)),
        scratch_shapes=[pltpu.VMEM((tt, c), F32)] * 7 + [pltpu.VMEM((1, c), F32)],
        compiler_params=_cp(("arbitrary",)),
    )(*ins)


def _conv_rows(nseg, seg):
    return nseg * (seg + CONV_PAD) + CONV_PAD


def _seg_base(s, seg):
    return CONV_PAD + s * (seg + CONV_PAD)


def _zero_gaps(scr, nseg, seg, c):
    for s in range(nseg + 1):
        scr[pl.ds(s * (seg + CONV_PAD), CONV_PAD), :] = jnp.zeros((CONV_PAD, c), F32)


def _build_shifts(scr, rot, n_rows, sl):
    for b in range(1, 8):
        rot[b - 1, pl.ds(0, n_rows - 8), :] = scr[pl.ds(b, n_rows - 8), sl]


def _tap(scr, rot, base, off, seg, sl):
    a, b = divmod(off, 8)
    if b == 0:
        return scr[pl.ds(base + 8 * a, seg), sl]
    return rot[b - 1, pl.ds(base + 8 * a, seg), :]


def _conv_chunks(nseg, seg):
    return [(s * seg + q * SEG, _seg_base(s, seg) + q * SEG) for s in range(nseg) for q in range(seg // SEG)]


def _conv_scratch(tt, c):
    rows = max(_conv_rows(1, tt), _conv_rows(tt // SEG, SEG))
    return pltpu.VMEM((rows, c), F32), pltpu.VMEM((7, rows, min(c, LANE)), F32)


def _ln_stats(v):
    mu = jnp.mean(v, axis=-1, keepdims=True)
    xc = v - mu
    rstd = lax.rsqrt(jnp.mean(xc * xc, axis=-1, keepdims=True) + LN_EPS)
    return xc * rstd, rstd


def _mix_mid(p, h_f, h_b, w31, b31, clg, clb, tt):
    n, c = h_f.shape
    taps = w31.shape[0]
    lc = min(c, LANE)

    half = taps // 2

    def body(gr_ref, cv_ref, cg_ref, hf_ref, hb_ref, w_ref, b_ref, g_ref, bb_ref, y_ref, uc_ref, scr, rot):
        i = pl.program_id(0)
        gel, _ = _gelu_and_grad(gr_ref[...])
        y_ref[:, 0:c] = ((hf_ref[...] + hb_ref[...]) * gel).astype(BF16)

        def conv(nseg, seg):
            _zero_gaps(scr, nseg, seg, c)
            for s in range(nseg):
                rows = pl.ds(s * seg, seg)
                scr[pl.ds(_seg_base(s, seg), seg), :] = cv_ref[rows, :] * _sig(cg_ref[rows, :])
            for ch in range(c // lc):
                sl = slice(ch * lc, (ch + 1) * lc)
                _build_shifts(scr, rot, _conv_rows(nseg, seg), sl)
                for out0, base in _conv_chunks(nseg, seg):
                    acc = jnp.broadcast_to(b_ref[:, sl], (SEG, lc))
                    for k in range(taps):
                        acc = acc + _tap(scr, rot, base, k - half, SEG, sl) * w_ref[pl.ds(k, 1), sl]
                    uc_ref[pl.ds(out0, SEG), sl] = acc

        @pl.when(i == 0)
        def _():
            conv(1, tt)

        @pl.when(i > 0)
        def _():
            conv(tt // SEG, SEG)

        xh, _ = _ln_stats(uc_ref[...])
        v = xh * g_ref[...] + bb_ref[...]
        y_ref[:, c:2 * c] = (v * _sig(v)).astype(BF16)

    return pl.pallas_call(
        body, name="mix_mid", grid=(n // tt,),
        in_specs=[_row_spec(tt, c, 1), _row_spec(tt, c, 2), _row_spec(tt, c, 3), _row_spec(tt, c), _row_spec(tt, c),
                  _full_spec(w31.shape), _full_spec((1, c)), _full_spec((1, c)), _full_spec((1, c))],
        out_specs=(_row_spec(tt, 2 * c), _row_spec(tt, c)),
        out_shape=(jax.ShapeDtypeStruct((n, 2 * c), BF16), jax.ShapeDtypeStruct((n, c), F32)),
        scratch_shapes=list(_conv_scratch(tt, c)),
        compiler_params=_cp(("parallel",)),
    )(p, p, p, h_f, h_b, w31, b31, clg, clb)


def _mix_mid_bwd(dymix, p, h_f, h_b, uc, w31, clg, clb, tt):
    n, c = h_f.shape
    taps = w31.shape[0]
    lc = min(c, LANE)
    sum_rows = 8 * ((taps + 3 + 7) // 8)

    half = taps // 2
    nt = n // tt

    def body(dyr_ref, dyc_ref, gr_ref, cv_ref, cg_ref, hf_ref, hb_ref, uc_ref, w_ref, g_ref, bb_ref,
             dh_ref, dp_ref, sum_ref, uscr, urot, dscr, drot, duc_scr, wacc):
        i = pl.program_id(0)

        @pl.when(i == 0)
        def _():
            sum_ref[...] = jnp.zeros(sum_ref.shape, F32)
            wacc[...] = jnp.zeros(wacc.shape, F32)

        dyr = dyr_ref[...]
        gel, dgel = _gelu_and_grad(gr_ref[...])
        dh_ref[...] = dyr * gel
        dp_ref[:, 0:c] = (dyr * (hf_ref[...] + hb_ref[...]) * dgel).astype(BF16)

        xh, rstd = _ln_stats(uc_ref[...])
        v = xh * g_ref[...] + bb_ref[...]
        sg = _sig(v)
        dv = dyc_ref[...] * (sg * (1.0 + v * (1.0 - sg)))
        dxh = dv * g_ref[...]
        m1 = jnp.mean(dxh, axis=-1, keepdims=True)
        m2 = jnp.mean(dxh * xh, axis=-1, keepdims=True)
        duc = rstd * (dxh - m1 - xh * m2)
        sum_ref[pl.ds(taps, 1), :] += _sum0(duc)
        sum_ref[pl.ds(taps + 1, 1), :] += _sum0(dv * xh)
        sum_ref[pl.ds(taps + 2, 1), :] += _sum0(dv)

        duc_scr[...] = duc

        def conv_bwd(nseg, seg):
            n_rows = _conv_rows(nseg, seg)
            _zero_gaps(dscr, nseg, seg, c)
            _zero_gaps(uscr, nseg, seg, c)
            for s in range(nseg):
                rows = pl.ds(s * seg, seg)
                dscr[pl.ds(_seg_base(s, seg), seg), :] = duc_scr[rows, :]
                uscr[pl.ds(_seg_base(s, seg), seg), :] = cv_ref[rows, :] * _sig(cg_ref[rows, :])
            for ch in range(c // lc):
                sl = slice(ch * lc, (ch + 1) * lc)
                _build_shifts(dscr, drot, n_rows, sl)
                _build_shifts(uscr, urot, n_rows, sl)
                chunks = _conv_chunks(nseg, seg)
                for out0, base in chunks:
                    rows = pl.ds(out0, SEG)
                    du = jnp.zeros((SEG, lc), F32)
                    for k in range(taps):
                        du = du + _tap(dscr, drot, base, k - half, SEG, sl) * w_ref[pl.ds(taps - 1 - k, 1), sl]
                    sgc, cv = _sig(cg_ref[rows, sl]), cv_ref[rows, sl]
                    dp_ref[rows, c + sl.start:c + sl.stop] = (du * sgc).astype(BF16)
                    dp_ref[rows, 2 * c + sl.start:2 * c + sl.stop] = (du * cv * sgc * (1.0 - sgc)).astype(BF16)
                for g0 in range(0, len(chunks), 4):
                    group = chunks[g0:g0 + 4]
                    duc_q = [duc_scr[pl.ds(out0, SEG), sl] for out0, _ in group]
                    for k in range(taps):
                        part = jnp.zeros((8, lc), F32)
                        for dq, (_, base) in zip(duc_q, group):
                            prod = dq * _tap(uscr, urot, base, k - half, SEG, sl)
                            part = part + jnp.sum(prod.reshape(SEG // 8, 8, lc), axis=0)
                        wacc[pl.ds(8 * k, 8), sl] += part

        @pl.when(i == 0)
        def _():
            conv_bwd(1, tt)

        @pl.when(i > 0)
        def _():
            conv_bwd(tt // SEG, SEG)

        @pl.when(i == nt - 1)
        def _():
            for k in range(taps):
                sum_ref[pl.ds(k, 1), :] = _sum0(wacc[pl.ds(8 * k, 8), :])

    scr_a, rot_a = _conv_scratch(tt, c)
    return pl.pallas_call(
        body, name="mix_mid_bwd", grid=(n // tt,),
        in_specs=[_row_spec(tt, c, 0), _row_spec(tt, c, 1), _row_spec(tt, c, 1), _row_spec(tt, c, 2),
                  _row_spec(tt, c, 3), _row_spec(tt, c), _row_spec(tt, c), _row_spec(tt, c),
                  _full_spec(w31.shape), _full_spec((1, c)), _full_spec((1, c))],
        out_specs=(_row_spec(tt, c), _row_spec(tt, 3 * c), _full_spec((sum_rows, c))),
        out_shape=(jax.ShapeDtypeStruct((n, c), F32), jax.ShapeDtypeStruct((n, 3 * c), BF16),
                   jax.ShapeDtypeStruct((sum_rows, c), F32)),
        scratch_shapes=[scr_a, rot_a, scr_a, rot_a, pltpu.VMEM((tt, c), F32), pltpu.VMEM((8 * taps, c), F32)],
        compiler_params=_cp(("arbitrary",), 48),
    )(dymix, dymix, p, p, p, h_f, h_b, uc, w31, clg, clb)


def _ada_fwd(cc, w_ada, b_cols):
    depth, d, wc = w_ada.shape
    tn = _pick(wc, (768, 512, 384, 256, 128))

    def body(c_ref, w_ref, b_ref, o_ref):
        cv = c_ref[...]
        o_ref[...] = _dot((cv * _sig(cv)).astype(BF16), w_ref[...].astype(BF16), 1, 0) + b_ref[...]

    return pl.pallas_call(
        body, name="ada_fwd", grid=(depth, wc // tn),
        in_specs=[_full_spec((MOD_ROWS, d)), pl.BlockSpec((None, d, tn), lambda l, j: (l, 0, j)),
                  pl.BlockSpec((None, 1, tn), lambda l, j: (l, 0, j))],
        out_specs=pl.BlockSpec((None, MOD_ROWS, tn), lambda l, j: (l, 0, j)),
        out_shape=jax.ShapeDtypeStruct((depth, MOD_ROWS, wc), F32),
        compiler_params=_cp(("parallel", "parallel")),
    )(cc, w_ada, b_cols)


def _ada_bwd(cc, dm, w_ada):
    depth, d, wc = w_ada.shape
    tn = _pick(wc, (768, 512, 384, 256, 128))

    def body(c_ref, dm_ref, w_ref, gw_ref, gc_ref):
        cv = c_ref[...]
        sg = _sig(cv)
        dmb = dm_ref[...].astype(BF16)
        gw_ref[...] = _dot((cv * sg).astype(BF16), dmb, 0, 0)

        @pl.when(jnp.logical_and(pl.program_id(0) == 0, pl.program_id(1) == 0))
        def _():
            gc_ref[...] = jnp.zeros(gc_ref.shape, F32)

        gc_ref[...] += _dot(dmb, w_ref[...].astype(BF16), 1, 1) * (sg * (1.0 + cv * (1.0 - sg)))

    return pl.pallas_call(
        body, name="ada_bwd", grid=(depth, wc // tn),
        in_specs=[_full_spec((MOD_ROWS, d)), pl.BlockSpec((None, MOD_ROWS, tn), lambda l, j: (l, 0, j)),
                  pl.BlockSpec((None, d, tn), lambda l, j: (l, 0, j))],
        out_specs=(pl.BlockSpec((None, d, tn), lambda l, j: (l, 0, j)), _full_spec((MOD_ROWS, d))),
        out_shape=(jax.ShapeDtypeStruct((depth, d, wc), F32), jax.ShapeDtypeStruct((MOD_ROWS, d), F32)),
        compiler_params=_cp(("arbitrary", "arbitrary")),
    )(cc, dm, w_ada)


def _adam(grads, w, m, v):
    depth, rows, cols = w.shape
    assert len(grads) == depth
    n_slab = grads[0].shape[0]
    tr = _pick(rows, (128, 64, 32, 16, 8))
    c1, c2 = 1.0 - ADAM_B1 ** ADAM_STEP, 1.0 - ADAM_B2 ** ADAM_STEP

    def body(*refs):
        g_refs = refs[:depth]
        w_ref, m_ref, v_ref, go_ref, do_ref, mo_ref, vo_ref = refs[depth:]
        for q in range(depth):
            @pl.when(pl.program_id(0) == q)
            def _(q=q):
                g = g_refs[q][0].astype(F32)
                for k in range(1, n_slab):
                    g = g + g_refs[q][k].astype(F32)
                go_ref[...] = g

        g = go_ref[...]
        m_new = ADAM_B1 * m_ref[...] + (1.0 - ADAM_B1) * g
        v_new = ADAM_B2 * v_ref[...] + (1.0 - ADAM_B2) * (g * g)
        mo_ref[...] = m_new
        vo_ref[...] = v_new
        do_ref[...] = -ADAM_LR * ((m_new / c1) / (jnp.sqrt(v_new / c2) + ADAM_EPS) + ADAM_WD * w_ref[...])

    blk = pl.BlockSpec((None, tr, cols), lambda l, i: (l, i, 0))
    g_specs = [pl.BlockSpec((n_slab, tr, cols), lambda l, i, q=q: (0, jnp.where(l == q, i, 0), 0))
               for q in range(depth)]
    out = jax.ShapeDtypeStruct((depth, rows, cols), F32)
    return pl.pallas_call(
        body, name="adamw", grid=(depth, rows // tr),
        in_specs=g_specs + [blk, blk, blk],
        out_specs=(blk, blk, blk, blk), out_shape=(out, out, out, out),
        compiler_params=_cp(("parallel", "parallel"), 48),
    )(*grads, w, m, v)


_WEIGHTS = ['c_ctx', 'w_ada', 'b_ada', 'ln_g', 'ln_b', 'ff1_in', 'ff1_out', 'ff2_in', 'ff2_out', 'w_in', 'conv4_w',
            'conv4_b', 'w_rg', 'b_rg', 'w_ig', 'b_ig', 'lam', 'conv31_w', 'conv31_b', 'cln_g', 'cln_b', 'w_out', 'b_out']
_SMALL_SHARDED = {'ln_g': 2, 'ln_b': 2, 'conv4_w': 2, 'w_rg': 3, 'b_rg': 2, 'w_ig': 3, 'b_ig': 2, 'lam': 2,
                  'conv31_w': 2}
_BIG_SHARDED = {'ff1_in': 2, 'ff1_out': 1, 'ff2_in': 2, 'ff2_out': 1, 'w_in': 2, 'w_out': 1}
_SMALL = [n for n in _WEIGHTS if n not in _BIG_SHARDED and n != 'w_ada']


def kernel(x, c, ctx, c_ctx, w_ada, b_ada, ln_g, ln_b, ff1_in, ff1_out, ff2_in, ff2_out, w_in, conv4_w, conv4_b, w_rg, b_rg, w_ig, b_ig, lam, conv31_w, conv31_b, cln_g, cln_b, w_out, b_out, loss_target, m_c_ctx, m_w_ada, m_b_ada, m_ln_g, m_ln_b, m_ff1_in, m_ff1_out, m_ff2_in, m_ff2_out, m_w_in, m_conv4_w, m_conv4_b, m_w_rg, m_b_rg, m_w_ig, m_b_ig, m_lam, m_conv31_w, m_conv31_b, m_cln_g, m_cln_b, m_w_out, m_b_out, v_c_ctx, v_w_ada, v_b_ada, v_ln_g, v_ln_b, v_ff1_in, v_ff1_out, v_ff2_in, v_ff2_out, v_w_in, v_conv4_w, v_conv4_b, v_w_rg, v_b_rg, v_w_ig, v_b_ig, v_lam, v_conv31_w, v_conv31_b, v_cln_g, v_cln_b, v_w_out, v_b_out):
    wts = dict(c_ctx=c_ctx, w_ada=w_ada, b_ada=b_ada, ln_g=ln_g, ln_b=ln_b, ff1_in=ff1_in, ff1_out=ff1_out,
               ff2_in=ff2_in, ff2_out=ff2_out, w_in=w_in, conv4_w=conv4_w, conv4_b=conv4_b, w_rg=w_rg, b_rg=b_rg,
               w_ig=w_ig, b_ig=b_ig, lam=lam, conv31_w=conv31_w, conv31_b=conv31_b, cln_g=cln_g, cln_b=cln_b,
               w_out=w_out, b_out=b_out)
    mom = dict(zip(_WEIGHTS, (m_c_ctx, m_w_ada, m_b_ada, m_ln_g, m_ln_b, m_ff1_in, m_ff1_out, m_ff2_in, m_ff2_out,
                              m_w_in, m_conv4_w, m_conv4_b, m_w_rg, m_b_rg, m_w_ig, m_b_ig, m_lam, m_conv31_w,
                              m_conv31_b, m_cln_g, m_cln_b, m_w_out, m_b_out)))
    var = dict(zip(_WEIGHTS, (v_c_ctx, v_w_ada, v_b_ada, v_ln_g, v_ln_b, v_ff1_in, v_ff1_out, v_ff2_in, v_ff2_out,
                              v_w_in, v_conv4_w, v_conv4_b, v_w_rg, v_b_rg, v_w_ig, v_b_ig, v_lam, v_conv31_w,
                              v_conv31_b, v_cln_g, v_cln_b, v_w_out, v_b_out)))

    depth, d, wc = w_ada.shape
    t_lat, t_ctx = x.shape[1], ctx.shape[1]
    tt, n_tok = t_ctx, t_ctx + x.shape[1]
    assert t_lat % tt == 0 and tt % SEG == 0 and tt & (tt - 1) == 0 and d % 2 == 0
    ch = d // 2
    alpha = (2.0 * depth) ** 0.25
    me = 4 * lax.axis_index("x") + 2 * lax.axis_index("y") + lax.axis_index("c")

    names = list(_SMALL_SHARDED)
    got = _gather_list([wts[k] for k in names] + [c], F32)
    full = {k: _unshard(g, _SMALL_SHARDED[k]) for k, g in zip(names, got[:-1])}
    c_all = got[-1].reshape(N_DEV, d)
    cc = jnp.concatenate([c_all, jnp.pad(c_ctx[None, :], ((0, MOD_ROWS - N_DEV - 1), (0, 0)))], axis=0)

    b_cols = _my_block(b_ada, me, 1)[:, None, :]
    (m_cols,) = _gather_list([_ada_fwd(cc, w_ada, b_cols)], F32)
    m_full = jnp.moveaxis(m_cols, 0, 2).reshape(depth, MOD_ROWS, N_DEV * wc)
    m_lat = lax.dynamic_index_in_dim(m_full, me, axis=1, keepdims=False).reshape(depth, 9, d)
    m_ctx = m_full[:, N_DEV].reshape(depth, 9, d)
    modtab = jnp.pad(jnp.stack([m_ctx, m_lat], axis=1), ((0, 0), (0, 0), (0, MOD_ROWS - 9), (0, 0)))

    big = {k: [None] * depth for k in _BIG_SHARDED}
    half = {}

    def shard(k, l):
        return wts[k][l].astype(BF16)

    def landed(k, l, g):
        big[k][l] = _unshard(g, _BIG_SHARDED[k] - 1)

    def jobs(*specs):
        return [(stage, k, l) for stage, k, l in specs if l < depth]

    def comm_of(todo):
        return [(stage, shard(k, l) if stage == "spread" else half[k, l]) for stage, k, l in todo]

    def settle(todo, got):
        for (stage, k, l), g in zip(todo, got):
            if stage == "spread":
                half[k, l] = g
            else:
                landed(k, l, g)

    got = _gather_list([shard('ff1_in', 0), shard('ff1_out', 0)], BF16)
    landed('ff1_in', 0, got[0])
    landed('ff1_out', 0, got[1])
    wr_b, wi_b = full['w_rg'].astype(BF16), full['w_ig'].astype(BF16)

    def row(v):
        return v.reshape(1, -1)

    s = jnp.concatenate([ctx[0], x[0]], axis=0)
    saved = []
    for l in range(depth):
        mt, sv = modtab[l], {}
        sv['s0'] = s
        sv['h1'] = _modulate(s, mt, 0, 1, tt)
        todo = jobs(("spread", 'w_in', l), ("spread", 'w_out', l), ("spread", 'ff2_in', l))
        (sv['z1'], sv['a1']), got = _mm_swiglu(sv['h1'], big['ff1_in'][l], comm=comm_of(todo))
        settle(todo, got)
        todo = jobs(("relay", 'w_in', l), ("relay", 'w_out', l), ("relay", 'ff2_in', l), ("spread", 'ff2_out', l))
        sv['f1'], got = _mm_nn("ffn_out", sv['a1'], big['ff1_out'][l], comm=comm_of(todo))
        settle(todo, got)
        sv['r1'], sv['s1'] = _res_ln(s, sv['f1'], mt, 2, 0.5, row(full['ln_g'][l, 0]), row(full['ln_b'][l, 0]), alpha, tt)
        sv['h2'] = _modulate(sv['s1'], mt, 3, 4, tt)
        todo = jobs(("relay", 'ff2_out', l))
        sv['p'], got = _mm_nn("mix_in", sv['h2'], big['w_in'][l], comm=comm_of(todo))
        settle(todo, got)
        sv['xr'] = _conv4(sv['p'], full['conv4_w'][l], row(conv4_b[l]), tt, ch)
        for dr_, rev in ((0, False), (1, True)):
            sv['h', dr_], sv['hin', dr_] = _scan(
                sv['xr'], wr_b[l, dr_], wi_b[l, dr_], row(full['b_rg'][l, dr_]), row(full['b_ig'][l, dr_]),
                row(full['lam'][l, dr_]), rev, tt)
        sv['ymix'], sv['uc'] = _mix_mid(sv['p'], sv['h', 0], sv['h', 1], full['conv31_w'][l], row(conv31_b[l]),
                                        row(cln_g[l]), row(cln_b[l]), tt)
        sv['y'], _ = _mm_nn("mix_out", sv['ymix'], big['w_out'][l], bias=row(b_out[l]))
        sv['r2'], sv['s2'] = _res_ln(sv['s1'], sv['y'], mt, 5, 1.0, row(full['ln_g'][l, 1]), row(full['ln_b'][l, 1]),
                                     alpha, tt)
        sv['h3'] = _modulate(sv['s2'], mt, 6, 7, tt)
        todo = jobs(("spread", 'ff1_in', l + 1), ("spread", 'ff1_out', l + 1))
        (sv['z3'], sv['a3']), got = _mm_swiglu(sv['h3'], big['ff2_in'][l], comm=comm_of(todo))
        settle(todo, got)
        todo = jobs(("relay", 'ff1_in', l + 1), ("relay", 'ff1_out', l + 1))
        sv['f3'], got = _mm_nn("ffn_out", sv['a3'], big['ff2_out'][l], comm=comm_of(todo))
        settle(todo, got)
        sv['r3'], s = _res_ln(sv['s2'], sv['f3'], mt, 8, 0.5, row(full['ln_g'][l, 2]), row(full['ln_b'][l, 2]), alpha, tt)
        saved.append(sv)

    cot, loss_blk = _loss_head(s, loss_target[0], tt)
    loss = lax.psum(loss_blk[0, 0], ("x", "y", "c"))

    gx = {k: [None] * depth for k in _BIG_SHARDED}
    gsm = {k: [None] * depth for k in ('ln_g', 'ln_b', 'conv4_w', 'conv4_b', 'b_rg', 'b_ig', 'lam',
                                       'conv31_w', 'conv31_b', 'cln_g', 'cln_b', 'b_out', 'dm')}
    g_gate = {'w_rg': [None] * depth, 'w_ig': [None] * depth}
    taps4, taps31 = conv4_w.shape[1], conv31_w.shape[1]
    heads, hs = w_rg.shape[2], w_rg.shape[4]

    def scatter(pend):
        return [("scatter", g) for _, _, g in pend]

    def exchanged(pend, got):
        for (k, l, _), g in zip(pend, got):
            gx[k][l] = g

    def ffn_bwd(cot, l, mt, r, f, z, a, h, s_in, w_in_name, w_out_name, ln_idx, i_gate, i_scale, pend_short, pend_long,
                final=False):
        dr, dfb, tot, per = _ln_bwd(cot, r, f, mt, i_gate, 0.5, row(full['ln_g'][l, ln_idx]), tt)
        dz, got = _mm_nt_swiglu_bwd(dfb, big[w_out_name][l], z, comm=scatter(pend_short))
        exchanged(pend_short, got)
        g_out, _ = _mm_tn("ffn_out_dw", a, dfb)
        pend_out = [(w_out_name, l, g_out.reshape(N_DEV, -1, d))]
        if final:
            g_in, got = _mm_tn("ffn_in_dw", h, dz, b_halves=True, shard_cols=True, comm=scatter(pend_long))
            exchanged(pend_long, got)
            pend_out.append((w_in_name, l, g_in))
            dh, got = _mm_nt("ffn_in_dx", dz, big[w_in_name][l], a_halves=True, comm=scatter(pend_out))
            exchanged(pend_out, got)
            pend_in = []
        else:
            dh, got = _mm_nt("ffn_in_dx", dz, big[w_in_name][l], a_halves=True, comm=scatter(pend_long))
            exchanged(pend_long, got)
            g_in, got = _mm_tn("ffn_in_dw", h, dz, b_halves=True, shard_cols=True, comm=scatter(pend_out))
            exchanged(pend_out, got)
            pend_in = [(w_in_name, l, g_in)]
        cot, per2 = _mod_bwd(dr, dh, s_in, mt, i_scale, alpha, tt)
        return cot, tot, per, per2, pend_in

    pend = []
    for l in reversed(range(depth)):
        mt, sv = modtab[l], saved[l]
        cot, tot3, per3, mod3, pend_ff2_in = ffn_bwd(cot, l, mt, sv['r3'], sv['f3'], sv['z3'], sv['a3'], sv['h3'],
                                                     sv['s2'], 'ff2_in', 'ff2_out', 2, 8, 7, [], pend)
        dr, dyb, tot2, per2 = _ln_bwd(cot, sv['r2'], sv['y'], mt, 5, 1.0, row(full['ln_g'][l, 1]), tt)
        dymix, _ = _mm_nt("mix_out_dx", dyb, big['w_out'][l])
        g_w_out, _ = _mm_tn("mix_out_dw", sv['ymix'], dyb)
        pend = [('w_out', l, g_w_out.reshape(N_DEV, -1, d))]
        dh_rec, dp_rest, csum = _mix_mid_bwd(dymix, sv['p'], sv['h', 0], sv['h', 1], sv['uc'], full['conv31_w'][l],
                                             row(cln_g[l]), row(cln_b[l]), tt)
        dxr, gw_r, gw_i, gsum = None, [], [], []
        for dr_, rev in ((0, False), (1, True)):
            dxr, gwr, gwi, gs = _scan_bwd(
                sv['xr'], dh_rec, sv['h', dr_], sv['hin', dr_], wr_b[l, dr_], wi_b[l, dr_],
                row(full['b_rg'][l, dr_]), row(full['b_ig'][l, dr_]), row(full['lam'][l, dr_]), rev, tt, dxr)
            gw_r.append(gwr)
            gw_i.append(gwi)
            gsum.append(gs)
        dp_xr, c4sum = _conv4_bwd(dxr, sv['p'], full['conv4_w'][l], tt, ch)
        dp = jnp.concatenate([dp_xr, dp_rest], axis=1)
        dh, got = _mm_nt("mix_in_dx", dp, big['w_in'][l], comm=scatter(pend))
        exchanged(pend, got)
        g_w_in, _ = _mm_tn("mix_in_dw", sv['h2'], dp, shard_cols=True)
        cot, mod2 = _mod_bwd(dr, dh, sv['s1'], mt, 4, alpha, tt)
        cot, tot1, per1, mod1, pend = ffn_bwd(cot, l, mt, sv['r1'], sv['f1'], sv['z1'], sv['a1'], sv['h1'], sv['s0'],
                                              'ff1_in', 'ff1_out', 0, 2, 1, [('w_in', l, g_w_in)], pend_ff2_in,
                                              final=(l == 0))

        gsm['ln_g'][l] = jnp.stack([tot1[0], tot2[0], tot3[0]])
        gsm['ln_b'][l] = jnp.stack([tot1[1], tot2[1], tot3[1]])
        gsm['b_out'][l] = tot2[2]
        gsm['conv4_w'][l], gsm['conv4_b'][l] = c4sum[:taps4], c4sum[taps4]
        for k, gws in (('w_rg', gw_r), ('w_ig', gw_i)):
            slabs = jnp.stack(gws).reshape(2, heads, N_DEV, hs // N_DEV, hs)
            g_gate[k][l] = jnp.moveaxis(slabs, 2, 0).reshape(N_DEV, -1, hs).astype(BF16)
        gsm['b_rg'][l] = jnp.stack([g[0] for g in gsum])
        gsm['b_ig'][l] = jnp.stack([g[1] for g in gsum])
        gsm['lam'][l] = jnp.stack([g[2] for g in gsum])
        gsm['conv31_w'][l], gsm['conv31_b'][l] = csum[:taps31], csum[taps31]
        gsm['cln_g'][l], gsm['cln_b'][l] = csum[taps31 + 1], csum[taps31 + 2]
        gsm['dm'][l] = jnp.stack([jnp.stack([mod1[8 * st + 1], mod1[8 * st], per1[8 * st],
                                             mod2[8 * st + 1], mod2[8 * st], per2[8 * st],
                                             mod3[8 * st + 1], mod3[8 * st], per3[8 * st]]) for st in (0, 1)])
    grad_x = cot[t_ctx:][None]
    assert not pend
    gate_x = _all_to_all(g_gate['w_rg'] + g_gate['w_ig'])

    keys = list(gsm)
    partial = [jnp.stack(gsm[k]) for k in keys]
    shapes = [a.shape for a in partial]
    gathered = _all_gather(_pack(partial, F32))
    totals = dict(zip(keys, _unpack(_sum_slabs(gathered), shapes)))
    dm_all = _unpack(gathered, shapes, lead=(N_DEV,))[keys.index('dm')]
    dm_tot = totals['dm'].reshape(depth, 2, 9 * d)
    dm_sum = _sum_slabs(jnp.stack([_pack([dm_tot[:, 0]], F32), _pack([dm_tot[:, 1]], F32)]))
    grad = {'b_ada': _unpack(dm_sum, [(depth, 9 * d)])[0]}
    for k in ('ln_g', 'ln_b', 'conv4_w', 'b_rg', 'b_ig', 'lam', 'conv31_w'):
        grad[k] = _my_block(totals[k], me, 2)
    for k in ('conv4_b', 'conv31_b', 'cln_g', 'cln_b', 'b_out'):
        grad[k] = totals[k]

    dm_rows = jnp.concatenate([jnp.moveaxis(dm_all[:, :, 1].reshape(N_DEV, depth, 9 * d), 0, 1),
                               jnp.pad(dm_tot[:, 0][:, None, :], ((0, 0), (0, MOD_ROWS - N_DEV - 1), (0, 0)))], axis=1)
    dm_cols = lax.dynamic_slice_in_dim(dm_rows, me * wc, wc, axis=2)
    grad_w_ada, gcc = _ada_bwd(cc, dm_cols, w_ada)
    gcc_all = _all_gather(_pack([gcc], F32))
    grad['c_ctx'] = _unpack(_sum_slabs(gcc_all), [(MOD_ROWS, d)])[0][N_DEV]

    res = {}
    small = [k for k in _SMALL if k not in g_gate]
    gs, ws, ms, vs = (_pack([src[k] for k in small], F32).reshape(1, -1, 8 * LANE) for src in (grad, wts, mom, var))
    outs = _adam([gs], ws, ms, vs)
    small_shapes = [wts[k].shape for k in small]
    for k, vals in zip(small, zip(*(_unpack(o[0], small_shapes) for o in outs))):
        res[k] = vals
    res['w_ada'] = _adam([grad_w_ada[l][None] for l in range(depth)], w_ada, m_w_ada, v_w_ada)
    for j, k in enumerate(g_gate):
        flat = (depth, -1, hs)
        outs = _adam([gate_x[j * depth + l] for l in range(depth)], wts[k].reshape(flat), mom[k].reshape(flat),
                     var[k].reshape(flat))
        res[k] = tuple(o.reshape(wts[k].shape) for o in outs)
    for k in _BIG_SHARDED:
        res[k] = _adam(gx[k], wts[k], mom[k], var[k])

    out = [loss, grad_x]
    for j in range(4):
        out += [res[k][j] for k in _WEIGHTS]
    return tuple(out)
```

```python
import math

import jax
import jax.numpy as jnp
from jax import lax
from jax.experimental import pallas as pl
from jax.experimental.pallas import tpu as pltpu

F32 = jnp.float32
BF16 = jnp.bfloat16
N_DEV = 8
MESH = pl.DeviceIdType.MESH
LN_EPS = 1e-6
RG_C = 8.0
SEG = 64
CONV_PAD = 16
ADAM_LR, ADAM_B1, ADAM_B2, ADAM_EPS, ADAM_WD, ADAM_STEP = 0.001, 0.9, 0.999, 1e-08, 0.01, 10
LANE = 128
PACK_QUANTUM = 16 * LANE
PACK_ROWS = 512
MOD_ROWS = 16
GELU_K = 0.7978845608028654
GELU_C = 0.044715

_TM = (768, 640, 512, 384, 256, 128)
_TN = (1024, 512, 256, 128)
_TK = (2816, 1024, 512, 256, 128)
_SUB = (256, 128)


def _cp(sem=None, vmem_mb=40):
    return pltpu.CompilerParams(dimension_semantics=sem, vmem_limit_bytes=vmem_mb * 2 ** 20)


def _pick(n, cands):
    for cand in cands:
        if n % cand == 0:
            return cand
    return n


def _sig(v):
    return 1.0 / (1.0 + jnp.exp(-v))


def _dot(a, b, ca, cb):
    return lax.dot_general(a, b, (((ca,), (cb,)), ((), ())), preferred_element_type=F32)


def _sum0(v):
    return jnp.sum(v, axis=0, keepdims=True)


def _gelu_and_grad(v):
    inner = GELU_K * (v + GELU_C * v * v * v)
    t = jnp.tanh(inner)
    gel = 0.5 * v * (1.0 + t)
    dgel = 0.5 * (1.0 + t) + 0.5 * v * (1.0 - t * t) * GELU_K * (1.0 + 3.0 * GELU_C * v * v)
    return gel, dgel


def _softplus_neg(lam):
    y = jnp.exp(-jnp.abs(lam))
    u = 1.0 + y
    log1p = jnp.where(u == 1.0, y, jnp.log(u) * (y / (u - 1.0)))
    return jnp.maximum(-lam, 0.0) + log1p


def _neg_expm1(z):
    ser = -z * (1.0 + z * (1 / 2 + z * (1 / 6 + z * (1 / 24 + z * (1 / 120 + z * (1 / 720 + z * (1 / 5040)))))))
    return jnp.where(z > -0.25, ser, 1.0 - jnp.exp(z))


def _row_spec(tt, w, col=0):
    return pl.BlockSpec((tt, w), lambda i: (i, col))


def _full_spec(shape):
    nd = len(shape)
    return pl.BlockSpec(shape, lambda *_: (0,) * nd)


def _mod_spec(d):
    return pl.BlockSpec((None, MOD_ROWS, d), lambda i: (jnp.minimum(i, 1), 0, 0))


def _stream_sum_spec(w):
    return pl.BlockSpec((8, w), lambda i: (jnp.minimum(i, 1), 0))


def _acc_rows(ref, init, rows):
    @pl.when(init)
    def _():
        ref[...] = jnp.zeros(ref.shape, F32)

    for k, row in enumerate(rows):
        ref[pl.ds(k, 1), :] += row


def _all_gather(v):
    def body(x_ref, out_ref, send_sems, recv_sems, local_sem):
        x, y, c = lax.axis_index("x"), lax.axis_index("y"), lax.axis_index("c")
        me, sibling = (x, y, c), (x, y, 1 - c)
        chips = [(1 - x, y), (x, 1 - y), (1 - x, 1 - y)]

        def slot(px, py, pc):
            return out_ref.at[4 * px + 2 * py + pc]

        def copy(k, block, to, src=None):
            return pltpu.make_async_remote_copy(
                src_ref=slot(*block) if src is None else src, dst_ref=slot(*block),
                send_sem=send_sems.at[k], recv_sem=recv_sems.at[k], device_id=to, device_id_type=MESH)

        mine = pltpu.make_async_copy(x_ref, slot(*me), local_sem)
        mine.start()
        first = [copy(0, me, sibling, src=x_ref)]
        first += [copy(1 + j, me, (*chip, c), src=x_ref) for j, chip in enumerate(chips)]
        for cp in first:
            cp.start()
        passed = [copy(4 + j, (*chip, c), sibling) for j, chip in enumerate(chips)]
        for j, chip in enumerate(chips):
            copy(1 + j, (*chip, c), me).wait_recv()
            passed[j].start()
        copy(0, sibling, me).wait_recv()
        for j, chip in enumerate(chips):
            copy(4 + j, (*chip, 1 - c), me).wait_recv()
        for cp in first + passed:
            cp.wait_send()
        mine.wait()

    return pl.pallas_call(
        body, name="all_gather",
        out_shape=jax.ShapeDtypeStruct((N_DEV,) + v.shape, v.dtype),
        in_specs=[pl.BlockSpec(memory_space=pltpu.HBM)],
        out_specs=pl.BlockSpec(memory_space=pltpu.HBM),
        scratch_shapes=[pltpu.SemaphoreType.DMA((7,)), pltpu.SemaphoreType.DMA((7,)), pltpu.SemaphoreType.DMA],
    )(v)


_PEER_FLIPS = [(0, 0, 1), (1, 0, 0), (0, 1, 0), (1, 1, 0), (1, 0, 1), (0, 1, 1), (1, 1, 1)]


def _all_to_all(parts):
    n_parts = len(parts)

    def body(*refs):
        x_refs, out_ref = refs[:n_parts], refs[n_parts]
        send_sems, recv_sems, local_sems = refs[n_parts + 1:]
        x, y, c = lax.axis_index("x"), lax.axis_index("y"), lax.axis_index("c")
        me = 4 * x + 2 * y + c
        local, remote = [], []
        for l in range(n_parts):
            cp = pltpu.make_async_copy(x_refs[l].at[me], out_ref.at[l, me], local_sems.at[l])
            cp.start()
            local.append(cp)
            for k, (fx, fy, fc) in enumerate(_PEER_FLIPS):
                px = 1 - x if fx else x
                py = 1 - y if fy else y
                pc = 1 - c if fc else c
                cp = pltpu.make_async_remote_copy(
                    src_ref=x_refs[l].at[4 * px + 2 * py + pc], dst_ref=out_ref.at[l, me],
                    send_sem=send_sems.at[7 * l + k], recv_sem=recv_sems.at[7 * l + k],
                    device_id=(px, py, pc), device_id_type=MESH)
                cp.start()
                remote.append(cp)
        for cp in remote:
            cp.wait()
        for cp in local:
            cp.wait()

    shape = parts[0].shape
    return pl.pallas_call(
        body, name="all_to_all",
        out_shape=jax.ShapeDtypeStruct((n_parts,) + shape, parts[0].dtype),
        in_specs=[pl.BlockSpec(memory_space=pltpu.HBM)] * n_parts,
        out_specs=pl.BlockSpec(memory_space=pltpu.HBM),
        scratch_shapes=[pltpu.SemaphoreType.DMA((7 * n_parts,)), pltpu.SemaphoreType.DMA((7 * n_parts,)),
                        pltpu.SemaphoreType.DMA((n_parts,))],
    )(*parts)


def _pack(arrs, dtype):
    pieces = []
    for a in arrs:
        flat = a.astype(dtype).reshape(-1)
        flat = jnp.pad(flat, (0, (-flat.shape[0]) % PACK_QUANTUM))
        pieces.append(flat.reshape(-1, LANE))
    packed = jnp.concatenate(pieces, axis=0) if len(pieces) > 1 else pieces[0]
    return jnp.pad(packed, ((0, (-packed.shape[0]) % PACK_ROWS), (0, 0)))


def _unpack(packed, shapes, lead=()):
    flat = packed.reshape(lead + (-1,))
    outs, off = [], 0
    for sh in shapes:
        n = math.prod(sh)
        outs.append(flat[..., off:off + n].reshape(lead + tuple(sh)))
        off += n + (-n) % PACK_QUANTUM
    return outs


def _gather_list(arrs, dtype):
    gathered = _all_gather(_pack(arrs, dtype))
    return _unpack(gathered, [a.shape for a in arrs], lead=(N_DEV,))


def _unshard(g, axis):
    y = jnp.moveaxis(g, 0, axis)
    sh = y.shape
    return y.reshape(sh[:axis] + (sh[axis] * sh[axis + 1],) + sh[axis + 2:])


def _my_block(full, me, axis):
    size = full.shape[axis] // N_DEV
    return lax.dynamic_slice_in_dim(full, me * size, size, axis=axis)


def _sum_slabs(v):
    n_slab, rows, width = v.shape
    tr = _pick(rows, (1024, 512, 256, 128, 64, 32, 16, 8))

    def body(v_ref, o_ref):
        acc = v_ref[0]
        for k in range(1, n_slab):
            acc = acc + v_ref[k]
        o_ref[...] = acc

    return pl.pallas_call(
        body, name="sum_slabs", grid=(rows // tr,),
        in_specs=[pl.BlockSpec((n_slab, tr, width), lambda i: (0, i, 0))],
        out_specs=pl.BlockSpec((tr, width), lambda i: (i, 0)),
        out_shape=jax.ShapeDtypeStruct((rows, width), F32),
        compiler_params=_cp(("parallel",)),
    )(v)


N_LOCAL = 1


def _comm_copies(kinds, src_refs, dst_refs, send_sems, recv_sems, local_sems):
    x, y, c = lax.axis_index("x"), lax.axis_index("y"), lax.axis_index("c")
    me = 4 * x + 2 * y + c
    copies = []

    def remote(j, k, src, dst, peer):
        copies.append(pltpu.make_async_remote_copy(
            src_ref=src, dst_ref=dst, send_sem=send_sems.at[7 * j + k], recv_sem=recv_sems.at[7 * j + k],
            device_id=peer, device_id_type=MESH))

    def local(j, k, src, dst):
        copies.append(pltpu.make_async_copy(src, dst, local_sems.at[N_LOCAL * j + k]))

    for j, kind in enumerate(kinds):
        src, dst = src_refs[j], dst_refs[j]
        peers = [(1 - x if fx else x, 1 - y if fy else y, 1 - c if fc else c) for fx, fy, fc in _PEER_FLIPS]
        if kind == "scatter":
            local(j, 0, src.at[me], dst.at[me])
            for k, (px, py, pc) in enumerate(peers):
                remote(j, k, src.at[4 * px + 2 * py + pc], dst.at[me], (px, py, pc))
        elif kind == "spread":
            local(j, 0, src, dst.at[me])
            for k, peer in enumerate(peers[:4]):
                remote(j, k, src, dst.at[me], peer)
        else:
            for k, (px, py, pc) in enumerate(peers[1:4]):
                remote(j, 4 + k, dst.at[4 * px + 2 * py + pc], dst.at[4 * px + 2 * py + pc], peers[0])
    return copies


def _mm(name, grid, nk, ins, in_specs, out_shape, out_specs, acc_shape, prod, fin, comm=()):
    n_in, n_out, n_c = len(ins), len(out_shape), len(comm)
    kinds = [kind for kind, _ in comm]
    srcs = [a for _, a in comm]
    dsts = [jax.ShapeDtypeStruct(((N_DEV,) + a.shape) if kind == "spread" else a.shape, a.dtype) for kind, a in comm]
    hbm = pl.BlockSpec(memory_space=pltpu.HBM)

    def body(*refs):
        in_refs, refs = refs[:n_in], refs[n_in:]
        src_refs, refs = refs[:n_c], refs[n_c:]
        out_refs, refs = refs[:n_out], refs[n_out:]
        dst_refs, scratch = refs[:n_c], refs[n_c:]
        if nk > 1:
            acc, scratch = scratch[0], scratch[1:]
        ids = [pl.program_id(a) for a in range(3)]
        if n_c:
            copies = _comm_copies(kinds, src_refs, dst_refs, *scratch)

            @pl.when(jnp.logical_and(ids[0] == 0, jnp.logical_and(ids[1] == 0, ids[2] == 0)))
            def _():
                for cp in copies:
                    cp.start()

        if nk == 1:
            fin(prod(*in_refs), in_refs, out_refs)
        else:
            @pl.when(ids[2] == 0)
            def _():
                acc[...] = jnp.zeros(acc.shape, F32)

            for idx, val in enumerate(prod(*in_refs)):
                acc[idx] += val

            @pl.when(ids[2] == nk - 1)
            def _():
                fin(tuple(acc[idx] for idx in range(acc_shape[0])), in_refs, out_refs)

        if n_c:
            @pl.when(jnp.logical_and(ids[0] == grid[0] - 1, jnp.logical_and(ids[1] == grid[1] - 1, ids[2] == nk - 1)))
            def _():
                for cp in copies:
                    cp.wait()

    scratch = [pltpu.VMEM(acc_shape, F32)] if nk > 1 else []
    if n_c:
        scratch += [pltpu.SemaphoreType.DMA((7 * n_c,)), pltpu.SemaphoreType.DMA((7 * n_c,)),
                    pltpu.SemaphoreType.DMA((N_LOCAL * n_c,))]
    sem = ("arbitrary",) * 3 if n_c else ("parallel", "parallel", "arbitrary")
    res = pl.pallas_call(
        body, name=name + (f"_x{n_c}" if n_c else ""), grid=grid,
        in_specs=list(in_specs) + [hbm] * n_c, out_specs=tuple(out_specs) + (hbm,) * n_c,
        out_shape=tuple(out_shape) + tuple(dsts), scratch_shapes=scratch,
        input_output_aliases={n_in + j: n_out + j for j, kind in enumerate(kinds) if kind == "relay"},
        compiler_params=_cp(sem, 48),
    )(*ins, *srcs)
    return res[:n_out], list(res[n_out:])


def _mm_nn(name, a, b, bias=None, comm=()):
    m, kk = a.shape
    nn = b.shape[1]
    tm, tn = _pick(m, _TM), _pick(nn, _TN)
    tk = kk if kk <= 2048 else _pick(kk, _TK)
    nk = kk // tk
    ins = [a, b]
    specs = [pl.BlockSpec((tm, tk), lambda i, j, k: (i, k)), pl.BlockSpec((tk, tn), lambda i, j, k: (k, j))]
    if bias is not None:
        ins.append(bias)
        specs.append(pl.BlockSpec((1, tn), lambda i, j, k: (0, j)))

    def prod(a_ref, b_ref, *_):
        return (_dot(a_ref[...], b_ref[...], 1, 0),)

    def fin(vals, in_refs, out_refs):
        out_refs[0][...] = vals[0] if bias is None else vals[0] + in_refs[2][...]

    outs, got = _mm(name, (m // tm, nn // tn, nk), nk, ins, specs,
                    (jax.ShapeDtypeStruct((m, nn), F32),), (pl.BlockSpec((tm, tn), lambda i, j, k: (i, j)),),
                    (1, tm, tn), prod, fin, comm)
    return outs[0], got


def _mm_swiglu(h, w1, comm=()):
    m, kk = h.shape
    f = w1.shape[1] // 2
    tm, tn = _pick(m, _TM), _pick(f, (512, 256, 128))
    nj = f // tn
    sub = _pick(tm, _SUB)

    def prod(*_):
        return ()

    def fin(_, in_refs, out_refs):
        h_ref, wg_ref, wu_ref = in_refs
        for q in range(tm // sub):
            rows = pl.ds(q * sub, sub)
            hq = h_ref[rows, :]
            zg, zu = _dot(hq, wg_ref[...], 1, 0), _dot(hq, wu_ref[...], 1, 0)
            out_refs[0][0, rows, :] = zg.astype(BF16)
            out_refs[0][1, rows, :] = zu.astype(BF16)
            out_refs[1][rows, :] = (zg * _sig(zg) * zu).astype(BF16)

    return _mm("ffn_in", (m // tm, nj, 1), 1, [h, w1, w1],
               [pl.BlockSpec((tm, kk), lambda i, j, k: (i, 0)),
                pl.BlockSpec((kk, tn), lambda i, j, k: (0, j)),
                pl.BlockSpec((kk, tn), lambda i, j, k: (0, j + nj))],
               (jax.ShapeDtypeStruct((2, m, f), BF16), jax.ShapeDtypeStruct((m, f), BF16)),
               (pl.BlockSpec((2, tm, tn), lambda i, j, k: (0, i, j)), pl.BlockSpec((tm, tn), lambda i, j, k: (i, j))),
               (2, tm, tn), prod, fin, comm)


def _mm_nt_swiglu_bwd(dfb, w2, z, comm=()):
    m, kk = dfb.shape
    f = w2.shape[0]
    tm, tn = _pick(m, _TM), _pick(f, (512, 256, 128))
    sub = _pick(tm, _SUB)

    def prod(*_):
        return ()

    def fin(_, in_refs, out_refs):
        a_ref, b_ref, z_ref = in_refs
        for q in range(tm // sub):
            rows = pl.ds(q * sub, sub)
            da = _dot(a_ref[rows, :], b_ref[...], 1, 1)
            zg, zu = z_ref[0, rows, :].astype(F32), z_ref[1, rows, :].astype(F32)
            sg = _sig(zg)
            out_refs[0][0, rows, :] = (da * zu * (sg * (1.0 + zg * (1.0 - sg)))).astype(BF16)
            out_refs[0][1, rows, :] = (da * (zg * sg)).astype(BF16)

    outs, got = _mm("ffn_out_dx", (m // tm, f // tn, 1), 1, [dfb, w2, z],
                    [pl.BlockSpec((tm, kk), lambda i, j, k: (i, 0)),
                     pl.BlockSpec((tn, kk), lambda i, j, k: (j, 0)),
                     pl.BlockSpec((2, tm, tn), lambda i, j, k: (0, i, j))],
                    (jax.ShapeDtypeStruct((2, m, f), BF16),),
                    (pl.BlockSpec((2, tm, tn), lambda i, j, k: (0, i, j)),),
                    (1, tm, tn), prod, fin, comm)
    return outs[0], got


def _mm_nt(name, a, b, a_halves=False, comm=()):
    nn, kk = b.shape
    m = a.shape[1] if a_halves else a.shape[0]
    kh = kk // 2 if a_halves else kk
    tm, tn = _pick(m, _TM), _pick(nn, _TN)
    tk = kh if (kh <= 2048 and not a_halves) else _pick(kh, _TK)
    nk, nkh = kk // tk, kh // tk
    if a_halves:
        a_spec = pl.BlockSpec((None, tm, tk), lambda i, j, k: (k // nkh, i, k % nkh))
    else:
        a_spec = pl.BlockSpec((tm, tk), lambda i, j, k: (i, k))

    def prod(a_ref, b_ref):
        return (_dot(a_ref[...], b_ref[...], 1, 1),)

    def fin(vals, in_refs, out_refs):
        out_refs[0][...] = vals[0]

    outs, got = _mm(name, (m // tm, nn // tn, nk), nk, [a, b],
                    [a_spec, pl.BlockSpec((tn, tk), lambda i, j, k: (j, k))],
                    (jax.ShapeDtypeStruct((m, nn), F32),), (pl.BlockSpec((tm, tn), lambda i, j, k: (i, j)),),
                    (1, tm, tn), prod, fin, comm)
    return outs[0], got


def _mm_tn(name, a, b, b_halves=False, shard_cols=False, comm=()):
    kt, m = a.shape
    nn = 2 * b.shape[2] if b_halves else b.shape[1]
    tk, tm = _pick(kt, _TM), _pick(m, (1408, 1024, 512, 256, 128))
    shard = nn // N_DEV
    per = 1
    while shard_cols and 2 * per * shard <= nn // 2 and ((per * shard) % 256 or per * shard < 1024):
        per *= 2
    tn = per * shard if shard_cols else _pick(nn, _TN)
    njh = (nn // 2) // tn if b_halves else 0
    if b_halves:
        b_spec = pl.BlockSpec((None, tk, tn), lambda i, j, k: (j // njh, k, j % njh))
    else:
        b_spec = pl.BlockSpec((tk, tn), lambda i, j, k: (k, j))
    if shard_cols:
        out_shape = jax.ShapeDtypeStruct((N_DEV, m, shard), BF16)
        out_spec = pl.BlockSpec((per, tm, shard), lambda i, j, k: (j, i, 0))
    else:
        out_shape = jax.ShapeDtypeStruct((m, nn), BF16)
        out_spec = pl.BlockSpec((tm, tn), lambda i, j, k: (i, j))

    def prod(a_ref, b_ref):
        return (_dot(a_ref[...], b_ref[...], 0, 0),)

    def fin(vals, in_refs, out_refs):
        r = vals[0].astype(BF16)
        if shard_cols:
            for q in range(per):
                out_refs[0][q] = r[:, q * shard:(q + 1) * shard]
        else:
            out_refs[0][...] = r

    outs, got = _mm(name, (m // tm, nn // tn, kt // tk), kt // tk, [a, b],
                    [pl.BlockSpec((tk, tm), lambda i, j, k: (k, i)), b_spec],
                    (out_shape,), (out_spec,), (1, tm, tn), prod, fin, comm)
    return outs[0], got


def _modulate(s, modtab, i_shift, i_scale, tt):
    n, d = s.shape

    def body(s_ref, m_ref, o_ref):
        shift, scale = m_ref[pl.ds(i_shift, 1), :], m_ref[pl.ds(i_scale, 1), :]
        o_ref[...] = (s_ref[...] * (1.0 + scale) + shift).astype(BF16)

    return pl.pallas_call(
        body, name="modulate", grid=(n // tt,),
        in_specs=[_row_spec(tt, d), _mod_spec(d)], out_specs=_row_spec(tt, d),
        out_shape=jax.ShapeDtypeStruct((n, d), BF16), compiler_params=_cp(("parallel",)),
    )(s, modtab)


def _res_ln(s, f, modtab, i_gate, coef, gam, bet, alpha, tt):
    n, d = s.shape

    def body(s_ref, f_ref, m_ref, g_ref, b_ref, r_ref, o_ref):
        gate = m_ref[pl.ds(i_gate, 1), :]
        r = alpha * s_ref[...] + (coef * gate) * f_ref[...]
        r_ref[...] = r
        mu = jnp.mean(r, axis=-1, keepdims=True)
        xc = r - mu
        var = jnp.mean(xc * xc, axis=-1, keepdims=True)
        o_ref[...] = xc * lax.rsqrt(var + LN_EPS) * g_ref[...] + b_ref[...]

    return pl.pallas_call(
        body, name="res_ln", grid=(n // tt,),
        in_specs=[_row_spec(tt, d), _row_spec(tt, d), _mod_spec(d), _full_spec((1, d)), _full_spec((1, d))],
        out_specs=(_row_spec(tt, d), _row_spec(tt, d)),
        out_shape=(jax.ShapeDtypeStruct((n, d), F32), jax.ShapeDtypeStruct((n, d), F32)),
        compiler_params=_cp(("parallel",)),
    )(s, f, modtab, gam, bet)


def _ln_bwd(do, r, f, modtab, i_gate, coef, gam, tt):
    n, d = do.shape

    def body(do_ref, r_ref, f_ref, m_ref, g_ref, dr_ref, dfb_ref, tot_ref, str_ref):
        i = pl.program_id(0)
        r = r_ref[...]
        mu = jnp.mean(r, axis=-1, keepdims=True)
        xc = r - mu
        rstd = lax.rsqrt(jnp.mean(xc * xc, axis=-1, keepdims=True) + LN_EPS)
        xh = xc * rstd
        dout = do_ref[...]
        dxh = dout * g_ref[...]
        m1 = jnp.mean(dxh, axis=-1, keepdims=True)
        m2 = jnp.mean(dxh * xh, axis=-1, keepdims=True)
        dr = rstd * (dxh - m1 - xh * m2)
        dr_ref[...] = dr
        dfb = (coef * m_ref[pl.ds(i_gate, 1), :]) * dr
        dfb_ref[...] = dfb.astype(BF16)
        _acc_rows(tot_ref, i == 0, [_sum0(dout * xh), _sum0(dout), _sum0(dfb)])
        _acc_rows(str_ref, i <= 1, [_sum0(coef * f_ref[...] * dr)])

    return pl.pallas_call(
        body, name="ln_bwd", grid=(n // tt,),
        in_specs=[_row_spec(tt, d), _row_spec(tt, d), _row_spec(tt, d), _mod_spec(d), _full_spec((1, d))],
        out_specs=(_row_spec(tt, d), _row_spec(tt, d), _full_spec((8, d)), _stream_sum_spec(d)),
        out_shape=(jax.ShapeDtypeStruct((n, d), F32), jax.ShapeDtypeStruct((n, d), BF16),
                   jax.ShapeDtypeStruct((8, d), F32), jax.ShapeDtypeStruct((16, d), F32)),
        compiler_params=_cp(("arbitrary",)),
    )(do, r, f, modtab, gam)


def _mod_bwd(dr, dh, s, modtab, i_scale, alpha, tt):
    n, d = dr.shape

    def body(dr_ref, dh_ref, s_ref, m_ref, o_ref, str_ref):
        i = pl.program_id(0)
        dh_v = dh_ref[...]
        o_ref[...] = alpha * dr_ref[...] + dh_v * (1.0 + m_ref[pl.ds(i_scale, 1), :])
        _acc_rows(str_ref, i <= 1, [_sum0(dh_v * s_ref[...]), _sum0(dh_v)])

    return pl.pallas_call(
        body, name="mod_bwd", grid=(n // tt,),
        in_specs=[_row_spec(tt, d), _row_spec(tt, d), _row_spec(tt, d), _mod_spec(d)],
        out_specs=(_row_spec(tt, d), _stream_sum_spec(d)),
        out_shape=(jax.ShapeDtypeStruct((n, d), F32), jax.ShapeDtypeStruct((16, d), F32)),
        compiler_params=_cp(("arbitrary",)),
    )(dr, dh, s, modtab)


def _loss_head(s, target, tt):
    n, d = s.shape
    nt = n // tt

    def body(s_ref, t_ref, do_ref, loss_ref, acc):
        i = pl.program_id(0)

        @pl.when(i == 0)
        def _():
            acc[...] = jnp.zeros(acc.shape, F32)
            do_ref[...] = jnp.zeros(do_ref.shape, F32)

        @pl.when(i > 0)
        def _():
            err = s_ref[...] - t_ref[...]
            do_ref[...] = err / d
            acc[...] += jnp.sum((err * err).reshape(tt // 8, 8, d), axis=0)

        @pl.when(i == nt - 1)
        def _():
            loss_ref[...] = jnp.full(loss_ref.shape, jnp.sum(acc[...]) * (0.5 / d), F32)

    return pl.pallas_call(
        body, name="loss_head", grid=(nt,),
        in_specs=[_row_spec(tt, d), pl.BlockSpec((tt, d), lambda i: (jnp.maximum(i - 1, 0), 0))],
        out_specs=(_row_spec(tt, d), _full_spec((8, LANE))),
        out_shape=(jax.ShapeDtypeStruct((n, d), F32), jax.ShapeDtypeStruct((8, LANE), F32)),
        scratch_shapes=[pltpu.VMEM((8, d), F32)],
        compiler_params=_cp(("arbitrary",)),
    )(s, target)


def _halo_specs(tt, c, n):
    nb, last = tt // 8, n // 8 - 1
    return [pl.BlockSpec((tt, c), lambda i: (i, 0)),
            pl.BlockSpec((8, c), lambda i: (jnp.maximum(i * nb - 1, 0), 0)),
            pl.BlockSpec((8, c), lambda i: (jnp.minimum((i + 1) * nb, last), 0))]


def _fill_halo(scr, main_ref, prev_ref, next_ref, i, nt, tt):
    has_prev = i >= 2
    has_next = jnp.logical_and(i >= 1, i < nt - 1)
    scr[pl.ds(0, 8), :] = jnp.where(has_prev, prev_ref[...], 0.0)
    scr[pl.ds(8, tt), :] = main_ref[...]
    scr[pl.ds(8 + tt, 8), :] = jnp.where(has_next, next_ref[...], 0.0)


def _conv4(p, w, b, tt, c):
    n = p.shape[0]
    nt, taps = n // tt, w.shape[0]
    left = taps // 2

    def body(x_ref, xp_ref, xn_ref, w_ref, b_ref, o_ref, scr):
        _fill_halo(scr, x_ref, xp_ref, xn_ref, pl.program_id(0), nt, tt)
        acc = jnp.broadcast_to(b_ref[...], (tt, c))
        for k in range(taps):
            acc = acc + w_ref[pl.ds(k, 1), :] * scr[pl.ds(8 + k - left, tt), :]
        o_ref[...] = acc

    return pl.pallas_call(
        body, name="conv4", grid=(nt,),
        in_specs=_halo_specs(tt, c, n) + [_full_spec(w.shape), _full_spec((1, c))],
        out_specs=_row_spec(tt, c), out_shape=jax.ShapeDtypeStruct((n, c), F32),
        scratch_shapes=[pltpu.VMEM((tt + 16, c), F32)], compiler_params=_cp(("parallel",)),
    )(p, p, p, w, b)


def _conv4_bwd(dxr, p, w, tt, c):
    n = p.shape[0]
    nt, taps = n // tt, w.shape[0]
    left = taps // 2

    def body(d_ref, dp_ref, dn_ref, x_ref, xp_ref, xn_ref, w_ref, o_ref, sum_ref, dscr, xscr):
        i = pl.program_id(0)
        _fill_halo(dscr, d_ref, dp_ref, dn_ref, i, nt, tt)
        _fill_halo(xscr, x_ref, xp_ref, xn_ref, i, nt, tt)
        acc = jnp.zeros((tt, c), F32)
        for k in range(taps):
            acc = acc + w_ref[pl.ds(k, 1), :] * dscr[pl.ds(8 - (k - left), tt), :]
        o_ref[...] = acc.astype(BF16)
        d = d_ref[...]
        rows = [_sum0(d * xscr[pl.ds(8 + k - left, tt), :]) for k in range(taps)] + [_sum0(d)]
        _acc_rows(sum_ref, i == 0, rows)

    return pl.pallas_call(
        body, name="conv4_bwd", grid=(nt,),
        in_specs=_halo_specs(tt, c, n) + _halo_specs(tt, c, n) + [_full_spec(w.shape)],
        out_specs=(_row_spec(tt, c), _full_spec((8, c))),
        out_shape=(jax.ShapeDtypeStruct((n, c), BF16), jax.ShapeDtypeStruct((8, c), F32)),
        scratch_shapes=[pltpu.VMEM((tt + 16, c), F32), pltpu.VMEM((tt + 16, c), F32)],
        compiler_params=_cp(("arbitrary",)),
    )(dxr, dxr, dxr, p, p, p, w)


def _head_gates(xv, wr, wi, br, bi, lam):
    xb = xv.astype(BF16)
    r = _sig(_dot(xb, wr, 1, 0) + br)
    ig = _sig(_dot(xb, wi, 1, 0) + bi)
    sp = _softplus_neg(lam)
    log_a = (-RG_C) * r * sp
    return r, ig, sp, jnp.exp(log_a), jnp.sqrt(_neg_expm1(2.0 * log_a))


def _tile_scan(a, b, increasing, tt, rows):
    s = 1
    while s < tt:
        if increasing:
            a_sh, b_sh, ok = pltpu.roll(a, s, 0), pltpu.roll(b, s, 0), rows >= s
        else:
            a_sh, b_sh, ok = pltpu.roll(a, tt - s, 0), pltpu.roll(b, tt - s, 0), rows < tt - s
        b = jnp.where(ok, a * b_sh + b, b)
        a = jnp.where(ok, a * a_sh, a)
        s *= 2
    return a, b


def _scan_tile_index(step, nt, reverse):
    return jnp.where(step == 0, 0, nt - step) if reverse else step


def _scan(xr, wr, wi, br, bi, lam, reverse, tt):
    n, c = xr.shape
    nt = n // tt
    heads, hs = wr.shape[0], wr.shape[2]
    lc = min(c, LANE)

    def tile(i):
        return _scan_tile_index(i, nt, reverse)

    def body(x_ref, wr_ref, wi_ref, br_ref, bi_ref, lam_ref, h_ref, hin_ref, a_scr, b_scr, carry):
        i = pl.program_id(0)

        @pl.when(i == 0)
        def _():
            carry[...] = jnp.zeros(carry.shape, F32)

        for hd in range(heads):
            sl = slice(hd * hs, (hd + 1) * hs)
            xv = x_ref[:, sl]
            _, ig, _, a, sq = _head_gates(xv, wr_ref[hd], wi_ref[hd], br_ref[:, sl], bi_ref[:, sl], lam_ref[:, sl])
            a_scr[:, sl] = a
            b_scr[:, sl] = sq * (ig * xv)
        hin_ref[...] = jnp.broadcast_to(carry[...], (8, c))
        rows = lax.broadcasted_iota(jnp.int32, (tt, lc), 0)
        for ch in range(c // lc):
            sl = slice(ch * lc, (ch + 1) * lc)
            big_a, big_b = _tile_scan(a_scr[:, sl], b_scr[:, sl], not reverse, tt, rows)
            h_ref[:, sl] = big_a * carry[:, sl] + big_b
        carry[...] = h_ref[pl.ds(0 if reverse else tt - 1, 1), :]

    return pl.pallas_call(
        body, name="lru_scan", grid=(nt,),
        in_specs=[pl.BlockSpec((tt, c), lambda i: (tile(i), 0)), _full_spec(wr.shape), _full_spec(wi.shape),
                  _full_spec((1, c)), _full_spec((1, c)), _full_spec((1, c))],
        out_specs=(pl.BlockSpec((tt, c), lambda i: (tile(i), 0)), pl.BlockSpec((None, 8, c), lambda i: (tile(i), 0, 0))),
        out_shape=(jax.ShapeDtypeStruct((n, c), F32), jax.ShapeDtypeStruct((nt, 8, c), F32)),
        scratch_shapes=[pltpu.VMEM((tt, c), F32), pltpu.VMEM((tt, c), F32), pltpu.VMEM((1, c), F32)],
        compiler_params=_cp(("arbitrary",)),
    )(xr, wr, wi, br, bi, lam)


def _scan_bwd(xr, dh, h, hin, wr, wi, br, bi, lam, reverse, tt, dx_prev=None):
    n, c = xr.shape
    nt = n // tt
    heads, hs = wr.shape[0], wr.shape[2]
    lc = min(c, LANE)
    inc = not reverse
    first, last = (0, tt - 1) if inc else (tt - 1, 0)

    def tile(i):
        return _scan_tile_index(nt - 1 - i, nt, reverse)

    def body(*refs):
        x_ref, dh_ref, h_ref, hin_ref, wr_ref, wi_ref, br_ref, bi_ref, lam_ref = refs[:9]
        refs = refs[9:]
        if dx_prev is not None:
            dxp_ref, refs = refs[0], refs[1:]
        dx_ref, dwr_ref, dwi_ref, sum_ref, a_scr, r_scr, i_scr, sq_scr, g_scr, da_scr, u_scr, ucarry = refs
        i = pl.program_id(0)

        @pl.when(i == 0)
        def _():
            ucarry[...] = jnp.zeros(ucarry.shape, F32)
            dwr_ref[...] = jnp.zeros(dwr_ref.shape, F32)
            dwi_ref[...] = jnp.zeros(dwi_ref.shape, F32)
            sum_ref[...] = jnp.zeros(sum_ref.shape, F32)

        for hd in range(heads):
            sl = slice(hd * hs, (hd + 1) * hs)
            r, ig, _, a, sq = _head_gates(x_ref[:, sl], wr_ref[hd], wi_ref[hd], br_ref[:, sl], bi_ref[:, sl],
                                          lam_ref[:, sl])
            a_scr[:, sl], r_scr[:, sl], i_scr[:, sl], sq_scr[:, sl] = a, r, ig, sq

        rows = lax.broadcasted_iota(jnp.int32, (tt, lc), 0)
        to_prev = 1 if inc else tt - 1
        to_next = tt - 1 if inc else 1
        for ch in range(c // lc):
            sl = slice(ch * lc, (ch + 1) * lc)
            a, dhv = a_scr[:, sl], dh_ref[:, sl]
            big_a, big_b = _tile_scan(a, a * dhv, not inc, tt, rows)
            u_in = ucarry[:, sl]
            u = big_a * u_in + big_b
            u_scr[:, sl] = u
            g = dhv + jnp.where(rows == last, u_in, pltpu.roll(u, to_next, 0))
            g_scr[:, sl] = g
            h_prev = jnp.where(rows == first, hin_ref[pl.ds(0, 1), sl], pltpu.roll(h_ref[:, sl], to_prev, 0))
            da_scr[:, sl] = g * h_prev
        ucarry[...] = u_scr[pl.ds(first, 1), :]

        for hd in range(heads):
            sl = slice(hd * hs, (hd + 1) * hs)
            xv, a, r, ig, sq = x_ref[:, sl], a_scr[:, sl], r_scr[:, sl], i_scr[:, sl], sq_scr[:, sl]
            g, lam_v = g_scr[:, sl], lam_ref[:, sl]
            sp = _softplus_neg(lam_v)
            d_sq = g * ig * xv
            d_em = d_sq * 0.5 / sq
            d_log_a = (da_scr[:, sl] - 2.0 * a * d_em) * a
            dzr = (d_log_a * ((-RG_C) * sp)) * r * (1.0 - r)
            dzi = (g * sq * xv) * ig * (1.0 - ig)
            dzr_b, dzi_b, xb = dzr.astype(BF16), dzi.astype(BF16), xv.astype(BF16)
            dx = g * sq * ig + _dot(dzr_b, wr_ref[hd], 1, 1) + _dot(dzi_b, wi_ref[hd], 1, 1)
            if dx_prev is not None:
                dx = dx + dxp_ref[:, sl]
            dx_ref[:, sl] = dx
            dwr_ref[hd] += _dot(xb, dzr_b, 0, 0)
            dwi_ref[hd] += _dot(xb, dzi_b, 0, 0)
            sum_ref[pl.ds(0, 1), sl] += _sum0(dzr)
            sum_ref[pl.ds(1, 1), sl] += _sum0(dzi)
            sum_ref[pl.ds(2, 1), sl] += _sum0(d_log_a * ((-RG_C) * r)) * (-_sig(-lam_v))

    tile_spec = pl.BlockSpec((tt, c), lambda i: (tile(i), 0))
    ins = [xr, dh, h, hin, wr, wi, br, bi, lam]
    in_specs = [tile_spec, tile_spec, tile_spec, pl.BlockSpec((None, 8, c), lambda i: (tile(i), 0, 0)),
                _full_spec(wr.shape), _full_spec(wi.shape), _full_spec((1, c)), _full_spec((1, c)), _full_spec((1, c))]
    if dx_prev is not None:
        ins.append(dx_prev)
        in_specs.append(tile_spec)
    return pl.pallas_call(
        body, name="lru_scan_bwd", grid=(nt,), in_specs=in_specs,
        out_specs=(tile_spec, _full_spec(wr.shape), _full_spec(wi.shape), _full_spec((8, c))),
        out_shape=(jax.ShapeDtypeStruct((n, c), F32), jax.ShapeDtypeStruct(wr.shape, F32),
                   jax.ShapeDtypeStruct(wi.shape, F32), jax.ShapeDtypeStruct((8, c), F32)),
        scratch_shapes=[pltpu.VMEM((tt, c), F32)] * 7 + [pltpu.VMEM((1, c), F32)],
        compiler_params=_cp(("arbitrary",)),
    )(*ins)


def _conv_rows(nseg, seg):
    return nseg * (seg + CONV_PAD) + CONV_PAD


def _seg_base(s, seg):
    return CONV_PAD + s * (seg + CONV_PAD)


def _zero_gaps(scr, nseg, seg, c):
    for s in range(nseg + 1):
        scr[pl.ds(s * (seg + CONV_PAD), CONV_PAD), :] = jnp.zeros((CONV_PAD, c), F32)


def _build_shifts(scr, rot, n_rows, sl):
    for b in range(1, 8):
        rot[b - 1, pl.ds(0, n_rows - 8), :] = scr[pl.ds(b, n_rows - 8), sl]


def _tap(scr, rot, base, off, seg, sl):
    a, b = divmod(off, 8)
    if b == 0:
        return scr[pl.ds(base + 8 * a, seg), sl]
    return rot[b - 1, pl.ds(base + 8 * a, seg), :]


def _conv_chunks(nseg, seg):
    return [(s * seg + q * SEG, _seg_base(s, seg) + q * SEG) for s in range(nseg) for q in range(seg // SEG)]


def _conv_scratch(tt, c):
    rows = max(_conv_rows(1, tt), _conv_rows(tt // SEG, SEG))
    return pltpu.VMEM((rows, c), F32), pltpu.VMEM((7, rows, min(c, LANE)), F32)


def _ln_stats(v):
    mu = jnp.mean(v, axis=-1, keepdims=True)
    xc = v - mu
    rstd = lax.rsqrt(jnp.mean(xc * xc, axis=-1, keepdims=True) + LN_EPS)
    return xc * rstd, rstd


def _mix_mid(p, h_f, h_b, w31, b31, clg, clb, tt):
    n, c = h_f.shape
    taps = w31.shape[0]
    lc = min(c, LANE)

    half = taps // 2

    def body(gr_ref, cv_ref, cg_ref, hf_ref, hb_ref, w_ref, b_ref, g_ref, bb_ref, y_ref, uc_ref, scr, rot):
        i = pl.program_id(0)
        gel, _ = _gelu_and_grad(gr_ref[...])
        y_ref[:, 0:c] = ((hf_ref[...] + hb_ref[...]) * gel).astype(BF16)

        def conv(nseg, seg):
            _zero_gaps(scr, nseg, seg, c)
            for s in range(nseg):
                rows = pl.ds(s * seg, seg)
                scr[pl.ds(_seg_base(s, seg), seg), :] = cv_ref[rows, :] * _sig(cg_ref[rows, :])
            for ch in range(c // lc):
                sl = slice(ch * lc, (ch + 1) * lc)
                _build_shifts(scr, rot, _conv_rows(nseg, seg), sl)
                for out0, base in _conv_chunks(nseg, seg):
                    acc = jnp.broadcast_to(b_ref[:, sl], (SEG, lc))
                    for k in range(taps):
                        acc = acc + _tap(scr, rot, base, k - half, SEG, sl) * w_ref[pl.ds(k, 1), sl]
                    uc_ref[pl.ds(out0, SEG), sl] = acc

        @pl.when(i == 0)
        def _():
            conv(1, tt)

        @pl.when(i > 0)
        def _():
            conv(tt // SEG, SEG)

        xh, _ = _ln_stats(uc_ref[...])
        v = xh * g_ref[...] + bb_ref[...]
        y_ref[:, c:2 * c] = (v * _sig(v)).astype(BF16)

    return pl.pallas_call(
        body, name="mix_mid", grid=(n // tt,),
        in_specs=[_row_spec(tt, c, 1), _row_spec(tt, c, 2), _row_spec(tt, c, 3), _row_spec(tt, c), _row_spec(tt, c),
                  _full_spec(w31.shape), _full_spec((1, c)), _full_spec((1, c)), _full_spec((1, c))],
        out_specs=(_row_spec(tt, 2 * c), _row_spec(tt, c)),
        out_shape=(jax.ShapeDtypeStruct((n, 2 * c), BF16), jax.ShapeDtypeStruct((n, c), F32)),
        scratch_shapes=list(_conv_scratch(tt, c)),
        compiler_params=_cp(("parallel",)),
    )(p, p, p, h_f, h_b, w31, b31, clg, clb)


def _mix_mid_bwd(dymix, p, h_f, h_b, uc, w31, clg, clb, tt):
    n, c = h_f.shape
    taps = w31.shape[0]
    lc = min(c, LANE)
    sum_rows = 8 * ((taps + 3 + 7) // 8)

    half = taps // 2
    nt = n // tt

    def body(dyr_ref, dyc_ref, gr_ref, cv_ref, cg_ref, hf_ref, hb_ref, uc_ref, w_ref, g_ref, bb_ref,
             dh_ref, dp_ref, sum_ref, uscr, urot, dscr, drot, duc_scr, wacc):
        i = pl.program_id(0)

        @pl.when(i == 0)
        def _():
            sum_ref[...] = jnp.zeros(sum_ref.shape, F32)
            wacc[...] = jnp.zeros(wacc.shape, F32)

        dyr = dyr_ref[...]
        gel, dgel = _gelu_and_grad(gr_ref[...])
        dh_ref[...] = dyr * gel
        dp_ref[:, 0:c] = (dyr * (hf_ref[...] + hb_ref[...]) * dgel).astype(BF16)

        xh, rstd = _ln_stats(uc_ref[...])
        v = xh * g_ref[...] + bb_ref[...]
        sg = _sig(v)
        dv = dyc_ref[...] * (sg * (1.0 + v * (1.0 - sg)))
        dxh = dv * g_ref[...]
        m1 = jnp.mean(dxh, axis=-1, keepdims=True)
        m2 = jnp.mean(dxh * xh, axis=-1, keepdims=True)
        duc = rstd * (dxh - m1 - xh * m2)
        sum_ref[pl.ds(taps, 1), :] += _sum0(duc)
        sum_ref[pl.ds(taps + 1, 1), :] += _sum0(dv * xh)
        sum_ref[pl.ds(taps + 2, 1), :] += _sum0(dv)

        duc_scr[...] = duc

        def conv_bwd(nseg, seg):
            n_rows = _conv_rows(nseg, seg)
            _zero_gaps(dscr, nseg, seg, c)
            _zero_gaps(uscr, nseg, seg, c)
            for s in range(nseg):
                rows = pl.ds(s * seg, seg)
                dscr[pl.ds(_seg_base(s, seg), seg), :] = duc_scr[rows, :]
                uscr[pl.ds(_seg_base(s, seg), seg), :] = cv_ref[rows, :] * _sig(cg_ref[rows, :])
            for ch in range(c // lc):
                sl = slice(ch * lc, (ch + 1) * lc)
                _build_shifts(dscr, drot, n_rows, sl)
                _build_shifts(uscr, urot, n_rows, sl)
                chunks = _conv_chunks(nseg, seg)
                for out0, base in chunks:
                    rows = pl.ds(out0, SEG)
                    du = jnp.zeros((SEG, lc), F32)
                    for k in range(taps):
                        du = du + _tap(dscr, drot, base, k - half, SEG, sl) * w_ref[pl.ds(taps - 1 - k, 1), sl]
                    sgc, cv = _sig(cg_ref[rows, sl]), cv_ref[rows, sl]
                    dp_ref[rows, c + sl.start:c + sl.stop] = (du * sgc).astype(BF16)
                    dp_ref[rows, 2 * c + sl.start:2 * c + sl.stop] = (du * cv * sgc * (1.0 - sgc)).astype(BF16)
                for g0 in range(0, len(chunks), 4):
                    group = chunks[g0:g0 + 4]
                    duc_q = [duc_scr[pl.ds(out0, SEG), sl] for out0, _ in group]
                    for k in range(taps):
                        part = jnp.zeros((8, lc), F32)
                        for dq, (_, base) in zip(duc_q, group):
                            prod = dq * _tap(uscr, urot, base, k - half, SEG, sl)
                            part = part + jnp.sum(prod.reshape(SEG // 8, 8, lc), axis=0)
                        wacc[pl.ds(8 * k, 8), sl] += part

        @pl.when(i == 0)
        def _():
            conv_bwd(1, tt)

        @pl.when(i > 0)
        def _():
            conv_bwd(tt // SEG, SEG)

        @pl.when(i == nt - 1)
        def _():
            for k in range(taps):
                sum_ref[pl.ds(k, 1), :] = _sum0(wacc[pl.ds(8 * k, 8), :])

    scr_a, rot_a = _conv_scratch(tt, c)
    return pl.pallas_call(
        body, name="mix_mid_bwd", grid=(n // tt,),
        in_specs=[_row_spec(tt, c, 0), _row_spec(tt, c, 1), _row_spec(tt, c, 1), _row_spec(tt, c, 2),
                  _row_spec(tt, c, 3), _row_spec(tt, c), _row_spec(tt, c), _row_spec(tt, c),
                  _full_spec(w31.shape), _full_spec((1, c)), _full_spec((1, c))],
        out_specs=(_row_spec(tt, c), _row_spec(tt, 3 * c), _full_spec((sum_rows, c))),
        out_shape=(jax.ShapeDtypeStruct((n, c), F32), jax.ShapeDtypeStruct((n, 3 * c), BF16),
                   jax.ShapeDtypeStruct((sum_rows, c), F32)),
        scratch_shapes=[scr_a, rot_a, scr_a, rot_a, pltpu.VMEM((tt, c), F32), pltpu.VMEM((8 * taps, c), F32)],
        compiler_params=_cp(("arbitrary",), 48),
    )(dymix, dymix, p, p, p, h_f, h_b, uc, w31, clg, clb)


def _ada_fwd(cc, w_ada, b_cols):
    depth, d, wc = w_ada.shape
    tn = _pick(wc, (768, 512, 384, 256, 128))

    def body(c_ref, w_ref, b_ref, o_ref):
        cv = c_ref[...]
        o_ref[...] = _dot((cv * _sig(cv)).astype(BF16), w_ref[...].astype(BF16), 1, 0) + b_ref[...]

    return pl.pallas_call(
        body, name="ada_fwd", grid=(depth, wc // tn),
        in_specs=[_full_spec((MOD_ROWS, d)), pl.BlockSpec((None, d, tn), lambda l, j: (l, 0, j)),
                  pl.BlockSpec((None, 1, tn), lambda l, j: (l, 0, j))],
        out_specs=pl.BlockSpec((None, MOD_ROWS, tn), lambda l, j: (l, 0, j)),
        out_shape=jax.ShapeDtypeStruct((depth, MOD_ROWS, wc), F32),
        compiler_params=_cp(("parallel", "parallel")),
    )(cc, w_ada, b_cols)


def _ada_bwd(cc, dm, w_ada):
    depth, d, wc = w_ada.shape
    tn = _pick(wc, (768, 512, 384, 256, 128))

    def body(c_ref, dm_ref, w_ref, gw_ref, gc_ref):
        cv = c_ref[...]
        sg = _sig(cv)
        dmb = dm_ref[...].astype(BF16)
        gw_ref[...] = _dot((cv * sg).astype(BF16), dmb, 0, 0)

        @pl.when(jnp.logical_and(pl.program_id(0) == 0, pl.program_id(1) == 0))
        def _():
            gc_ref[...] = jnp.zeros(gc_ref.shape, F32)

        gc_ref[...] += _dot(dmb, w_ref[...].astype(BF16), 1, 1) * (sg * (1.0 + cv * (1.0 - sg)))

    return pl.pallas_call(
        body, name="ada_bwd", grid=(depth, wc // tn),
        in_specs=[_full_spec((MOD_ROWS, d)), pl.BlockSpec((None, MOD_ROWS, tn), lambda l, j: (l, 0, j)),
                  pl.BlockSpec((None, d, tn), lambda l, j: (l, 0, j))],
        out_specs=(pl.BlockSpec((None, d, tn), lambda l, j: (l, 0, j)), _full_spec((MOD_ROWS, d))),
        out_shape=(jax.ShapeDtypeStruct((depth, d, wc), F32), jax.ShapeDtypeStruct((MOD_ROWS, d), F32)),
        compiler_params=_cp(("arbitrary", "arbitrary")),
    )(cc, dm, w_ada)


def _adam(grads, w, m, v):
    depth, rows, cols = w.shape
    assert len(grads) == depth
    n_slab = grads[0].shape[0]
    tr = _pick(rows, (128, 64, 32, 16, 8))
    c1, c2 = 1.0 - ADAM_B1 ** ADAM_STEP, 1.0 - ADAM_B2 ** ADAM_STEP

    def body(*refs):
        g_refs = refs[:depth]
        w_ref, m_ref, v_ref, go_ref, do_ref, mo_ref, vo_ref = refs[depth:]
        for q in range(depth):
            @pl.when(pl.program_id(0) == q)
            def _(q=q):
                g = g_refs[q][0].astype(F32)
                for k in range(1, n_slab):
                    g = g + g_refs[q][k].astype(F32)
                go_ref[...] = g

        g = go_ref[...]
        m_new = ADAM_B1 * m_ref[...] + (1.0 - ADAM_B1) * g
        v_new = ADAM_B2 * v_ref[...] + (1.0 - ADAM_B2) * (g * g)
        mo_ref[...] = m_new
        vo_ref[...] = v_new
        do_ref[...] = -ADAM_LR * ((m_new / c1) / (jnp.sqrt(v_new / c2) + ADAM_EPS) + ADAM_WD * w_ref[...])

    blk = pl.BlockSpec((None, tr, cols), lambda l, i: (l, i, 0))
    g_specs = [pl.BlockSpec((n_slab, tr, cols), lambda l, i, q=q: (0, jnp.where(l == q, i, 0), 0))
               for q in range(depth)]
    out = jax.ShapeDtypeStruct((depth, rows, cols), F32)
    return pl.pallas_call(
        body, name="adamw", grid=(depth, rows // tr),
        in_specs=g_specs + [blk, blk, blk],
        out_specs=(blk, blk, blk, blk), out_shape=(out, out, out, out),
        compiler_params=_cp(("parallel", "parallel"), 48),
    )(*grads, w, m, v)


_WEIGHTS = ['c_ctx', 'w_ada', 'b_ada', 'ln_g', 'ln_b', 'ff1_in', 'ff1_out', 'ff2_in', 'ff2_out', 'w_in', 'conv4_w',
            'conv4_b', 'w_rg', 'b_rg', 'w_ig', 'b_ig', 'lam', 'conv31_w', 'conv31_b', 'cln_g', 'cln_b', 'w_out', 'b_out']
_SMALL_SHARDED = {'ln_g': 2, 'ln_b': 2, 'conv4_w': 2, 'w_rg': 3, 'b_rg': 2, 'w_ig': 3, 'b_ig': 2, 'lam': 2,
                  'conv31_w': 2}
_BIG_SHARDED = {'ff1_in': 2, 'ff1_out': 1, 'ff2_in': 2, 'ff2_out': 1, 'w_in': 2, 'w_out': 1}
_SMALL = [n for n in _WEIGHTS if n not in _BIG_SHARDED and n != 'w_ada']


def kernel(x, c, ctx, c_ctx, w_ada, b_ada, ln_g, ln_b, ff1_in, ff1_out, ff2_in, ff2_out, w_in, conv4_w, conv4_b, w_rg, b_rg, w_ig, b_ig, lam, conv31_w, conv31_b, cln_g, cln_b, w_out, b_out, loss_target, m_c_ctx, m_w_ada, m_b_ada, m_ln_g, m_ln_b, m_ff1_in, m_ff1_out, m_ff2_in, m_ff2_out, m_w_in, m_conv4_w, m_conv4_b, m_w_rg, m_b_rg, m_w_ig, m_b_ig, m_lam, m_conv31_w, m_conv31_b, m_cln_g, m_cln_b, m_w_out, m_b_out, v_c_ctx, v_w_ada, v_b_ada, v_ln_g, v_ln_b, v_ff1_in, v_ff1_out, v_ff2_in, v_ff2_out, v_w_in, v_conv4_w, v_conv4_b, v_w_rg, v_b_rg, v_w_ig, v_b_ig, v_lam, v_conv31_w, v_conv31_b, v_cln_g, v_cln_b, v_w_out, v_b_out):
    wts = dict(c_ctx=c_ctx, w_ada=w_ada, b_ada=b_ada, ln_g=ln_g, ln_b=ln_b, ff1_in=ff1_in, ff1_out=ff1_out,
               ff2_in=ff2_in, ff2_out=ff2_out, w_in=w_in, conv4_w=conv4_w, conv4_b=conv4_b, w_rg=w_rg, b_rg=b_rg,
               w_ig=w_ig, b_ig=b_ig, lam=lam, conv31_w=conv31_w, conv31_b=conv31_b, cln_g=cln_g, cln_b=cln_b,
               w_out=w_out, b_out=b_out)
    mom = dict(zip(_WEIGHTS, (m_c_ctx, m_w_ada, m_b_ada, m_ln_g, m_ln_b, m_ff1_in, m_ff1_out, m_ff2_in, m_ff2_out,
                              m_w_in, m_conv4_w, m_conv4_b, m_w_rg, m_b_rg, m_w_ig, m_b_ig, m_lam, m_conv31_w,
                              m_conv31_b, m_cln_g, m_cln_b, m_w_out, m_b_out)))
    var = dict(zip(_WEIGHTS, (v_c_ctx, v_w_ada, v_b_ada, v_ln_g, v_ln_b, v_ff1_in, v_ff1_out, v_ff2_in, v_ff2_out,
                              v_w_in, v_conv4_w, v_conv4_b, v_w_rg, v_b_rg, v_w_ig, v_b_ig, v_lam, v_conv31_w,
                              v_conv31_b, v_cln_g, v_cln_b, v_w_out, v_b_out)))

    depth, d, wc = w_ada.shape
    t_lat, t_ctx = x.shape[1], ctx.shape[1]
    tt, n_tok = t_ctx, t_ctx + x.shape[1]
    assert t_lat % tt == 0 and tt % SEG == 0 and tt & (tt - 1) == 0 and d % 2 == 0
    ch = d // 2
    alpha = (2.0 * depth) ** 0.25
    me = 4 * lax.axis_index("x") + 2 * lax.axis_index("y") + lax.axis_index("c")

    names = list(_SMALL_SHARDED)
    got = _gather_list([wts[k] for k in names] + [c], F32)
    full = {k: _unshard(g, _SMALL_SHARDED[k]) for k, g in zip(names, got[:-1])}
    c_all = got[-1].reshape(N_DEV, d)
    cc = jnp.concatenate([c_all, jnp.pad(c_ctx[None, :], ((0, MOD_ROWS - N_DEV - 1), (0, 0)))], axis=0)

    b_cols = _my_block(b_ada, me, 1)[:, None, :]
    (m_cols,) = _gather_list([_ada_fwd(cc, w_ada, b_cols)], F32)
    m_full = jnp.moveaxis(m_cols, 0, 2).reshape(depth, MOD_ROWS, N_DEV * wc)
    m_lat = lax.dynamic_index_in_dim(m_full, me, axis=1, keepdims=False).reshape(depth, 9, d)
    m_ctx = m_full[:, N_DEV].reshape(depth, 9, d)
    modtab = jnp.pad(jnp.stack([m_ctx, m_lat], axis=1), ((0, 0), (0, 0), (0, MOD_ROWS - 9), (0, 0)))

    big = {k: [None] * depth for k in _BIG_SHARDED}
    half = {}

    def shard(k, l):
        return wts[k][l].astype(BF16)

    def landed(k, l, g):
        big[k][l] = _unshard(g, _BIG_SHARDED[k] - 1)

    def jobs(*specs):
        return [(stage, k, l) for stage, k, l in specs if l < depth]

    def comm_of(todo):
        return [(stage, shard(k, l) if stage == "spread" else half[k, l]) for stage, k, l in todo]

    def settle(todo, got):
        for (stage, k, l), g in zip(todo, got):
            if stage == "spread":
                half[k, l] = g
            else:
                landed(k, l, g)

    got = _gather_list([shard('ff1_in', 0), shard('ff1_out', 0)], BF16)
    landed('ff1_in', 0, got[0])
    landed('ff1_out', 0, got[1])
    wr_b, wi_b = full['w_rg'].astype(BF16), full['w_ig'].astype(BF16)

    def row(v):
        return v.reshape(1, -1)

    s = jnp.concatenate([ctx[0], x[0]], axis=0)
    saved = []
    for l in range(depth):
        mt, sv = modtab[l], {}
        sv['s0'] = s
        sv['h1'] = _modulate(s, mt, 0, 1, tt)
        todo = jobs(("spread", 'w_in', l), ("spread", 'w_out', l), ("spread", 'ff2_in', l))
        (sv['z1'], sv['a1']), got = _mm_swiglu(sv['h1'], big['ff1_in'][l], comm=comm_of(todo))
        settle(todo, got)
        todo = jobs(("relay", 'w_in', l), ("relay", 'w_out', l), ("relay", 'ff2_in', l), ("spread", 'ff2_out', l))
        sv['f1'], got = _mm_nn("ffn_out", sv['a1'], big['ff1_out'][l], comm=comm_of(todo))
        settle(todo, got)
        sv['r1'], sv['s1'] = _res_ln(s, sv['f1'], mt, 2, 0.5, row(full['ln_g'][l, 0]), row(full['ln_b'][l, 0]), alpha, tt)
        sv['h2'] = _modulate(sv['s1'], mt, 3, 4, tt)
        todo = jobs(("relay", 'ff2_out', l))
        sv['p'], got = _mm_nn("mix_in", sv['h2'], big['w_in'][l], comm=comm_of(todo))
        settle(todo, got)
        sv['xr'] = _conv4(sv['p'], full['conv4_w'][l], row(conv4_b[l]), tt, ch)
        for dr_, rev in ((0, False), (1, True)):
            sv['h', dr_], sv['hin', dr_] = _scan(
                sv['xr'], wr_b[l, dr_], wi_b[l, dr_], row(full['b_rg'][l, dr_]), row(full['b_ig'][l, dr_]),
                row(full['lam'][l, dr_]), rev, tt)
        sv['ymix'], sv['uc'] = _mix_mid(sv['p'], sv['h', 0], sv['h', 1], full['conv31_w'][l], row(conv31_b[l]),
                                        row(cln_g[l]), row(cln_b[l]), tt)
        sv['y'], _ = _mm_nn("mix_out", sv['ymix'], big['w_out'][l], bias=row(b_out[l]))
        sv['r2'], sv['s2'] = _res_ln(sv['s1'], sv['y'], mt, 5, 1.0, row(full['ln_g'][l, 1]), row(full['ln_b'][l, 1]),
                                     alpha, tt)
        sv['h3'] = _modulate(sv['s2'], mt, 6, 7, tt)
        todo = jobs(("spread", 'ff1_in', l + 1), ("spread", 'ff1_out', l + 1))
        (sv['z3'], sv['a3']), got = _mm_swiglu(sv['h3'], big['ff2_in'][l], comm=comm_of(todo))
        settle(todo, got)
        todo = jobs(("relay", 'ff1_in', l + 1), ("relay", 'ff1_out', l + 1))
        sv['f3'], got = _mm_nn("ffn_out", sv['a3'], big['ff2_out'][l], comm=comm_of(todo))
        settle(todo, got)
        sv['r3'], s = _res_ln(sv['s2'], sv['f3'], mt, 8, 0.5, row(full['ln_g'][l, 2]), row(full['ln_b'][l, 2]), alpha, tt)
        saved.append(sv)

    cot, loss_blk = _loss_head(s, loss_target[0], tt)
    loss = lax.psum(loss_blk[0, 0], ("x", "y", "c"))

    gx = {k: [None] * depth for k in _BIG_SHARDED}
    gsm = {k: [None] * depth for k in ('ln_g', 'ln_b', 'conv4_w', 'conv4_b', 'b_rg', 'b_ig', 'lam',
                                       'conv31_w', 'conv31_b', 'cln_g', 'cln_b', 'b_out', 'dm')}
    g_gate = {'w_rg': [None] * depth, 'w_ig': [None] * depth}
    taps4, taps31 = conv4_w.shape[1], conv31_w.shape[1]
    heads, hs = w_rg.shape[2], w_rg.shape[4]

    def scatter(pend):
        return [("scatter", g) for _, _, g in pend]

    def exchanged(pend, got):
        for (k, l, _), g in zip(pend, got):
            gx[k][l] = g

    def ffn_bwd(cot, l, mt, r, f, z, a, h, s_in, w_in_name, w_out_name, ln_idx, i_gate, i_scale, pend_short, pend_long,
                final=False):
        dr, dfb, tot, per = _ln_bwd(cot, r, f, mt, i_gate, 0.5, row(full['ln_g'][l, ln_idx]), tt)
        dz, got = _mm_nt_swiglu_bwd(dfb, big[w_out_name][l], z, comm=scatter(pend_short))
        exchanged(pend_short, got)
        g_out, _ = _mm_tn("ffn_out_dw", a, dfb)
        pend_out = [(w_out_name, l, g_out.reshape(N_DEV, -1, d))]
        if final:
            g_in, got = _mm_tn("ffn_in_dw", h, dz, b_halves=True, shard_cols=True, comm=scatter(pend_long))
            exchanged(pend_long, got)
            pend_out.append((w_in_name, l, g_in))
            dh, got = _mm_nt("ffn_in_dx", dz, big[w_in_name][l], a_halves=True, comm=scatter(pend_out))
            exchanged(pend_out, got)
            pend_in = []
        else:
            dh, got = _mm_nt("ffn_in_dx", dz, big[w_in_name][l], a_halves=True, comm=scatter(pend_long))
            exchanged(pend_long, got)
            g_in, got = _mm_tn("ffn_in_dw", h, dz, b_halves=True, shard_cols=True, comm=scatter(pend_out))
            exchanged(pend_out, got)
            pend_in = [(w_in_name, l, g_in)]
        cot, per2 = _mod_bwd(dr, dh, s_in, mt, i_scale, alpha, tt)
        return cot, tot, per, per2, pend_in

    pend = []
    for l in reversed(range(depth)):
        mt, sv = modtab[l], saved[l]
        cot, tot3, per3, mod3, pend_ff2_in = ffn_bwd(cot, l, mt, sv['r3'], sv['f3'], sv['z3'], sv['a3'], sv['h3'],
                                                     sv['s2'], 'ff2_in', 'ff2_out', 2, 8, 7, [], pend)
        dr, dyb, tot2, per2 = _ln_bwd(cot, sv['r2'], sv['y'], mt, 5, 1.0, row(full['ln_g'][l, 1]), tt)
        dymix, _ = _mm_nt("mix_out_dx", dyb, big['w_out'][l])
        g_w_out, _ = _mm_tn("mix_out_dw", sv['ymix'], dyb)
        pend = [('w_out', l, g_w_out.reshape(N_DEV, -1, d))]
        dh_rec, dp_rest, csum = _mix_mid_bwd(dymix, sv['p'], sv['h', 0], sv['h', 1], sv['uc'], full['conv31_w'][l],
                                             row(cln_g[l]), row(cln_b[l]), tt)
        dxr, gw_r, gw_i, gsum = None, [], [], []
        for dr_, rev in ((0, False), (1, True)):
            dxr, gwr, gwi, gs = _scan_bwd(
                sv['xr'], dh_rec, sv['h', dr_], sv['hin', dr_], wr_b[l, dr_], wi_b[l, dr_],
                row(full['b_rg'][l, dr_]), row(full['b_ig'][l, dr_]), row(full['lam'][l, dr_]), rev, tt, dxr)
            gw_r.append(gwr)
            gw_i.append(gwi)
            gsum.append(gs)
        dp_xr, c4sum = _conv4_bwd(dxr, sv['p'], full['conv4_w'][l], tt, ch)
        dp = jnp.concatenate([dp_xr, dp_rest], axis=1)
        dh, got = _mm_nt("mix_in_dx", dp, big['w_in'][l], comm=scatter(pend))
        exchanged(pend, got)
        g_w_in, _ = _mm_tn("mix_in_dw", sv['h2'], dp, shard_cols=True)
        cot, mod2 = _mod_bwd(dr, dh, sv['s1'], mt, 4, alpha, tt)
        cot, tot1, per1, mod1, pend = ffn_bwd(cot, l, mt, sv['r1'], sv['f1'], sv['z1'], sv['a1'], sv['h1'], sv['s0'],
                                              'ff1_in', 'ff1_out', 0, 2, 1, [('w_in', l, g_w_in)], pend_ff2_in,
                                              final=(l == 0))

        gsm['ln_g'][l] = jnp.stack([tot1[0], tot2[0], tot3[0]])
        gsm['ln_b'][l] = jnp.stack([tot1[1], tot2[1], tot3[1]])
        gsm['b_out'][l] = tot2[2]
        gsm['conv4_w'][l], gsm['conv4_b'][l] = c4sum[:taps4], c4sum[taps4]
        for k, gws in (('w_rg', gw_r), ('w_ig', gw_i)):
            slabs = jnp.stack(gws).reshape(2, heads, N_DEV, hs // N_DEV, hs)
            g_gate[k][l] = jnp.moveaxis(slabs, 2, 0).reshape(N_DEV, -1, hs).astype(BF16)
        gsm['b_rg'][l] = jnp.stack([g[0] for g in gsum])
        gsm['b_ig'][l] = jnp.stack([g[1] for g in gsum])
        gsm['lam'][l] = jnp.stack([g[2] for g in gsum])
        gsm['conv31_w'][l], gsm['conv31_b'][l] = csum[:taps31], csum[taps31]
        gsm['cln_g'][l], gsm['cln_b'][l] = csum[taps31 + 1], csum[taps31 + 2]
        gsm['dm'][l] = jnp.stack([jnp.stack([mod1[8 * st + 1], mod1[8 * st], per1[8 * st],
                                             mod2[8 * st + 1], mod2[8 * st], per2[8 * st],
                                             mod3[8 * st + 1], mod3[8 * st], per3[8 * st]]) for st in (0, 1)])
    grad_x = cot[t_ctx:][None]
    assert not pend
    gate_x = _all_to_all(g_gate['w_rg'] + g_gate['w_ig'])

    keys = list(gsm)
    partial = [jnp.stack(gsm[k]) for k in keys]
    shapes = [a.shape for a in partial]
    gathered = _all_gather(_pack(partial, F32))
    totals = dict(zip(keys, _unpack(_sum_slabs(gathered), shapes)))
    dm_all = _unpack(gathered, shapes, lead=(N_DEV,))[keys.index('dm')]
    dm_tot = totals['dm'].reshape(depth, 2, 9 * d)
    dm_sum = _sum_slabs(jnp.stack([_pack([dm_tot[:, 0]], F32), _pack([dm_tot[:, 1]], F32)]))
    grad = {'b_ada': _unpack(dm_sum, [(depth, 9 * d)])[0]}
    for k in ('ln_g', 'ln_b', 'conv4_w', 'b_rg', 'b_ig', 'lam', 'conv31_w'):
        grad[k] = _my_block(totals[k], me, 2)
    for k in ('conv4_b', 'conv31_b', 'cln_g', 'cln_b', 'b_out'):
        grad[k] = totals[k]

    dm_rows = jnp.concatenate([jnp.moveaxis(dm_all[:, :, 1].reshape(N_DEV, depth, 9 * d), 0, 1),
                               jnp.pad(dm_tot[:, 0][:, None, :], ((0, 0), (0, MOD_ROWS - N_DEV - 1), (0, 0)))], axis=1)
    dm_cols = lax.dynamic_slice_in_dim(dm_rows, me * wc, wc, axis=2)
    grad_w_ada, gcc = _ada_bwd(cc, dm_cols, w_ada)
    gcc_all = _all_gather(_pack([gcc], F32))
    grad['c_ctx'] = _unpack(_sum_slabs(gcc_all), [(MOD_ROWS, d)])[0][N_DEV]

    res = {}
    small = [k for k in _SMALL if k not in g_gate]
    gs, ws, ms, vs = (_pack([src[k] for k in small], F32).reshape(1, -1, 8 * LANE) for src in (grad, wts, mom, var))
    outs = _adam([gs], ws, ms, vs)
    small_shapes = [wts[k].shape for k in small]
    for k, vals in zip(small, zip(*(_unpack(o[0], small_shapes) for o in outs))):
        res[k] = vals
    res['w_ada'] = _adam([grad_w_ada[l][None] for l in range(depth)], w_ada, m_w_ada, v_w_ada)
    for j, k in enumerate(g_gate):
        flat = (depth, -1, hs)
        outs = _adam([gate_x[j * depth + l] for l in range(depth)], wts[k].reshape(flat), mom[k].reshape(flat),
                     var[k].reshape(flat))
        res[k] = tuple(o.reshape(wts[k].shape) for o in outs)
    for k in _BIG_SHARDED:
        res[k] = _adam(gx[k], wts[k], mom[k], var[k])

    out = [loss, grad_x]
    for j in range(4):
        out += [res[k][j] for k in _WEIGHTS]
    return tuple(out)
```

```python
import math

import jax
import jax.numpy as jnp
from jax import lax
from jax.experimental import pallas as pl
from jax.experimental.pallas import tpu as pltpu

F32 = jnp.float32
BF16 = jnp.bfloat16
N_DEV = 8
MESH = pl.DeviceIdType.MESH
LN_EPS = 1e-6
RG_C = 8.0
SEG = 64
CONV_PAD = 16
ADAM_LR, ADAM_B1, ADAM_B2, ADAM_EPS, ADAM_WD, ADAM_STEP = 0.001, 0.9, 0.999, 1e-08, 0.01, 10
LANE = 128
PACK_QUANTUM = 16 * LANE
PACK_ROWS = 512
MOD_ROWS = 16
GELU_K = 0.7978845608028654
GELU_C = 0.044715

_TM = (768, 640, 512, 384, 256, 128)
_TN = (1024, 512, 256, 128)
_TK = (2816, 1024, 512, 256, 128)
_SUB = (256, 128)


def _cp(sem=None, vmem_mb=40):
    return pltpu.CompilerParams(dimension_semantics=sem, vmem_limit_bytes=vmem_mb * 2 ** 20)


def _pick(n, cands):
    for cand in cands:
        if n % cand == 0:
            return cand
    return n


def _sig(v):
    return 1.0 / (1.0 + jnp.exp(-v))


def _dot(a, b, ca, cb):
    return lax.dot_general(a, b, (((ca,), (cb,)), ((), ())), preferred_element_type=F32)


def _sum0(v):
    return jnp.sum(v, axis=0, keepdims=True)


def _gelu_and_grad(v):
    inner = GELU_K * (v + GELU_C * v * v * v)
    t = jnp.tanh(inner)
    gel = 0.5 * v * (1.0 + t)
    dgel = 0.5 * (1.0 + t) + 0.5 * v * (1.0 - t * t) * GELU_K * (1.0 + 3.0 * GELU_C * v * v)
    return gel, dgel


def _softplus_neg(lam):
    y = jnp.exp(-jnp.abs(lam))
    u = 1.0 + y
    log1p = jnp.where(u == 1.0, y, jnp.log(u) * (y / (u - 1.0)))
    return jnp.maximum(-lam, 0.0) + log1p


def _neg_expm1(z):
    ser = -z * (1.0 + z * (1 / 2 + z * (1 / 6 + z * (1 / 24 + z * (1 / 120 + z * (1 / 720 + z * (1 / 5040)))))))
    return jnp.where(z > -0.25, ser, 1.0 - jnp.exp(z))


def _row_spec(tt, w, col=0):
    return pl.BlockSpec((tt, w), lambda i: (i, col))


def _full_spec(shape):
    nd = len(shape)
    return pl.BlockSpec(shape, lambda *_: (0,) * nd)


def _mod_spec(d):
    return pl.BlockSpec((None, MOD_ROWS, d), lambda i: (jnp.minimum(i, 1), 0, 0))


def _stream_sum_spec(w):
    return pl.BlockSpec((8, w), lambda i: (jnp.minimum(i, 1), 0))


def _acc_rows(ref, init, rows):
    @pl.when(init)
    def _():
        ref[...] = jnp.zeros(ref.shape, F32)

    for k, row in enumerate(rows):
        ref[pl.ds(k, 1), :] += row


def _all_gather(v):
    def body(x_ref, out_ref, send_sems, recv_sems, local_sem):
        x, y, c = lax.axis_index("x"), lax.axis_index("y"), lax.axis_index("c")
        me, sibling = (x, y, c), (x, y, 1 - c)
        chips = [(1 - x, y), (x, 1 - y), (1 - x, 1 - y)]

        def slot(px, py, pc):
            return out_ref.at[4 * px + 2 * py + pc]

        def copy(k, block, to, src=None):
            return pltpu.make_async_remote_copy(
                src_ref=slot(*block) if src is None else src, dst_ref=slot(*block),
                send_sem=send_sems.at[k], recv_sem=recv_sems.at[k], device_id=to, device_id_type=MESH)

        mine = pltpu.make_async_copy(x_ref, slot(*me), local_sem)
        mine.start()
        first = [copy(0, me, sibling, src=x_ref)]
        first += [copy(1 + j, me, (*chip, c), src=x_ref) for j, chip in enumerate(chips)]
        for cp in first:
            cp.start()
        passed = [copy(4 + j, (*chip, c), sibling) for j, chip in enumerate(chips)]
        for j, chip in enumerate(chips):
            copy(1 + j, (*chip, c), me).wait_recv()
            passed[j].start()
        copy(0, sibling, me).wait_recv()
        for j, chip in enumerate(chips):
            copy(4 + j, (*chip, 1 - c), me).wait_recv()
        for cp in first + passed:
            cp.wait_send()
        mine.wait()

    return pl.pallas_call(
        body, name="all_gather",
        out_shape=jax.ShapeDtypeStruct((N_DEV,) + v.shape, v.dtype),
        in_specs=[pl.BlockSpec(memory_space=pltpu.HBM)],
        out_specs=pl.BlockSpec(memory_space=pltpu.HBM),
        scratch_shapes=[pltpu.SemaphoreType.DMA((7,)), pltpu.SemaphoreType.DMA((7,)), pltpu.SemaphoreType.DMA],
    )(v)


_PEER_FLIPS = [(0, 0, 1), (1, 0, 0), (0, 1, 0), (1, 1, 0), (1, 0, 1), (0, 1, 1), (1, 1, 1)]


def _all_to_all(parts):
    n_parts = len(parts)

    def body(*refs):
        x_refs, out_ref = refs[:n_parts], refs[n_parts]
        send_sems, recv_sems, local_sems = refs[n_parts + 1:]
        x, y, c = lax.axis_index("x"), lax.axis_index("y"), lax.axis_index("c")
        me = 4 * x + 2 * y + c
        local, remote = [], []
        for l in range(n_parts):
            cp = pltpu.make_async_copy(x_refs[l].at[me], out_ref.at[l, me], local_sems.at[l])
            cp.start()
            local.append(cp)
            for k, (fx, fy, fc) in enumerate(_PEER_FLIPS):
                px = 1 - x if fx else x
                py = 1 - y if fy else y
                pc = 1 - c if fc else c
                cp = pltpu.make_async_remote_copy(
                    src_ref=x_refs[l].at[4 * px + 2 * py + pc], dst_ref=out_ref.at[l, me],
                    send_sem=send_sems.at[7 * l + k], recv_sem=recv_sems.at[7 * l + k],
                    device_id=(px, py, pc), device_id_type=MESH)
                cp.start()
                remote.append(cp)
        for cp in remote:
            cp.wait()
        for cp in local:
            cp.wait()

    shape = parts[0].shape
    return pl.pallas_call(
        body, name="all_to_all",
        out_shape=jax.ShapeDtypeStruct((n_parts,) + shape, parts[0].dtype),
        in_specs=[pl.BlockSpec(memory_space=pltpu.HBM)] * n_parts,
        out_specs=pl.BlockSpec(memory_space=pltpu.HBM),
        scratch_shapes=[pltpu.SemaphoreType.DMA((7 * n_parts,)), pltpu.SemaphoreType.DMA((7 * n_parts,)),
                        pltpu.SemaphoreType.DMA((n_parts,))],
    )(*parts)


def _pack(arrs, dtype):
    pieces = []
    for a in arrs:
        flat = a.astype(dtype).reshape(-1)
        flat = jnp.pad(flat, (0, (-flat.shape[0]) % PACK_QUANTUM))
        pieces.append(flat.reshape(-1, LANE))
    packed = jnp.concatenate(pieces, axis=0) if len(pieces) > 1 else pieces[0]
    return jnp.pad(packed, ((0, (-packed.shape[0]) % PACK_ROWS), (0, 0)))


def _unpack(packed, shapes, lead=()):
    flat = packed.reshape(lead + (-1,))
    outs, off = [], 0
    for sh in shapes:
        n = math.prod(sh)
        outs.append(flat[..., off:off + n].reshape(lead + tuple(sh)))
        off += n + (-n) % PACK_QUANTUM
    return outs


def _gather_list(arrs, dtype):
    gathered = _all_gather(_pack(arrs, dtype))
    return _unpack(gathered, [a.shape for a in arrs], lead=(N_DEV,))


def _unshard(g, axis):
    y = jnp.moveaxis(g, 0, axis)
    sh = y.shape
    return y.reshape(sh[:axis] + (sh[axis] * sh[axis + 1],) + sh[axis + 2:])


def _my_block(full, me, axis):
    size = full.shape[axis] // N_DEV
    return lax.dynamic_slice_in_dim(full, me * size, size, axis=axis)


def _sum_slabs(v):
    n_slab, rows, width = v.shape
    tr = _pick(rows, (1024, 512, 256, 128, 64, 32, 16, 8))

    def body(v_ref, o_ref):
        acc = v_ref[0]
        for k in range(1, n_slab):
            acc = acc + v_ref[k]
        o_ref[...] = acc

    return pl.pallas_call(
        body, name="sum_slabs", grid=(rows // tr,),
        in_specs=[pl.BlockSpec((n_slab, tr, width), lambda i: (0, i, 0))],
        out_specs=pl.BlockSpec((tr, width), lambda i: (i, 0)),
        out_shape=jax.ShapeDtypeStruct((rows, width), F32),
        compiler_params=_cp(("parallel",)),
    )(v)


N_LOCAL = 1


def _comm_copies(kinds, src_refs, dst_refs, send_sems, recv_sems, local_sems):
    x, y, c = lax.axis_index("x"), lax.axis_index("y"), lax.axis_index("c")
    me = 4 * x + 2 * y + c
    copies = []

    def remote(j, k, src, dst, peer):
        copies.append(pltpu.make_async_remote_copy(
            src_ref=src, dst_ref=dst, send_sem=send_sems.at[7 * j + k], recv_sem=recv_sems.at[7 * j + k],
            device_id=peer, device_id_type=MESH))

    def local(j, k, src, dst):
        copies.append(pltpu.make_async_copy(src, dst, local_sems.at[N_LOCAL * j + k]))

    for j, kind in enumerate(kinds):
        src, dst = src_refs[j], dst_refs[j]
        peers = [(1 - x if fx else x, 1 - y if fy else y, 1 - c if fc else c) for fx, fy, fc in _PEER_FLIPS]
        if kind == "scatter":
            local(j, 0, src.at[me], dst.at[me])
            for k, (px, py, pc) in enumerate(peers):
                remote(j, k, src.at[4 * px + 2 * py + pc], dst.at[me], (px, py, pc))
        elif kind == "spread":
            local(j, 0, src, dst.at[me])
            for k, peer in enumerate(peers[:4]):
                remote(j, k, src, dst.at[me], peer)
        else:
            for k, (px, py, pc) in enumerate(peers[1:4]):
                remote(j, 4 + k, dst.at[4 * px + 2 * py + pc], dst.at[4 * px + 2 * py + pc], peers[0])
    return copies


def _mm(name, grid, nk, ins, in_specs, out_shape, out_specs, acc_shape, prod, fin, comm=(), vmem_mb=48):
    n_in, n_out, n_c = len(ins), len(out_shape), len(comm)
    kinds = [kind for kind, _ in comm]
    srcs = [a for _, a in comm]
    dsts = [jax.ShapeDtypeStruct(((N_DEV,) + a.shape) if kind == "spread" else a.shape, a.dtype) for kind, a in comm]
    hbm = pl.BlockSpec(memory_space=pltpu.HBM)

    def body(*refs):
        in_refs, refs = refs[:n_in], refs[n_in:]
        src_refs, refs = refs[:n_c], refs[n_c:]
        out_refs, refs = refs[:n_out], refs[n_out:]
        dst_refs, scratch = refs[:n_c], refs[n_c:]
        if nk > 1:
            acc, scratch = scratch[0], scratch[1:]
        ids = [pl.program_id(a) for a in range(3)]
        if n_c:
            copies = _comm_copies(kinds, src_refs, dst_refs, *scratch)

            @pl.when(jnp.logical_and(ids[0] == 0, jnp.logical_and(ids[1] == 0, ids[2] == 0)))
            def _():
                for cp in copies:
                    cp.start()

        if nk == 1:
            fin(prod(*in_refs), in_refs, out_refs)
        else:
            @pl.when(ids[2] == 0)
            def _():
                acc[...] = jnp.zeros(acc.shape, F32)

            for idx, val in enumerate(prod(*in_refs)):
                acc[idx] += val

            @pl.when(ids[2] == nk - 1)
            def _():
                fin(tuple(acc[idx] for idx in range(acc_shape[0])), in_refs, out_refs)

        if n_c:
            @pl.when(jnp.logical_and(ids[0] == grid[0] - 1, jnp.logical_and(ids[1] == grid[1] - 1, ids[2] == nk - 1)))
            def _():
                for cp in copies:
                    cp.wait()

    scratch = [pltpu.VMEM(acc_shape, F32)] if nk > 1 else []
    if n_c:
        scratch += [pltpu.SemaphoreType.DMA((7 * n_c,)), pltpu.SemaphoreType.DMA((7 * n_c,)),
                    pltpu.SemaphoreType.DMA((N_LOCAL * n_c,))]
    sem = ("arbitrary",) * 3 if n_c else ("parallel", "parallel", "arbitrary")
    res = pl.pallas_call(
        body, name=name + (f"_x{n_c}" if n_c else ""), grid=grid,
        in_specs=list(in_specs) + [hbm] * n_c, out_specs=tuple(out_specs) + (hbm,) * n_c,
        out_shape=tuple(out_shape) + tuple(dsts), scratch_shapes=scratch,
        input_output_aliases={n_in + j: n_out + j for j, kind in enumerate(kinds) if kind == "relay"},
        compiler_params=_cp(sem, vmem_mb),
    )(*ins, *srcs)
    return res[:n_out], list(res[n_out:])


def _mm_nn(name, a, b, bias=None, comm=()):
    m, kk = a.shape
    nn = b.shape[1]
    tm, tn = _pick(m, _TM), _pick(nn, _TN)
    tk = kk if kk <= 2048 else _pick(kk, _TK)
    nk = kk // tk
    ins = [a, b]
    specs = [pl.BlockSpec((tm, tk), lambda i, j, k: (i, k)), pl.BlockSpec((tk, tn), lambda i, j, k: (k, j))]
    if bias is not None:
        ins.append(bias)
        specs.append(pl.BlockSpec((1, tn), lambda i, j, k: (0, j)))

    def prod(a_ref, b_ref, *_):
        return (_dot(a_ref[...], b_ref[...], 1, 0),)

    def fin(vals, in_refs, out_refs):
        out_refs[0][...] = vals[0] if bias is None else vals[0] + in_refs[2][...]

    outs, got = _mm(name, (m // tm, nn // tn, nk), nk, ins, specs,
                    (jax.ShapeDtypeStruct((m, nn), F32),), (pl.BlockSpec((tm, tn), lambda i, j, k: (i, j)),),
                    (1, tm, tn), prod, fin, comm)
    return outs[0], got


def _mm_swiglu(h, w1, comm=()):
    m, kk = h.shape
    f = w1.shape[1] // 2
    tm, tn = _pick(m, _TM), _pick(f, (512, 256, 128))
    nj = f // tn
    sub = _pick(tm, _SUB)

    def prod(*_):
        return ()

    def fin(_, in_refs, out_refs):
        h_ref, wg_ref, wu_ref = in_refs
        for q in range(tm // sub):
            rows = pl.ds(q * sub, sub)
            hq = h_ref[rows, :]
            zg, zu = _dot(hq, wg_ref[...], 1, 0), _dot(hq, wu_ref[...], 1, 0)
            out_refs[0][0, rows, :] = zg.astype(BF16)
            out_refs[0][1, rows, :] = zu.astype(BF16)
            out_refs[1][rows, :] = (zg * _sig(zg) * zu).astype(BF16)

    return _mm("ffn_in", (m // tm, nj, 1), 1, [h, w1, w1],
               [pl.BlockSpec((tm, kk), lambda i, j, k: (i, 0)),
                pl.BlockSpec((kk, tn), lambda i, j, k: (0, j)),
                pl.BlockSpec((kk, tn), lambda i, j, k: (0, j + nj))],
               (jax.ShapeDtypeStruct((2, m, f), BF16), jax.ShapeDtypeStruct((m, f), BF16)),
               (pl.BlockSpec((2, tm, tn), lambda i, j, k: (0, i, j)), pl.BlockSpec((tm, tn), lambda i, j, k: (i, j))),
               (2, tm, tn), prod, fin, comm)


def _mm_nt_swiglu_bwd(dfb, w2, z, comm=()):
    m, kk = dfb.shape
    f = w2.shape[0]
    tm, tn = _pick(m, _TM), _pick(f, (512, 256, 128))
    sub = _pick(tm, _SUB)

    def prod(*_):
        return ()

    def fin(_, in_refs, out_refs):
        a_ref, b_ref, z_ref = in_refs
        for q in range(tm // sub):
            rows = pl.ds(q * sub, sub)
            da = _dot(a_ref[rows, :], b_ref[...], 1, 1)
            zg, zu = z_ref[0, rows, :].astype(F32), z_ref[1, rows, :].astype(F32)
            sg = _sig(zg)
            out_refs[0][0, rows, :] = (da * zu * (sg * (1.0 + zg * (1.0 - sg)))).astype(BF16)
            out_refs[0][1, rows, :] = (da * (zg * sg)).astype(BF16)

    outs, got = _mm("ffn_out_dx", (m // tm, f // tn, 1), 1, [dfb, w2, z],
                    [pl.BlockSpec((tm, kk), lambda i, j, k: (i, 0)),
                     pl.BlockSpec((tn, kk), lambda i, j, k: (j, 0)),
                     pl.BlockSpec((2, tm, tn), lambda i, j, k: (0, i, j))],
                    (jax.ShapeDtypeStruct((2, m, f), BF16),),
                    (pl.BlockSpec((2, tm, tn), lambda i, j, k: (0, i, j)),),
                    (1, tm, tn), prod, fin, comm)
    return outs[0], got


def _mm_nt(name, a, b, a_halves=False, comm=()):
    nn, kk = b.shape
    m = a.shape[1] if a_halves else a.shape[0]
    kh = kk // 2 if a_halves else kk
    tm, tn = _pick(m, _TM), _pick(nn, _TN)
    tk = kh if (kh <= 4096 and not a_halves) else _pick(kh, _TK)
    nk, nkh = kk // tk, kh // tk
    if a_halves:
        a_spec = pl.BlockSpec((None, tm, tk), lambda i, j, k: (k // nkh, i, k % nkh))
    else:
        a_spec = pl.BlockSpec((tm, tk), lambda i, j, k: (i, k))

    def prod(a_ref, b_ref):
        return (_dot(a_ref[...], b_ref[...], 1, 1),)

    def fin(vals, in_refs, out_refs):
        out_refs[0][...] = vals[0]

    outs, got = _mm(name, (m // tm, nn // tn, nk), nk, [a, b],
                    [a_spec, pl.BlockSpec((tn, tk), lambda i, j, k: (j, k))],
                    (jax.ShapeDtypeStruct((m, nn), F32),), (pl.BlockSpec((tm, tn), lambda i, j, k: (i, j)),),
                    (1, tm, tn), prod, fin, comm)
    return outs[0], got


def _mm_tn(name, a, b, b_halves=False, shard_cols=False, comm=()):
    kt, m = a.shape
    nn = 2 * b.shape[2] if b_halves else b.shape[1]
    tk, tm = _pick(kt, (1408,) + _TM), _pick(m, (1408, 1024, 512, 256, 128))
    shard = nn // N_DEV
    per = 1
    while shard_cols and 2 * per * shard <= nn // 2 and ((per * shard) % 256 or per * shard < 1024):
        per *= 2
    tn = per * shard if shard_cols else _pick(nn, _TN)
    njh = (nn // 2) // tn if b_halves else 0
    if b_halves:
        b_spec = pl.BlockSpec((None, tk, tn), lambda i, j, k: (j // njh, k, j % njh))
    else:
        b_spec = pl.BlockSpec((tk, tn), lambda i, j, k: (k, j))
    if shard_cols:
        out_shape = jax.ShapeDtypeStruct((N_DEV, m, shard), BF16)
        out_spec = pl.BlockSpec((per, tm, shard), lambda i, j, k: (j, i, 0))
    else:
        out_shape = jax.ShapeDtypeStruct((m, nn), BF16)
        out_spec = pl.BlockSpec((tm, tn), lambda i, j, k: (i, j))

    def prod(a_ref, b_ref):
        return (_dot(a_ref[...], b_ref[...], 0, 0),)

    def fin(vals, in_refs, out_refs):
        r = vals[0].astype(BF16)
        if shard_cols:
            for q in range(per):
                out_refs[0][q] = r[:, q * shard:(q + 1) * shard]
        else:
            out_refs[0][...] = r

    outs, got = _mm(name, (m // tm, nn // tn, kt // tk), kt // tk, [a, b],
                    [pl.BlockSpec((tk, tm), lambda i, j, k: (k, i)), b_spec],
                    (out_shape,), (out_spec,), (1, tm, tn), prod, fin, comm, vmem_mb=56)
    return outs[0], got


def _modulate(s, modtab, i_shift, i_scale, tt):
    n, d = s.shape

    def body(s_ref, m_ref, o_ref):
        shift, scale = m_ref[pl.ds(i_shift, 1), :], m_ref[pl.ds(i_scale, 1), :]
        o_ref[...] = (s_ref[...] * (1.0 + scale) + shift).astype(BF16)

    return pl.pallas_call(
        body, name="modulate", grid=(n // tt,),
        in_specs=[_row_spec(tt, d), _mod_spec(d)], out_specs=_row_spec(tt, d),
        out_shape=jax.ShapeDtypeStruct((n, d), BF16), compiler_params=_cp(("parallel",)),
    )(s, modtab)


def _res_ln(s, f, modtab, i_gate, coef, gam, bet, alpha, tt, nxt=None):
    n, d = s.shape

    def body(s_ref, f_ref, m_ref, g_ref, b_ref, *refs):
        gate = m_ref[pl.ds(i_gate, 1), :]
        r = alpha * s_ref[...] + (coef * gate) * f_ref[...]
        mu = jnp.mean(r, axis=-1, keepdims=True)
        xc = r - mu
        var = jnp.mean(xc * xc, axis=-1, keepdims=True)
        out = xc * lax.rsqrt(var + LN_EPS) * g_ref[...] + b_ref[...]
        if nxt is None:
            r_ref, o_ref = refs
        else:
            m2_ref, r_ref, o_ref, h_ref = refs
            shift, scale = m2_ref[pl.ds(nxt[1], 1), :], m2_ref[pl.ds(nxt[2], 1), :]
            h_ref[...] = (out * (1.0 + scale) + shift).astype(BF16)
        r_ref[...] = r
        o_ref[...] = out

    ins = [s, f, modtab, gam, bet] + ([] if nxt is None else [nxt[0]])
    n_h = 0 if nxt is None else 1
    return pl.pallas_call(
        body, name="res_ln", grid=(n // tt,),
        in_specs=[_row_spec(tt, d), _row_spec(tt, d), _mod_spec(d), _full_spec((1, d)), _full_spec((1, d))]
        + [_mod_spec(d)] * n_h,
        out_specs=(_row_spec(tt, d),) * (2 + n_h),
        out_shape=(jax.ShapeDtypeStruct((n, d), F32),) * 2 + (jax.ShapeDtypeStruct((n, d), BF16),) * n_h,
        compiler_params=_cp(("parallel",)),
    )(*ins)


def _ln_bwd(do, r, f, modtab, i_gate, coef, gam, tt):
    n, d = do.shape

    def body(do_ref, r_ref, f_ref, m_ref, g_ref, dr_ref, dfb_ref, tot_ref, str_ref):
        i = pl.program_id(0)
        r = r_ref[...]
        mu = jnp.mean(r, axis=-1, keepdims=True)
        xc = r - mu
        rstd = lax.rsqrt(jnp.mean(xc * xc, axis=-1, keepdims=True) + LN_EPS)
        xh = xc * rstd
        dout = do_ref[...]
        dxh = dout * g_ref[...]
        m1 = jnp.mean(dxh, axis=-1, keepdims=True)
        m2 = jnp.mean(dxh * xh, axis=-1, keepdims=True)
        dr = rstd * (dxh - m1 - xh * m2)
        dr_ref[...] = dr
        dfb = (coef * m_ref[pl.ds(i_gate, 1), :]) * dr
        dfb_ref[...] = dfb.astype(BF16)
        _acc_rows(tot_ref, i == 0, [_sum0(dout * xh), _sum0(dout), _sum0(dfb)])
        _acc_rows(str_ref, i <= 1, [_sum0(coef * f_ref[...] * dr)])

    return pl.pallas_call(
        body, name="ln_bwd", grid=(n // tt,),
        in_specs=[_row_spec(tt, d), _row_spec(tt, d), _row_spec(tt, d), _mod_spec(d), _full_spec((1, d))],
        out_specs=(_row_spec(tt, d), _row_spec(tt, d), _full_spec((8, d)), _stream_sum_spec(d)),
        out_shape=(jax.ShapeDtypeStruct((n, d), F32), jax.ShapeDtypeStruct((n, d), BF16),
                   jax.ShapeDtypeStruct((8, d), F32), jax.ShapeDtypeStruct((16, d), F32)),
        compiler_params=_cp(("arbitrary",)),
    )(do, r, f, modtab, gam)


def _mod_bwd(dr, dh, s, modtab, i_scale, alpha, tt):
    n, d = dr.shape

    def body(dr_ref, dh_ref, s_ref, m_ref, o_ref, str_ref):
        i = pl.program_id(0)
        dh_v = dh_ref[...]
        o_ref[...] = alpha * dr_ref[...] + dh_v * (1.0 + m_ref[pl.ds(i_scale, 1), :])
        _acc_rows(str_ref, i <= 1, [_sum0(dh_v * s_ref[...]), _sum0(dh_v)])

    return pl.pallas_call(
        body, name="mod_bwd", grid=(n // tt,),
        in_specs=[_row_spec(tt, d), _row_spec(tt, d), _row_spec(tt, d), _mod_spec(d)],
        out_specs=(_row_spec(tt, d), _stream_sum_spec(d)),
        out_shape=(jax.ShapeDtypeStruct((n, d), F32), jax.ShapeDtypeStruct((16, d), F32)),
        compiler_params=_cp(("arbitrary",)),
    )(dr, dh, s, modtab)


def _loss_head(s, target, tt):
    n, d = s.shape
    nt = n // tt

    def body(s_ref, t_ref, do_ref, loss_ref, acc):
        i = pl.program_id(0)

        @pl.when(i == 0)
        def _():
            acc[...] = jnp.zeros(acc.shape, F32)
            do_ref[...] = jnp.zeros(do_ref.shape, F32)

        @pl.when(i > 0)
        def _():
            err = s_ref[...] - t_ref[...]
            do_ref[...] = err / d
            acc[...] += jnp.sum((err * err).reshape(tt // 8, 8, d), axis=0)

        @pl.when(i == nt - 1)
        def _():
            loss_ref[...] = jnp.full(loss_ref.shape, jnp.sum(acc[...]) * (0.5 / d), F32)

    return pl.pallas_call(
        body, name="loss_head", grid=(nt,),
        in_specs=[_row_spec(tt, d), pl.BlockSpec((tt, d), lambda i: (jnp.maximum(i - 1, 0), 0))],
        out_specs=(_row_spec(tt, d), _full_spec((8, LANE))),
        out_shape=(jax.ShapeDtypeStruct((n, d), F32), jax.ShapeDtypeStruct((8, LANE), F32)),
        scratch_shapes=[pltpu.VMEM((8, d), F32)],
        compiler_params=_cp(("arbitrary",)),
    )(s, target)


def _halo_specs(tt, c, n):
    nb, last = tt // 8, n // 8 - 1
    return [pl.BlockSpec((tt, c), lambda i: (i, 0)),
            pl.BlockSpec((8, c), lambda i: (jnp.maximum(i * nb - 1, 0), 0)),
            pl.BlockSpec((8, c), lambda i: (jnp.minimum((i + 1) * nb, last), 0))]


def _fill_halo(scr, main_ref, prev_ref, next_ref, i, nt, tt):
    has_prev = i >= 2
    has_next = jnp.logical_and(i >= 1, i < nt - 1)
    scr[pl.ds(0, 8), :] = jnp.where(has_prev, prev_ref[...], 0.0)
    scr[pl.ds(8, tt), :] = main_ref[...]
    scr[pl.ds(8 + tt, 8), :] = jnp.where(has_next, next_ref[...], 0.0)


def _conv4(p, w, b, tt, c):
    n = p.shape[0]
    nt, taps = n // tt, w.shape[0]
    left = taps // 2

    def body(x_ref, xp_ref, xn_ref, w_ref, b_ref, o_ref, scr):
        _fill_halo(scr, x_ref, xp_ref, xn_ref, pl.program_id(0), nt, tt)
        acc = jnp.broadcast_to(b_ref[...], (tt, c))
        for k in range(taps):
            acc = acc + w_ref[pl.ds(k, 1), :] * scr[pl.ds(8 + k - left, tt), :]
        o_ref[...] = acc

    return pl.pallas_call(
        body, name="conv4", grid=(nt,),
        in_specs=_halo_specs(tt, c, n) + [_full_spec(w.shape), _full_spec((1, c))],
        out_specs=_row_spec(tt, c), out_shape=jax.ShapeDtypeStruct((n, c), F32),
        scratch_shapes=[pltpu.VMEM((tt + 16, c), F32)], compiler_params=_cp(("parallel",)),
    )(p, p, p, w, b)


def _conv4_bwd(dxr, p, w, tt, c):
    n = p.shape[0]
    nt, taps = n // tt, w.shape[0]
    left = taps // 2

    def body(d_ref, dp_ref, dn_ref, x_ref, xp_ref, xn_ref, w_ref, o_ref, sum_ref, dscr, xscr):
        i = pl.program_id(0)
        _fill_halo(dscr, d_ref, dp_ref, dn_ref, i, nt, tt)
        _fill_halo(xscr, x_ref, xp_ref, xn_ref, i, nt, tt)
        acc = jnp.zeros((tt, c), F32)
        for k in range(taps):
            acc = acc + w_ref[pl.ds(k, 1), :] * dscr[pl.ds(8 - (k - left), tt), :]
        o_ref[...] = acc.astype(BF16)
        d = d_ref[...]
        rows = [_sum0(d * xscr[pl.ds(8 + k - left, tt), :]) for k in range(taps)] + [_sum0(d)]
        _acc_rows(sum_ref, i == 0, rows)

    return pl.pallas_call(
        body, name="conv4_bwd", grid=(nt,),
        in_specs=_halo_specs(tt, c, n) + _halo_specs(tt, c, n) + [_full_spec(w.shape)],
        out_specs=(_row_spec(tt, c), _full_spec((8, c))),
        out_shape=(jax.ShapeDtypeStruct((n, c), BF16), jax.ShapeDtypeStruct((8, c), F32)),
        scratch_shapes=[pltpu.VMEM((tt + 16, c), F32), pltpu.VMEM((tt + 16, c), F32)],
        compiler_params=_cp(("arbitrary",)),
    )(dxr, dxr, dxr, p, p, p, w)


def _head_gates(xv, wr, wi, br, bi, lam):
    xb = xv.astype(BF16)
    r = _sig(_dot(xb, wr, 1, 0) + br)
    ig = _sig(_dot(xb, wi, 1, 0) + bi)
    sp = _softplus_neg(lam)
    log_a = (-RG_C) * r * sp
    return r, ig, sp, jnp.exp(log_a), jnp.sqrt(_neg_expm1(2.0 * log_a))


def _tile_scan(a, b, increasing, tt, rows):
    s = 1
    while s < tt:
        if increasing:
            a_sh, b_sh, ok = pltpu.roll(a, s, 0), pltpu.roll(b, s, 0), rows >= s
        else:
            a_sh, b_sh, ok = pltpu.roll(a, tt - s, 0), pltpu.roll(b, tt - s, 0), rows < tt - s
        b = jnp.where(ok, a * b_sh + b, b)
        a = jnp.where(ok, a * a_sh, a)
        s *= 2
    return a, b


def _scan_tile_index(step, nt, reverse):
    return jnp.where(step == 0, 0, nt - step) if reverse else step


def _scan(xr, wr, wi, br, bi, lam, reverse, tt):
    n, c = xr.shape
    nt = n // tt
    heads, hs = wr.shape[0], wr.shape[2]
    lc = min(c, LANE)

    def tile(i):
        return _scan_tile_index(i, nt, reverse)

    def body(x_ref, wr_ref, wi_ref, br_ref, bi_ref, lam_ref, h_ref, hin_ref, a_scr, b_scr, carry):
        i = pl.program_id(0)

        @pl.when(i == 0)
        def _():
            carry[...] = jnp.zeros(carry.shape, F32)

        for hd in range(heads):
            sl = slice(hd * hs, (hd + 1) * hs)
            xv = x_ref[:, sl]
            _, ig, _, a, sq = _head_gates(xv, wr_ref[hd], wi_ref[hd], br_ref[:, sl], bi_ref[:, sl], lam_ref[:, sl])
            a_scr[:, sl] = a
            b_scr[:, sl] = sq * (ig * xv)
        hin_ref[...] = jnp.broadcast_to(carry[...], (8, c))
        rows = lax.broadcasted_iota(jnp.int32, (tt, lc), 0)
        for ch in range(c // lc):
            sl = slice(ch * lc, (ch + 1) * lc)
            big_a, big_b = _tile_scan(a_scr[:, sl], b_scr[:, sl], not reverse, tt, rows)
            h_ref[:, sl] = big_a * carry[:, sl] + big_b
        carry[...] = h_ref[pl.ds(0 if reverse else tt - 1, 1), :]

    return pl.pallas_call(
        body, name="lru_scan", grid=(nt,),
        in_specs=[pl.BlockSpec((tt, c), lambda i: (tile(i), 0)), _full_spec(wr.shape), _full_spec(wi.shape),
                  _full_spec((1, c)), _full_spec((1, c)), _full_spec((1, c))],
        out_specs=(pl.BlockSpec((tt, c), lambda i: (tile(i), 0)), pl.BlockSpec((None, 8, c), lambda i: (tile(i), 0, 0))),
        out_shape=(jax.ShapeDtypeStruct((n, c), F32), jax.ShapeDtypeStruct((nt, 8, c), F32)),
        scratch_shapes=[pltpu.VMEM((tt, c), F32), pltpu.VMEM((tt, c), F32), pltpu.VMEM((1, c), F32)],
        compiler_params=_cp(("arbitrary",)),
    )(xr, wr, wi, br, bi, lam)


def _scan_bwd(xr, dh, h, hin, wr, wi, br, bi, lam, reverse, tt, dx_prev=None):
    n, c = xr.shape
    nt = n // tt
    heads, hs = wr.shape[0], wr.shape[2]
    lc = min(c, LANE)
    inc = not reverse
    first, last = (0, tt - 1) if inc else (tt - 1, 0)

    def tile(i):
        return _scan_tile_index(nt - 1 - i, nt, reverse)

    def body(*refs):
        x_ref, dh_ref, h_ref, hin_ref, wr_ref, wi_ref, br_ref, bi_ref, lam_ref = refs[:9]
        refs = refs[9:]
        if dx_prev is not None:
            dxp_ref, refs = refs[0], refs[1:]
        dx_ref, dwr_ref, dwi_ref, sum_ref, a_scr, r_scr, i_scr, sq_scr, g_scr, da_scr, u_scr, ucarry = refs
        i = pl.program_id(0)

        @pl.when(i == 0)
        def _():
            ucarry[...] = jnp.zeros(ucarry.shape, F32)
            dwr_ref[...] = jnp.zeros(dwr_ref.shape, F32)
            dwi_ref[...] = jnp.zeros(dwi_ref.shape, F32)
            sum_ref[...] = jnp.zeros(sum_ref.shape, F32)

        for hd in range(heads):
            sl = slice(hd * hs, (hd + 1) * hs)
            r, ig, _, a, sq = _head_gates(x_ref[:, sl], wr_ref[hd], wi_ref[hd], br_ref[:, sl], bi_ref[:, sl],
                                          lam_ref[:, sl])
            a_scr[:, sl], r_scr[:, sl], i_scr[:, sl], sq_scr[:, sl] = a, r, ig, sq

        rows = lax.broadcasted_iota(jnp.int32, (tt, lc), 0)
        to_prev = 1 if inc else tt - 1
        to_next = tt - 1 if inc else 1
        for ch in range(c // lc):
            sl = slice(ch * lc, (ch + 1) * lc)
            a, dhv = a_scr[:, sl], dh_ref[:, sl]
            big_a, big_b = _tile_scan(a, a * dhv, not inc, tt, rows)
            u_in = ucarry[:, sl]
            u = big_a * u_in + big_b
            u_scr[:, sl] = u
            g = dhv + jnp.where(rows == last, u_in, pltpu.roll(u, to_next, 0))
            g_scr[:, sl] = g
            h_prev = jnp.where(rows == first, hin_ref[pl.ds(0, 1), sl], pltpu.roll(h_ref[:, sl], to_prev, 0))
            da_scr[:, sl] = g * h_prev
        ucarry[...] = u_scr[pl.ds(first, 1), :]

        for hd in range(heads):
            sl = slice(hd * hs, (hd + 1) * hs)
            xv, a, r, ig, sq = x_ref[:, sl], a_scr[:, sl], r_scr[:, sl], i_scr[:, sl], sq_scr[:, sl]
            g, lam_v = g_scr[:, sl], lam_ref[:, sl]
            sp = _softplus_neg(lam_v)
            d_sq = g * ig * xv
            d_em = d_sq * 0.5 / sq
            d_log_a = (da_scr[:, sl] - 2.0 * a * d_em) * a
            dzr = (d_log_a * ((-RG_C) * sp)) * r * (1.0 - r)
            dzi = (g * sq * xv) * ig * (1.0 - ig)
            dzr_b, dzi_b, xb = dzr.astype(BF16), dzi.astype(BF16), xv.astype(BF16)
            dx = g * sq * ig + _dot(dzr_b, wr_ref[hd], 1, 1) + _dot(dzi_b, wi_ref[hd], 1, 1)
            if dx_prev is not None:
                dx = dx + dxp_ref[:, sl]
            dx_ref[:, sl] = dx
            dwr_ref[hd] += _dot(xb, dzr_b, 0, 0)
            dwi_ref[hd] += _dot(xb, dzi_b, 0, 0)
            sum_ref[pl.ds(0, 1), sl] += _sum0(dzr)
            sum_ref[pl.ds(1, 1), sl] += _sum0(dzi)
            sum_ref[pl.ds(2, 1), sl] += _sum0(d_log_a * ((-RG_C) * r)) * (-_sig(-lam_v))

    tile_spec = pl.BlockSpec((tt, c), lambda i: (tile(i), 0))
    ins = [xr, dh, h, hin, wr, wi, br, bi, lam]
    in_specs = [tile_spec, tile_spec, tile_spec, pl.BlockSpec((None, 8, c), lambda i: (tile(i), 0, 0)),
                _full_spec(wr.shape), _full_spec(wi.shape), _full_spec((1, c)), _full_spec((1, c)), _full_spec((1, c))]
    if dx_prev is not None:
        ins.append(dx_prev)
        in_specs.append(tile_spec)
    return pl.pallas_call(
        body, name="lru_scan_bwd", grid=(nt,), in_specs=in_specs,
        out_specs=(tile_spec, _full_spec(wr.shape), _full_spec(wi.shape), _full_spec((8, c))),
        out_shape=(jax.ShapeDtypeStruct((n, c), F32), jax.ShapeDtypeStruct(wr.shape, F32),
                   jax.ShapeDtypeStruct(wi.shape, F32), jax.ShapeDtypeStruct((8, c), F32)),
        scratch_shapes=[pltpu.VMEM((tt, c), F32)] * 7 + [pltpu.VMEM((1, c), F32)],
        compiler_params=_cp(("arbitrary",)),
    )(*ins)


def _conv_rows(nseg, seg):
    return nseg * (seg + CONV_PAD) + CONV_PAD


def _seg_base(s, seg):
    return CONV_PAD + s * (seg + CONV_PAD)


def _zero_gaps(scr, nseg, seg, c):
    for s in range(nseg + 1):
        scr[pl.ds(s * (seg + CONV_PAD), CONV_PAD), :] = jnp.zeros((CONV_PAD, c), F32)


def _build_shifts(scr, rot, n_rows, sl):
    for b in range(1, 8):
        rot[b - 1, pl.ds(0, n_rows - 8), :] = scr[pl.ds(b, n_rows - 8), sl]


def _tap(scr, rot, base, off, seg, sl):
    a, b = divmod(off, 8)
    if b == 0:
        return scr[pl.ds(base + 8 * a, seg), sl]
    return rot[b - 1, pl.ds(base + 8 * a, seg), :]


def _conv_chunks(nseg, seg):
    return [(s * seg + q * SEG, _seg_base(s, seg) + q * SEG) for s in range(nseg) for q in range(seg // SEG)]


def _conv_scratch(tt, c):
    rows = max(_conv_rows(1, tt), _conv_rows(tt // SEG, SEG))
    return pltpu.VMEM((rows, c), F32), pltpu.VMEM((7, rows, min(c, LANE)), F32)


def _ln_stats(v):
    mu = jnp.mean(v, axis=-1, keepdims=True)
    xc = v - mu
    rstd = lax.rsqrt(jnp.mean(xc * xc, axis=-1, keepdims=True) + LN_EPS)
    return xc * rstd, rstd


def _mix_mid(p, h_f, h_b, w31, b31, clg, clb, tt):
    n, c = h_f.shape
    taps = w31.shape[0]
    lc = min(c, LANE)

    half = taps // 2

    def body(gr_ref, cv_ref, cg_ref, hf_ref, hb_ref, w_ref, b_ref, g_ref, bb_ref, y_ref, uc_ref, scr, rot):
        i = pl.program_id(0)
        gel, _ = _gelu_and_grad(gr_ref[...])
        y_ref[:, 0:c] = ((hf_ref[...] + hb_ref[...]) * gel).astype(BF16)

        def conv(nseg, seg):
            _zero_gaps(scr, nseg, seg, c)
            for s in range(nseg):
                rows = pl.ds(s * seg, seg)
                scr[pl.ds(_seg_base(s, seg), seg), :] = cv_ref[rows, :] * _sig(cg_ref[rows, :])
            for ch in range(c // lc):
                sl = slice(ch * lc, (ch + 1) * lc)
                _build_shifts(scr, rot, _conv_rows(nseg, seg), sl)
                for out0, base in _conv_chunks(nseg, seg):
                    acc = jnp.broadcast_to(b_ref[:, sl], (SEG, lc))
                    for k in range(taps):
                        acc = acc + _tap(scr, rot, base, k - half, SEG, sl) * w_ref[pl.ds(k, 1), sl]
                    uc_ref[pl.ds(out0, SEG), sl] = acc

        @pl.when(i == 0)
        def _():
            conv(1, tt)

        @pl.when(i > 0)
        def _():
            conv(tt // SEG, SEG)

        xh, _ = _ln_stats(uc_ref[...])
        v = xh * g_ref[...] + bb_ref[...]
        y_ref[:, c:2 * c] = (v * _sig(v)).astype(BF16)

    return pl.pallas_call(
        body, name="mix_mid", grid=(n // tt,),
        in_specs=[_row_spec(tt, c, 1), _row_spec(tt, c, 2), _row_spec(tt, c, 3), _row_spec(tt, c), _row_spec(tt, c),
                  _full_spec(w31.shape), _full_spec((1, c)), _full_spec((1, c)), _full_spec((1, c))],
        out_specs=(_row_spec(tt, 2 * c), _row_spec(tt, c)),
        out_shape=(jax.ShapeDtypeStruct((n, 2 * c), BF16), jax.ShapeDtypeStruct((n, c), F32)),
        scratch_shapes=list(_conv_scratch(tt, c)),
        compiler_params=_cp(("parallel",)),
    )(p, p, p, h_f, h_b, w31, b31, clg, clb)


def _mix_mid_bwd(dymix, p, h_f, h_b, uc, w31, clg, clb, tt):
    n, c = h_f.shape
    taps = w31.shape[0]
    lc = min(c, LANE)
    sum_rows = 8 * ((taps + 3 + 7) // 8)

    half = taps // 2
    nt = n // tt

    def body(dyr_ref, dyc_ref, gr_ref, cv_ref, cg_ref, hf_ref, hb_ref, uc_ref, w_ref, g_ref, bb_ref,
             dh_ref, dp_ref, sum_ref, uscr, urot, dscr, drot, duc_scr, wacc):
        i = pl.program_id(0)

        @pl.when(i == 0)
        def _():
            sum_ref[...] = jnp.zeros(sum_ref.shape, F32)
            wacc[...] = jnp.zeros(wacc.shape, F32)

        dyr = dyr_ref[...]
        gel, dgel = _gelu_and_grad(gr_ref[...])
        dh_ref[...] = dyr * gel
        dp_ref[:, 0:c] = (dyr * (hf_ref[...] + hb_ref[...]) * dgel).astype(BF16)

        xh, rstd = _ln_stats(uc_ref[...])
        v = xh * g_ref[...] + bb_ref[...]
        sg = _sig(v)
        dv = dyc_ref[...] * (sg * (1.0 + v * (1.0 - sg)))
        dxh = dv * g_ref[...]
        m1 = jnp.mean(dxh, axis=-1, keepdims=True)
        m2 = jnp.mean(dxh * xh, axis=-1, keepdims=True)
        duc = rstd * (dxh - m1 - xh * m2)
        sum_ref[pl.ds(taps, 1), :] += _sum0(duc)
        sum_ref[pl.ds(taps + 1, 1), :] += _sum0(dv * xh)
        sum_ref[pl.ds(taps + 2, 1), :] += _sum0(dv)

        duc_scr[...] = duc

        def conv_bwd(nseg, seg):
            n_rows = _conv_rows(nseg, seg)
            _zero_gaps(dscr, nseg, seg, c)
            _zero_gaps(uscr, nseg, seg, c)
            for s in range(nseg):
                rows = pl.ds(s * seg, seg)
                dscr[pl.ds(_seg_base(s, seg), seg), :] = duc_scr[rows, :]
                uscr[pl.ds(_seg_base(s, seg), seg), :] = cv_ref[rows, :] * _sig(cg_ref[rows, :])
            for ch in range(c // lc):
                sl = slice(ch * lc, (ch + 1) * lc)
                _build_shifts(dscr, drot, n_rows, sl)
                _build_shifts(uscr, urot, n_rows, sl)
                chunks = _conv_chunks(nseg, seg)
                for out0, base in chunks:
                    rows = pl.ds(out0, SEG)
                    du = jnp.zeros((SEG, lc), F32)
                    for k in range(taps):
                        du = du + _tap(dscr, drot, base, k - half, SEG, sl) * w_ref[pl.ds(taps - 1 - k, 1), sl]
                    sgc, cv = _sig(cg_ref[rows, sl]), cv_ref[rows, sl]
                    dp_ref[rows, c + sl.start:c + sl.stop] = (du * sgc).astype(BF16)
                    dp_ref[rows, 2 * c + sl.start:2 * c + sl.stop] = (du * cv * sgc * (1.0 - sgc)).astype(BF16)
                for g0 in range(0, len(chunks), 4):
                    group = chunks[g0:g0 + 4]
                    duc_q = [duc_scr[pl.ds(out0, SEG), sl] for out0, _ in group]
                    for k in range(taps):
                        part = jnp.zeros((8, lc), F32)
                        for dq, (_, base) in zip(duc_q, group):
                            prod = dq * _tap(uscr, urot, base, k - half, SEG, sl)
                            part = part + jnp.sum(prod.reshape(SEG // 8, 8, lc), axis=0)
                        wacc[pl.ds(8 * k, 8), sl] += part

        @pl.when(i == 0)
        def _():
            conv_bwd(1, tt)

        @pl.when(i > 0)
        def _():
            conv_bwd(tt // SEG, SEG)

        @pl.when(i == nt - 1)
        def _():
            for k in range(taps):
                sum_ref[pl.ds(k, 1), :] = _sum0(wacc[pl.ds(8 * k, 8), :])

    scr_a, rot_a = _conv_scratch(tt, c)
    return pl.pallas_call(
        body, name="mix_mid_bwd", grid=(n // tt,),
        in_specs=[_row_spec(tt, c, 0), _row_spec(tt, c, 1), _row_spec(tt, c, 1), _row_spec(tt, c, 2),
                  _row_spec(tt, c, 3), _row_spec(tt, c), _row_spec(tt, c), _row_spec(tt, c),
                  _full_spec(w31.shape), _full_spec((1, c)), _full_spec((1, c))],
        out_specs=(_row_spec(tt, c), _row_spec(tt, 3 * c), _full_spec((sum_rows, c))),
        out_shape=(jax.ShapeDtypeStruct((n, c), F32), jax.ShapeDtypeStruct((n, 3 * c), BF16),
                   jax.ShapeDtypeStruct((sum_rows, c), F32)),
        scratch_shapes=[scr_a, rot_a, scr_a, rot_a, pltpu.VMEM((tt, c), F32), pltpu.VMEM((8 * taps, c), F32)],
        compiler_params=_cp(("arbitrary",), 48),
    )(dymix, dymix, p, p, p, h_f, h_b, uc, w31, clg, clb)


def _ada_fwd(cc, w_ada, b_cols):
    depth, d, wc = w_ada.shape
    tn = _pick(wc, (768, 512, 384, 256, 128))

    def body(c_ref, w_ref, b_ref, o_ref):
        cv = c_ref[...]
        o_ref[...] = _dot((cv * _sig(cv)).astype(BF16), w_ref[...].astype(BF16), 1, 0) + b_ref[...]

    return pl.pallas_call(
        body, name="ada_fwd", grid=(depth, wc // tn),
        in_specs=[_full_spec((MOD_ROWS, d)), pl.BlockSpec((None, d, tn), lambda l, j: (l, 0, j)),
                  pl.BlockSpec((None, 1, tn), lambda l, j: (l, 0, j))],
        out_specs=pl.BlockSpec((None, MOD_ROWS, tn), lambda l, j: (l, 0, j)),
        out_shape=jax.ShapeDtypeStruct((depth, MOD_ROWS, wc), F32),
        compiler_params=_cp(("parallel", "parallel")),
    )(cc, w_ada, b_cols)


def _ada_bwd(cc, dm, w_ada):
    depth, d, wc = w_ada.shape
    tn = _pick(wc, (768, 512, 384, 256, 128))

    def body(c_ref, dm_ref, w_ref, gw_ref, gc_ref):
        cv = c_ref[...]
        sg = _sig(cv)
        dmb = dm_ref[...].astype(BF16)
        gw_ref[...] = _dot((cv * sg).astype(BF16), dmb, 0, 0)

        @pl.when(jnp.logical_and(pl.program_id(0) == 0, pl.program_id(1) == 0))
        def _():
            gc_ref[...] = jnp.zeros(gc_ref.shape, F32)

        gc_ref[...] += _dot(dmb, w_ref[...].astype(BF16), 1, 1) * (sg * (1.0 + cv * (1.0 - sg)))

    return pl.pallas_call(
        body, name="ada_bwd", grid=(depth, wc // tn),
        in_specs=[_full_spec((MOD_ROWS, d)), pl.BlockSpec((None, MOD_ROWS, tn), lambda l, j: (l, 0, j)),
                  pl.BlockSpec((None, d, tn), lambda l, j: (l, 0, j))],
        out_specs=(pl.BlockSpec((None, d, tn), lambda l, j: (l, 0, j)), _full_spec((MOD_ROWS, d))),
        out_shape=(jax.ShapeDtypeStruct((depth, d, wc), F32), jax.ShapeDtypeStruct((MOD_ROWS, d), F32)),
        compiler_params=_cp(("arbitrary", "arbitrary")),
    )(cc, dm, w_ada)


def _adam(grads, w, m, v):
    depth, rows, cols = w.shape
    assert len(grads) == depth
    n_slab = grads[0].shape[0]
    tr = _pick(rows, (128, 64, 32, 16, 8))
    c1, c2 = 1.0 - ADAM_B1 ** ADAM_STEP, 1.0 - ADAM_B2 ** ADAM_STEP

    def body(*refs):
        g_refs = refs[:depth]
        w_ref, m_ref, v_ref, go_ref, do_ref, mo_ref, vo_ref = refs[depth:]
        for q in range(depth):
            @pl.when(pl.program_id(0) == q)
            def _(q=q):
                g = g_refs[q][0].astype(F32)
                for k in range(1, n_slab):
                    g = g + g_refs[q][k].astype(F32)
                go_ref[...] = g

        g = go_ref[...]
        m_new = ADAM_B1 * m_ref[...] + (1.0 - ADAM_B1) * g
        v_new = ADAM_B2 * v_ref[...] + (1.0 - ADAM_B2) * (g * g)
        mo_ref[...] = m_new
        vo_ref[...] = v_new
        do_ref[...] = -ADAM_LR * ((m_new / c1) / (jnp.sqrt(v_new / c2) + ADAM_EPS) + ADAM_WD * w_ref[...])

    blk = pl.BlockSpec((None, tr, cols), lambda l, i: (l, i, 0))
    g_specs = [pl.BlockSpec((n_slab, tr, cols), lambda l, i, q=q: (0, jnp.where(l == q, i, 0), 0))
               for q in range(depth)]
    out = jax.ShapeDtypeStruct((depth, rows, cols), F32)
    return pl.pallas_call(
        body, name="adamw", grid=(depth, rows // tr),
        in_specs=g_specs + [blk, blk, blk],
        out_specs=(blk, blk, blk, blk), out_shape=(out, out, out, out),
        compiler_params=_cp(("parallel", "parallel"), 48),
    )(*grads, w, m, v)


_WEIGHTS = ['c_ctx', 'w_ada', 'b_ada', 'ln_g', 'ln_b', 'ff1_in', 'ff1_out', 'ff2_in', 'ff2_out', 'w_in', 'conv4_w',
            'conv4_b', 'w_rg', 'b_rg', 'w_ig', 'b_ig', 'lam', 'conv31_w', 'conv31_b', 'cln_g', 'cln_b', 'w_out', 'b_out']
_SMALL_SHARDED = {'ln_g': 2, 'ln_b': 2, 'conv4_w': 2, 'w_rg': 3, 'b_rg': 2, 'w_ig': 3, 'b_ig': 2, 'lam': 2,
                  'conv31_w': 2}
_BIG_SHARDED = {'ff1_in': 2, 'ff1_out': 1, 'ff2_in': 2, 'ff2_out': 1, 'w_in': 2, 'w_out': 1}
_SMALL = [n for n in _WEIGHTS if n not in _BIG_SHARDED and n != 'w_ada']


def kernel(x, c, ctx, c_ctx, w_ada, b_ada, ln_g, ln_b, ff1_in, ff1_out, ff2_in, ff2_out, w_in, conv4_w, conv4_b, w_rg, b_rg, w_ig, b_ig, lam, conv31_w, conv31_b, cln_g, cln_b, w_out, b_out, loss_target, m_c_ctx, m_w_ada, m_b_ada, m_ln_g, m_ln_b, m_ff1_in, m_ff1_out, m_ff2_in, m_ff2_out, m_w_in, m_conv4_w, m_conv4_b, m_w_rg, m_b_rg, m_w_ig, m_b_ig, m_lam, m_conv31_w, m_conv31_b, m_cln_g, m_cln_b, m_w_out, m_b_out, v_c_ctx, v_w_ada, v_b_ada, v_ln_g, v_ln_b, v_ff1_in, v_ff1_out, v_ff2_in, v_ff2_out, v_w_in, v_conv4_w, v_conv4_b, v_w_rg, v_b_rg, v_w_ig, v_b_ig, v_lam, v_conv31_w, v_conv31_b, v_cln_g, v_cln_b, v_w_out, v_b_out):
    wts = dict(c_ctx=c_ctx, w_ada=w_ada, b_ada=b_ada, ln_g=ln_g, ln_b=ln_b, ff1_in=ff1_in, ff1_out=ff1_out,
               ff2_in=ff2_in, ff2_out=ff2_out, w_in=w_in, conv4_w=conv4_w, conv4_b=conv4_b, w_rg=w_rg, b_rg=b_rg,
               w_ig=w_ig, b_ig=b_ig, lam=lam, conv31_w=conv31_w, conv31_b=conv31_b, cln_g=cln_g, cln_b=cln_b,
               w_out=w_out, b_out=b_out)
    mom = dict(zip(_WEIGHTS, (m_c_ctx, m_w_ada, m_b_ada, m_ln_g, m_ln_b, m_ff1_in, m_ff1_out, m_ff2_in, m_ff2_out,
                              m_w_in, m_conv4_w, m_conv4_b, m_w_rg, m_b_rg, m_w_ig, m_b_ig, m_lam, m_conv31_w,
                              m_conv31_b, m_cln_g, m_cln_b, m_w_out, m_b_out)))
    var = dict(zip(_WEIGHTS, (v_c_ctx, v_w_ada, v_b_ada, v_ln_g, v_ln_b, v_ff1_in, v_ff1_out, v_ff2_in, v_ff2_out,
                              v_w_in, v_conv4_w, v_conv4_b, v_w_rg, v_b_rg, v_w_ig, v_b_ig, v_lam, v_conv31_w,
                              v_conv31_b, v_cln_g, v_cln_b, v_w_out, v_b_out)))

    depth, d, wc = w_ada.shape
    t_lat, t_ctx = x.shape[1], ctx.shape[1]
    tt, n_tok = t_ctx, t_ctx + x.shape[1]
    assert t_lat % tt == 0 and tt % SEG == 0 and tt & (tt - 1) == 0 and d % 2 == 0
    ch = d // 2
    alpha = (2.0 * depth) ** 0.25
    me = 4 * lax.axis_index("x") + 2 * lax.axis_index("y") + lax.axis_index("c")

    names = list(_SMALL_SHARDED)
    got = _gather_list([wts[k] for k in names] + [c], F32)
    full = {k: _unshard(g, _SMALL_SHARDED[k]) for k, g in zip(names, got[:-1])}
    c_all = got[-1].reshape(N_DEV, d)
    cc = jnp.concatenate([c_all, jnp.pad(c_ctx[None, :], ((0, MOD_ROWS - N_DEV - 1), (0, 0)))], axis=0)

    b_cols = _my_block(b_ada, me, 1)[:, None, :]
    (m_cols,) = _gather_list([_ada_fwd(cc, w_ada, b_cols)], F32)
    m_full = jnp.moveaxis(m_cols, 0, 2).reshape(depth, MOD_ROWS, N_DEV * wc)
    m_lat = lax.dynamic_index_in_dim(m_full, me, axis=1, keepdims=False).reshape(depth, 9, d)
    m_ctx = m_full[:, N_DEV].reshape(depth, 9, d)
    modtab = jnp.pad(jnp.stack([m_ctx, m_lat], axis=1), ((0, 0), (0, 0), (0, MOD_ROWS - 9), (0, 0)))

    big = {k: [None] * depth for k in _BIG_SHARDED}
    half = {}

    def shard(k, l):
        return wts[k][l].astype(BF16)

    def landed(k, l, g):
        big[k][l] = _unshard(g, _BIG_SHARDED[k] - 1)

    def jobs(*specs):
        return [(stage, k, l) for stage, k, l in specs if l < depth]

    def comm_of(todo):
        return [(stage, shard(k, l) if stage == "spread" else half[k, l]) for stage, k, l in todo]

    def settle(todo, got):
        for (stage, k, l), g in zip(todo, got):
            if stage == "spread":
                half[k, l] = g
            else:
                landed(k, l, g)

    got = _gather_list([shard('ff1_in', 0), shard('ff1_out', 0)], BF16)
    landed('ff1_in', 0, got[0])
    landed('ff1_out', 0, got[1])
    wr_b, wi_b = full['w_rg'].astype(BF16), full['w_ig'].astype(BF16)

    def row(v):
        return v.reshape(1, -1)

    s = jnp.concatenate([ctx[0], x[0]], axis=0)
    saved = []
    h_next = _modulate(s, modtab[0], 0, 1, tt)
    for l in range(depth):
        mt, sv = modtab[l], {}
        sv['s0'] = s
        sv['h1'] = h_next
        todo = jobs(("spread", 'w_in', l), ("spread", 'w_out', l), ("spread", 'ff2_in', l))
        (sv['z1'], sv['a1']), got = _mm_swiglu(sv['h1'], big['ff1_in'][l], comm=comm_of(todo))
        settle(todo, got)
        todo = jobs(("relay", 'w_in', l), ("relay", 'w_out', l), ("relay", 'ff2_in', l), ("spread", 'ff2_out', l))
        sv['f1'], got = _mm_nn("ffn_out", sv['a1'], big['ff1_out'][l], comm=comm_of(todo))
        settle(todo, got)
        sv['r1'], sv['s1'], sv['h2'] = _res_ln(s, sv['f1'], mt, 2, 0.5, row(full['ln_g'][l, 0]), row(full['ln_b'][l, 0]),
                                               alpha, tt, nxt=(mt, 3, 4))
        todo = jobs(("relay", 'ff2_out', l))
        sv['p'], got = _mm_nn("mix_in", sv['h2'], big['w_in'][l], comm=comm_of(todo))
        settle(todo, got)
        sv['xr'] = _conv4(sv['p'], full['conv4_w'][l], row(conv4_b[l]), tt, ch)
        for dr_, rev in ((0, False), (1, True)):
            sv['h', dr_], sv['hin', dr_] = _scan(
                sv['xr'], wr_b[l, dr_], wi_b[l, dr_], row(full['b_rg'][l, dr_]), row(full['b_ig'][l, dr_]),
                row(full['lam'][l, dr_]), rev, tt)
        sv['ymix'], sv['uc'] = _mix_mid(sv['p'], sv['h', 0], sv['h', 1], full['conv31_w'][l], row(conv31_b[l]),
                                        row(cln_g[l]), row(cln_b[l]), tt)
        sv['y'], _ = _mm_nn("mix_out", sv['ymix'], big['w_out'][l], bias=row(b_out[l]))
        sv['r2'], sv['s2'], sv['h3'] = _res_ln(sv['s1'], sv['y'], mt, 5, 1.0, row(full['ln_g'][l, 1]),
                                               row(full['ln_b'][l, 1]), alpha, tt, nxt=(mt, 6, 7))
        todo = jobs(("spread", 'ff1_in', l + 1), ("spread", 'ff1_out', l + 1))
        (sv['z3'], sv['a3']), got = _mm_swiglu(sv['h3'], big['ff2_in'][l], comm=comm_of(todo))
        settle(todo, got)
        todo = jobs(("relay", 'ff1_in', l + 1), ("relay", 'ff1_out', l + 1))
        sv['f3'], got = _mm_nn("ffn_out", sv['a3'], big['ff2_out'][l], comm=comm_of(todo))
        settle(todo, got)
        outs = _res_ln(sv['s2'], sv['f3'], mt, 8, 0.5, row(full['ln_g'][l, 2]), row(full['ln_b'][l, 2]), alpha, tt,
                       nxt=(modtab[l + 1], 0, 1) if l + 1 < depth else None)
        sv['r3'], s = outs[0], outs[1]
        h_next = outs[2] if l + 1 < depth else None
        saved.append(sv)

    cot, loss_blk = _loss_head(s, loss_target[0], tt)
    loss = lax.psum(loss_blk[0, 0], ("x", "y", "c"))

    gx = {k: [None] * depth for k in _BIG_SHARDED}
    gsm = {k: [None] * depth for k in ('ln_g', 'ln_b', 'conv4_w', 'conv4_b', 'b_rg', 'b_ig', 'lam',
                                       'conv31_w', 'conv31_b', 'cln_g', 'cln_b', 'b_out', 'dm')}
    g_gate = {'w_rg': [None] * depth, 'w_ig': [None] * depth}
    taps4, taps31 = conv4_w.shape[1], conv31_w.shape[1]
    heads, hs = w_rg.shape[2], w_rg.shape[4]

    def scatter(pend):
        return [("scatter", g) for _, _, g in pend]

    def exchanged(pend, got):
        for (k, l, _), g in zip(pend, got):
            gx[k][l] = g

    def ffn_bwd(cot, l, mt, r, f, z, a, h, s_in, w_in_name, w_out_name, ln_idx, i_gate, i_scale, pend_short, pend_long,
                final=False):
        dr, dfb, tot, per = _ln_bwd(cot, r, f, mt, i_gate, 0.5, row(full['ln_g'][l, ln_idx]), tt)
        dz, got = _mm_nt_swiglu_bwd(dfb, big[w_out_name][l], z, comm=scatter(pend_short))
        exchanged(pend_short, got)
        g_out, _ = _mm_tn("ffn_out_dw", a, dfb)
        pend_out = [(w_out_name, l, g_out.reshape(N_DEV, -1, d))]
        if final:
            g_in, got = _mm_tn("ffn_in_dw", h, dz, b_halves=True, shard_cols=True, comm=scatter(pend_long))
            exchanged(pend_long, got)
            pend_out.append((w_in_name, l, g_in))
            dh, got = _mm_nt("ffn_in_dx", dz, big[w_in_name][l], a_halves=True, comm=scatter(pend_out))
            exchanged(pend_out, got)
            pend_in = []
        else:
            dh, got = _mm_nt("ffn_in_dx", dz, big[w_in_name][l], a_halves=True, comm=scatter(pend_long))
            exchanged(pend_long, got)
            g_in, got = _mm_tn("ffn_in_dw", h, dz, b_halves=True, shard_cols=True, comm=scatter(pend_out))
            exchanged(pend_out, got)
            pend_in = [(w_in_name, l, g_in)]
        cot, per2 = _mod_bwd(dr, dh, s_in, mt, i_scale, alpha, tt)
        return cot, tot, per, per2, pend_in

    pend = []
    for l in reversed(range(depth)):
        mt, sv = modtab[l], saved[l]
        cot, tot3, per3, mod3, pend_ff2_in = ffn_bwd(cot, l, mt, sv['r3'], sv['f3'], sv['z3'], sv['a3'], sv['h3'],
                                                     sv['s2'], 'ff2_in', 'ff2_out', 2, 8, 7, [], pend)
        dr, dyb, tot2, per2 = _ln_bwd(cot, sv['r2'], sv['y'], mt, 5, 1.0, row(full['ln_g'][l, 1]), tt)
        dymix, _ = _mm_nt("mix_out_dx", dyb, big['w_out'][l])
        g_w_out, _ = _mm_tn("mix_out_dw", sv['ymix'], dyb)
        pend = [('w_out', l, g_w_out.reshape(N_DEV, -1, d))]
        dh_rec, dp_rest, csum = _mix_mid_bwd(dymix, sv['p'], sv['h', 0], sv['h', 1], sv['uc'], full['conv31_w'][l],
                                             row(cln_g[l]), row(cln_b[l]), tt)
        dxr, gw_r, gw_i, gsum = None, [], [], []
        for dr_, rev in ((0, False), (1, True)):
            dxr, gwr, gwi, gs = _scan_bwd(
                sv['xr'], dh_rec, sv['h', dr_], sv['hin', dr_], wr_b[l, dr_], wi_b[l, dr_],
                row(full['b_rg'][l, dr_]), row(full['b_ig'][l, dr_]), row(full['lam'][l, dr_]), rev, tt, dxr)
            gw_r.append(gwr)
            gw_i.append(gwi)
            gsum.append(gs)
        dp_xr, c4sum = _conv4_bwd(dxr, sv['p'], full['conv4_w'][l], tt, ch)
        dp = jnp.concatenate([dp_xr, dp_rest], axis=1)
        dh, got = _mm_nt("mix_in_dx", dp, big['w_in'][l], comm=scatter(pend))
        exchanged(pend, got)
        g_w_in, _ = _mm_tn("mix_in_dw", sv['h2'], dp, shard_cols=True)
        cot, mod2 = _mod_bwd(dr, dh, sv['s1'], mt, 4, alpha, tt)
        cot, tot1, per1, mod1, pend = ffn_bwd(cot, l, mt, sv['r1'], sv['f1'], sv['z1'], sv['a1'], sv['h1'], sv['s0'],
                                              'ff1_in', 'ff1_out', 0, 2, 1, [('w_in', l, g_w_in)], pend_ff2_in,
                                              final=(l == 0))

        gsm['ln_g'][l] = jnp.stack([tot1[0], tot2[0], tot3[0]])
        gsm['ln_b'][l] = jnp.stack([tot1[1], tot2[1], tot3[1]])
        gsm['b_out'][l] = tot2[2]
        gsm['conv4_w'][l], gsm['conv4_b'][l] = c4sum[:taps4], c4sum[taps4]
        for k, gws in (('w_rg', gw_r), ('w_ig', gw_i)):
            slabs = jnp.stack(gws).reshape(2, heads, N_DEV, hs // N_DEV, hs)
            g_gate[k][l] = jnp.moveaxis(slabs, 2, 0).reshape(N_DEV, -1, hs).astype(BF16)
        gsm['b_rg'][l] = jnp.stack([g[0] for g in gsum])
        gsm['b_ig'][l] = jnp.stack([g[1] for g in gsum])
        gsm['lam'][l] = jnp.stack([g[2] for g in gsum])
        gsm['conv31_w'][l], gsm['conv31_b'][l] = csum[:taps31], csum[taps31]
        gsm['cln_g'][l], gsm['cln_b'][l] = csum[taps31 + 1], csum[taps31 + 2]
        gsm['dm'][l] = jnp.stack([jnp.stack([mod1[8 * st + 1], mod1[8 * st], per1[8 * st],
                                             mod2[8 * st + 1], mod2[8 * st], per2[8 * st],
                                             mod3[8 * st + 1], mod3[8 * st], per3[8 * st]]) for st in (0, 1)])
    grad_x = cot[t_ctx:][None]
    assert not pend
    gate_x = _all_to_all(g_gate['w_rg'] + g_gate['w_ig'])

    keys = list(gsm)
    partial = [jnp.stack(gsm[k]) for k in keys]
    shapes = [a.shape for a in partial]
    gathered = _all_gather(_pack(partial, F32))
    totals = dict(zip(keys, _unpack(_sum_slabs(gathered), shapes)))
    dm_all = _unpack(gathered, shapes, lead=(N_DEV,))[keys.index('dm')]
    dm_tot = totals['dm'].reshape(depth, 2, 9 * d)
    dm_sum = _sum_slabs(jnp.stack([_pack([dm_tot[:, 0]], F32), _pack([dm_tot[:, 1]], F32)]))
    grad = {'b_ada': _unpack(dm_sum, [(depth, 9 * d)])[0]}
    for k in ('ln_g', 'ln_b', 'conv4_w', 'b_rg', 'b_ig', 'lam', 'conv31_w'):
        grad[k] = _my_block(totals[k], me, 2)
    for k in ('conv4_b', 'conv31_b', 'cln_g', 'cln_b', 'b_out'):
        grad[k] = totals[k]

    dm_rows = jnp.concatenate([jnp.moveaxis(dm_all[:, :, 1].reshape(N_DEV, depth, 9 * d), 0, 1),
                               jnp.pad(dm_tot[:, 0][:, None, :], ((0, 0), (0, MOD_ROWS - N_DEV - 1), (0, 0)))], axis=1)
    dm_cols = lax.dynamic_slice_in_dim(dm_rows, me * wc, wc, axis=2)
    grad_w_ada, gcc = _ada_bwd(cc, dm_cols, w_ada)
    gcc_all = _all_gather(_pack([gcc], F32))
    grad['c_ctx'] = _unpack(_sum_slabs(gcc_all), [(MOD_ROWS, d)])[0][N_DEV]

    res = {}
    small = [k for k in _SMALL if k not in g_gate]
    gs, ws, ms, vs = (_pack([src[k] for k in small], F32).reshape(1, -1, 8 * LANE) for src in (grad, wts, mom, var))
    outs = _adam([gs], ws, ms, vs)
    small_shapes = [wts[k].shape for k in small]
    for k, vals in zip(small, zip(*(_unpack(o[0], small_shapes) for o in outs))):
        res[k] = vals
    res['w_ada'] = _adam([grad_w_ada[l][None] for l in range(depth)], w_ada, m_w_ada, v_w_ada)
    for j, k in enumerate(g_gate):
        flat = (depth, -1, hs)
        outs = _adam([gate_x[j * depth + l] for l in range(depth)], wts[k].reshape(flat), mom[k].reshape(flat),
                     var[k].reshape(flat))
        res[k] = tuple(o.reshape(wts[k].shape) for o in outs)
    for k in _BIG_SHARDED:
        res[k] = _adam(gx[k], wts[k], mom[k], var[k])

    out = [loss, grad_x]
    for j in range(4):
        out += [res[k][j] for k in _WEIGHTS]
    return tuple(out)
```

```python
import math

import jax
import jax.numpy as jnp
from jax import lax
from jax.experimental import pallas as pl
from jax.experimental.pallas import tpu as pltpu

F32 = jnp.float32
BF16 = jnp.bfloat16
N_DEV = 8
MESH = pl.DeviceIdType.MESH
LN_EPS = 1e-6
RG_C = 8.0
SEG = 64
CONV_PAD = 16
ADAM_LR, ADAM_B1, ADAM_B2, ADAM_EPS, ADAM_WD, ADAM_STEP = 0.001, 0.9, 0.999, 1e-08, 0.01, 10
LANE = 128
PACK_QUANTUM = 16 * LANE
PACK_ROWS = 512
MOD_ROWS = 16
GELU_K = 0.7978845608028654
GELU_C = 0.044715

_TM = (768, 640, 512, 384, 256, 128)
_TN = (1024, 512, 256, 128)
_TK = (2816, 1024, 512, 256, 128)
_SUB = (256, 128)


def _cp(sem=None, vmem_mb=40):
    return pltpu.CompilerParams(dimension_semantics=sem, vmem_limit_bytes=vmem_mb * 2 ** 20)


def _pick(n, cands):
    for cand in cands:
        if n % cand == 0:
            return cand
    return n


def _sig(v):
    return 1.0 / (1.0 + jnp.exp(-v))


def _dot(a, b, ca, cb):
    return lax.dot_general(a, b, (((ca,), (cb,)), ((), ())), preferred_element_type=F32)


def _sum0(v):
    return jnp.sum(v, axis=0, keepdims=True)


def _gelu_and_grad(v):
    inner = GELU_K * (v + GELU_C * v * v * v)
    t = jnp.tanh(inner)
    gel = 0.5 * v * (1.0 + t)
    dgel = 0.5 * (1.0 + t) + 0.5 * v * (1.0 - t * t) * GELU_K * (1.0 + 3.0 * GELU_C * v * v)
    return gel, dgel


def _softplus_neg(lam):
    y = jnp.exp(-jnp.abs(lam))
    u = 1.0 + y
    log1p = jnp.where(u == 1.0, y, jnp.log(u) * (y / (u - 1.0)))
    return jnp.maximum(-lam, 0.0) + log1p


def _neg_expm1(z):
    ser = -z * (1.0 + z * (1 / 2 + z * (1 / 6 + z * (1 / 24 + z * (1 / 120 + z * (1 / 720 + z * (1 / 5040)))))))
    return jnp.where(z > -0.25, ser, 1.0 - jnp.exp(z))


def _row_spec(tt, w, col=0):
    return pl.BlockSpec((tt, w), lambda i: (i, col))


def _full_spec(shape):
    nd = len(shape)
    return pl.BlockSpec(shape, lambda *_: (0,) * nd)


def _mod_spec(d):
    return pl.BlockSpec((None, MOD_ROWS, d), lambda i: (jnp.minimum(i, 1), 0, 0))


def _stream_sum_spec(w):
    return pl.BlockSpec((8, w), lambda i: (jnp.minimum(i, 1), 0))


def _acc_rows(ref, init, rows):
    @pl.when(init)
    def _():
        ref[...] = jnp.zeros(ref.shape, F32)

    for k, row in enumerate(rows):
        ref[pl.ds(k, 1), :] += row


def _all_gather(v):
    def body(x_ref, out_ref, send_sems, recv_sems, local_sem):
        x, y, c = lax.axis_index("x"), lax.axis_index("y"), lax.axis_index("c")
        me, sibling = (x, y, c), (x, y, 1 - c)
        chips = [(1 - x, y), (x, 1 - y), (1 - x, 1 - y)]

        def slot(px, py, pc):
            return out_ref.at[4 * px + 2 * py + pc]

        def copy(k, block, to, src=None):
            return pltpu.make_async_remote_copy(
                src_ref=slot(*block) if src is None else src, dst_ref=slot(*block),
                send_sem=send_sems.at[k], recv_sem=recv_sems.at[k], device_id=to, device_id_type=MESH)

        mine = pltpu.make_async_copy(x_ref, slot(*me), local_sem)
        mine.start()
        first = [copy(0, me, sibling, src=x_ref)]
        first += [copy(1 + j, me, (*chip, c), src=x_ref) for j, chip in enumerate(chips)]
        for cp in first:
            cp.start()
        passed = [copy(4 + j, (*chip, c), sibling) for j, chip in enumerate(chips)]
        for j, chip in enumerate(chips):
            copy(1 + j, (*chip, c), me).wait_recv()
            passed[j].start()
        copy(0, sibling, me).wait_recv()
        for j, chip in enumerate(chips):
            copy(4 + j, (*chip, 1 - c), me).wait_recv()
        for cp in first + passed:
            cp.wait_send()
        mine.wait()

    return pl.pallas_call(
        body, name="all_gather",
        out_shape=jax.ShapeDtypeStruct((N_DEV,) + v.shape, v.dtype),
        in_specs=[pl.BlockSpec(memory_space=pltpu.HBM)],
        out_specs=pl.BlockSpec(memory_space=pltpu.HBM),
        scratch_shapes=[pltpu.SemaphoreType.DMA((7,)), pltpu.SemaphoreType.DMA((7,)), pltpu.SemaphoreType.DMA],
    )(v)


_PEER_FLIPS = [(0, 0, 1), (1, 0, 0), (0, 1, 0), (1, 1, 0), (1, 0, 1), (0, 1, 1), (1, 1, 1)]


def _all_to_all(parts):
    n_parts = len(parts)

    def body(*refs):
        x_refs, out_ref = refs[:n_parts], refs[n_parts]
        send_sems, recv_sems, local_sems = refs[n_parts + 1:]
        x, y, c = lax.axis_index("x"), lax.axis_index("y"), lax.axis_index("c")
        me = 4 * x + 2 * y + c
        local, remote = [], []
        for l in range(n_parts):
            cp = pltpu.make_async_copy(x_refs[l].at[me], out_ref.at[l, me], local_sems.at[l])
            cp.start()
            local.append(cp)
            for k, (fx, fy, fc) in enumerate(_PEER_FLIPS):
                px = 1 - x if fx else x
                py = 1 - y if fy else y
                pc = 1 - c if fc else c
                cp = pltpu.make_async_remote_copy(
                    src_ref=x_refs[l].at[4 * px + 2 * py + pc], dst_ref=out_ref.at[l, me],
                    send_sem=send_sems.at[7 * l + k], recv_sem=recv_sems.at[7 * l + k],
                    device_id=(px, py, pc), device_id_type=MESH)
                cp.start()
                remote.append(cp)
        for cp in remote:
            cp.wait()
        for cp in local:
            cp.wait()

    shape = parts[0].shape
    return pl.pallas_call(
        body, name="all_to_all",
        out_shape=jax.ShapeDtypeStruct((n_parts,) + shape, parts[0].dtype),
        in_specs=[pl.BlockSpec(memory_space=pltpu.HBM)] * n_parts,
        out_specs=pl.BlockSpec(memory_space=pltpu.HBM),
        scratch_shapes=[pltpu.SemaphoreType.DMA((7 * n_parts,)), pltpu.SemaphoreType.DMA((7 * n_parts,)),
                        pltpu.SemaphoreType.DMA((n_parts,))],
    )(*parts)


def _pack(arrs, dtype):
    pieces = []
    for a in arrs:
        flat = a.astype(dtype).reshape(-1)
        flat = jnp.pad(flat, (0, (-flat.shape[0]) % PACK_QUANTUM))
        pieces.append(flat.reshape(-1, LANE))
    packed = jnp.concatenate(pieces, axis=0) if len(pieces) > 1 else pieces[0]
    return jnp.pad(packed, ((0, (-packed.shape[0]) % PACK_ROWS), (0, 0)))


def _unpack(packed, shapes, lead=()):
    flat = packed.reshape(lead + (-1,))
    outs, off = [], 0
    for sh in shapes:
        n = math.prod(sh)
        outs.append(flat[..., off:off + n].reshape(lead + tuple(sh)))
        off += n + (-n) % PACK_QUANTUM
    return outs


def _gather_list(arrs, dtype):
    gathered = _all_gather(_pack(arrs, dtype))
    return _unpack(gathered, [a.shape for a in arrs], lead=(N_DEV,))


def _unshard(g, axis):
    y = jnp.moveaxis(g, 0, axis)
    sh = y.shape
    return y.reshape(sh[:axis] + (sh[axis] * sh[axis + 1],) + sh[axis + 2:])


def _my_block(full, me, axis):
    size = full.shape[axis] // N_DEV
    return lax.dynamic_slice_in_dim(full, me * size, size, axis=axis)


def _sum_slabs(v):
    n_slab, rows, width = v.shape
    tr = _pick(rows, (1024, 512, 256, 128, 64, 32, 16, 8))

    def body(v_ref, o_ref):
        acc = v_ref[0]
        for k in range(1, n_slab):
            acc = acc + v_ref[k]
        o_ref[...] = acc

    return pl.pallas_call(
        body, name="sum_slabs", grid=(rows // tr,),
        in_specs=[pl.BlockSpec((n_slab, tr, width), lambda i: (0, i, 0))],
        out_specs=pl.BlockSpec((tr, width), lambda i: (i, 0)),
        out_shape=jax.ShapeDtypeStruct((rows, width), F32),
        compiler_params=_cp(("parallel",)),
    )(v)


N_LOCAL = 1


def _comm_copies(kinds, src_refs, dst_refs, send_sems, recv_sems, local_sems):
    x, y, c = lax.axis_index("x"), lax.axis_index("y"), lax.axis_index("c")
    me = 4 * x + 2 * y + c
    copies = []

    def remote(j, k, src, dst, peer):
        copies.append(pltpu.make_async_remote_copy(
            src_ref=src, dst_ref=dst, send_sem=send_sems.at[7 * j + k], recv_sem=recv_sems.at[7 * j + k],
            device_id=peer, device_id_type=MESH))

    def local(j, k, src, dst):
        copies.append(pltpu.make_async_copy(src, dst, local_sems.at[N_LOCAL * j + k]))

    for j, kind in enumerate(kinds):
        src, dst = src_refs[j], dst_refs[j]
        peers = [(1 - x if fx else x, 1 - y if fy else y, 1 - c if fc else c) for fx, fy, fc in _PEER_FLIPS]
        if kind == "scatter":
            local(j, 0, src.at[me], dst.at[me])
            for k, (px, py, pc) in enumerate(peers):
                remote(j, k, src.at[4 * px + 2 * py + pc], dst.at[me], (px, py, pc))
        elif kind == "spread":
            local(j, 0, src, dst.at[me])
            for k, peer in enumerate(peers[:4]):
                remote(j, k, src, dst.at[me], peer)
        else:
            for k, (px, py, pc) in enumerate(peers[1:4]):
                remote(j, 4 + k, dst.at[4 * px + 2 * py + pc], dst.at[4 * px + 2 * py + pc], peers[0])
    return copies


def _mm(name, grid, nk, ins, in_specs, out_shape, out_specs, acc_shape, prod, fin, comm=(), vmem_mb=48):
    n_in, n_out, n_c = len(ins), len(out_shape), len(comm)
    kinds = [kind for kind, _ in comm]
    srcs = [a for _, a in comm]
    dsts = [jax.ShapeDtypeStruct(((N_DEV,) + a.shape) if kind == "spread" else a.shape, a.dtype) for kind, a in comm]
    hbm = pl.BlockSpec(memory_space=pltpu.HBM)

    def body(*refs):
        in_refs, refs = refs[:n_in], refs[n_in:]
        src_refs, refs = refs[:n_c], refs[n_c:]
        out_refs, refs = refs[:n_out], refs[n_out:]
        dst_refs, scratch = refs[:n_c], refs[n_c:]
        if nk > 1:
            acc, scratch = scratch[0], scratch[1:]
        ids = [pl.program_id(a) for a in range(3)]
        if n_c:
            copies = _comm_copies(kinds, src_refs, dst_refs, *scratch)

            @pl.when(jnp.logical_and(ids[0] == 0, jnp.logical_and(ids[1] == 0, ids[2] == 0)))
            def _():
                for cp in copies:
                    cp.start()

        if nk == 1:
            fin(prod(*in_refs), in_refs, out_refs)
        else:
            @pl.when(ids[2] == 0)
            def _():
                acc[...] = jnp.zeros(acc.shape, F32)

            for idx, val in enumerate(prod(*in_refs)):
                acc[idx] += val

            @pl.when(ids[2] == nk - 1)
            def _():
                fin(tuple(acc[idx] for idx in range(acc_shape[0])), in_refs, out_refs)

        if n_c:
            @pl.when(jnp.logical_and(ids[0] == grid[0] - 1, jnp.logical_and(ids[1] == grid[1] - 1, ids[2] == nk - 1)))
            def _():
                for cp in copies:
                    cp.wait()

    scratch = [pltpu.VMEM(acc_shape, F32)] if nk > 1 else []
    if n_c:
        scratch += [pltpu.SemaphoreType.DMA((7 * n_c,)), pltpu.SemaphoreType.DMA((7 * n_c,)),
                    pltpu.SemaphoreType.DMA((N_LOCAL * n_c,))]
    sem = ("arbitrary",) * 3 if n_c else ("parallel", "parallel", "arbitrary")
    res = pl.pallas_call(
        body, name=name + (f"_x{n_c}" if n_c else ""), grid=grid,
        in_specs=list(in_specs) + [hbm] * n_c, out_specs=tuple(out_specs) + (hbm,) * n_c,
        out_shape=tuple(out_shape) + tuple(dsts), scratch_shapes=scratch,
        input_output_aliases={n_in + j: n_out + j for j, kind in enumerate(kinds) if kind == "relay"},
        compiler_params=_cp(sem, vmem_mb),
    )(*ins, *srcs)
    return res[:n_out], list(res[n_out:])


def _mm_nn(name, a, b, bias=None, comm=()):
    m, kk = a.shape
    nn = b.shape[1]
    tm, tn = _pick(m, _TM), _pick(nn, _TN)
    tk = kk if kk <= 2048 else _pick(kk, _TK)
    nk = kk // tk
    ins = [a, b]
    specs = [pl.BlockSpec((tm, tk), lambda i, j, k: (i, k)), pl.BlockSpec((tk, tn), lambda i, j, k: (k, j))]
    if bias is not None:
        ins.append(bias)
        specs.append(pl.BlockSpec((1, tn), lambda i, j, k: (0, j)))

    def prod(a_ref, b_ref, *_):
        return (_dot(a_ref[...], b_ref[...], 1, 0),)

    def fin(vals, in_refs, out_refs):
        out_refs[0][...] = vals[0] if bias is None else vals[0] + in_refs[2][...]

    outs, got = _mm(name, (m // tm, nn // tn, nk), nk, ins, specs,
                    (jax.ShapeDtypeStruct((m, nn), F32),), (pl.BlockSpec((tm, tn), lambda i, j, k: (i, j)),),
                    (1, tm, tn), prod, fin, comm)
    return outs[0], got


def _mm_swiglu(h, w1, comm=()):
    m, kk = h.shape
    f = w1.shape[1] // 2
    tm, tn = _pick(m, _TM), _pick(f, (512, 256, 128))
    nj = f // tn
    sub = _pick(tm, _SUB)

    def prod(*_):
        return ()

    def fin(_, in_refs, out_refs):
        h_ref, wg_ref, wu_ref = in_refs
        for q in range(tm // sub):
            rows = pl.ds(q * sub, sub)
            hq = h_ref[rows, :]
            zg, zu = _dot(hq, wg_ref[...], 1, 0), _dot(hq, wu_ref[...], 1, 0)
            out_refs[0][0, rows, :] = zg.astype(BF16)
            out_refs[0][1, rows, :] = zu.astype(BF16)
            out_refs[1][rows, :] = (zg * _sig(zg) * zu).astype(BF16)

    return _mm("ffn_in", (m // tm, nj, 1), 1, [h, w1, w1],
               [pl.BlockSpec((tm, kk), lambda i, j, k: (i, 0)),
                pl.BlockSpec((kk, tn), lambda i, j, k: (0, j)),
                pl.BlockSpec((kk, tn), lambda i, j, k: (0, j + nj))],
               (jax.ShapeDtypeStruct((2, m, f), BF16), jax.ShapeDtypeStruct((m, f), BF16)),
               (pl.BlockSpec((2, tm, tn), lambda i, j, k: (0, i, j)), pl.BlockSpec((tm, tn), lambda i, j, k: (i, j))),
               (2, tm, tn), prod, fin, comm)


def _mm_nt_swiglu_bwd(dfb, w2, z, comm=()):
    m, kk = dfb.shape
    f = w2.shape[0]
    tm, tn = _pick(m, _TM), _pick(f, (512, 256, 128))
    sub = _pick(tm, _SUB)

    def prod(*_):
        return ()

    def fin(_, in_refs, out_refs):
        a_ref, b_ref, z_ref = in_refs
        for q in range(tm // sub):
            rows = pl.ds(q * sub, sub)
            da = _dot(a_ref[rows, :], b_ref[...], 1, 1)
            zg, zu = z_ref[0, rows, :].astype(F32), z_ref[1, rows, :].astype(F32)
            sg = _sig(zg)
            out_refs[0][0, rows, :] = (da * zu * (sg * (1.0 + zg * (1.0 - sg)))).astype(BF16)
            out_refs[0][1, rows, :] = (da * (zg * sg)).astype(BF16)

    outs, got = _mm("ffn_out_dx", (m // tm, f // tn, 1), 1, [dfb, w2, z],
                    [pl.BlockSpec((tm, kk), lambda i, j, k: (i, 0)),
                     pl.BlockSpec((tn, kk), lambda i, j, k: (j, 0)),
                     pl.BlockSpec((2, tm, tn), lambda i, j, k: (0, i, j))],
                    (jax.ShapeDtypeStruct((2, m, f), BF16),),
                    (pl.BlockSpec((2, tm, tn), lambda i, j, k: (0, i, j)),),
                    (1, tm, tn), prod, fin, comm)
    return outs[0], got


def _mm_nt(name, a, b, a_halves=False, comm=()):
    nn, kk = b.shape
    m = a.shape[1] if a_halves else a.shape[0]
    kh = kk // 2 if a_halves else kk
    tm, tn = _pick(m, _TM), _pick(nn, _TN)
    tk = kh if (kh <= 4096 and not a_halves) else _pick(kh, _TK)
    nk, nkh = kk // tk, kh // tk
    if a_halves:
        a_spec = pl.BlockSpec((None, tm, tk), lambda i, j, k: (k // nkh, i, k % nkh))
    else:
        a_spec = pl.BlockSpec((tm, tk), lambda i, j, k: (i, k))

    def prod(a_ref, b_ref):
        return (_dot(a_ref[...], b_ref[...], 1, 1),)

    def fin(vals, in_refs, out_refs):
        out_refs[0][...] = vals[0]

    outs, got = _mm(name, (m // tm, nn // tn, nk), nk, [a, b],
                    [a_spec, pl.BlockSpec((tn, tk), lambda i, j, k: (j, k))],
                    (jax.ShapeDtypeStruct((m, nn), F32),), (pl.BlockSpec((tm, tn), lambda i, j, k: (i, j)),),
                    (1, tm, tn), prod, fin, comm)
    return outs[0], got


def _mm_tn(name, a, b, b_halves=False, shard_cols=False, comm=()):
    kt, m = a.shape
    nn = 2 * b.shape[2] if b_halves else b.shape[1]
    tk, tm = _pick(kt, _TM), _pick(m, (1408, 1024, 512, 256, 128))
    shard = nn // N_DEV
    per = 1
    while shard_cols and 2 * per * shard <= nn // 2 and ((per * shard) % 256 or per * shard < 1024):
        per *= 2
    tn = per * shard if shard_cols else _pick(nn, _TN)
    njh = (nn // 2) // tn if b_halves else 0
    if b_halves:
        b_spec = pl.BlockSpec((None, tk, tn), lambda i, j, k: (j // njh, k, j % njh))
    else:
        b_spec = pl.BlockSpec((tk, tn), lambda i, j, k: (k, j))
    if shard_cols:
        out_shape = jax.ShapeDtypeStruct((N_DEV, m, shard), BF16)
        out_spec = pl.BlockSpec((per, tm, shard), lambda i, j, k: (j, i, 0))
    else:
        out_shape = jax.ShapeDtypeStruct((m, nn), BF16)
        out_spec = pl.BlockSpec((tm, tn), lambda i, j, k: (i, j))

    def prod(a_ref, b_ref):
        return (_dot(a_ref[...], b_ref[...], 0, 0),)

    def fin(vals, in_refs, out_refs):
        r = vals[0].astype(BF16)
        if shard_cols:
            for q in range(per):
                out_refs[0][q] = r[:, q * shard:(q + 1) * shard]
        else:
            out_refs[0][...] = r

    outs, got = _mm(name, (m // tm, nn // tn, kt // tk), kt // tk, [a, b],
                    [pl.BlockSpec((tk, tm), lambda i, j, k: (k, i)), b_spec],
                    (out_shape,), (out_spec,), (1, tm, tn), prod, fin, comm, vmem_mb=56)
    return outs[0], got


def _modulate(s, modtab, i_shift, i_scale, tt):
    n, d = s.shape

    def body(s_ref, m_ref, o_ref):
        shift, scale = m_ref[pl.ds(i_shift, 1), :], m_ref[pl.ds(i_scale, 1), :]
        o_ref[...] = (s_ref[...] * (1.0 + scale) + shift).astype(BF16)

    return pl.pallas_call(
        body, name="modulate", grid=(n // tt,),
        in_specs=[_row_spec(tt, d), _mod_spec(d)], out_specs=_row_spec(tt, d),
        out_shape=jax.ShapeDtypeStruct((n, d), BF16), compiler_params=_cp(("parallel",)),
    )(s, modtab)


def _res_ln(s, f, modtab, i_gate, coef, gam, bet, alpha, tt, nxt=None):
    n, d = s.shape

    def body(s_ref, f_ref, m_ref, g_ref, b_ref, *refs):
        gate = m_ref[pl.ds(i_gate, 1), :]
        r = alpha * s_ref[...] + (coef * gate) * f_ref[...]
        mu = jnp.mean(r, axis=-1, keepdims=True)
        xc = r - mu
        var = jnp.mean(xc * xc, axis=-1, keepdims=True)
        out = xc * lax.rsqrt(var + LN_EPS) * g_ref[...] + b_ref[...]
        if nxt is None:
            r_ref, o_ref = refs
        else:
            m2_ref, r_ref, o_ref, h_ref = refs
            shift, scale = m2_ref[pl.ds(nxt[1], 1), :], m2_ref[pl.ds(nxt[2], 1), :]
            h_ref[...] = (out * (1.0 + scale) + shift).astype(BF16)
        r_ref[...] = r
        o_ref[...] = out

    ins = [s, f, modtab, gam, bet] + ([] if nxt is None else [nxt[0]])
    n_h = 0 if nxt is None else 1
    return pl.pallas_call(
        body, name="res_ln", grid=(n // tt,),
        in_specs=[_row_spec(tt, d), _row_spec(tt, d), _mod_spec(d), _full_spec((1, d)), _full_spec((1, d))]
        + [_mod_spec(d)] * n_h,
        out_specs=(_row_spec(tt, d),) * (2 + n_h),
        out_shape=(jax.ShapeDtypeStruct((n, d), F32),) * 2 + (jax.ShapeDtypeStruct((n, d), BF16),) * n_h,
        compiler_params=_cp(("parallel",)),
    )(*ins)


def _ln_bwd(do, r, f, modtab, i_gate, coef, gam, tt, alpha=None):
    fused = isinstance(do, tuple)
    n, d = r.shape

    def body(*refs):
        if fused:
            dru_ref, dhu_ref, su_ref, mu_ref = refs[:4]
            r_ref, f_ref, m_ref, g_ref, dr_ref, dfb_ref, tot_ref, str_ref, up_ref = refs[4:]
        else:
            do_ref, r_ref, f_ref, m_ref, g_ref, dr_ref, dfb_ref, tot_ref, str_ref = refs
        i = pl.program_id(0)
        r = r_ref[...]
        mu = jnp.mean(r, axis=-1, keepdims=True)
        xc = r - mu
        rstd = lax.rsqrt(jnp.mean(xc * xc, axis=-1, keepdims=True) + LN_EPS)
        xh = xc * rstd
        if fused:
            dh_up = dhu_ref[...]
            dout = alpha * dru_ref[...] + dh_up * (1.0 + mu_ref[pl.ds(do[4], 1), :])
            _acc_rows(up_ref, i <= 1, [_sum0(dh_up * su_ref[...]), _sum0(dh_up)])
        else:
            dout = do_ref[...]
        dxh = dout * g_ref[...]
        m1 = jnp.mean(dxh, axis=-1, keepdims=True)
        m2 = jnp.mean(dxh * xh, axis=-1, keepdims=True)
        dr = rstd * (dxh - m1 - xh * m2)
        dr_ref[...] = dr
        dfb = (coef * m_ref[pl.ds(i_gate, 1), :]) * dr
        dfb_ref[...] = dfb.astype(BF16)
        _acc_rows(tot_ref, i == 0, [_sum0(dout * xh), _sum0(dout), _sum0(dfb)])
        _acc_rows(str_ref, i <= 1, [_sum0(coef * f_ref[...] * dr)])

    row = _row_spec(tt, d)
    if fused:
        ins, in_specs = list(do[:4]), [row, row, row, _mod_spec(d)]
    else:
        ins, in_specs = [do], [row]
    n_up = 1 if fused else 0
    return pl.pallas_call(
        body, name="ln_bwd", grid=(n // tt,),
        in_specs=in_specs + [row, row, _mod_spec(d), _full_spec((1, d))],
        out_specs=(row, row, _full_spec((8, d)), _stream_sum_spec(d)) + (_stream_sum_spec(d),) * n_up,
        out_shape=(jax.ShapeDtypeStruct((n, d), F32), jax.ShapeDtypeStruct((n, d), BF16),
                   jax.ShapeDtypeStruct((8, d), F32), jax.ShapeDtypeStruct((16, d), F32))
        + (jax.ShapeDtypeStruct((16, d), F32),) * n_up,
        compiler_params=_cp(("arbitrary",), 48),
    )(*ins, r, f, modtab, gam)


def _mod_bwd(dr, dh, s, modtab, i_scale, alpha, tt):
    n, d = dr.shape

    def body(dr_ref, dh_ref, s_ref, m_ref, o_ref, str_ref):
        i = pl.program_id(0)
        dh_v = dh_ref[...]
        o_ref[...] = alpha * dr_ref[...] + dh_v * (1.0 + m_ref[pl.ds(i_scale, 1), :])
        _acc_rows(str_ref, i <= 1, [_sum0(dh_v * s_ref[...]), _sum0(dh_v)])

    return pl.pallas_call(
        body, name="mod_bwd", grid=(n // tt,),
        in_specs=[_row_spec(tt, d), _row_spec(tt, d), _row_spec(tt, d), _mod_spec(d)],
        out_specs=(_row_spec(tt, d), _stream_sum_spec(d)),
        out_shape=(jax.ShapeDtypeStruct((n, d), F32), jax.ShapeDtypeStruct((16, d), F32)),
        compiler_params=_cp(("arbitrary",)),
    )(dr, dh, s, modtab)


def _loss_head(s, target, tt):
    n, d = s.shape
    nt = n // tt

    def body(s_ref, t_ref, do_ref, loss_ref, acc):
        i = pl.program_id(0)

        @pl.when(i == 0)
        def _():
            acc[...] = jnp.zeros(acc.shape, F32)
            do_ref[...] = jnp.zeros(do_ref.shape, F32)

        @pl.when(i > 0)
        def _():
            err = s_ref[...] - t_ref[...]
            do_ref[...] = err / d
            acc[...] += jnp.sum((err * err).reshape(tt // 8, 8, d), axis=0)

        @pl.when(i == nt - 1)
        def _():
            loss_ref[...] = jnp.full(loss_ref.shape, jnp.sum(acc[...]) * (0.5 / d), F32)

    return pl.pallas_call(
        body, name="loss_head", grid=(nt,),
        in_specs=[_row_spec(tt, d), pl.BlockSpec((tt, d), lambda i: (jnp.maximum(i - 1, 0), 0))],
        out_specs=(_row_spec(tt, d), _full_spec((8, LANE))),
        out_shape=(jax.ShapeDtypeStruct((n, d), F32), jax.ShapeDtypeStruct((8, LANE), F32)),
        scratch_shapes=[pltpu.VMEM((8, d), F32)],
        compiler_params=_cp(("arbitrary",)),
    )(s, target)


def _halo_specs(tt, c, n):
    nb, last = tt // 8, n // 8 - 1
    return [pl.BlockSpec((tt, c), lambda i: (i, 0)),
            pl.BlockSpec((8, c), lambda i: (jnp.maximum(i * nb - 1, 0), 0)),
            pl.BlockSpec((8, c), lambda i: (jnp.minimum((i + 1) * nb, last), 0))]


def _fill_halo(scr, main_ref, prev_ref, next_ref, i, nt, tt):
    has_prev = i >= 2
    has_next = jnp.logical_and(i >= 1, i < nt - 1)
    scr[pl.ds(0, 8), :] = jnp.where(has_prev, prev_ref[...], 0.0)
    scr[pl.ds(8, tt), :] = main_ref[...]
    scr[pl.ds(8 + tt, 8), :] = jnp.where(has_next, next_ref[...], 0.0)


def _conv4(p, w, b, tt, c):
    n = p.shape[0]
    nt, taps = n // tt, w.shape[0]
    left = taps // 2

    def body(x_ref, xp_ref, xn_ref, w_ref, b_ref, o_ref, scr):
        _fill_halo(scr, x_ref, xp_ref, xn_ref, pl.program_id(0), nt, tt)
        acc = jnp.broadcast_to(b_ref[...], (tt, c))
        for k in range(taps):
            acc = acc + w_ref[pl.ds(k, 1), :] * scr[pl.ds(8 + k - left, tt), :]
        o_ref[...] = acc

    return pl.pallas_call(
        body, name="conv4", grid=(nt,),
        in_specs=_halo_specs(tt, c, n) + [_full_spec(w.shape), _full_spec((1, c))],
        out_specs=_row_spec(tt, c), out_shape=jax.ShapeDtypeStruct((n, c), F32),
        scratch_shapes=[pltpu.VMEM((tt + 16, c), F32)], compiler_params=_cp(("parallel",)),
    )(p, p, p, w, b)


def _conv4_bwd(dxr, p, w, tt, c):
    n = p.shape[0]
    nt, taps = n // tt, w.shape[0]
    left = taps // 2

    def body(d_ref, dp_ref, dn_ref, x_ref, xp_ref, xn_ref, w_ref, o_ref, sum_ref, dscr, xscr):
        i = pl.program_id(0)
        _fill_halo(dscr, d_ref, dp_ref, dn_ref, i, nt, tt)
        _fill_halo(xscr, x_ref, xp_ref, xn_ref, i, nt, tt)
        acc = jnp.zeros((tt, c), F32)
        for k in range(taps):
            acc = acc + w_ref[pl.ds(k, 1), :] * dscr[pl.ds(8 - (k - left), tt), :]
        o_ref[...] = acc.astype(BF16)
        d = d_ref[...]
        rows = [_sum0(d * xscr[pl.ds(8 + k - left, tt), :]) for k in range(taps)] + [_sum0(d)]
        _acc_rows(sum_ref, i == 0, rows)

    return pl.pallas_call(
        body, name="conv4_bwd", grid=(nt,),
        in_specs=_halo_specs(tt, c, n) + _halo_specs(tt, c, n) + [_full_spec(w.shape)],
        out_specs=(_row_spec(tt, c), _full_spec((8, c))),
        out_shape=(jax.ShapeDtypeStruct((n, c), BF16), jax.ShapeDtypeStruct((8, c), F32)),
        scratch_shapes=[pltpu.VMEM((tt + 16, c), F32), pltpu.VMEM((tt + 16, c), F32)],
        compiler_params=_cp(("arbitrary",)),
    )(dxr, dxr, dxr, p, p, p, w)


def _head_gates(xv, wr, wi, br, bi, lam):
    xb = xv.astype(BF16)
    r = _sig(_dot(xb, wr, 1, 0) + br)
    ig = _sig(_dot(xb, wi, 1, 0) + bi)
    sp = _softplus_neg(lam)
    log_a = (-RG_C) * r * sp
    return r, ig, sp, jnp.exp(log_a), jnp.sqrt(_neg_expm1(2.0 * log_a))


def _tile_scan(a, b, increasing, tt, rows):
    s = 1
    while s < tt:
        if increasing:
            a_sh, b_sh, ok = pltpu.roll(a, s, 0), pltpu.roll(b, s, 0), rows >= s
        else:
            a_sh, b_sh, ok = pltpu.roll(a, tt - s, 0), pltpu.roll(b, tt - s, 0), rows < tt - s
        b = jnp.where(ok, a * b_sh + b, b)
        a = jnp.where(ok, a * a_sh, a)
        s *= 2
    return a, b


def _scan_tile_index(step, nt, reverse):
    return jnp.where(step == 0, 0, nt - step) if reverse else step


def _scan(xr, wr, wi, br, bi, lam, reverse, tt):
    n, c = xr.shape
    nt = n // tt
    heads, hs = wr.shape[0], wr.shape[2]
    lc = min(c, LANE)

    def tile(i):
        return _scan_tile_index(i, nt, reverse)

    def body(x_ref, wr_ref, wi_ref, br_ref, bi_ref, lam_ref, h_ref, hin_ref, a_scr, b_scr, carry):
        i = pl.program_id(0)

        @pl.when(i == 0)
        def _():
            carry[...] = jnp.zeros(carry.shape, F32)

        for hd in range(heads):
            sl = slice(hd * hs, (hd + 1) * hs)
            xv = x_ref[:, sl]
            _, ig, _, a, sq = _head_gates(xv, wr_ref[hd], wi_ref[hd], br_ref[:, sl], bi_ref[:, sl], lam_ref[:, sl])
            a_scr[:, sl] = a
            b_scr[:, sl] = sq * (ig * xv)
        hin_ref[...] = jnp.broadcast_to(carry[...], (8, c))
        rows = lax.broadcasted_iota(jnp.int32, (tt, lc), 0)
        for ch in range(c // lc):
            sl = slice(ch * lc, (ch + 1) * lc)
            big_a, big_b = _tile_scan(a_scr[:, sl], b_scr[:, sl], not reverse, tt, rows)
            h_ref[:, sl] = big_a * carry[:, sl] + big_b
        carry[...] = h_ref[pl.ds(0 if reverse else tt - 1, 1), :]

    return pl.pallas_call(
        body, name="lru_scan", grid=(nt,),
        in_specs=[pl.BlockSpec((tt, c), lambda i: (tile(i), 0)), _full_spec(wr.shape), _full_spec(wi.shape),
                  _full_spec((1, c)), _full_spec((1, c)), _full_spec((1, c))],
        out_specs=(pl.BlockSpec((tt, c), lambda i: (tile(i), 0)), pl.BlockSpec((None, 8, c), lambda i: (tile(i), 0, 0))),
        out_shape=(jax.ShapeDtypeStruct((n, c), F32), jax.ShapeDtypeStruct((nt, 8, c), F32)),
        scratch_shapes=[pltpu.VMEM((tt, c), F32), pltpu.VMEM((tt, c), F32), pltpu.VMEM((1, c), F32)],
        compiler_params=_cp(("arbitrary",)),
    )(xr, wr, wi, br, bi, lam)


def _scan_bwd(xr, dh, h, hin, wr, wi, br, bi, lam, reverse, tt, dx_prev=None):
    n, c = xr.shape
    nt = n // tt
    heads, hs = wr.shape[0], wr.shape[2]
    lc = min(c, LANE)
    inc = not reverse
    first, last = (0, tt - 1) if inc else (tt - 1, 0)

    def tile(i):
        return _scan_tile_index(nt - 1 - i, nt, reverse)

    def body(*refs):
        x_ref, dh_ref, h_ref, hin_ref, wr_ref, wi_ref, br_ref, bi_ref, lam_ref = refs[:9]
        refs = refs[9:]
        if dx_prev is not None:
            dxp_ref, refs = refs[0], refs[1:]
        dx_ref, dwr_ref, dwi_ref, sum_ref, a_scr, r_scr, i_scr, sq_scr, g_scr, da_scr, u_scr, ucarry = refs
        i = pl.program_id(0)

        @pl.when(i == 0)
        def _():
            ucarry[...] = jnp.zeros(ucarry.shape, F32)
            dwr_ref[...] = jnp.zeros(dwr_ref.shape, F32)
            dwi_ref[...] = jnp.zeros(dwi_ref.shape, F32)
            sum_ref[...] = jnp.zeros(sum_ref.shape, F32)

        for hd in range(heads):
            sl = slice(hd * hs, (hd + 1) * hs)
            r, ig, _, a, sq = _head_gates(x_ref[:, sl], wr_ref[hd], wi_ref[hd], br_ref[:, sl], bi_ref[:, sl],
                                          lam_ref[:, sl])
            a_scr[:, sl], r_scr[:, sl], i_scr[:, sl], sq_scr[:, sl] = a, r, ig, sq

        rows = lax.broadcasted_iota(jnp.int32, (tt, lc), 0)
        to_prev = 1 if inc else tt - 1
        to_next = tt - 1 if inc else 1
        for ch in range(c // lc):
            sl = slice(ch * lc, (ch + 1) * lc)
            a, dhv = a_scr[:, sl], dh_ref[:, sl]
            big_a, big_b = _tile_scan(a, a * dhv, not inc, tt, rows)
            u_in = ucarry[:, sl]
            u = big_a * u_in + big_b
            u_scr[:, sl] = u
            g = dhv + jnp.where(rows == last, u_in, pltpu.roll(u, to_next, 0))
            g_scr[:, sl] = g
            h_prev = jnp.where(rows == first, hin_ref[pl.ds(0, 1), sl], pltpu.roll(h_ref[:, sl], to_prev, 0))
            da_scr[:, sl] = g * h_prev
        ucarry[...] = u_scr[pl.ds(first, 1), :]

        for hd in range(heads):
            sl = slice(hd * hs, (hd + 1) * hs)
            xv, a, r, ig, sq = x_ref[:, sl], a_scr[:, sl], r_scr[:, sl], i_scr[:, sl], sq_scr[:, sl]
            g, lam_v = g_scr[:, sl], lam_ref[:, sl]
            sp = _softplus_neg(lam_v)
            d_sq = g * ig * xv
            d_em = d_sq * 0.5 / sq
            d_log_a = (da_scr[:, sl] - 2.0 * a * d_em) * a
            dzr = (d_log_a * ((-RG_C) * sp)) * r * (1.0 - r)
            dzi = (g * sq * xv) * ig * (1.0 - ig)
            dzr_b, dzi_b, xb = dzr.astype(BF16), dzi.astype(BF16), xv.astype(BF16)
            dx = g * sq * ig + _dot(dzr_b, wr_ref[hd], 1, 1) + _dot(dzi_b, wi_ref[hd], 1, 1)
            if dx_prev is not None:
                dx = dx + dxp_ref[:, sl]
            dx_ref[:, sl] = dx
            dwr_ref[hd] += _dot(xb, dzr_b, 0, 0)
            dwi_ref[hd] += _dot(xb, dzi_b, 0, 0)
            sum_ref[pl.ds(0, 1), sl] += _sum0(dzr)
            sum_ref[pl.ds(1, 1), sl] += _sum0(dzi)
            sum_ref[pl.ds(2, 1), sl] += _sum0(d_log_a * ((-RG_C) * r)) * (-_sig(-lam_v))

    tile_spec = pl.BlockSpec((tt, c), lambda i: (tile(i), 0))
    ins = [xr, dh, h, hin, wr, wi, br, bi, lam]
    in_specs = [tile_spec, tile_spec, tile_spec, pl.BlockSpec((None, 8, c), lambda i: (tile(i), 0, 0)),
                _full_spec(wr.shape), _full_spec(wi.shape), _full_spec((1, c)), _full_spec((1, c)), _full_spec((1, c))]
    if dx_prev is not None:
        ins.append(dx_prev)
        in_specs.append(tile_spec)
    return pl.pallas_call(
        body, name="lru_scan_bwd", grid=(nt,), in_specs=in_specs,
        out_specs=(tile_spec, _full_spec(wr.shape), _full_spec(wi.shape), _full_spec((8, c))),
        out_shape=(jax.ShapeDtypeStruct((n, c), F32), jax.ShapeDtypeStruct(wr.shape, F32),
                   jax.ShapeDtypeStruct(wi.shape, F32), jax.ShapeDtypeStruct((8, c), F32)),
        scratch_shapes=[pltpu.VMEM((tt, c), F32)] * 7 + [pltpu.VMEM((1, c), F32)],
        compiler_params=_cp(("arbitrary",)),
    )(*ins)


def _conv_rows(nseg, seg):
    return nseg * (seg + CONV_PAD) + CONV_PAD


def _seg_base(s, seg):
    return CONV_PAD + s * (seg + CONV_PAD)


def _zero_gaps(scr, nseg, seg, c):
    for s in range(nseg + 1):
        scr[pl.ds(s * (seg + CONV_PAD), CONV_PAD), :] = jnp.zeros((CONV_PAD, c), F32)


def _build_shifts(scr, rot, n_rows, sl):
    for b in range(1, 8):
        rot[b - 1, pl.ds(0, n_rows - 8), :] = scr[pl.ds(b, n_rows - 8), sl]


def _tap(scr, rot, base, off, seg, sl):
    a, b = divmod(off, 8)
    if b == 0:
        return scr[pl.ds(base + 8 * a, seg), sl]
    return rot[b - 1, pl.ds(base + 8 * a, seg), :]


def _conv_chunks(nseg, seg):
    return [(s * seg + q * SEG, _seg_base(s, seg) + q * SEG) for s in range(nseg) for q in range(seg // SEG)]


def _conv_scratch(tt, c):
    rows = max(_conv_rows(1, tt), _conv_rows(tt // SEG, SEG))
    return pltpu.VMEM((rows, c), F32), pltpu.VMEM((7, rows, min(c, LANE)), F32)


def _ln_stats(v):
    mu = jnp.mean(v, axis=-1, keepdims=True)
    xc = v - mu
    rstd = lax.rsqrt(jnp.mean(xc * xc, axis=-1, keepdims=True) + LN_EPS)
    return xc * rstd, rstd


def _mix_mid(p, h_f, h_b, w31, b31, clg, clb, tt):
    n, c = h_f.shape
    taps = w31.shape[0]
    lc = min(c, LANE)

    half = taps // 2

    def body(gr_ref, cv_ref, cg_ref, hf_ref, hb_ref, w_ref, b_ref, g_ref, bb_ref, y_ref, uc_ref, scr, rot):
        i = pl.program_id(0)
        gel, _ = _gelu_and_grad(gr_ref[...])
        y_ref[:, 0:c] = ((hf_ref[...] + hb_ref[...]) * gel).astype(BF16)

        def conv(nseg, seg):
            _zero_gaps(scr, nseg, seg, c)
            for s in range(nseg):
                rows = pl.ds(s * seg, seg)
                scr[pl.ds(_seg_base(s, seg), seg), :] = cv_ref[rows, :] * _sig(cg_ref[rows, :])
            for ch in range(c // lc):
                sl = slice(ch * lc, (ch + 1) * lc)
                _build_shifts(scr, rot, _conv_rows(nseg, seg), sl)
                for out0, base in _conv_chunks(nseg, seg):
                    acc = jnp.broadcast_to(b_ref[:, sl], (SEG, lc))
                    for k in range(taps):
                        acc = acc + _tap(scr, rot, base, k - half, SEG, sl) * w_ref[pl.ds(k, 1), sl]
                    uc_ref[pl.ds(out0, SEG), sl] = acc

        @pl.when(i == 0)
        def _():
            conv(1, tt)

        @pl.when(i > 0)
        def _():
            conv(tt // SEG, SEG)

        xh, _ = _ln_stats(uc_ref[...])
        v = xh * g_ref[...] + bb_ref[...]
        y_ref[:, c:2 * c] = (v * _sig(v)).astype(BF16)

    return pl.pallas_call(
        body, name="mix_mid", grid=(n // tt,),
        in_specs=[_row_spec(tt, c, 1), _row_spec(tt, c, 2), _row_spec(tt, c, 3), _row_spec(tt, c), _row_spec(tt, c),
                  _full_spec(w31.shape), _full_spec((1, c)), _full_spec((1, c)), _full_spec((1, c))],
        out_specs=(_row_spec(tt, 2 * c), _row_spec(tt, c)),
        out_shape=(jax.ShapeDtypeStruct((n, 2 * c), BF16), jax.ShapeDtypeStruct((n, c), F32)),
        scratch_shapes=list(_conv_scratch(tt, c)),
        compiler_params=_cp(("parallel",)),
    )(p, p, p, h_f, h_b, w31, b31, clg, clb)


def _mix_mid_bwd(dymix, p, h_f, h_b, uc, w31, clg, clb, tt):
    n, c = h_f.shape
    taps = w31.shape[0]
    lc = min(c, LANE)
    sum_rows = 8 * ((taps + 3 + 7) // 8)

    half = taps // 2
    nt = n // tt

    def body(dyr_ref, dyc_ref, gr_ref, cv_ref, cg_ref, hf_ref, hb_ref, uc_ref, w_ref, g_ref, bb_ref,
             dh_ref, dp_ref, sum_ref, uscr, urot, dscr, drot, duc_scr, wacc):
        i = pl.program_id(0)

        @pl.when(i == 0)
        def _():
            sum_ref[...] = jnp.zeros(sum_ref.shape, F32)
            wacc[...] = jnp.zeros(wacc.shape, F32)

        dyr = dyr_ref[...]
        gel, dgel = _gelu_and_grad(gr_ref[...])
        dh_ref[...] = dyr * gel
        dp_ref[:, 0:c] = (dyr * (hf_ref[...] + hb_ref[...]) * dgel).astype(BF16)

        xh, rstd = _ln_stats(uc_ref[...])
        v = xh * g_ref[...] + bb_ref[...]
        sg = _sig(v)
        dv = dyc_ref[...] * (sg * (1.0 + v * (1.0 - sg)))
        dxh = dv * g_ref[...]
        m1 = jnp.mean(dxh, axis=-1, keepdims=True)
        m2 = jnp.mean(dxh * xh, axis=-1, keepdims=True)
        duc = rstd * (dxh - m1 - xh * m2)
        sum_ref[pl.ds(taps, 1), :] += _sum0(duc)
        sum_ref[pl.ds(taps + 1, 1), :] += _sum0(dv * xh)
        sum_ref[pl.ds(taps + 2, 1), :] += _sum0(dv)

        duc_scr[...] = duc

        def conv_bwd(nseg, seg):
            n_rows = _conv_rows(nseg, seg)
            _zero_gaps(dscr, nseg, seg, c)
            _zero_gaps(uscr, nseg, seg, c)
            for s in range(nseg):
                rows = pl.ds(s * seg, seg)
                dscr[pl.ds(_seg_base(s, seg), seg), :] = duc_scr[rows, :]
                uscr[pl.ds(_seg_base(s, seg), seg), :] = cv_ref[rows, :] * _sig(cg_ref[rows, :])
            for ch in range(c // lc):
                sl = slice(ch * lc, (ch + 1) * lc)
                _build_shifts(dscr, drot, n_rows, sl)
                _build_shifts(uscr, urot, n_rows, sl)
                chunks = _conv_chunks(nseg, seg)
                for out0, base in chunks:
                    rows = pl.ds(out0, SEG)
                    du = jnp.zeros((SEG, lc), F32)
                    for k in range(taps):
                        du = du + _tap(dscr, drot, base, k - half, SEG, sl) * w_ref[pl.ds(taps - 1 - k, 1), sl]
                    sgc, cv = _sig(cg_ref[rows, sl]), cv_ref[rows, sl]
                    dp_ref[rows, c + sl.start:c + sl.stop] = (du * sgc).astype(BF16)
                    dp_ref[rows, 2 * c + sl.start:2 * c + sl.stop] = (du * cv * sgc * (1.0 - sgc)).astype(BF16)
                for g0 in range(0, len(chunks), 4):
                    group = chunks[g0:g0 + 4]
                    duc_q = [duc_scr[pl.ds(out0, SEG), sl] for out0, _ in group]
                    for k in range(taps):
                        part = jnp.zeros((8, lc), F32)
                        for dq, (_, base) in zip(duc_q, group):
                            prod = dq * _tap(uscr, urot, base, k - half, SEG, sl)
                            part = part + jnp.sum(prod.reshape(SEG // 8, 8, lc), axis=0)
                        wacc[pl.ds(8 * k, 8), sl] += part

        @pl.when(i == 0)
        def _():
            conv_bwd(1, tt)

        @pl.when(i > 0)
        def _():
            conv_bwd(tt // SEG, SEG)

        @pl.when(i == nt - 1)
        def _():
            for k in range(taps):
                sum_ref[pl.ds(k, 1), :] = _sum0(wacc[pl.ds(8 * k, 8), :])

    scr_a, rot_a = _conv_scratch(tt, c)
    return pl.pallas_call(
        body, name="mix_mid_bwd", grid=(n // tt,),
        in_specs=[_row_spec(tt, c, 0), _row_spec(tt, c, 1), _row_spec(tt, c, 1), _row_spec(tt, c, 2),
                  _row_spec(tt, c, 3), _row_spec(tt, c), _row_spec(tt, c), _row_spec(tt, c),
                  _full_spec(w31.shape), _full_spec((1, c)), _full_spec((1, c))],
        out_specs=(_row_spec(tt, c), _row_spec(tt, 3 * c), _full_spec((sum_rows, c))),
        out_shape=(jax.ShapeDtypeStruct((n, c), F32), jax.ShapeDtypeStruct((n, 3 * c), BF16),
                   jax.ShapeDtypeStruct((sum_rows, c), F32)),
        scratch_shapes=[scr_a, rot_a, scr_a, rot_a, pltpu.VMEM((tt, c), F32), pltpu.VMEM((8 * taps, c), F32)],
        compiler_params=_cp(("arbitrary",), 48),
    )(dymix, dymix, p, p, p, h_f, h_b, uc, w31, clg, clb)


def _ada_fwd(cc, w_ada, b_cols):
    depth, d, wc = w_ada.shape
    tn = _pick(wc, (768, 512, 384, 256, 128))

    def body(c_ref, w_ref, b_ref, o_ref):
        cv = c_ref[...]
        o_ref[...] = _dot((cv * _sig(cv)).astype(BF16), w_ref[...].astype(BF16), 1, 0) + b_ref[...]

    return pl.pallas_call(
        body, name="ada_fwd", grid=(depth, wc // tn),
        in_specs=[_full_spec((MOD_ROWS, d)), pl.BlockSpec((None, d, tn), lambda l, j: (l, 0, j)),
                  pl.BlockSpec((None, 1, tn), lambda l, j: (l, 0, j))],
        out_specs=pl.BlockSpec((None, MOD_ROWS, tn), lambda l, j: (l, 0, j)),
        out_shape=jax.ShapeDtypeStruct((depth, MOD_ROWS, wc), F32),
        compiler_params=_cp(("parallel", "parallel")),
    )(cc, w_ada, b_cols)


def _ada_bwd(cc, dm, w_ada):
    depth, d, wc = w_ada.shape
    tn = _pick(wc, (768, 512, 384, 256, 128))

    def body(c_ref, dm_ref, w_ref, gw_ref, gc_ref):
        cv = c_ref[...]
        sg = _sig(cv)
        dmb = dm_ref[...].astype(BF16)
        gw_ref[...] = _dot((cv * sg).astype(BF16), dmb, 0, 0)

        @pl.when(jnp.logical_and(pl.program_id(0) == 0, pl.program_id(1) == 0))
        def _():
            gc_ref[...] = jnp.zeros(gc_ref.shape, F32)

        gc_ref[...] += _dot(dmb, w_ref[...].astype(BF16), 1, 1) * (sg * (1.0 + cv * (1.0 - sg)))

    return pl.pallas_call(
        body, name="ada_bwd", grid=(depth, wc // tn),
        in_specs=[_full_spec((MOD_ROWS, d)), pl.BlockSpec((None, MOD_ROWS, tn), lambda l, j: (l, 0, j)),
                  pl.BlockSpec((None, d, tn), lambda l, j: (l, 0, j))],
        out_specs=(pl.BlockSpec((None, d, tn), lambda l, j: (l, 0, j)), _full_spec((MOD_ROWS, d))),
        out_shape=(jax.ShapeDtypeStruct((depth, d, wc), F32), jax.ShapeDtypeStruct((MOD_ROWS, d), F32)),
        compiler_params=_cp(("arbitrary", "arbitrary")),
    )(cc, dm, w_ada)


def _adam(grads, w, m, v):
    depth, rows, cols = w.shape
    assert len(grads) == depth
    n_slab = grads[0].shape[0]
    tr = _pick(rows, (128, 64, 32, 16, 8))
    c1, c2 = 1.0 - ADAM_B1 ** ADAM_STEP, 1.0 - ADAM_B2 ** ADAM_STEP

    def body(*refs):
        g_refs = refs[:depth]
        w_ref, m_ref, v_ref, go_ref, do_ref, mo_ref, vo_ref = refs[depth:]
        for q in range(depth):
            @pl.when(pl.program_id(0) == q)
            def _(q=q):
                g = g_refs[q][0].astype(F32)
                for k in range(1, n_slab):
                    g = g + g_refs[q][k].astype(F32)
                go_ref[...] = g

        g = go_ref[...]
        m_new = ADAM_B1 * m_ref[...] + (1.0 - ADAM_B1) * g
        v_new = ADAM_B2 * v_ref[...] + (1.0 - ADAM_B2) * (g * g)
        mo_ref[...] = m_new
        vo_ref[...] = v_new
        do_ref[...] = -ADAM_LR * ((m_new / c1) / (jnp.sqrt(v_new / c2) + ADAM_EPS) + ADAM_WD * w_ref[...])

    blk = pl.BlockSpec((None, tr, cols), lambda l, i: (l, i, 0))
    g_specs = [pl.BlockSpec((n_slab, tr, cols), lambda l, i, q=q: (0, jnp.where(l == q, i, 0), 0))
               for q in range(depth)]
    out = jax.ShapeDtypeStruct((depth, rows, cols), F32)
    return pl.pallas_call(
        body, name="adamw", grid=(depth, rows // tr),
        in_specs=g_specs + [blk, blk, blk],
        out_specs=(blk, blk, blk, blk), out_shape=(out, out, out, out),
        compiler_params=_cp(("parallel", "parallel"), 48),
    )(*grads, w, m, v)


_WEIGHTS = ['c_ctx', 'w_ada', 'b_ada', 'ln_g', 'ln_b', 'ff1_in', 'ff1_out', 'ff2_in', 'ff2_out', 'w_in', 'conv4_w',
            'conv4_b', 'w_rg', 'b_rg', 'w_ig', 'b_ig', 'lam', 'conv31_w', 'conv31_b', 'cln_g', 'cln_b', 'w_out', 'b_out']
_SMALL_SHARDED = {'ln_g': 2, 'ln_b': 2, 'conv4_w': 2, 'w_rg': 3, 'b_rg': 2, 'w_ig': 3, 'b_ig': 2, 'lam': 2,
                  'conv31_w': 2}
_BIG_SHARDED = {'ff1_in': 2, 'ff1_out': 1, 'ff2_in': 2, 'ff2_out': 1, 'w_in': 2, 'w_out': 1}
_SMALL = [n for n in _WEIGHTS if n not in _BIG_SHARDED and n != 'w_ada']


def kernel(x, c, ctx, c_ctx, w_ada, b_ada, ln_g, ln_b, ff1_in, ff1_out, ff2_in, ff2_out, w_in, conv4_w, conv4_b, w_rg, b_rg, w_ig, b_ig, lam, conv31_w, conv31_b, cln_g, cln_b, w_out, b_out, loss_target, m_c_ctx, m_w_ada, m_b_ada, m_ln_g, m_ln_b, m_ff1_in, m_ff1_out, m_ff2_in, m_ff2_out, m_w_in, m_conv4_w, m_conv4_b, m_w_rg, m_b_rg, m_w_ig, m_b_ig, m_lam, m_conv31_w, m_conv31_b, m_cln_g, m_cln_b, m_w_out, m_b_out, v_c_ctx, v_w_ada, v_b_ada, v_ln_g, v_ln_b, v_ff1_in, v_ff1_out, v_ff2_in, v_ff2_out, v_w_in, v_conv4_w, v_conv4_b, v_w_rg, v_b_rg, v_w_ig, v_b_ig, v_lam, v_conv31_w, v_conv31_b, v_cln_g, v_cln_b, v_w_out, v_b_out):
    wts = dict(c_ctx=c_ctx, w_ada=w_ada, b_ada=b_ada, ln_g=ln_g, ln_b=ln_b, ff1_in=ff1_in, ff1_out=ff1_out,
               ff2_in=ff2_in, ff2_out=ff2_out, w_in=w_in, conv4_w=conv4_w, conv4_b=conv4_b, w_rg=w_rg, b_rg=b_rg,
               w_ig=w_ig, b_ig=b_ig, lam=lam, conv31_w=conv31_w, conv31_b=conv31_b, cln_g=cln_g, cln_b=cln_b,
               w_out=w_out, b_out=b_out)
    mom = dict(zip(_WEIGHTS, (m_c_ctx, m_w_ada, m_b_ada, m_ln_g, m_ln_b, m_ff1_in, m_ff1_out, m_ff2_in, m_ff2_out,
                              m_w_in, m_conv4_w, m_conv4_b, m_w_rg, m_b_rg, m_w_ig, m_b_ig, m_lam, m_conv31_w,
                              m_conv31_b, m_cln_g, m_cln_b, m_w_out, m_b_out)))
    var = dict(zip(_WEIGHTS, (v_c_ctx, v_w_ada, v_b_ada, v_ln_g, v_ln_b, v_ff1_in, v_ff1_out, v_ff2_in, v_ff2_out,
                              v_w_in, v_conv4_w, v_conv4_b, v_w_rg, v_b_rg, v_w_ig, v_b_ig, v_lam, v_conv31_w,
                              v_conv31_b, v_cln_g, v_cln_b, v_w_out, v_b_out)))

    depth, d, wc = w_ada.shape
    t_lat, t_ctx = x.shape[1], ctx.shape[1]
    tt, n_tok = t_ctx, t_ctx + x.shape[1]
    assert t_lat % tt == 0 and tt % SEG == 0 and tt & (tt - 1) == 0 and d % 2 == 0
    ch = d // 2
    alpha = (2.0 * depth) ** 0.25
    me = 4 * lax.axis_index("x") + 2 * lax.axis_index("y") + lax.axis_index("c")

    names = list(_SMALL_SHARDED)
    got = _gather_list([wts[k] for k in names] + [c], F32)
    full = {k: _unshard(g, _SMALL_SHARDED[k]) for k, g in zip(names, got[:-1])}
    c_all = got[-1].reshape(N_DEV, d)
    cc = jnp.concatenate([c_all, jnp.pad(c_ctx[None, :], ((0, MOD_ROWS - N_DEV - 1), (0, 0)))], axis=0)

    b_cols = _my_block(b_ada, me, 1)[:, None, :]
    (m_cols,) = _gather_list([_ada_fwd(cc, w_ada, b_cols)], F32)
    m_full = jnp.moveaxis(m_cols, 0, 2).reshape(depth, MOD_ROWS, N_DEV * wc)
    m_lat = lax.dynamic_index_in_dim(m_full, me, axis=1, keepdims=False).reshape(depth, 9, d)
    m_ctx = m_full[:, N_DEV].reshape(depth, 9, d)
    modtab = jnp.pad(jnp.stack([m_ctx, m_lat], axis=1), ((0, 0), (0, 0), (0, MOD_ROWS - 9), (0, 0)))

    big = {k: [None] * depth for k in _BIG_SHARDED}
    half = {}

    def shard(k, l):
        return wts[k][l].astype(BF16)

    def landed(k, l, g):
        big[k][l] = _unshard(g, _BIG_SHARDED[k] - 1)

    def jobs(*specs):
        return [(stage, k, l) for stage, k, l in specs if l < depth]

    def comm_of(todo):
        return [(stage, shard(k, l) if stage == "spread" else half[k, l]) for stage, k, l in todo]

    def settle(todo, got):
        for (stage, k, l), g in zip(todo, got):
            if stage == "spread":
                half[k, l] = g
            else:
                landed(k, l, g)

    got = _gather_list([shard('ff1_in', 0), shard('ff1_out', 0)], BF16)
    landed('ff1_in', 0, got[0])
    landed('ff1_out', 0, got[1])
    wr_b, wi_b = full['w_rg'].astype(BF16), full['w_ig'].astype(BF16)

    def row(v):
        return v.reshape(1, -1)

    s = jnp.concatenate([ctx[0], x[0]], axis=0)
    saved = []
    h_next = _modulate(s, modtab[0], 0, 1, tt)
    for l in range(depth):
        mt, sv = modtab[l], {}
        sv['s0'] = s
        sv['h1'] = h_next
        todo = jobs(("spread", 'w_in', l), ("spread", 'w_out', l), ("spread", 'ff2_in', l))
        (sv['z1'], sv['a1']), got = _mm_swiglu(sv['h1'], big['ff1_in'][l], comm=comm_of(todo))
        settle(todo, got)
        todo = jobs(("relay", 'w_in', l), ("relay", 'w_out', l), ("relay", 'ff2_in', l), ("spread", 'ff2_out', l))
        sv['f1'], got = _mm_nn("ffn_out", sv['a1'], big['ff1_out'][l], comm=comm_of(todo))
        settle(todo, got)
        sv['r1'], sv['s1'], sv['h2'] = _res_ln(s, sv['f1'], mt, 2, 0.5, row(full['ln_g'][l, 0]), row(full['ln_b'][l, 0]),
                                               alpha, tt, nxt=(mt, 3, 4))
        todo = jobs(("relay", 'ff2_out', l))
        sv['p'], got = _mm_nn("mix_in", sv['h2'], big['w_in'][l], comm=comm_of(todo))
        settle(todo, got)
        sv['xr'] = _conv4(sv['p'], full['conv4_w'][l], row(conv4_b[l]), tt, ch)
        for dr_, rev in ((0, False), (1, True)):
            sv['h', dr_], sv['hin', dr_] = _scan(
                sv['xr'], wr_b[l, dr_], wi_b[l, dr_], row(full['b_rg'][l, dr_]), row(full['b_ig'][l, dr_]),
                row(full['lam'][l, dr_]), rev, tt)
        sv['ymix'], sv['uc'] = _mix_mid(sv['p'], sv['h', 0], sv['h', 1], full['conv31_w'][l], row(conv31_b[l]),
                                        row(cln_g[l]), row(cln_b[l]), tt)
        sv['y'], _ = _mm_nn("mix_out", sv['ymix'], big['w_out'][l], bias=row(b_out[l]))
        sv['r2'], sv['s2'], sv['h3'] = _res_ln(sv['s1'], sv['y'], mt, 5, 1.0, row(full['ln_g'][l, 1]),
                                               row(full['ln_b'][l, 1]), alpha, tt, nxt=(mt, 6, 7))
        todo = jobs(("spread", 'ff1_in', l + 1), ("spread", 'ff1_out', l + 1))
        (sv['z3'], sv['a3']), got = _mm_swiglu(sv['h3'], big['ff2_in'][l], comm=comm_of(todo))
        settle(todo, got)
        todo = jobs(("relay", 'ff1_in', l + 1), ("relay", 'ff1_out', l + 1))
        sv['f3'], got = _mm_nn("ffn_out", sv['a3'], big['ff2_out'][l], comm=comm_of(todo))
        settle(todo, got)
        outs = _res_ln(sv['s2'], sv['f3'], mt, 8, 0.5, row(full['ln_g'][l, 2]), row(full['ln_b'][l, 2]), alpha, tt,
                       nxt=(modtab[l + 1], 0, 1) if l + 1 < depth else None)
        sv['r3'], s = outs[0], outs[1]
        h_next = outs[2] if l + 1 < depth else None
        saved.append(sv)

    cot, loss_blk = _loss_head(s, loss_target[0], tt)
    loss = lax.psum(loss_blk[0, 0], ("x", "y", "c"))

    gx = {k: [None] * depth for k in _BIG_SHARDED}
    gsm = {k: [None] * depth for k in ('ln_g', 'ln_b', 'conv4_w', 'conv4_b', 'b_rg', 'b_ig', 'lam',
                                       'conv31_w', 'conv31_b', 'cln_g', 'cln_b', 'b_out', 'dm')}
    g_gate = {'w_rg': [None] * depth, 'w_ig': [None] * depth}
    taps4, taps31 = conv4_w.shape[1], conv31_w.shape[1]
    heads, hs = w_rg.shape[2], w_rg.shape[4]

    mods, pers = {}, {}

    def ln_bwd(cot, r, f, mt, i_gate, coef, gam):
        if isinstance(cot, tuple):
            return _ln_bwd(cot, r, f, mt, i_gate, coef, gam, tt, alpha=alpha)
        return _ln_bwd(cot, r, f, mt, i_gate, coef, gam, tt) + (None,)

    def scatter(pend):
        return [("scatter", g) for _, _, g in pend]

    def exchanged(pend, got):
        for (k, l, _), g in zip(pend, got):
            gx[k][l] = g

    def ffn_bwd(cot, l, mt, r, f, z, a, h, s_in, w_in_name, w_out_name, ln_idx, i_gate, i_scale, pend_short, pend_long,
                final=False):
        dr, dfb, tot, per, up = ln_bwd(cot, r, f, mt, i_gate, 0.5, row(full['ln_g'][l, ln_idx]))
        dz, got = _mm_nt_swiglu_bwd(dfb, big[w_out_name][l], z, comm=scatter(pend_short))
        exchanged(pend_short, got)
        g_out, _ = _mm_tn("ffn_out_dw", a, dfb)
        pend_out = [(w_out_name, l, g_out.reshape(N_DEV, -1, d))]
        if final:
            g_in, got = _mm_tn("ffn_in_dw", h, dz, b_halves=True, shard_cols=True, comm=scatter(pend_long))
            exchanged(pend_long, got)
            pend_out.append((w_in_name, l, g_in))
            dh, got = _mm_nt("ffn_in_dx", dz, big[w_in_name][l], a_halves=True, comm=scatter(pend_out))
            exchanged(pend_out, got)
            pend_in = []
        else:
            dh, got = _mm_nt("ffn_in_dx", dz, big[w_in_name][l], a_halves=True, comm=scatter(pend_long))
            exchanged(pend_long, got)
            g_in, got = _mm_tn("ffn_in_dw", h, dz, b_halves=True, shard_cols=True, comm=scatter(pend_out))
            exchanged(pend_out, got)
            pend_in = [(w_in_name, l, g_in)]
        return (dr, dh, s_in, mt, i_scale), tot, per, up, pend_in

    pend = []
    for l in reversed(range(depth)):
        mt, sv = modtab[l], saved[l]
        cot, tot3, per3, up, pend_ff2_in = ffn_bwd(cot, l, mt, sv['r3'], sv['f3'], sv['z3'], sv['a3'], sv['h3'],
                                                   sv['s2'], 'ff2_in', 'ff2_out', 2, 8, 7, [], pend)
        if up is not None:
            mods[l + 1, 1] = up
        dr, dyb, tot2, per2, mods[l, 3] = ln_bwd(cot, sv['r2'], sv['y'], mt, 5, 1.0, row(full['ln_g'][l, 1]))
        dymix, _ = _mm_nt("mix_out_dx", dyb, big['w_out'][l])
        g_w_out, _ = _mm_tn("mix_out_dw", sv['ymix'], dyb)
        pend = [('w_out', l, g_w_out.reshape(N_DEV, -1, d))]
        dh_rec, dp_rest, csum = _mix_mid_bwd(dymix, sv['p'], sv['h', 0], sv['h', 1], sv['uc'], full['conv31_w'][l],
                                             row(cln_g[l]), row(cln_b[l]), tt)
        dxr, gw_r, gw_i, gsum = None, [], [], []
        for dr_, rev in ((0, False), (1, True)):
            dxr, gwr, gwi, gs = _scan_bwd(
                sv['xr'], dh_rec, sv['h', dr_], sv['hin', dr_], wr_b[l, dr_], wi_b[l, dr_],
                row(full['b_rg'][l, dr_]), row(full['b_ig'][l, dr_]), row(full['lam'][l, dr_]), rev, tt, dxr)
            gw_r.append(gwr)
            gw_i.append(gwi)
            gsum.append(gs)
        dp_xr, c4sum = _conv4_bwd(dxr, sv['p'], full['conv4_w'][l], tt, ch)
        dp = jnp.concatenate([dp_xr, dp_rest], axis=1)
        dh, got = _mm_nt("mix_in_dx", dp, big['w_in'][l], comm=scatter(pend))
        exchanged(pend, got)
        g_w_in, _ = _mm_tn("mix_in_dw", sv['h2'], dp, shard_cols=True)
        cot = (dr, dh, sv['s1'], mt, 4)
        cot, tot1, per1, mods[l, 2], pend = ffn_bwd(cot, l, mt, sv['r1'], sv['f1'], sv['z1'], sv['a1'], sv['h1'],
                                                    sv['s0'], 'ff1_in', 'ff1_out', 0, 2, 1, [('w_in', l, g_w_in)],
                                                    pend_ff2_in, final=(l == 0))
        pers[l] = (per1, per2, per3)

        gsm['ln_g'][l] = jnp.stack([tot1[0], tot2[0], tot3[0]])
        gsm['ln_b'][l] = jnp.stack([tot1[1], tot2[1], tot3[1]])
        gsm['b_out'][l] = tot2[2]
        gsm['conv4_w'][l], gsm['conv4_b'][l] = c4sum[:taps4], c4sum[taps4]
        for k, gws in (('w_rg', gw_r), ('w_ig', gw_i)):
            slabs = jnp.stack(gws).reshape(2, heads, N_DEV, hs // N_DEV, hs)
            g_gate[k][l] = jnp.moveaxis(slabs, 2, 0).reshape(N_DEV, -1, hs).astype(BF16)
        gsm['b_rg'][l] = jnp.stack([g[0] for g in gsum])
        gsm['b_ig'][l] = jnp.stack([g[1] for g in gsum])
        gsm['lam'][l] = jnp.stack([g[2] for g in gsum])
        gsm['conv31_w'][l], gsm['conv31_b'][l] = csum[:taps31], csum[taps31]
        gsm['cln_g'][l], gsm['cln_b'][l] = csum[taps31 + 1], csum[taps31 + 2]
    cot, mods[0, 1] = _mod_bwd(*cot[:5], alpha, tt)
    grad_x = cot[t_ctx:][None]
    assert not pend
    for l in range(depth):
        rows = lambda st: [v[8 * st + q] for k in (1, 2, 3) for v, q in ((mods[l, k], 1), (mods[l, k], 0), (pers[l][k - 1], 0))]
        gsm['dm'][l] = jnp.stack([jnp.stack(rows(st)) for st in (0, 1)])
    gate_x = _all_to_all(g_gate['w_rg'] + g_gate['w_ig'])

    keys = list(gsm)
    partial = [jnp.stack(gsm[k]) for k in keys]
    shapes = [a.shape for a in partial]
    gathered = _all_gather(_pack(partial, F32))
    totals = dict(zip(keys, _unpack(_sum_slabs(gathered), shapes)))
    dm_all = _unpack(gathered, shapes, lead=(N_DEV,))[keys.index('dm')]
    dm_tot = totals['dm'].reshape(depth, 2, 9 * d)
    dm_sum = _sum_slabs(jnp.stack([_pack([dm_tot[:, 0]], F32), _pack([dm_tot[:, 1]], F32)]))
    grad = {'b_ada': _unpack(dm_sum, [(depth, 9 * d)])[0]}
    for k in ('ln_g', 'ln_b', 'conv4_w', 'b_rg', 'b_ig', 'lam', 'conv31_w'):
        grad[k] = _my_block(totals[k], me, 2)
    for k in ('conv4_b', 'conv31_b', 'cln_g', 'cln_b', 'b_out'):
        grad[k] = totals[k]

    dm_rows = jnp.concatenate([jnp.moveaxis(dm_all[:, :, 1].reshape(N_DEV, depth, 9 * d), 0, 1),
                               jnp.pad(dm_tot[:, 0][:, None, :], ((0, 0), (0, MOD_ROWS - N_DEV - 1), (0, 0)))], axis=1)
    dm_cols = lax.dynamic_slice_in_dim(dm_rows, me * wc, wc, axis=2)
    grad_w_ada, gcc = _ada_bwd(cc, dm_cols, w_ada)
    gcc_all = _all_gather(_pack([gcc], F32))
    grad['c_ctx'] = _unpack(_sum_slabs(gcc_all), [(MOD_ROWS, d)])[0][N_DEV]

    res = {}
    small = [k for k in _SMALL if k not in g_gate]
    gs, ws, ms, vs = (_pack([src[k] for k in small], F32).reshape(1, -1, 8 * LANE) for src in (grad, wts, mom, var))
    outs = _adam([gs], ws, ms, vs)
    small_shapes = [wts[k].shape for k in small]
    for k, vals in zip(small, zip(*(_unpack(o[0], small_shapes) for o in outs))):
        res[k] = vals
    res['w_ada'] = _adam([grad_w_ada[l][None] for l in range(depth)], w_ada, m_w_ada, v_w_ada)
    for j, k in enumerate(g_gate):
        flat = (depth, -1, hs)
        outs = _adam([gate_x[j * depth + l] for l in range(depth)], wts[k].reshape(flat), mom[k].reshape(flat),
                     var[k].reshape(flat))
        res[k] = tuple(o.reshape(wts[k].shape) for o in outs)
    for k in _BIG_SHARDED:
        res[k] = _adam(gx[k], wts[k], mom[k], var[k])

    out = [loss, grad_x]
    for j in range(4):
        out += [res[k][j] for k in _WEIGHTS]
    return tuple(out)
```

```python
import math

import jax
import jax.numpy as jnp
from jax import lax
from jax.experimental import pallas as pl
from jax.experimental.pallas import tpu as pltpu

F32 = jnp.float32
BF16 = jnp.bfloat16
N_DEV = 8
MESH = pl.DeviceIdType.MESH
LN_EPS = 1e-6
RG_C = 8.0
SEG = 64
CONV_PAD = 16
ADAM_LR, ADAM_B1, ADAM_B2, ADAM_EPS, ADAM_WD, ADAM_STEP = 0.001, 0.9, 0.999, 1e-08, 0.01, 10
LANE = 128
PACK_QUANTUM = 16 * LANE
PACK_ROWS = 512
MOD_ROWS = 16
GELU_K = 0.7978845608028654
GELU_C = 0.044715

_TM = (768, 640, 512, 384, 256, 128)
_TN = (1024, 512, 256, 128)
_TK = (2816, 1024, 512, 256, 128)
_SUB = (256, 128)


def _cp(sem=None, vmem_mb=40):
    return pltpu.CompilerParams(dimension_semantics=sem, vmem_limit_bytes=vmem_mb * 2 ** 20)


def _pick(n, cands):
    for cand in cands:
        if n % cand == 0:
            return cand
    return n


def _sig(v):
    return 1.0 / (1.0 + jnp.exp(-v))


def _dot(a, b, ca, cb):
    return lax.dot_general(a, b, (((ca,), (cb,)), ((), ())), preferred_element_type=F32)


def _sum0(v):
    return jnp.sum(v, axis=0, keepdims=True)


def _gelu_and_grad(v):
    inner = GELU_K * (v + GELU_C * v * v * v)
    t = jnp.tanh(inner)
    gel = 0.5 * v * (1.0 + t)
    dgel = 0.5 * (1.0 + t) + 0.5 * v * (1.0 - t * t) * GELU_K * (1.0 + 3.0 * GELU_C * v * v)
    return gel, dgel


def _softplus_neg(lam):
    y = jnp.exp(-jnp.abs(lam))
    u = 1.0 + y
    log1p = jnp.where(u == 1.0, y, jnp.log(u) * (y / (u - 1.0)))
    return jnp.maximum(-lam, 0.0) + log1p


def _neg_expm1(z):
    ser = -z * (1.0 + z * (1 / 2 + z * (1 / 6 + z * (1 / 24 + z * (1 / 120 + z * (1 / 720 + z * (1 / 5040)))))))
    return jnp.where(z > -0.25, ser, 1.0 - jnp.exp(z))


def _row_spec(tt, w, col=0):
    return pl.BlockSpec((tt, w), lambda i: (i, col))


def _full_spec(shape):
    nd = len(shape)
    return pl.BlockSpec(shape, lambda *_: (0,) * nd)


def _mod_spec(d):
    return pl.BlockSpec((None, MOD_ROWS, d), lambda i: (jnp.minimum(i, 1), 0, 0))


def _stream_sum_spec(w):
    return pl.BlockSpec((8, w), lambda i: (jnp.minimum(i, 1), 0))


def _acc_rows(ref, init, rows):
    @pl.when(init)
    def _():
        ref[...] = jnp.zeros(ref.shape, F32)

    for k, row in enumerate(rows):
        ref[pl.ds(k, 1), :] += row


def _all_gather_arrays(arrs):
    n_a = len(arrs)

    def body(*refs):
        x_refs, out_refs = refs[:n_a], refs[n_a:2 * n_a]
        send_sems, recv_sems, local_sems = refs[2 * n_a:]
        x, y, c = lax.axis_index("x"), lax.axis_index("y"), lax.axis_index("c")
        me, sibling = (x, y, c), (x, y, 1 - c)
        chips = [(1 - x, y), (x, 1 - y), (1 - x, 1 - y)]

        def slot(a, px, py, pc):
            return out_refs[a].at[4 * px + 2 * py + pc]

        def copy(a, k, block, to, src=None):
            return pltpu.make_async_remote_copy(
                src_ref=slot(a, *block) if src is None else src, dst_ref=slot(a, *block),
                send_sem=send_sems.at[7 * a + k], recv_sem=recv_sems.at[7 * a + k], device_id=to,
                device_id_type=MESH)

        started = []
        for a in range(n_a):
            mine = pltpu.make_async_copy(x_refs[a], slot(a, *me), local_sems.at[a])
            mine.start()
            first = [copy(a, 0, me, sibling, src=x_refs[a])]
            first += [copy(a, 1 + j, me, (*chip, c), src=x_refs[a]) for j, chip in enumerate(chips)]
            for cp in first:
                cp.start()
            started.append((mine, first))
        for a in range(n_a):
            mine, first = started[a]
            passed = [copy(a, 4 + j, (*chip, c), sibling) for j, chip in enumerate(chips)]
            for j, chip in enumerate(chips):
                copy(a, 1 + j, (*chip, c), me).wait_recv()
                passed[j].start()
            copy(a, 0, sibling, me).wait_recv()
            for j, chip in enumerate(chips):
                copy(a, 4 + j, (*chip, 1 - c), me).wait_recv()
            for cp in first + passed:
                cp.wait_send()
            mine.wait()

    return pl.pallas_call(
        body, name="all_gather",
        out_shape=tuple(jax.ShapeDtypeStruct((N_DEV,) + v.shape, v.dtype) for v in arrs),
        in_specs=[pl.BlockSpec(memory_space=pltpu.HBM)] * n_a,
        out_specs=(pl.BlockSpec(memory_space=pltpu.HBM),) * n_a,
        scratch_shapes=[pltpu.SemaphoreType.DMA((7 * n_a,)), pltpu.SemaphoreType.DMA((7 * n_a,)),
                        pltpu.SemaphoreType.DMA((n_a,))],
    )(*arrs)


def _all_gather(v):
    return _all_gather_arrays([v])[0]


_PEER_FLIPS = [(0, 0, 1), (1, 0, 0), (0, 1, 0), (1, 1, 0), (1, 0, 1), (0, 1, 1), (1, 1, 1)]


def _all_to_all(parts):
    n_parts = len(parts)

    def body(*refs):
        x_refs, out_ref = refs[:n_parts], refs[n_parts]
        send_sems, recv_sems, local_sems = refs[n_parts + 1:]
        x, y, c = lax.axis_index("x"), lax.axis_index("y"), lax.axis_index("c")
        me = 4 * x + 2 * y + c
        local, remote = [], []
        for l in range(n_parts):
            cp = pltpu.make_async_copy(x_refs[l].at[me], out_ref.at[l, me], local_sems.at[l])
            cp.start()
            local.append(cp)
            for k, (fx, fy, fc) in enumerate(_PEER_FLIPS):
                px = 1 - x if fx else x
                py = 1 - y if fy else y
                pc = 1 - c if fc else c
                cp = pltpu.make_async_remote_copy(
                    src_ref=x_refs[l].at[4 * px + 2 * py + pc], dst_ref=out_ref.at[l, me],
                    send_sem=send_sems.at[7 * l + k], recv_sem=recv_sems.at[7 * l + k],
                    device_id=(px, py, pc), device_id_type=MESH)
                cp.start()
                remote.append(cp)
        for cp in remote:
            cp.wait()
        for cp in local:
            cp.wait()

    shape = parts[0].shape
    return pl.pallas_call(
        body, name="all_to_all",
        out_shape=jax.ShapeDtypeStruct((n_parts,) + shape, parts[0].dtype),
        in_specs=[pl.BlockSpec(memory_space=pltpu.HBM)] * n_parts,
        out_specs=pl.BlockSpec(memory_space=pltpu.HBM),
        scratch_shapes=[pltpu.SemaphoreType.DMA((7 * n_parts,)), pltpu.SemaphoreType.DMA((7 * n_parts,)),
                        pltpu.SemaphoreType.DMA((n_parts,))],
    )(*parts)


def _pack(arrs, dtype):
    pieces = []
    for a in arrs:
        flat = a.astype(dtype).reshape(-1)
        flat = jnp.pad(flat, (0, (-flat.shape[0]) % PACK_QUANTUM))
        pieces.append(flat.reshape(-1, LANE))
    packed = jnp.concatenate(pieces, axis=0) if len(pieces) > 1 else pieces[0]
    return jnp.pad(packed, ((0, (-packed.shape[0]) % PACK_ROWS), (0, 0)))


def _unpack(packed, shapes, lead=()):
    flat = packed.reshape(lead + (-1,))
    outs, off = [], 0
    for sh in shapes:
        n = math.prod(sh)
        outs.append(flat[..., off:off + n].reshape(lead + tuple(sh)))
        off += n + (-n) % PACK_QUANTUM
    return outs


def _gather_list(arrs, dtype):
    gathered = _all_gather(_pack(arrs, dtype))
    return _unpack(gathered, [a.shape for a in arrs], lead=(N_DEV,))


def _unshard(g, axis):
    y = jnp.moveaxis(g, 0, axis)
    sh = y.shape
    return y.reshape(sh[:axis] + (sh[axis] * sh[axis + 1],) + sh[axis + 2:])


def _my_block(full, me, axis):
    size = full.shape[axis] // N_DEV
    return lax.dynamic_slice_in_dim(full, me * size, size, axis=axis)


def _sum_slabs(v):
    n_slab, rows, width = v.shape
    tr = _pick(rows, (1024, 512, 256, 128, 64, 32, 16, 8))

    def body(v_ref, o_ref):
        acc = v_ref[0]
        for k in range(1, n_slab):
            acc = acc + v_ref[k]
        o_ref[...] = acc

    return pl.pallas_call(
        body, name="sum_slabs", grid=(rows // tr,),
        in_specs=[pl.BlockSpec((n_slab, tr, width), lambda i: (0, i, 0))],
        out_specs=pl.BlockSpec((tr, width), lambda i: (i, 0)),
        out_shape=jax.ShapeDtypeStruct((rows, width), F32),
        compiler_params=_cp(("parallel",)),
    )(v)


N_LOCAL = 1


def _comm_copies(kinds, src_refs, dst_refs, send_sems, recv_sems, local_sems):
    x, y, c = lax.axis_index("x"), lax.axis_index("y"), lax.axis_index("c")
    me = 4 * x + 2 * y + c
    copies = []

    def remote(j, k, src, dst, peer):
        copies.append(pltpu.make_async_remote_copy(
            src_ref=src, dst_ref=dst, send_sem=send_sems.at[7 * j + k], recv_sem=recv_sems.at[7 * j + k],
            device_id=peer, device_id_type=MESH))

    def local(j, k, src, dst):
        copies.append(pltpu.make_async_copy(src, dst, local_sems.at[N_LOCAL * j + k]))

    for j, kind in enumerate(kinds):
        src, dst = src_refs[j], dst_refs[j]
        peers = [(1 - x if fx else x, 1 - y if fy else y, 1 - c if fc else c) for fx, fy, fc in _PEER_FLIPS]
        if kind == "scatter":
            local(j, 0, src.at[me], dst.at[me])
            for k, (px, py, pc) in enumerate(peers):
                remote(j, k, src.at[4 * px + 2 * py + pc], dst.at[me], (px, py, pc))
        elif kind == "spread":
            local(j, 0, src, dst.at[me])
            for k, peer in enumerate(peers[:4]):
                remote(j, k, src, dst.at[me], peer)
        else:
            for k, (px, py, pc) in enumerate(peers[1:4]):
                remote(j, 4 + k, dst.at[4 * px + 2 * py + pc], dst.at[4 * px + 2 * py + pc], peers[0])
    return copies


def _mm(name, grid, nk, ins, in_specs, out_shape, out_specs, acc_shape, prod, fin, comm=(), vmem_mb=48):
    n_in, n_out, n_c = len(ins), len(out_shape), len(comm)
    kinds = [kind for kind, _ in comm]
    srcs = [a for _, a in comm]
    dsts = [jax.ShapeDtypeStruct(((N_DEV,) + a.shape) if kind == "spread" else a.shape, a.dtype) for kind, a in comm]
    hbm = pl.BlockSpec(memory_space=pltpu.HBM)

    def body(*refs):
        in_refs, refs = refs[:n_in], refs[n_in:]
        src_refs, refs = refs[:n_c], refs[n_c:]
        out_refs, refs = refs[:n_out], refs[n_out:]
        dst_refs, scratch = refs[:n_c], refs[n_c:]
        if nk > 1:
            acc, scratch = scratch[0], scratch[1:]
        ids = [pl.program_id(a) for a in range(3)]
        if n_c:
            copies = _comm_copies(kinds, src_refs, dst_refs, *scratch)

            @pl.when(jnp.logical_and(ids[0] == 0, jnp.logical_and(ids[1] == 0, ids[2] == 0)))
            def _():
                for cp in copies:
                    cp.start()

        if nk == 1:
            fin(prod(*in_refs), in_refs, out_refs)
        else:
            @pl.when(ids[2] == 0)
            def _():
                acc[...] = jnp.zeros(acc.shape, F32)

            for idx, val in enumerate(prod(*in_refs)):
                acc[idx] += val

            @pl.when(ids[2] == nk - 1)
            def _():
                fin(tuple(acc[idx] for idx in range(acc_shape[0])), in_refs, out_refs)

        if n_c:
            @pl.when(jnp.logical_and(ids[0] == grid[0] - 1, jnp.logical_and(ids[1] == grid[1] - 1, ids[2] == nk - 1)))
            def _():
                for cp in copies:
                    cp.wait()

    scratch = [pltpu.VMEM(acc_shape, F32)] if nk > 1 else []
    if n_c:
        scratch += [pltpu.SemaphoreType.DMA((7 * n_c,)), pltpu.SemaphoreType.DMA((7 * n_c,)),
                    pltpu.SemaphoreType.DMA((N_LOCAL * n_c,))]
    sem = ("arbitrary",) * 3 if n_c else ("parallel", "parallel", "arbitrary")
    res = pl.pallas_call(
        body, name=name + (f"_x{n_c}" if n_c else ""), grid=grid,
        in_specs=list(in_specs) + [hbm] * n_c, out_specs=tuple(out_specs) + (hbm,) * n_c,
        out_shape=tuple(out_shape) + tuple(dsts), scratch_shapes=scratch,
        input_output_aliases={n_in + j: n_out + j for j, kind in enumerate(kinds) if kind == "relay"},
        compiler_params=_cp(sem, vmem_mb),
    )(*ins, *srcs)
    return res[:n_out], list(res[n_out:])


def _mm_nn(name, a, b, bias=None, comm=()):
    m, kk = a.shape
    nn = b.shape[1]
    tm, tn = _pick(m, _TM), _pick(nn, _TN)
    tk = kk if kk <= 2048 else _pick(kk, _TK)
    nk = kk // tk
    ins = [a, b]
    specs = [pl.BlockSpec((tm, tk), lambda i, j, k: (i, k)), pl.BlockSpec((tk, tn), lambda i, j, k: (k, j))]
    if bias is not None:
        ins.append(bias)
        specs.append(pl.BlockSpec((1, tn), lambda i, j, k: (0, j)))

    def prod(a_ref, b_ref, *_):
        return (_dot(a_ref[...], b_ref[...], 1, 0),)

    def fin(vals, in_refs, out_refs):
        out_refs[0][...] = vals[0] if bias is None else vals[0] + in_refs[2][...]

    outs, got = _mm(name, (m // tm, nn // tn, nk), nk, ins, specs,
                    (jax.ShapeDtypeStruct((m, nn), F32),), (pl.BlockSpec((tm, tn), lambda i, j, k: (i, j)),),
                    (1, tm, tn), prod, fin, comm)
    return outs[0], got


def _mm_swiglu(h, w1, comm=()):
    m, kk = h.shape
    f = w1.shape[1] // 2
    tm, tn = _pick(m, _TM), _pick(f, (512, 256, 128))
    nj = f // tn
    sub = _pick(tm, _SUB)

    def prod(*_):
        return ()

    def fin(_, in_refs, out_refs):
        h_ref, wg_ref, wu_ref = in_refs
        for q in range(tm // sub):
            rows = pl.ds(q * sub, sub)
            hq = h_ref[rows, :]
            zg, zu = _dot(hq, wg_ref[...], 1, 0), _dot(hq, wu_ref[...], 1, 0)
            out_refs[0][0, rows, :] = zg.astype(BF16)
            out_refs[0][1, rows, :] = zu.astype(BF16)
            out_refs[1][rows, :] = (zg * _sig(zg) * zu).astype(BF16)

    return _mm("ffn_in", (m // tm, nj, 1), 1, [h, w1, w1],
               [pl.BlockSpec((tm, kk), lambda i, j, k: (i, 0)),
                pl.BlockSpec((kk, tn), lambda i, j, k: (0, j)),
                pl.BlockSpec((kk, tn), lambda i, j, k: (0, j + nj))],
               (jax.ShapeDtypeStruct((2, m, f), BF16), jax.ShapeDtypeStruct((m, f), BF16)),
               (pl.BlockSpec((2, tm, tn), lambda i, j, k: (0, i, j)), pl.BlockSpec((tm, tn), lambda i, j, k: (i, j))),
               (2, tm, tn), prod, fin, comm)


def _mm_nt_swiglu_bwd(dfb, w2, z, comm=()):
    m, kk = dfb.shape
    f = w2.shape[0]
    tm, tn = _pick(m, _TM), _pick(f, (512, 256, 128))
    sub = _pick(tm, _SUB)

    def prod(*_):
        return ()

    def fin(_, in_refs, out_refs):
        a_ref, b_ref, z_ref = in_refs
        for q in range(tm // sub):
            rows = pl.ds(q * sub, sub)
            da = _dot(a_ref[rows, :], b_ref[...], 1, 1)
            zg, zu = z_ref[0, rows, :].astype(F32), z_ref[1, rows, :].astype(F32)
            sg = _sig(zg)
            out_refs[0][0, rows, :] = (da * zu * (sg * (1.0 + zg * (1.0 - sg)))).astype(BF16)
            out_refs[0][1, rows, :] = (da * (zg * sg)).astype(BF16)

    outs, got = _mm("ffn_out_dx", (m // tm, f // tn, 1), 1, [dfb, w2, z],
                    [pl.BlockSpec((tm, kk), lambda i, j, k: (i, 0)),
                     pl.BlockSpec((tn, kk), lambda i, j, k: (j, 0)),
                     pl.BlockSpec((2, tm, tn), lambda i, j, k: (0, i, j))],
                    (jax.ShapeDtypeStruct((2, m, f), BF16),),
                    (pl.BlockSpec((2, tm, tn), lambda i, j, k: (0, i, j)),),
                    (1, tm, tn), prod, fin, comm)
    return outs[0], got


def _mm_nt(name, a, b, a_halves=False, comm=()):
    nn, kk = b.shape
    m = a.shape[1] if a_halves else a.shape[0]
    kh = kk // 2 if a_halves else kk
    tm, tn = _pick(m, _TM), _pick(nn, _TN)
    tk = kh if (kh <= 4096 and not a_halves) else _pick(kh, _TK)
    nk, nkh = kk // tk, kh // tk
    if a_halves:
        a_spec = pl.BlockSpec((None, tm, tk), lambda i, j, k: (k // nkh, i, k % nkh))
    else:
        a_spec = pl.BlockSpec((tm, tk), lambda i, j, k: (i, k))

    def prod(a_ref, b_ref):
        return (_dot(a_ref[...], b_ref[...], 1, 1),)

    def fin(vals, in_refs, out_refs):
        out_refs[0][...] = vals[0]

    outs, got = _mm(name, (m // tm, nn // tn, nk), nk, [a, b],
                    [a_spec, pl.BlockSpec((tn, tk), lambda i, j, k: (j, k))],
                    (jax.ShapeDtypeStruct((m, nn), F32),), (pl.BlockSpec((tm, tn), lambda i, j, k: (i, j)),),
                    (1, tm, tn), prod, fin, comm)
    return outs[0], got


def _mm_tn(name, a, b, b_halves=False, shard_cols=False, comm=()):
    kt, m = a.shape
    nn = 2 * b.shape[2] if b_halves else b.shape[1]
    tk, tm = _pick(kt, _TM), _pick(m, (1408, 1024, 512, 256, 128))
    shard = nn // N_DEV
    per = 1
    while shard_cols and 2 * per * shard <= nn // 2 and ((per * shard) % 256 or per * shard < 1024):
        per *= 2
    tn = per * shard if shard_cols else _pick(nn, _TN)
    njh = (nn // 2) // tn if b_halves else 0
    if b_halves:
        b_spec = pl.BlockSpec((None, tk, tn), lambda i, j, k: (j // njh, k, j % njh))
    else:
        b_spec = pl.BlockSpec((tk, tn), lambda i, j, k: (k, j))
    if shard_cols:
        out_shape = jax.ShapeDtypeStruct((N_DEV, m, shard), BF16)
        out_spec = pl.BlockSpec((per, tm, shard), lambda i, j, k: (j, i, 0))
    else:
        out_shape = jax.ShapeDtypeStruct((m, nn), BF16)
        out_spec = pl.BlockSpec((tm, tn), lambda i, j, k: (i, j))

    def prod(a_ref, b_ref):
        return (_dot(a_ref[...], b_ref[...], 0, 0),)

    def fin(vals, in_refs, out_refs):
        r = vals[0].astype(BF16)
        if shard_cols:
            for q in range(per):
                out_refs[0][q] = r[:, q * shard:(q + 1) * shard]
        else:
            out_refs[0][...] = r

    outs, got = _mm(name, (m // tm, nn // tn, kt // tk), kt // tk, [a, b],
                    [pl.BlockSpec((tk, tm), lambda i, j, k: (k, i)), b_spec],
                    (out_shape,), (out_spec,), (1, tm, tn), prod, fin, comm, vmem_mb=56)
    return outs[0], got


def _modulate(s, modtab, i_shift, i_scale, tt):
    n, d = s.shape

    def body(s_ref, m_ref, o_ref):
        shift, scale = m_ref[pl.ds(i_shift, 1), :], m_ref[pl.ds(i_scale, 1), :]
        o_ref[...] = (s_ref[...] * (1.0 + scale) + shift).astype(BF16)

    return pl.pallas_call(
        body, name="modulate", grid=(n // tt,),
        in_specs=[_row_spec(tt, d), _mod_spec(d)], out_specs=_row_spec(tt, d),
        out_shape=jax.ShapeDtypeStruct((n, d), BF16), compiler_params=_cp(("parallel",)),
    )(s, modtab)


def _res_ln(s, f, modtab, i_gate, coef, gam, bet, alpha, tt, nxt=None):
    n, d = s.shape

    def body(s_ref, f_ref, m_ref, g_ref, b_ref, *refs):
        gate = m_ref[pl.ds(i_gate, 1), :]
        r = alpha * s_ref[...] + (coef * gate) * f_ref[...]
        mu = jnp.mean(r, axis=-1, keepdims=True)
        xc = r - mu
        var = jnp.mean(xc * xc, axis=-1, keepdims=True)
        out = xc * lax.rsqrt(var + LN_EPS) * g_ref[...] + b_ref[...]
        if nxt is None:
            r_ref, o_ref = refs
        else:
            m2_ref, r_ref, o_ref, h_ref = refs
            shift, scale = m2_ref[pl.ds(nxt[1], 1), :], m2_ref[pl.ds(nxt[2], 1), :]
            h_ref[...] = (out * (1.0 + scale) + shift).astype(BF16)
        r_ref[...] = r
        o_ref[...] = out

    ins = [s, f, modtab, gam, bet] + ([] if nxt is None else [nxt[0]])
    n_h = 0 if nxt is None else 1
    return pl.pallas_call(
        body, name="res_ln", grid=(n // tt,),
        in_specs=[_row_spec(tt, d), _row_spec(tt, d), _mod_spec(d), _full_spec((1, d)), _full_spec((1, d))]
        + [_mod_spec(d)] * n_h,
        out_specs=(_row_spec(tt, d),) * (2 + n_h),
        out_shape=(jax.ShapeDtypeStruct((n, d), F32),) * 2 + (jax.ShapeDtypeStruct((n, d), BF16),) * n_h,
        compiler_params=_cp(("parallel",)),
    )(*ins)


def _ln_bwd(do, r, f, modtab, i_gate, coef, gam, tt, alpha=None):
    fused = isinstance(do, tuple)
    n, d = r.shape

    def body(*refs):
        if fused:
            dru_ref, dhu_ref, su_ref, mu_ref = refs[:4]
            r_ref, f_ref, m_ref, g_ref, dr_ref, dfb_ref, tot_ref, str_ref, up_ref = refs[4:]
        else:
            do_ref, r_ref, f_ref, m_ref, g_ref, dr_ref, dfb_ref, tot_ref, str_ref = refs
        i = pl.program_id(0)
        r = r_ref[...]
        mu = jnp.mean(r, axis=-1, keepdims=True)
        xc = r - mu
        rstd = lax.rsqrt(jnp.mean(xc * xc, axis=-1, keepdims=True) + LN_EPS)
        xh = xc * rstd
        if fused:
            dh_up = dhu_ref[...]
            dout = alpha * dru_ref[...] + dh_up * (1.0 + mu_ref[pl.ds(do[4], 1), :])
            _acc_rows(up_ref, i <= 1, [_sum0(dh_up * su_ref[...]), _sum0(dh_up)])
        else:
            dout = do_ref[...]
        dxh = dout * g_ref[...]
        m1 = jnp.mean(dxh, axis=-1, keepdims=True)
        m2 = jnp.mean(dxh * xh, axis=-1, keepdims=True)
        dr = rstd * (dxh - m1 - xh * m2)
        dr_ref[...] = dr
        dfb = (coef * m_ref[pl.ds(i_gate, 1), :]) * dr
        dfb_ref[...] = dfb.astype(BF16)
        _acc_rows(tot_ref, i == 0, [_sum0(dout * xh), _sum0(dout), _sum0(dfb)])
        _acc_rows(str_ref, i <= 1, [_sum0(coef * f_ref[...] * dr)])

    row = _row_spec(tt, d)
    if fused:
        ins, in_specs = list(do[:4]), [row, row, row, _mod_spec(d)]
    else:
        ins, in_specs = [do], [row]
    n_up = 1 if fused else 0
    return pl.pallas_call(
        body, name="ln_bwd", grid=(n // tt,),
        in_specs=in_specs + [row, row, _mod_spec(d), _full_spec((1, d))],
        out_specs=(row, row, _full_spec((8, d)), _stream_sum_spec(d)) + (_stream_sum_spec(d),) * n_up,
        out_shape=(jax.ShapeDtypeStruct((n, d), F32), jax.ShapeDtypeStruct((n, d), BF16),
                   jax.ShapeDtypeStruct((8, d), F32), jax.ShapeDtypeStruct((16, d), F32))
        + (jax.ShapeDtypeStruct((16, d), F32),) * n_up,
        compiler_params=_cp(("arbitrary",), 48),
    )(*ins, r, f, modtab, gam)


def _mod_bwd(dr, dh, s, modtab, i_scale, alpha, tt):
    n, d = dr.shape

    def body(dr_ref, dh_ref, s_ref, m_ref, o_ref, str_ref):
        i = pl.program_id(0)
        dh_v = dh_ref[...]
        o_ref[...] = alpha * dr_ref[...] + dh_v * (1.0 + m_ref[pl.ds(i_scale, 1), :])
        _acc_rows(str_ref, i <= 1, [_sum0(dh_v * s_ref[...]), _sum0(dh_v)])

    return pl.pallas_call(
        body, name="mod_bwd", grid=(n // tt,),
        in_specs=[_row_spec(tt, d), _row_spec(tt, d), _row_spec(tt, d), _mod_spec(d)],
        out_specs=(_row_spec(tt, d), _stream_sum_spec(d)),
        out_shape=(jax.ShapeDtypeStruct((n, d), F32), jax.ShapeDtypeStruct((16, d), F32)),
        compiler_params=_cp(("arbitrary",)),
    )(dr, dh, s, modtab)


def _loss_head(s, target, tt):
    n, d = s.shape
    nt = n // tt

    def body(s_ref, t_ref, do_ref, loss_ref, acc):
        i = pl.program_id(0)

        @pl.when(i == 0)
        def _():
            acc[...] = jnp.zeros(acc.shape, F32)
            do_ref[...] = jnp.zeros(do_ref.shape, F32)

        @pl.when(i > 0)
        def _():
            err = s_ref[...] - t_ref[...]
            do_ref[...] = err / d
            acc[...] += jnp.sum((err * err).reshape(tt // 8, 8, d), axis=0)

        @pl.when(i == nt - 1)
        def _():
            loss_ref[...] = jnp.full(loss_ref.shape, jnp.sum(acc[...]) * (0.5 / d), F32)

    return pl.pallas_call(
        body, name="loss_head", grid=(nt,),
        in_specs=[_row_spec(tt, d), pl.BlockSpec((tt, d), lambda i: (jnp.maximum(i - 1, 0), 0))],
        out_specs=(_row_spec(tt, d), _full_spec((8, LANE))),
        out_shape=(jax.ShapeDtypeStruct((n, d), F32), jax.ShapeDtypeStruct((8, LANE), F32)),
        scratch_shapes=[pltpu.VMEM((8, d), F32)],
        compiler_params=_cp(("arbitrary",)),
    )(s, target)


def _halo_specs(tt, c, n):
    nb, last = tt // 8, n // 8 - 1
    return [pl.BlockSpec((tt, c), lambda i: (i, 0)),
            pl.BlockSpec((8, c), lambda i: (jnp.maximum(i * nb - 1, 0), 0)),
            pl.BlockSpec((8, c), lambda i: (jnp.minimum((i + 1) * nb, last), 0))]


def _fill_halo(scr, main_ref, prev_ref, next_ref, i, nt, tt):
    has_prev = i >= 2
    has_next = jnp.logical_and(i >= 1, i < nt - 1)
    scr[pl.ds(0, 8), :] = jnp.where(has_prev, prev_ref[...], 0.0)
    scr[pl.ds(8, tt), :] = main_ref[...]
    scr[pl.ds(8 + tt, 8), :] = jnp.where(has_next, next_ref[...], 0.0)


def _conv4(p, w, b, tt, c):
    n = p.shape[0]
    nt, taps = n // tt, w.shape[0]
    left = taps // 2

    def body(x_ref, xp_ref, xn_ref, w_ref, b_ref, o_ref, scr):
        _fill_halo(scr, x_ref, xp_ref, xn_ref, pl.program_id(0), nt, tt)
        acc = jnp.broadcast_to(b_ref[...], (tt, c))
        for k in range(taps):
            acc = acc + w_ref[pl.ds(k, 1), :] * scr[pl.ds(8 + k - left, tt), :]
        o_ref[...] = acc

    return pl.pallas_call(
        body, name="conv4", grid=(nt,),
        in_specs=_halo_specs(tt, c, n) + [_full_spec(w.shape), _full_spec((1, c))],
        out_specs=_row_spec(tt, c), out_shape=jax.ShapeDtypeStruct((n, c), F32),
        scratch_shapes=[pltpu.VMEM((tt + 16, c), F32)], compiler_params=_cp(("parallel",)),
    )(p, p, p, w, b)


def _conv4_bwd(dxr, p, w, tt, c):
    n = p.shape[0]
    nt, taps = n // tt, w.shape[0]
    left = taps // 2

    def body(d_ref, dp_ref, dn_ref, x_ref, xp_ref, xn_ref, w_ref, o_ref, sum_ref, dscr, xscr):
        i = pl.program_id(0)
        _fill_halo(dscr, d_ref, dp_ref, dn_ref, i, nt, tt)
        _fill_halo(xscr, x_ref, xp_ref, xn_ref, i, nt, tt)
        acc = jnp.zeros((tt, c), F32)
        for k in range(taps):
            acc = acc + w_ref[pl.ds(k, 1), :] * dscr[pl.ds(8 - (k - left), tt), :]
        o_ref[...] = acc.astype(BF16)
        d = d_ref[...]
        rows = [_sum0(d * xscr[pl.ds(8 + k - left, tt), :]) for k in range(taps)] + [_sum0(d)]
        _acc_rows(sum_ref, i == 0, rows)

    return pl.pallas_call(
        body, name="conv4_bwd", grid=(nt,),
        in_specs=_halo_specs(tt, c, n) + _halo_specs(tt, c, n) + [_full_spec(w.shape)],
        out_specs=(_row_spec(tt, c), _full_spec((8, c))),
        out_shape=(jax.ShapeDtypeStruct((n, c), BF16), jax.ShapeDtypeStruct((8, c), F32)),
        scratch_shapes=[pltpu.VMEM((tt + 16, c), F32), pltpu.VMEM((tt + 16, c), F32)],
        compiler_params=_cp(("arbitrary",)),
    )(dxr, dxr, dxr, p, p, p, w)


def _head_gates(xv, wr, wi, br, bi, lam):
    xb = xv.astype(BF16)
    r = _sig(_dot(xb, wr, 1, 0) + br)
    ig = _sig(_dot(xb, wi, 1, 0) + bi)
    sp = _softplus_neg(lam)
    log_a = (-RG_C) * r * sp
    return r, ig, sp, jnp.exp(log_a), jnp.sqrt(_neg_expm1(2.0 * log_a))


def _tile_scan(a, b, increasing, tt, rows):
    s = 1
    while s < tt:
        if increasing:
            a_sh, b_sh, ok = pltpu.roll(a, s, 0), pltpu.roll(b, s, 0), rows >= s
        else:
            a_sh, b_sh, ok = pltpu.roll(a, tt - s, 0), pltpu.roll(b, tt - s, 0), rows < tt - s
        b = jnp.where(ok, a * b_sh + b, b)
        a = jnp.where(ok, a * a_sh, a)
        s *= 2
    return a, b


def _scan_tile_index(step, nt, reverse):
    return jnp.where(step == 0, 0, nt - step) if reverse else step


def _scan(xr, wr, wi, br, bi, lam, reverse, tt):
    n, c = xr.shape
    nt = n // tt
    heads, hs = wr.shape[0], wr.shape[2]
    lc = min(c, LANE)

    def tile(i):
        return _scan_tile_index(i, nt, reverse)

    def body(x_ref, wr_ref, wi_ref, br_ref, bi_ref, lam_ref, h_ref, hin_ref, a_scr, b_scr, carry):
        i = pl.program_id(0)

        @pl.when(i == 0)
        def _():
            carry[...] = jnp.zeros(carry.shape, F32)

        for hd in range(heads):
            sl = slice(hd * hs, (hd + 1) * hs)
            xv = x_ref[:, sl]
            _, ig, _, a, sq = _head_gates(xv, wr_ref[hd], wi_ref[hd], br_ref[:, sl], bi_ref[:, sl], lam_ref[:, sl])
            a_scr[:, sl] = a
            b_scr[:, sl] = sq * (ig * xv)
        hin_ref[...] = jnp.broadcast_to(carry[...], (8, c))
        rows = lax.broadcasted_iota(jnp.int32, (tt, lc), 0)
        for ch in range(c // lc):
            sl = slice(ch * lc, (ch + 1) * lc)
            big_a, big_b = _tile_scan(a_scr[:, sl], b_scr[:, sl], not reverse, tt, rows)
            h_ref[:, sl] = big_a * carry[:, sl] + big_b
        carry[...] = h_ref[pl.ds(0 if reverse else tt - 1, 1), :]

    return pl.pallas_call(
        body, name="lru_scan", grid=(nt,),
        in_specs=[pl.BlockSpec((tt, c), lambda i: (tile(i), 0)), _full_spec(wr.shape), _full_spec(wi.shape),
                  _full_spec((1, c)), _full_spec((1, c)), _full_spec((1, c))],
        out_specs=(pl.BlockSpec((tt, c), lambda i: (tile(i), 0)), pl.BlockSpec((None, 8, c), lambda i: (tile(i), 0, 0))),
        out_shape=(jax.ShapeDtypeStruct((n, c), F32), jax.ShapeDtypeStruct((nt, 8, c), F32)),
        scratch_shapes=[pltpu.VMEM((tt, c), F32), pltpu.VMEM((tt, c), F32), pltpu.VMEM((1, c), F32)],
        compiler_params=_cp(("arbitrary",)),
    )(xr, wr, wi, br, bi, lam)


def _scan_bwd(xr, dh, h, hin, wr, wi, br, bi, lam, reverse, tt, dx_prev=None):
    n, c = xr.shape
    nt = n // tt
    heads, hs = wr.shape[0], wr.shape[2]
    lc = min(c, LANE)
    inc = not reverse
    first, last = (0, tt - 1) if inc else (tt - 1, 0)

    def tile(i):
        return _scan_tile_index(nt - 1 - i, nt, reverse)

    def body(*refs):
        x_ref, dh_ref, h_ref, hin_ref, wr_ref, wi_ref, br_ref, bi_ref, lam_ref = refs[:9]
        refs = refs[9:]
        if dx_prev is not None:
            dxp_ref, refs = refs[0], refs[1:]
        dx_ref, dwr_ref, dwi_ref, sum_ref, a_scr, r_scr, i_scr, sq_scr, g_scr, da_scr, u_scr, ucarry = refs
        i = pl.program_id(0)

        @pl.when(i == 0)
        def _():
            ucarry[...] = jnp.zeros(ucarry.shape, F32)
            dwr_ref[...] = jnp.zeros(dwr_ref.shape, F32)
            dwi_ref[...] = jnp.zeros(dwi_ref.shape, F32)
            sum_ref[...] = jnp.zeros(sum_ref.shape, F32)

        for hd in range(heads):
            sl = slice(hd * hs, (hd + 1) * hs)
            r, ig, _, a, sq = _head_gates(x_ref[:, sl], wr_ref[hd], wi_ref[hd], br_ref[:, sl], bi_ref[:, sl],
                                          lam_ref[:, sl])
            a_scr[:, sl], r_scr[:, sl], i_scr[:, sl], sq_scr[:, sl] = a, r, ig, sq

        rows = lax.broadcasted_iota(jnp.int32, (tt, lc), 0)
        to_prev = 1 if inc else tt - 1
        to_next = tt - 1 if inc else 1
        for ch in range(c // lc):
            sl = slice(ch * lc, (ch + 1) * lc)
            a, dhv = a_scr[:, sl], dh_ref[:, sl]
            big_a, big_b = _tile_scan(a, a * dhv, not inc, tt, rows)
            u_in = ucarry[:, sl]
            u = big_a * u_in + big_b
            u_scr[:, sl] = u
            g = dhv + jnp.where(rows == last, u_in, pltpu.roll(u, to_next, 0))
            g_scr[:, sl] = g
            h_prev = jnp.where(rows == first, hin_ref[pl.ds(0, 1), sl], pltpu.roll(h_ref[:, sl], to_prev, 0))
            da_scr[:, sl] = g * h_prev
        ucarry[...] = u_scr[pl.ds(first, 1), :]

        for hd in range(heads):
            sl = slice(hd * hs, (hd + 1) * hs)
            xv, a, r, ig, sq = x_ref[:, sl], a_scr[:, sl], r_scr[:, sl], i_scr[:, sl], sq_scr[:, sl]
            g, lam_v = g_scr[:, sl], lam_ref[:, sl]
            sp = _softplus_neg(lam_v)
            d_sq = g * ig * xv
            d_em = d_sq * 0.5 / sq
            d_log_a = (da_scr[:, sl] - 2.0 * a * d_em) * a
            dzr = (d_log_a * ((-RG_C) * sp)) * r * (1.0 - r)
            dzi = (g * sq * xv) * ig * (1.0 - ig)
            dzr_b, dzi_b, xb = dzr.astype(BF16), dzi.astype(BF16), xv.astype(BF16)
            dx = g * sq * ig + _dot(dzr_b, wr_ref[hd], 1, 1) + _dot(dzi_b, wi_ref[hd], 1, 1)
            if dx_prev is not None:
                dx = dx + dxp_ref[:, sl]
            dx_ref[:, sl] = dx
            dwr_ref[hd] += _dot(xb, dzr_b, 0, 0)
            dwi_ref[hd] += _dot(xb, dzi_b, 0, 0)
            sum_ref[pl.ds(0, 1), sl] += _sum0(dzr)
            sum_ref[pl.ds(1, 1), sl] += _sum0(dzi)
            sum_ref[pl.ds(2, 1), sl] += _sum0(d_log_a * ((-RG_C) * r)) * (-_sig(-lam_v))

    tile_spec = pl.BlockSpec((tt, c), lambda i: (tile(i), 0))
    ins = [xr, dh, h, hin, wr, wi, br, bi, lam]
    in_specs = [tile_spec, tile_spec, tile_spec, pl.BlockSpec((None, 8, c), lambda i: (tile(i), 0, 0)),
                _full_spec(wr.shape), _full_spec(wi.shape), _full_spec((1, c)), _full_spec((1, c)), _full_spec((1, c))]
    if dx_prev is not None:
        ins.append(dx_prev)
        in_specs.append(tile_spec)
    return pl.pallas_call(
        body, name="lru_scan_bwd", grid=(nt,), in_specs=in_specs,
        out_specs=(tile_spec, _full_spec(wr.shape), _full_spec(wi.shape), _full_spec((8, c))),
        out_shape=(jax.ShapeDtypeStruct((n, c), F32), jax.ShapeDtypeStruct(wr.shape, F32),
                   jax.ShapeDtypeStruct(wi.shape, F32), jax.ShapeDtypeStruct((8, c), F32)),
        scratch_shapes=[pltpu.VMEM((tt, c), F32)] * 7 + [pltpu.VMEM((1, c), F32)],
        compiler_params=_cp(("arbitrary",)),
    )(*ins)


def _conv_rows(nseg, seg):
    return nseg * (seg + CONV_PAD) + CONV_PAD


def _seg_base(s, seg):
    return CONV_PAD + s * (seg + CONV_PAD)


def _zero_gaps(scr, nseg, seg, c):
    for s in range(nseg + 1):
        scr[pl.ds(s * (seg + CONV_PAD), CONV_PAD), :] = jnp.zeros((CONV_PAD, c), F32)


def _build_shifts(scr, rot, n_rows, sl):
    for b in range(1, 8):
        rot[b - 1, pl.ds(0, n_rows - 8), :] = scr[pl.ds(b, n_rows - 8), sl]


def _tap(scr, rot, base, off, seg, sl):
    a, b = divmod(off, 8)
    if b == 0:
        return scr[pl.ds(base + 8 * a, seg), sl]
    return rot[b - 1, pl.ds(base + 8 * a, seg), :]


def _conv_chunks(nseg, seg):
    return [(s * seg + q * SEG, _seg_base(s, seg) + q * SEG) for s in range(nseg) for q in range(seg // SEG)]


def _conv_scratch(tt, c):
    rows = max(_conv_rows(1, tt), _conv_rows(tt // SEG, SEG))
    return pltpu.VMEM((rows, c), F32), pltpu.VMEM((7, rows, min(c, LANE)), F32)


def _ln_stats(v):
    mu = jnp.mean(v, axis=-1, keepdims=True)
    xc = v - mu
    rstd = lax.rsqrt(jnp.mean(xc * xc, axis=-1, keepdims=True) + LN_EPS)
    return xc * rstd, rstd


def _mix_mid(p, h_f, h_b, w31, b31, clg, clb, tt):
    n, c = h_f.shape
    taps = w31.shape[0]
    lc = min(c, LANE)

    half = taps // 2

    def body(gr_ref, cv_ref, cg_ref, hf_ref, hb_ref, w_ref, b_ref, g_ref, bb_ref, y_ref, uc_ref, scr, rot):
        i = pl.program_id(0)
        gel, _ = _gelu_and_grad(gr_ref[...])
        y_ref[:, 0:c] = ((hf_ref[...] + hb_ref[...]) * gel).astype(BF16)

        def conv(nseg, seg):
            _zero_gaps(scr, nseg, seg, c)
            for s in range(nseg):
                rows = pl.ds(s * seg, seg)
                scr[pl.ds(_seg_base(s, seg), seg), :] = cv_ref[rows, :] * _sig(cg_ref[rows, :])
            for ch in range(c // lc):
                sl = slice(ch * lc, (ch + 1) * lc)
                _build_shifts(scr, rot, _conv_rows(nseg, seg), sl)
                for out0, base in _conv_chunks(nseg, seg):
                    acc = jnp.broadcast_to(b_ref[:, sl], (SEG, lc))
                    for k in range(taps):
                        acc = acc + _tap(scr, rot, base, k - half, SEG, sl) * w_ref[pl.ds(k, 1), sl]
                    uc_ref[pl.ds(out0, SEG), sl] = acc

        @pl.when(i == 0)
        def _():
            conv(1, tt)

        @pl.when(i > 0)
        def _():
            conv(tt // SEG, SEG)

        xh, _ = _ln_stats(uc_ref[...])
        v = xh * g_ref[...] + bb_ref[...]
        y_ref[:, c:2 * c] = (v * _sig(v)).astype(BF16)

    return pl.pallas_call(
        body, name="mix_mid", grid=(n // tt,),
        in_specs=[_row_spec(tt, c, 1), _row_spec(tt, c, 2), _row_spec(tt, c, 3), _row_spec(tt, c), _row_spec(tt, c),
                  _full_spec(w31.shape), _full_spec((1, c)), _full_spec((1, c)), _full_spec((1, c))],
        out_specs=(_row_spec(tt, 2 * c), _row_spec(tt, c)),
        out_shape=(jax.ShapeDtypeStruct((n, 2 * c), BF16), jax.ShapeDtypeStruct((n, c), F32)),
        scratch_shapes=list(_conv_scratch(tt, c)),
        compiler_params=_cp(("parallel",)),
    )(p, p, p, h_f, h_b, w31, b31, clg, clb)


def _mix_mid_bwd(dymix, p, h_f, h_b, uc, w31, clg, clb, tt):
    n, c = h_f.shape
    taps = w31.shape[0]
    lc = min(c, LANE)
    sum_rows = 8 * ((taps + 3 + 7) // 8)

    half = taps // 2
    nt = n // tt

    def body(dyr_ref, dyc_ref, gr_ref, cv_ref, cg_ref, hf_ref, hb_ref, uc_ref, w_ref, g_ref, bb_ref,
             dh_ref, dp_ref, sum_ref, uscr, urot, dscr, drot, duc_scr, wacc):
        i = pl.program_id(0)

        @pl.when(i == 0)
        def _():
            sum_ref[...] = jnp.zeros(sum_ref.shape, F32)
            wacc[...] = jnp.zeros(wacc.shape, F32)

        dyr = dyr_ref[...]
        gel, dgel = _gelu_and_grad(gr_ref[...])
        dh_ref[...] = dyr * gel
        dp_ref[:, 0:c] = (dyr * (hf_ref[...] + hb_ref[...]) * dgel).astype(BF16)

        xh, rstd = _ln_stats(uc_ref[...])
        v = xh * g_ref[...] + bb_ref[...]
        sg = _sig(v)
        dv = dyc_ref[...] * (sg * (1.0 + v * (1.0 - sg)))
        dxh = dv * g_ref[...]
        m1 = jnp.mean(dxh, axis=-1, keepdims=True)
        m2 = jnp.mean(dxh * xh, axis=-1, keepdims=True)
        duc = rstd * (dxh - m1 - xh * m2)
        sum_ref[pl.ds(taps, 1), :] += _sum0(duc)
        sum_ref[pl.ds(taps + 1, 1), :] += _sum0(dv * xh)
        sum_ref[pl.ds(taps + 2, 1), :] += _sum0(dv)

        duc_scr[...] = duc

        def conv_bwd(nseg, seg):
            n_rows = _conv_rows(nseg, seg)
            _zero_gaps(dscr, nseg, seg, c)
            _zero_gaps(uscr, nseg, seg, c)
            for s in range(nseg):
                rows = pl.ds(s * seg, seg)
                dscr[pl.ds(_seg_base(s, seg), seg), :] = duc_scr[rows, :]
                uscr[pl.ds(_seg_base(s, seg), seg), :] = cv_ref[rows, :] * _sig(cg_ref[rows, :])
            for ch in range(c // lc):
                sl = slice(ch * lc, (ch + 1) * lc)
                _build_shifts(dscr, drot, n_rows, sl)
                _build_shifts(uscr, urot, n_rows, sl)
                chunks = _conv_chunks(nseg, seg)
                for out0, base in chunks:
                    rows = pl.ds(out0, SEG)
                    du = jnp.zeros((SEG, lc), F32)
                    for k in range(taps):
                        du = du + _tap(dscr, drot, base, k - half, SEG, sl) * w_ref[pl.ds(taps - 1 - k, 1), sl]
                    sgc, cv = _sig(cg_ref[rows, sl]), cv_ref[rows, sl]
                    dp_ref[rows, c + sl.start:c + sl.stop] = (du * sgc).astype(BF16)
                    dp_ref[rows, 2 * c + sl.start:2 * c + sl.stop] = (du * cv * sgc * (1.0 - sgc)).astype(BF16)
                for g0 in range(0, len(chunks), 4):
                    group = chunks[g0:g0 + 4]
                    duc_q = [duc_scr[pl.ds(out0, SEG), sl] for out0, _ in group]
                    for k in range(taps):
                        part = jnp.zeros((8, lc), F32)
                        for dq, (_, base) in zip(duc_q, group):
                            prod = dq * _tap(uscr, urot, base, k - half, SEG, sl)
                            part = part + jnp.sum(prod.reshape(SEG // 8, 8, lc), axis=0)
                        wacc[pl.ds(8 * k, 8), sl] += part

        @pl.when(i == 0)
        def _():
            conv_bwd(1, tt)

        @pl.when(i > 0)
        def _():
            conv_bwd(tt // SEG, SEG)

        @pl.when(i == nt - 1)
        def _():
            for k in range(taps):
                sum_ref[pl.ds(k, 1), :] = _sum0(wacc[pl.ds(8 * k, 8), :])

    scr_a, rot_a = _conv_scratch(tt, c)
    return pl.pallas_call(
        body, name="mix_mid_bwd", grid=(n // tt,),
        in_specs=[_row_spec(tt, c, 0), _row_spec(tt, c, 1), _row_spec(tt, c, 1), _row_spec(tt, c, 2),
                  _row_spec(tt, c, 3), _row_spec(tt, c), _row_spec(tt, c), _row_spec(tt, c),
                  _full_spec(w31.shape), _full_spec((1, c)), _full_spec((1, c))],
        out_specs=(_row_spec(tt, c), _row_spec(tt, 3 * c), _full_spec((sum_rows, c))),
        out_shape=(jax.ShapeDtypeStruct((n, c), F32), jax.ShapeDtypeStruct((n, 3 * c), BF16),
                   jax.ShapeDtypeStruct((sum_rows, c), F32)),
        scratch_shapes=[scr_a, rot_a, scr_a, rot_a, pltpu.VMEM((tt, c), F32), pltpu.VMEM((8 * taps, c), F32)],
        compiler_params=_cp(("arbitrary",), 48),
    )(dymix, dymix, p, p, p, h_f, h_b, uc, w31, clg, clb)


def _ada_fwd(cc, w_ada, b_cols):
    depth, d, wc = w_ada.shape
    tn = _pick(wc, (768, 512, 384, 256, 128))

    def body(c_ref, w_ref, b_ref, o_ref):
        cv = c_ref[...]
        o_ref[...] = _dot((cv * _sig(cv)).astype(BF16), w_ref[...].astype(BF16), 1, 0) + b_ref[...]

    return pl.pallas_call(
        body, name="ada_fwd", grid=(depth, wc // tn),
        in_specs=[_full_spec((MOD_ROWS, d)), pl.BlockSpec((None, d, tn), lambda l, j: (l, 0, j)),
                  pl.BlockSpec((None, 1, tn), lambda l, j: (l, 0, j))],
        out_specs=pl.BlockSpec((None, MOD_ROWS, tn), lambda l, j: (l, 0, j)),
        out_shape=jax.ShapeDtypeStruct((depth, MOD_ROWS, wc), F32),
        compiler_params=_cp(("parallel", "parallel")),
    )(cc, w_ada, b_cols)


def _ada_bwd(cc, dm, w_ada):
    depth, d, wc = w_ada.shape
    tn = _pick(wc, (768, 512, 384, 256, 128))

    def body(c_ref, dm_ref, w_ref, gw_ref, gc_ref):
        cv = c_ref[...]
        sg = _sig(cv)
        dmb = dm_ref[...].astype(BF16)
        gw_ref[...] = _dot((cv * sg).astype(BF16), dmb, 0, 0)

        @pl.when(jnp.logical_and(pl.program_id(0) == 0, pl.program_id(1) == 0))
        def _():
            gc_ref[...] = jnp.zeros(gc_ref.shape, F32)

        gc_ref[...] += _dot(dmb, w_ref[...].astype(BF16), 1, 1) * (sg * (1.0 + cv * (1.0 - sg)))

    return pl.pallas_call(
        body, name="ada_bwd", grid=(depth, wc // tn),
        in_specs=[_full_spec((MOD_ROWS, d)), pl.BlockSpec((None, MOD_ROWS, tn), lambda l, j: (l, 0, j)),
                  pl.BlockSpec((None, d, tn), lambda l, j: (l, 0, j))],
        out_specs=(pl.BlockSpec((None, d, tn), lambda l, j: (l, 0, j)), _full_spec((MOD_ROWS, d))),
        out_shape=(jax.ShapeDtypeStruct((depth, d, wc), F32), jax.ShapeDtypeStruct((MOD_ROWS, d), F32)),
        compiler_params=_cp(("arbitrary", "arbitrary")),
    )(cc, dm, w_ada)


def _adam(grads, w, m, v):
    depth, rows, cols = w.shape
    assert len(grads) == depth
    n_slab = grads[0].shape[0]
    tr = _pick(rows, (128, 64, 32, 16, 8))
    c1, c2 = 1.0 - ADAM_B1 ** ADAM_STEP, 1.0 - ADAM_B2 ** ADAM_STEP

    def body(*refs):
        g_refs = refs[:depth]
        w_ref, m_ref, v_ref, go_ref, do_ref, mo_ref, vo_ref = refs[depth:]
        for q in range(depth):
            @pl.when(pl.program_id(0) == q)
            def _(q=q):
                g = g_refs[q][0].astype(F32)
                for k in range(1, n_slab):
                    g = g + g_refs[q][k].astype(F32)
                go_ref[...] = g

        g = go_ref[...]
        m_new = ADAM_B1 * m_ref[...] + (1.0 - ADAM_B1) * g
        v_new = ADAM_B2 * v_ref[...] + (1.0 - ADAM_B2) * (g * g)
        mo_ref[...] = m_new
        vo_ref[...] = v_new
        do_ref[...] = -ADAM_LR * ((m_new / c1) / (jnp.sqrt(v_new / c2) + ADAM_EPS) + ADAM_WD * w_ref[...])

    blk = pl.BlockSpec((None, tr, cols), lambda l, i: (l, i, 0))
    g_specs = [pl.BlockSpec((n_slab, tr, cols), lambda l, i, q=q: (0, jnp.where(l == q, i, 0), 0))
               for q in range(depth)]
    out = jax.ShapeDtypeStruct((depth, rows, cols), F32)
    return pl.pallas_call(
        body, name="adamw", grid=(depth, rows // tr),
        in_specs=g_specs + [blk, blk, blk],
        out_specs=(blk, blk, blk, blk), out_shape=(out, out, out, out),
        compiler_params=_cp(("parallel", "parallel"), 48),
    )(*grads, w, m, v)


_WEIGHTS = ['c_ctx', 'w_ada', 'b_ada', 'ln_g', 'ln_b', 'ff1_in', 'ff1_out', 'ff2_in', 'ff2_out', 'w_in', 'conv4_w',
            'conv4_b', 'w_rg', 'b_rg', 'w_ig', 'b_ig', 'lam', 'conv31_w', 'conv31_b', 'cln_g', 'cln_b', 'w_out', 'b_out']
_SMALL_SHARDED = {'ln_g': 2, 'ln_b': 2, 'conv4_w': 2, 'w_rg': 3, 'b_rg': 2, 'w_ig': 3, 'b_ig': 2, 'lam': 2,
                  'conv31_w': 2}
_BIG_SHARDED = {'ff1_in': 2, 'ff1_out': 1, 'ff2_in': 2, 'ff2_out': 1, 'w_in': 2, 'w_out': 1}
_SMALL = [n for n in _WEIGHTS if n not in _BIG_SHARDED and n != 'w_ada']


def kernel(x, c, ctx, c_ctx, w_ada, b_ada, ln_g, ln_b, ff1_in, ff1_out, ff2_in, ff2_out, w_in, conv4_w, conv4_b, w_rg, b_rg, w_ig, b_ig, lam, conv31_w, conv31_b, cln_g, cln_b, w_out, b_out, loss_target, m_c_ctx, m_w_ada, m_b_ada, m_ln_g, m_ln_b, m_ff1_in, m_ff1_out, m_ff2_in, m_ff2_out, m_w_in, m_conv4_w, m_conv4_b, m_w_rg, m_b_rg, m_w_ig, m_b_ig, m_lam, m_conv31_w, m_conv31_b, m_cln_g, m_cln_b, m_w_out, m_b_out, v_c_ctx, v_w_ada, v_b_ada, v_ln_g, v_ln_b, v_ff1_in, v_ff1_out, v_ff2_in, v_ff2_out, v_w_in, v_conv4_w, v_conv4_b, v_w_rg, v_b_rg, v_w_ig, v_b_ig, v_lam, v_conv31_w, v_conv31_b, v_cln_g, v_cln_b, v_w_out, v_b_out):
    wts = dict(c_ctx=c_ctx, w_ada=w_ada, b_ada=b_ada, ln_g=ln_g, ln_b=ln_b, ff1_in=ff1_in, ff1_out=ff1_out,
               ff2_in=ff2_in, ff2_out=ff2_out, w_in=w_in, conv4_w=conv4_w, conv4_b=conv4_b, w_rg=w_rg, b_rg=b_rg,
               w_ig=w_ig, b_ig=b_ig, lam=lam, conv31_w=conv31_w, conv31_b=conv31_b, cln_g=cln_g, cln_b=cln_b,
               w_out=w_out, b_out=b_out)
    mom = dict(zip(_WEIGHTS, (m_c_ctx, m_w_ada, m_b_ada, m_ln_g, m_ln_b, m_ff1_in, m_ff1_out, m_ff2_in, m_ff2_out,
                              m_w_in, m_conv4_w, m_conv4_b, m_w_rg, m_b_rg, m_w_ig, m_b_ig, m_lam, m_conv31_w,
                              m_conv31_b, m_cln_g, m_cln_b, m_w_out, m_b_out)))
    var = dict(zip(_WEIGHTS, (v_c_ctx, v_w_ada, v_b_ada, v_ln_g, v_ln_b, v_ff1_in, v_ff1_out, v_ff2_in, v_ff2_out,
                              v_w_in, v_conv4_w, v_conv4_b, v_w_rg, v_b_rg, v_w_ig, v_b_ig, v_lam, v_conv31_w,
                              v_conv31_b, v_cln_g, v_cln_b, v_w_out, v_b_out)))

    depth, d, wc = w_ada.shape
    t_lat, t_ctx = x.shape[1], ctx.shape[1]
    tt, n_tok = t_ctx, t_ctx + x.shape[1]
    assert t_lat % tt == 0 and tt % SEG == 0 and tt & (tt - 1) == 0 and d % 2 == 0
    ch = d // 2
    alpha = (2.0 * depth) ** 0.25
    me = 4 * lax.axis_index("x") + 2 * lax.axis_index("y") + lax.axis_index("c")

    names = list(_SMALL_SHARDED)
    got = _gather_list([wts[k] for k in names] + [c], F32)
    full = {k: _unshard(g, _SMALL_SHARDED[k]) for k, g in zip(names, got[:-1])}
    c_all = got[-1].reshape(N_DEV, d)
    cc = jnp.concatenate([c_all, jnp.pad(c_ctx[None, :], ((0, MOD_ROWS - N_DEV - 1), (0, 0)))], axis=0)

    b_cols = _my_block(b_ada, me, 1)[:, None, :]
    (m_cols,) = _gather_list([_ada_fwd(cc, w_ada, b_cols)], F32)
    m_full = jnp.moveaxis(m_cols, 0, 2).reshape(depth, MOD_ROWS, N_DEV * wc)
    m_lat = lax.dynamic_index_in_dim(m_full, me, axis=1, keepdims=False).reshape(depth, 9, d)
    m_ctx = m_full[:, N_DEV].reshape(depth, 9, d)
    modtab = jnp.pad(jnp.stack([m_ctx, m_lat], axis=1), ((0, 0), (0, 0), (0, MOD_ROWS - 9), (0, 0)))

    big = {k: [None] * depth for k in _BIG_SHARDED}
    half = {}

    def shard(k, l):
        return wts[k][l].astype(BF16)

    def landed(k, l, g):
        big[k][l] = _unshard(g, _BIG_SHARDED[k] - 1)

    def jobs(*specs):
        return [(stage, k, l) for stage, k, l in specs if l < depth]

    def comm_of(todo):
        return [(stage, shard(k, l) if stage == "spread" else half[k, l]) for stage, k, l in todo]

    def settle(todo, got):
        for (stage, k, l), g in zip(todo, got):
            if stage == "spread":
                half[k, l] = g
            else:
                landed(k, l, g)

    got = _all_gather_arrays([shard('ff1_in', 0), shard('ff1_out', 0)])
    landed('ff1_in', 0, got[0])
    landed('ff1_out', 0, got[1])
    wr_b, wi_b = full['w_rg'].astype(BF16), full['w_ig'].astype(BF16)

    def row(v):
        return v.reshape(1, -1)

    s = jnp.concatenate([ctx[0], x[0]], axis=0)
    saved = []
    h_next = _modulate(s, modtab[0], 0, 1, tt)
    for l in range(depth):
        mt, sv = modtab[l], {}
        sv['s0'] = s
        sv['h1'] = h_next
        todo = jobs(("spread", 'w_in', l), ("spread", 'w_out', l), ("spread", 'ff2_in', l))
        (sv['z1'], sv['a1']), got = _mm_swiglu(sv['h1'], big['ff1_in'][l], comm=comm_of(todo))
        settle(todo, got)
        todo = jobs(("relay", 'w_in', l), ("relay", 'w_out', l), ("relay", 'ff2_in', l), ("spread", 'ff2_out', l))
        sv['f1'], got = _mm_nn("ffn_out", sv['a1'], big['ff1_out'][l], comm=comm_of(todo))
        settle(todo, got)
        sv['r1'], sv['s1'], sv['h2'] = _res_ln(s, sv['f1'], mt, 2, 0.5, row(full['ln_g'][l, 0]), row(full['ln_b'][l, 0]),
                                               alpha, tt, nxt=(mt, 3, 4))
        todo = jobs(("relay", 'ff2_out', l))
        sv['p'], got = _mm_nn("mix_in", sv['h2'], big['w_in'][l], comm=comm_of(todo))
        settle(todo, got)
        sv['xr'] = _conv4(sv['p'], full['conv4_w'][l], row(conv4_b[l]), tt, ch)
        for dr_, rev in ((0, False), (1, True)):
            sv['h', dr_], sv['hin', dr_] = _scan(
                sv['xr'], wr_b[l, dr_], wi_b[l, dr_], row(full['b_rg'][l, dr_]), row(full['b_ig'][l, dr_]),
                row(full['lam'][l, dr_]), rev, tt)
        sv['ymix'], sv['uc'] = _mix_mid(sv['p'], sv['h', 0], sv['h', 1], full['conv31_w'][l], row(conv31_b[l]),
                                        row(cln_g[l]), row(cln_b[l]), tt)
        sv['y'], _ = _mm_nn("mix_out", sv['ymix'], big['w_out'][l], bias=row(b_out[l]))
        sv['r2'], sv['s2'], sv['h3'] = _res_ln(sv['s1'], sv['y'], mt, 5, 1.0, row(full['ln_g'][l, 1]),
                                               row(full['ln_b'][l, 1]), alpha, tt, nxt=(mt, 6, 7))
        todo = jobs(("spread", 'ff1_in', l + 1), ("spread", 'ff1_out', l + 1))
        (sv['z3'], sv['a3']), got = _mm_swiglu(sv['h3'], big['ff2_in'][l], comm=comm_of(todo))
        settle(todo, got)
        todo = jobs(("relay", 'ff1_in', l + 1), ("relay", 'ff1_out', l + 1))
        sv['f3'], got = _mm_nn("ffn_out", sv['a3'], big['ff2_out'][l], comm=comm_of(todo))
        settle(todo, got)
        outs = _res_ln(sv['s2'], sv['f3'], mt, 8, 0.5, row(full['ln_g'][l, 2]), row(full['ln_b'][l, 2]), alpha, tt,
                       nxt=(modtab[l + 1], 0, 1) if l + 1 < depth else None)
        sv['r3'], s = outs[0], outs[1]
        h_next = outs[2] if l + 1 < depth else None
        saved.append(sv)

    cot, loss_blk = _loss_head(s, loss_target[0], tt)
    loss = lax.psum(loss_blk[0, 0], ("x", "y", "c"))

    gx = {k: [None] * depth for k in _BIG_SHARDED}
    gsm = {k: [None] * depth for k in ('ln_g', 'ln_b', 'conv4_w', 'conv4_b', 'b_rg', 'b_ig', 'lam',
                                       'conv31_w', 'conv31_b', 'cln_g', 'cln_b', 'b_out', 'dm')}
    g_gate = {'w_rg': [None] * depth, 'w_ig': [None] * depth}
    taps4, taps31 = conv4_w.shape[1], conv31_w.shape[1]
    heads, hs = w_rg.shape[2], w_rg.shape[4]

    mods, pers = {}, {}

    def ln_bwd(cot, r, f, mt, i_gate, coef, gam):
        if isinstance(cot, tuple):
            return _ln_bwd(cot, r, f, mt, i_gate, coef, gam, tt, alpha=alpha)
        return _ln_bwd(cot, r, f, mt, i_gate, coef, gam, tt) + (None,)

    def scatter(pend):
        return [("scatter", g) for _, _, g in pend]

    def exchanged(pend, got):
        for (k, l, _), g in zip(pend, got):
            gx[k][l] = g

    def ffn_bwd(cot, l, mt, r, f, z, a, h, s_in, w_in_name, w_out_name, ln_idx, i_gate, i_scale, pend_short, pend_long,
                final=False):
        dr, dfb, tot, per, up = ln_bwd(cot, r, f, mt, i_gate, 0.5, row(full['ln_g'][l, ln_idx]))
        dz, got = _mm_nt_swiglu_bwd(dfb, big[w_out_name][l], z, comm=scatter(pend_short))
        exchanged(pend_short, got)
        g_out, _ = _mm_tn("ffn_out_dw", a, dfb)
        pend_out = [(w_out_name, l, g_out.reshape(N_DEV, -1, d))]
        if final:
            g_in, got = _mm_tn("ffn_in_dw", h, dz, b_halves=True, shard_cols=True, comm=scatter(pend_long))
            exchanged(pend_long, got)
            pend_out.append((w_in_name, l, g_in))
            dh, got = _mm_nt("ffn_in_dx", dz, big[w_in_name][l], a_halves=True, comm=scatter(pend_out))
            exchanged(pend_out, got)
            pend_in = []
        else:
            dh, got = _mm_nt("ffn_in_dx", dz, big[w_in_name][l], a_halves=True, comm=scatter(pend_long))
            exchanged(pend_long, got)
            g_in, got = _mm_tn("ffn_in_dw", h, dz, b_halves=True, shard_cols=True, comm=scatter(pend_out))
            exchanged(pend_out, got)
            pend_in = [(w_in_name, l, g_in)]
        return (dr, dh, s_in, mt, i_scale), tot, per, up, pend_in

    pend = []
    for l in reversed(range(depth)):
        mt, sv = modtab[l], saved[l]
        cot, tot3, per3, up, pend_ff2_in = ffn_bwd(cot, l, mt, sv['r3'], sv['f3'], sv['z3'], sv['a3'], sv['h3'],
                                                   sv['s2'], 'ff2_in', 'ff2_out', 2, 8, 7, [], pend)
        if up is not None:
            mods[l + 1, 1] = up
        dr, dyb, tot2, per2, mods[l, 3] = ln_bwd(cot, sv['r2'], sv['y'], mt, 5, 1.0, row(full['ln_g'][l, 1]))
        dymix, _ = _mm_nt("mix_out_dx", dyb, big['w_out'][l])
        g_w_out, _ = _mm_tn("mix_out_dw", sv['ymix'], dyb)
        pend = [('w_out', l, g_w_out.reshape(N_DEV, -1, d))]
        dh_rec, dp_rest, csum = _mix_mid_bwd(dymix, sv['p'], sv['h', 0], sv['h', 1], sv['uc'], full['conv31_w'][l],
                                             row(cln_g[l]), row(cln_b[l]), tt)
        dxr, gw_r, gw_i, gsum = None, [], [], []
        for dr_, rev in ((0, False), (1, True)):
            dxr, gwr, gwi, gs = _scan_bwd(
                sv['xr'], dh_rec, sv['h', dr_], sv['hin', dr_], wr_b[l, dr_], wi_b[l, dr_],
                row(full['b_rg'][l, dr_]), row(full['b_ig'][l, dr_]), row(full['lam'][l, dr_]), rev, tt, dxr)
            gw_r.append(gwr)
            gw_i.append(gwi)
            gsum.append(gs)
        dp_xr, c4sum = _conv4_bwd(dxr, sv['p'], full['conv4_w'][l], tt, ch)
        dp = jnp.concatenate([dp_xr, dp_rest], axis=1)
        dh, got = _mm_nt("mix_in_dx", dp, big['w_in'][l], comm=scatter(pend))
        exchanged(pend, got)
        g_w_in, _ = _mm_tn("mix_in_dw", sv['h2'], dp, shard_cols=True)
        cot = (dr, dh, sv['s1'], mt, 4)
        cot, tot1, per1, mods[l, 2], pend = ffn_bwd(cot, l, mt, sv['r1'], sv['f1'], sv['z1'], sv['a1'], sv['h1'],
                                                    sv['s0'], 'ff1_in', 'ff1_out', 0, 2, 1, [('w_in', l, g_w_in)],
                                                    pend_ff2_in, final=(l == 0))
        pers[l] = (per1, per2, per3)

        gsm['ln_g'][l] = jnp.stack([tot1[0], tot2[0], tot3[0]])
        gsm['ln_b'][l] = jnp.stack([tot1[1], tot2[1], tot3[1]])
        gsm['b_out'][l] = tot2[2]
        gsm['conv4_w'][l], gsm['conv4_b'][l] = c4sum[:taps4], c4sum[taps4]
        for k, gws in (('w_rg', gw_r), ('w_ig', gw_i)):
            slabs = jnp.stack(gws).reshape(2, heads, N_DEV, hs // N_DEV, hs)
            g_gate[k][l] = jnp.moveaxis(slabs, 2, 0).reshape(N_DEV, -1, hs).astype(BF16)
        gsm['b_rg'][l] = jnp.stack([g[0] for g in gsum])
        gsm['b_ig'][l] = jnp.stack([g[1] for g in gsum])
        gsm['lam'][l] = jnp.stack([g[2] for g in gsum])
        gsm['conv31_w'][l], gsm['conv31_b'][l] = csum[:taps31], csum[taps31]
        gsm['cln_g'][l], gsm['cln_b'][l] = csum[taps31 + 1], csum[taps31 + 2]
    cot, mods[0, 1] = _mod_bwd(*cot[:5], alpha, tt)
    grad_x = cot[t_ctx:][None]
    assert not pend
    for l in range(depth):
        rows = lambda st: [v[8 * st + q] for k in (1, 2, 3) for v, q in ((mods[l, k], 1), (mods[l, k], 0), (pers[l][k - 1], 0))]
        gsm['dm'][l] = jnp.stack([jnp.stack(rows(st)) for st in (0, 1)])
    gate_x = _all_to_all(g_gate['w_rg'] + g_gate['w_ig'])

    keys = list(gsm)
    partial = [jnp.stack(gsm[k]) for k in keys]
    shapes = [a.shape for a in partial]
    gathered = _all_gather(_pack(partial, F32))
    totals = dict(zip(keys, _unpack(_sum_slabs(gathered), shapes)))
    dm_all = _unpack(gathered, shapes, lead=(N_DEV,))[keys.index('dm')]
    dm_tot = totals['dm'].reshape(depth, 2, 9 * d)
    dm_sum = _sum_slabs(jnp.stack([_pack([dm_tot[:, 0]], F32), _pack([dm_tot[:, 1]], F32)]))
    grad = {'b_ada': _unpack(dm_sum, [(depth, 9 * d)])[0]}
    for k in ('ln_g', 'ln_b', 'conv4_w', 'b_rg', 'b_ig', 'lam', 'conv31_w'):
        grad[k] = _my_block(totals[k], me, 2)
    for k in ('conv4_b', 'conv31_b', 'cln_g', 'cln_b', 'b_out'):
        grad[k] = totals[k]

    dm_rows = jnp.concatenate([jnp.moveaxis(dm_all[:, :, 1].reshape(N_DEV, depth, 9 * d), 0, 1),
                               jnp.pad(dm_tot[:, 0][:, None, :], ((0, 0), (0, MOD_ROWS - N_DEV - 1), (0, 0)))], axis=1)
    dm_cols = lax.dynamic_slice_in_dim(dm_rows, me * wc, wc, axis=2)
    grad_w_ada, gcc = _ada_bwd(cc, dm_cols, w_ada)
    gcc_all = _all_gather(_pack([gcc], F32))
    grad['c_ctx'] = _unpack(_sum_slabs(gcc_all), [(MOD_ROWS, d)])[0][N_DEV]

    res = {}
    small = [k for k in _SMALL if k not in g_gate]
    gs, ws, ms, vs = (_pack([src[k] for k in small], F32).reshape(1, -1, 8 * LANE) for src in (grad, wts, mom, var))
    outs = _adam([gs], ws, ms, vs)
    small_shapes = [wts[k].shape for k in small]
    for k, vals in zip(small, zip(*(_unpack(o[0], small_shapes) for o in outs))):
        res[k] = vals
    res['w_ada'] = _adam([grad_w_ada[l][None] for l in range(depth)], w_ada, m_w_ada, v_w_ada)
    for j, k in enumerate(g_gate):
        flat = (depth, -1, hs)
        outs = _adam([gate_x[j * depth + l] for l in range(depth)], wts[k].reshape(flat), mom[k].reshape(flat),
                     var[k].reshape(flat))
        res[k] = tuple(o.reshape(wts[k].shape) for o in outs)
    for k in _BIG_SHARDED:
        res[k] = _adam(gx[k], wts[k], mom[k], var[k])

    out = [loss, grad_x]
    for j in range(4):
        out += [res[k][j] for k in _WEIGHTS]
    return tuple(out)
```

```python
import math

import jax
import jax.numpy as jnp
from jax import lax
from jax.experimental import pallas as pl
from jax.experimental.pallas import tpu as pltpu

F32 = jnp.float32
BF16 = jnp.bfloat16
N_DEV = 8
MESH = pl.DeviceIdType.MESH
LN_EPS = 1e-6
RG_C = 8.0
SEG = 64
CONV_PAD = 16
ADAM_LR, ADAM_B1, ADAM_B2, ADAM_EPS, ADAM_WD, ADAM_STEP = 0.001, 0.9, 0.999, 1e-08, 0.01, 10
LANE = 128
PACK_QUANTUM = 16 * LANE
PACK_ROWS = 512
MOD_ROWS = 16
GELU_K = 0.7978845608028654
GELU_C = 0.044715

_TM = (768, 640, 512, 384, 256, 128)
_TN = (1024, 512, 256, 128)
_TK = (2816, 1024, 512, 256, 128)
_SUB = (256, 128)


def _cp(sem=None, vmem_mb=40):
    return pltpu.CompilerParams(dimension_semantics=sem, vmem_limit_bytes=vmem_mb * 2 ** 20)


def _pick(n, cands):
    for cand in cands:
        if n % cand == 0:
            return cand
    return n


def _sig(v):
    return 1.0 / (1.0 + jnp.exp(-v))


def _dot(a, b, ca, cb):
    return lax.dot_general(a, b, (((ca,), (cb,)), ((), ())), preferred_element_type=F32)


def _sum0(v):
    return jnp.sum(v, axis=0, keepdims=True)


def _gelu_and_grad(v):
    inner = GELU_K * (v + GELU_C * v * v * v)
    t = jnp.tanh(inner)
    gel = 0.5 * v * (1.0 + t)
    dgel = 0.5 * (1.0 + t) + 0.5 * v * (1.0 - t * t) * GELU_K * (1.0 + 3.0 * GELU_C * v * v)
    return gel, dgel


def _softplus_neg(lam):
    y = jnp.exp(-jnp.abs(lam))
    u = 1.0 + y
    log1p = jnp.where(u == 1.0, y, jnp.log(u) * (y / (u - 1.0)))
    return jnp.maximum(-lam, 0.0) + log1p


def _neg_expm1(z):
    ser = -z * (1.0 + z * (1 / 2 + z * (1 / 6 + z * (1 / 24 + z * (1 / 120 + z * (1 / 720 + z * (1 / 5040)))))))
    return jnp.where(z > -0.25, ser, 1.0 - jnp.exp(z))


def _row_spec(tt, w, col=0):
    return pl.BlockSpec((tt, w), lambda i: (i, col))


def _full_spec(shape):
    nd = len(shape)
    return pl.BlockSpec(shape, lambda *_: (0,) * nd)


def _mod_spec(d):
    return pl.BlockSpec((None, MOD_ROWS, d), lambda i: (jnp.minimum(i, 1), 0, 0))


def _stream_sum_spec(w):
    return pl.BlockSpec((8, w), lambda i: (jnp.minimum(i, 1), 0))


def _acc_rows(ref, init, rows):
    @pl.when(init)
    def _():
        ref[...] = jnp.zeros(ref.shape, F32)

    for k, row in enumerate(rows):
        ref[pl.ds(k, 1), :] += row


def _all_gather_arrays(arrs):
    n_a = len(arrs)

    def body(*refs):
        x_refs, out_refs = refs[:n_a], refs[n_a:2 * n_a]
        send_sems, recv_sems, local_sems = refs[2 * n_a:]
        x, y, c = lax.axis_index("x"), lax.axis_index("y"), lax.axis_index("c")
        me, sibling = (x, y, c), (x, y, 1 - c)
        chips = [(1 - x, y), (x, 1 - y), (1 - x, 1 - y)]

        def slot(a, px, py, pc):
            return out_refs[a].at[4 * px + 2 * py + pc]

        def copy(a, k, block, to, src=None):
            return pltpu.make_async_remote_copy(
                src_ref=slot(a, *block) if src is None else src, dst_ref=slot(a, *block),
                send_sem=send_sems.at[7 * a + k], recv_sem=recv_sems.at[7 * a + k], device_id=to,
                device_id_type=MESH)

        started = []
        for a in range(n_a):
            mine = pltpu.make_async_copy(x_refs[a], slot(a, *me), local_sems.at[a])
            mine.start()
            first = [copy(a, 0, me, sibling, src=x_refs[a])]
            first += [copy(a, 1 + j, me, (*chip, c), src=x_refs[a]) for j, chip in enumerate(chips)]
            for cp in first:
                cp.start()
            started.append((mine, first))
        for a in range(n_a):
            mine, first = started[a]
            passed = [copy(a, 4 + j, (*chip, c), sibling) for j, chip in enumerate(chips)]
            for j, chip in enumerate(chips):
                copy(a, 1 + j, (*chip, c), me).wait_recv()
                passed[j].start()
            copy(a, 0, sibling, me).wait_recv()
            for j, chip in enumerate(chips):
                copy(a, 4 + j, (*chip, 1 - c), me).wait_recv()
            for cp in first + passed:
                cp.wait_send()
            mine.wait()

    return pl.pallas_call(
        body, name="all_gather",
        out_shape=tuple(jax.ShapeDtypeStruct((N_DEV,) + v.shape, v.dtype) for v in arrs),
        in_specs=[pl.BlockSpec(memory_space=pltpu.HBM)] * n_a,
        out_specs=(pl.BlockSpec(memory_space=pltpu.HBM),) * n_a,
        scratch_shapes=[pltpu.SemaphoreType.DMA((7 * n_a,)), pltpu.SemaphoreType.DMA((7 * n_a,)),
                        pltpu.SemaphoreType.DMA((n_a,))],
    )(*arrs)


def _all_gather(v):
    return _all_gather_arrays([v])[0]


_PEER_FLIPS = [(0, 0, 1), (1, 0, 0), (0, 1, 0), (1, 1, 0), (1, 0, 1), (0, 1, 1), (1, 1, 1)]


def _all_to_all(parts):
    n_parts = len(parts)

    def body(*refs):
        x_refs, out_ref = refs[:n_parts], refs[n_parts]
        send_sems, recv_sems, local_sems = refs[n_parts + 1:]
        x, y, c = lax.axis_index("x"), lax.axis_index("y"), lax.axis_index("c")
        me = 4 * x + 2 * y + c
        local, remote = [], []
        for l in range(n_parts):
            cp = pltpu.make_async_copy(x_refs[l].at[me], out_ref.at[l, me], local_sems.at[l])
            cp.start()
            local.append(cp)
            for k, (fx, fy, fc) in enumerate(_PEER_FLIPS):
                px = 1 - x if fx else x
                py = 1 - y if fy else y
                pc = 1 - c if fc else c
                cp = pltpu.make_async_remote_copy(
                    src_ref=x_refs[l].at[4 * px + 2 * py + pc], dst_ref=out_ref.at[l, me],
                    send_sem=send_sems.at[7 * l + k], recv_sem=recv_sems.at[7 * l + k],
                    device_id=(px, py, pc), device_id_type=MESH)
                cp.start()
                remote.append(cp)
        for cp in remote:
            cp.wait()
        for cp in local:
            cp.wait()

    shape = parts[0].shape
    return pl.pallas_call(
        body, name="all_to_all",
        out_shape=jax.ShapeDtypeStruct((n_parts,) + shape, parts[0].dtype),
        in_specs=[pl.BlockSpec(memory_space=pltpu.HBM)] * n_parts,
        out_specs=pl.BlockSpec(memory_space=pltpu.HBM),
        scratch_shapes=[pltpu.SemaphoreType.DMA((7 * n_parts,)), pltpu.SemaphoreType.DMA((7 * n_parts,)),
                        pltpu.SemaphoreType.DMA((n_parts,))],
    )(*parts)


def _pack(arrs, dtype):
    pieces = []
    for a in arrs:
        flat = a.astype(dtype).reshape(-1)
        flat = jnp.pad(flat, (0, (-flat.shape[0]) % PACK_QUANTUM))
        pieces.append(flat.reshape(-1, LANE))
    packed = jnp.concatenate(pieces, axis=0) if len(pieces) > 1 else pieces[0]
    return jnp.pad(packed, ((0, (-packed.shape[0]) % PACK_ROWS), (0, 0)))


def _unpack(packed, shapes, lead=()):
    flat = packed.reshape(lead + (-1,))
    outs, off = [], 0
    for sh in shapes:
        n = math.prod(sh)
        outs.append(flat[..., off:off + n].reshape(lead + tuple(sh)))
        off += n + (-n) % PACK_QUANTUM
    return outs


def _gather_list(arrs, dtype):
    gathered = _all_gather(_pack(arrs, dtype))
    return _unpack(gathered, [a.shape for a in arrs], lead=(N_DEV,))


def _unshard(g, axis):
    y = jnp.moveaxis(g, 0, axis)
    sh = y.shape
    return y.reshape(sh[:axis] + (sh[axis] * sh[axis + 1],) + sh[axis + 2:])


def _my_block(full, me, axis):
    size = full.shape[axis] // N_DEV
    return lax.dynamic_slice_in_dim(full, me * size, size, axis=axis)


def _sum_slabs(v):
    n_slab, rows, width = v.shape
    tr = _pick(rows, (1024, 512, 256, 128, 64, 32, 16, 8))

    def body(v_ref, o_ref):
        acc = v_ref[0]
        for k in range(1, n_slab):
            acc = acc + v_ref[k]
        o_ref[...] = acc

    return pl.pallas_call(
        body, name="sum_slabs", grid=(rows // tr,),
        in_specs=[pl.BlockSpec((n_slab, tr, width), lambda i: (0, i, 0))],
        out_specs=pl.BlockSpec((tr, width), lambda i: (i, 0)),
        out_shape=jax.ShapeDtypeStruct((rows, width), F32),
        compiler_params=_cp(("parallel",)),
    )(v)


N_LOCAL = 1


def _comm_copies(kinds, src_refs, dst_refs, send_sems, recv_sems, local_sems):
    x, y, c = lax.axis_index("x"), lax.axis_index("y"), lax.axis_index("c")
    me = 4 * x + 2 * y + c
    copies = []

    def remote(j, k, src, dst, peer):
        copies.append(pltpu.make_async_remote_copy(
            src_ref=src, dst_ref=dst, send_sem=send_sems.at[7 * j + k], recv_sem=recv_sems.at[7 * j + k],
            device_id=peer, device_id_type=MESH))

    def local(j, k, src, dst):
        copies.append(pltpu.make_async_copy(src, dst, local_sems.at[N_LOCAL * j + k]))

    for j, kind in enumerate(kinds):
        src, dst = src_refs[j], dst_refs[j]
        peers = [(1 - x if fx else x, 1 - y if fy else y, 1 - c if fc else c) for fx, fy, fc in _PEER_FLIPS]
        if kind == "scatter":
            local(j, 0, src.at[me], dst.at[me])
            for k, (px, py, pc) in enumerate(peers):
                remote(j, k, src.at[4 * px + 2 * py + pc], dst.at[me], (px, py, pc))
        elif kind == "spread":
            local(j, 0, src, dst.at[me])
            for k, peer in enumerate(peers[:4]):
                remote(j, k, src, dst.at[me], peer)
        else:
            for k, (px, py, pc) in enumerate(peers[1:4]):
                remote(j, 4 + k, dst.at[4 * px + 2 * py + pc], dst.at[4 * px + 2 * py + pc], peers[0])
    return copies


def _mm(name, grid, nk, ins, in_specs, out_shape, out_specs, acc_shape, prod, fin, comm=(), vmem_mb=48):
    n_in, n_out, n_c = len(ins), len(out_shape), len(comm)
    kinds = [kind for kind, _ in comm]
    srcs = [a for _, a in comm]
    dsts = [jax.ShapeDtypeStruct(((N_DEV,) + a.shape) if kind == "spread" else a.shape, a.dtype) for kind, a in comm]
    hbm = pl.BlockSpec(memory_space=pltpu.HBM)

    def body(*refs):
        in_refs, refs = refs[:n_in], refs[n_in:]
        src_refs, refs = refs[:n_c], refs[n_c:]
        out_refs, refs = refs[:n_out], refs[n_out:]
        dst_refs, scratch = refs[:n_c], refs[n_c:]
        if nk > 1:
            acc, scratch = scratch[0], scratch[1:]
        ids = [pl.program_id(a) for a in range(3)]
        if n_c:
            copies = _comm_copies(kinds, src_refs, dst_refs, *scratch)

            @pl.when(jnp.logical_and(ids[0] == 0, jnp.logical_and(ids[1] == 0, ids[2] == 0)))
            def _():
                for cp in copies:
                    cp.start()

        if nk == 1:
            fin(prod(*in_refs), in_refs, out_refs)
        else:
            @pl.when(ids[2] == 0)
            def _():
                acc[...] = jnp.zeros(acc.shape, F32)

            for idx, val in enumerate(prod(*in_refs)):
                acc[idx] += val

            @pl.when(ids[2] == nk - 1)
            def _():
                fin(tuple(acc[idx] for idx in range(acc_shape[0])), in_refs, out_refs)

        if n_c:
            @pl.when(jnp.logical_and(ids[0] == grid[0] - 1, jnp.logical_and(ids[1] == grid[1] - 1, ids[2] == nk - 1)))
            def _():
                for cp in copies:
                    cp.wait()

    scratch = [pltpu.VMEM(acc_shape, F32)] if nk > 1 else []
    if n_c:
        scratch += [pltpu.SemaphoreType.DMA((7 * n_c,)), pltpu.SemaphoreType.DMA((7 * n_c,)),
                    pltpu.SemaphoreType.DMA((N_LOCAL * n_c,))]
    sem = ("arbitrary",) * 3 if n_c else ("parallel", "parallel", "arbitrary")
    res = pl.pallas_call(
        body, name=name + (f"_x{n_c}" if n_c else ""), grid=grid,
        in_specs=list(in_specs) + [hbm] * n_c, out_specs=tuple(out_specs) + (hbm,) * n_c,
        out_shape=tuple(out_shape) + tuple(dsts), scratch_shapes=scratch,
        input_output_aliases={n_in + j: n_out + j for j, kind in enumerate(kinds) if kind == "relay"},
        compiler_params=_cp(sem, vmem_mb),
    )(*ins, *srcs)
    return res[:n_out], list(res[n_out:])


def _mm_nn(name, a, b, bias=None, comm=()):
    m, kk = a.shape
    nn = b.shape[1]
    tm, tn = _pick(m, _TM), _pick(nn, _TN)
    tk = kk if kk <= 2048 else _pick(kk, _TK)
    nk = kk // tk
    ins = [a, b]
    specs = [pl.BlockSpec((tm, tk), lambda i, j, k: (i, k)), pl.BlockSpec((tk, tn), lambda i, j, k: (k, j))]
    if bias is not None:
        ins.append(bias)
        specs.append(pl.BlockSpec((1, tn), lambda i, j, k: (0, j)))

    def prod(a_ref, b_ref, *_):
        return (_dot(a_ref[...], b_ref[...], 1, 0),)

    def fin(vals, in_refs, out_refs):
        out_refs[0][...] = vals[0] if bias is None else vals[0] + in_refs[2][...]

    outs, got = _mm(name, (m // tm, nn // tn, nk), nk, ins, specs,
                    (jax.ShapeDtypeStruct((m, nn), F32),), (pl.BlockSpec((tm, tn), lambda i, j, k: (i, j)),),
                    (1, tm, tn), prod, fin, comm)
    return outs[0], got


def _mm_swiglu(h, w1, comm=()):
    m, kk = h.shape
    f = w1.shape[1] // 2
    tm, tn = _pick(m, _TM), _pick(f, (512, 256, 128))
    nj = f // tn
    sub = _pick(tm, _SUB)

    def prod(*_):
        return ()

    def fin(_, in_refs, out_refs):
        h_ref, wg_ref, wu_ref = in_refs
        for q in range(tm // sub):
            rows = pl.ds(q * sub, sub)
            hq = h_ref[rows, :]
            zg, zu = _dot(hq, wg_ref[...], 1, 0), _dot(hq, wu_ref[...], 1, 0)
            out_refs[0][0, rows, :] = zg.astype(BF16)
            out_refs[0][1, rows, :] = zu.astype(BF16)
            out_refs[1][rows, :] = (zg * _sig(zg) * zu).astype(BF16)

    return _mm("ffn_in", (m // tm, nj, 1), 1, [h, w1, w1],
               [pl.BlockSpec((tm, kk), lambda i, j, k: (i, 0)),
                pl.BlockSpec((kk, tn), lambda i, j, k: (0, j)),
                pl.BlockSpec((kk, tn), lambda i, j, k: (0, j + nj))],
               (jax.ShapeDtypeStruct((2, m, f), BF16), jax.ShapeDtypeStruct((m, f), BF16)),
               (pl.BlockSpec((2, tm, tn), lambda i, j, k: (0, i, j)), pl.BlockSpec((tm, tn), lambda i, j, k: (i, j))),
               (2, tm, tn), prod, fin, comm)


def _mm_nt_swiglu_bwd(dfb, w2, z, comm=()):
    m, kk = dfb.shape
    f = w2.shape[0]
    tm, tn = _pick(m, _TM), _pick(f, (512, 256, 128))
    sub = _pick(tm, _SUB)

    def prod(*_):
        return ()

    def fin(_, in_refs, out_refs):
        a_ref, b_ref, z_ref = in_refs
        for q in range(tm // sub):
            rows = pl.ds(q * sub, sub)
            da = _dot(a_ref[rows, :], b_ref[...], 1, 1)
            zg, zu = z_ref[0, rows, :].astype(F32), z_ref[1, rows, :].astype(F32)
            sg = _sig(zg)
            out_refs[0][0, rows, :] = (da * zu * (sg * (1.0 + zg * (1.0 - sg)))).astype(BF16)
            out_refs[0][1, rows, :] = (da * (zg * sg)).astype(BF16)

    outs, got = _mm("ffn_out_dx", (m // tm, f // tn, 1), 1, [dfb, w2, z],
                    [pl.BlockSpec((tm, kk), lambda i, j, k: (i, 0)),
                     pl.BlockSpec((tn, kk), lambda i, j, k: (j, 0)),
                     pl.BlockSpec((2, tm, tn), lambda i, j, k: (0, i, j))],
                    (jax.ShapeDtypeStruct((2, m, f), BF16),),
                    (pl.BlockSpec((2, tm, tn), lambda i, j, k: (0, i, j)),),
                    (1, tm, tn), prod, fin, comm)
    return outs[0], got


def _mm_nt(name, a, b, a_halves=False, comm=()):
    nn, kk = b.shape
    m = a.shape[1] if a_halves else a.shape[0]
    kh = kk // 2 if a_halves else kk
    tm, tn = _pick(m, _TM), _pick(nn, _TN)
    tk = kh if (kh <= 4096 and not a_halves) else _pick(kh, _TK)
    nk, nkh = kk // tk, kh // tk
    if a_halves:
        a_spec = pl.BlockSpec((None, tm, tk), lambda i, j, k: (k // nkh, i, k % nkh))
    else:
        a_spec = pl.BlockSpec((tm, tk), lambda i, j, k: (i, k))

    def prod(a_ref, b_ref):
        return (_dot(a_ref[...], b_ref[...], 1, 1),)

    def fin(vals, in_refs, out_refs):
        out_refs[0][...] = vals[0]

    outs, got = _mm(name, (m // tm, nn // tn, nk), nk, [a, b],
                    [a_spec, pl.BlockSpec((tn, tk), lambda i, j, k: (j, k))],
                    (jax.ShapeDtypeStruct((m, nn), F32),), (pl.BlockSpec((tm, tn), lambda i, j, k: (i, j)),),
                    (1, tm, tn), prod, fin, comm)
    return outs[0], got


def _mm_tn(name, a, b, b_halves=False, shard_cols=False, comm=()):
    kt, m = a.shape
    nn = 2 * b.shape[2] if b_halves else b.shape[1]
    tk, tm = _pick(kt, _TM), _pick(m, (1408, 1024, 512, 256, 128))
    shard = nn // N_DEV
    per = 1
    while shard_cols and 2 * per * shard <= nn // 2 and ((per * shard) % 256 or per * shard < 1024):
        per *= 2
    tn = per * shard if shard_cols else _pick(nn, _TN)
    njh = (nn // 2) // tn if b_halves else 0
    if b_halves:
        b_spec = pl.BlockSpec((None, tk, tn), lambda i, j, k: (j // njh, k, j % njh))
    else:
        b_spec = pl.BlockSpec((tk, tn), lambda i, j, k: (k, j))
    if shard_cols:
        out_shape = jax.ShapeDtypeStruct((N_DEV, m, shard), BF16)
        out_spec = pl.BlockSpec((per, tm, shard), lambda i, j, k: (j, i, 0))
    else:
        out_shape = jax.ShapeDtypeStruct((m, nn), BF16)
        out_spec = pl.BlockSpec((tm, tn), lambda i, j, k: (i, j))

    def prod(a_ref, b_ref):
        return (_dot(a_ref[...], b_ref[...], 0, 0),)

    def fin(vals, in_refs, out_refs):
        r = vals[0].astype(BF16)
        if shard_cols:
            for q in range(per):
                out_refs[0][q] = r[:, q * shard:(q + 1) * shard]
        else:
            out_refs[0][...] = r

    outs, got = _mm(name, (m // tm, nn // tn, kt // tk), kt // tk, [a, b],
                    [pl.BlockSpec((tk, tm), lambda i, j, k: (k, i)), b_spec],
                    (out_shape,), (out_spec,), (1, tm, tn), prod, fin, comm, vmem_mb=56)
    return outs[0], got


def _modulate(s, modtab, i_shift, i_scale, tt):
    n, d = s.shape

    def body(s_ref, m_ref, o_ref):
        shift, scale = m_ref[pl.ds(i_shift, 1), :], m_ref[pl.ds(i_scale, 1), :]
        o_ref[...] = (s_ref[...] * (1.0 + scale) + shift).astype(BF16)

    return pl.pallas_call(
        body, name="modulate", grid=(n // tt,),
        in_specs=[_row_spec(tt, d), _mod_spec(d)], out_specs=_row_spec(tt, d),
        out_shape=jax.ShapeDtypeStruct((n, d), BF16), compiler_params=_cp(("parallel",)),
    )(s, modtab)


def _res_ln(s, f, modtab, i_gate, coef, gam, bet, alpha, tt, nxt=None):
    n, d = s.shape

    def body(s_ref, f_ref, m_ref, g_ref, b_ref, *refs):
        gate = m_ref[pl.ds(i_gate, 1), :]
        r = alpha * s_ref[...] + (coef * gate) * f_ref[...]
        mu = jnp.mean(r, axis=-1, keepdims=True)
        xc = r - mu
        var = jnp.mean(xc * xc, axis=-1, keepdims=True)
        out = xc * lax.rsqrt(var + LN_EPS) * g_ref[...] + b_ref[...]
        if nxt is None:
            r_ref, o_ref = refs
        else:
            m2_ref, r_ref, o_ref, h_ref = refs
            shift, scale = m2_ref[pl.ds(nxt[1], 1), :], m2_ref[pl.ds(nxt[2], 1), :]
            h_ref[...] = (out * (1.0 + scale) + shift).astype(BF16)
        r_ref[...] = r
        o_ref[...] = out

    ins = [s, f, modtab, gam, bet] + ([] if nxt is None else [nxt[0]])
    n_h = 0 if nxt is None else 1
    return pl.pallas_call(
        body, name="res_ln", grid=(n // tt,),
        in_specs=[_row_spec(tt, d), _row_spec(tt, d), _mod_spec(d), _full_spec((1, d)), _full_spec((1, d))]
        + [_mod_spec(d)] * n_h,
        out_specs=(_row_spec(tt, d),) * (2 + n_h),
        out_shape=(jax.ShapeDtypeStruct((n, d), F32),) * 2 + (jax.ShapeDtypeStruct((n, d), BF16),) * n_h,
        compiler_params=_cp(("parallel",)),
    )(*ins)


def _ln_bwd(do, r, f, modtab, i_gate, coef, gam, tt, alpha=None):
    fused = isinstance(do, tuple)
    n, d = r.shape

    def body(*refs):
        if fused:
            dru_ref, dhu_ref, su_ref, mu_ref = refs[:4]
            r_ref, f_ref, m_ref, g_ref, dr_ref, dfb_ref, tot_ref, str_ref, up_ref = refs[4:]
        else:
            do_ref, r_ref, f_ref, m_ref, g_ref, dr_ref, dfb_ref, tot_ref, str_ref = refs
        i = pl.program_id(0)
        r = r_ref[...]
        mu = jnp.mean(r, axis=-1, keepdims=True)
        xc = r - mu
        rstd = lax.rsqrt(jnp.mean(xc * xc, axis=-1, keepdims=True) + LN_EPS)
        xh = xc * rstd
        if fused:
            dh_up = dhu_ref[...]
            dout = alpha * dru_ref[...] + dh_up * (1.0 + mu_ref[pl.ds(do[4], 1), :])
            _acc_rows(up_ref, i <= 1, [_sum0(dh_up * su_ref[...]), _sum0(dh_up)])
        else:
            dout = do_ref[...]
        dxh = dout * g_ref[...]
        m1 = jnp.mean(dxh, axis=-1, keepdims=True)
        m2 = jnp.mean(dxh * xh, axis=-1, keepdims=True)
        dr = rstd * (dxh - m1 - xh * m2)
        dr_ref[...] = dr
        dfb = (coef * m_ref[pl.ds(i_gate, 1), :]) * dr
        dfb_ref[...] = dfb.astype(BF16)
        _acc_rows(tot_ref, i == 0, [_sum0(dout * xh), _sum0(dout), _sum0(dfb)])
        _acc_rows(str_ref, i <= 1, [_sum0(coef * f_ref[...] * dr)])

    row = _row_spec(tt, d)
    if fused:
        ins, in_specs = list(do[:4]), [row, row, row, _mod_spec(d)]
    else:
        ins, in_specs = [do], [row]
    n_up = 1 if fused else 0
    return pl.pallas_call(
        body, name="ln_bwd", grid=(n // tt,),
        in_specs=in_specs + [row, row, _mod_spec(d), _full_spec((1, d))],
        out_specs=(row, row, _full_spec((8, d)), _stream_sum_spec(d)) + (_stream_sum_spec(d),) * n_up,
        out_shape=(jax.ShapeDtypeStruct((n, d), F32), jax.ShapeDtypeStruct((n, d), BF16),
                   jax.ShapeDtypeStruct((8, d), F32), jax.ShapeDtypeStruct((16, d), F32))
        + (jax.ShapeDtypeStruct((16, d), F32),) * n_up,
        compiler_params=_cp(("arbitrary",), 48),
    )(*ins, r, f, modtab, gam)


def _mod_bwd(dr, dh, s, modtab, i_scale, alpha, tt):
    n, d = dr.shape

    def body(dr_ref, dh_ref, s_ref, m_ref, o_ref, str_ref):
        i = pl.program_id(0)
        dh_v = dh_ref[...]
        o_ref[...] = alpha * dr_ref[...] + dh_v * (1.0 + m_ref[pl.ds(i_scale, 1), :])
        _acc_rows(str_ref, i <= 1, [_sum0(dh_v * s_ref[...]), _sum0(dh_v)])

    return pl.pallas_call(
        body, name="mod_bwd", grid=(n // tt,),
        in_specs=[_row_spec(tt, d), _row_spec(tt, d), _row_spec(tt, d), _mod_spec(d)],
        out_specs=(_row_spec(tt, d), _stream_sum_spec(d)),
        out_shape=(jax.ShapeDtypeStruct((n, d), F32), jax.ShapeDtypeStruct((16, d), F32)),
        compiler_params=_cp(("arbitrary",)),
    )(dr, dh, s, modtab)


def _loss_head(s, target, tt):
    n, d = s.shape
    nt = n // tt

    def body(s_ref, t_ref, do_ref, loss_ref, acc):
        i = pl.program_id(0)

        @pl.when(i == 0)
        def _():
            acc[...] = jnp.zeros(acc.shape, F32)
            do_ref[...] = jnp.zeros(do_ref.shape, F32)

        @pl.when(i > 0)
        def _():
            err = s_ref[...] - t_ref[...]
            do_ref[...] = err / d
            acc[...] += jnp.sum((err * err).reshape(tt // 8, 8, d), axis=0)

        @pl.when(i == nt - 1)
        def _():
            loss_ref[...] = jnp.full(loss_ref.shape, jnp.sum(acc[...]) * (0.5 / d), F32)

    return pl.pallas_call(
        body, name="loss_head", grid=(nt,),
        in_specs=[_row_spec(tt, d), pl.BlockSpec((tt, d), lambda i: (jnp.maximum(i - 1, 0), 0))],
        out_specs=(_row_spec(tt, d), _full_spec((8, LANE))),
        out_shape=(jax.ShapeDtypeStruct((n, d), F32), jax.ShapeDtypeStruct((8, LANE), F32)),
        scratch_shapes=[pltpu.VMEM((8, d), F32)],
        compiler_params=_cp(("arbitrary",)),
    )(s, target)


def _halo_specs(tt, c, n):
    nb, last = tt // 8, n // 8 - 1
    return [pl.BlockSpec((tt, c), lambda i: (i, 0)),
            pl.BlockSpec((8, c), lambda i: (jnp.maximum(i * nb - 1, 0), 0)),
            pl.BlockSpec((8, c), lambda i: (jnp.minimum((i + 1) * nb, last), 0))]


def _fill_halo(scr, main_ref, prev_ref, next_ref, i, nt, tt):
    has_prev = i >= 2
    has_next = jnp.logical_and(i >= 1, i < nt - 1)
    scr[pl.ds(0, 8), :] = jnp.where(has_prev, prev_ref[...], 0.0)
    scr[pl.ds(8, tt), :] = main_ref[...]
    scr[pl.ds(8 + tt, 8), :] = jnp.where(has_next, next_ref[...], 0.0)


def _conv4(p, w, b, tt, c):
    n = p.shape[0]
    nt, taps = n // tt, w.shape[0]
    left = taps // 2

    def body(x_ref, xp_ref, xn_ref, w_ref, b_ref, o_ref, scr):
        _fill_halo(scr, x_ref, xp_ref, xn_ref, pl.program_id(0), nt, tt)
        acc = jnp.broadcast_to(b_ref[...], (tt, c))
        for k in range(taps):
            acc = acc + w_ref[pl.ds(k, 1), :] * scr[pl.ds(8 + k - left, tt), :]
        o_ref[...] = acc

    return pl.pallas_call(
        body, name="conv4", grid=(nt,),
        in_specs=_halo_specs(tt, c, n) + [_full_spec(w.shape), _full_spec((1, c))],
        out_specs=_row_spec(tt, c), out_shape=jax.ShapeDtypeStruct((n, c), F32),
        scratch_shapes=[pltpu.VMEM((tt + 16, c), F32)], compiler_params=_cp(("parallel",)),
    )(p, p, p, w, b)


def _conv4_bwd(dxr, p, w, tt, c):
    n = p.shape[0]
    nt, taps = n // tt, w.shape[0]
    left = taps // 2

    def body(d_ref, dp_ref, dn_ref, x_ref, xp_ref, xn_ref, w_ref, o_ref, sum_ref, dscr, xscr):
        i = pl.program_id(0)
        _fill_halo(dscr, d_ref, dp_ref, dn_ref, i, nt, tt)
        _fill_halo(xscr, x_ref, xp_ref, xn_ref, i, nt, tt)
        acc = jnp.zeros((tt, c), F32)
        for k in range(taps):
            acc = acc + w_ref[pl.ds(k, 1), :] * dscr[pl.ds(8 - (k - left), tt), :]
        o_ref[...] = acc.astype(BF16)
        d = d_ref[...]
        rows = [_sum0(d * xscr[pl.ds(8 + k - left, tt), :]) for k in range(taps)] + [_sum0(d)]
        _acc_rows(sum_ref, i == 0, rows)

    return pl.pallas_call(
        body, name="conv4_bwd", grid=(nt,),
        in_specs=_halo_specs(tt, c, n) + _halo_specs(tt, c, n) + [_full_spec(w.shape)],
        out_specs=(_row_spec(tt, c), _full_spec((8, c))),
        out_shape=(jax.ShapeDtypeStruct((n, c), BF16), jax.ShapeDtypeStruct((8, c), F32)),
        scratch_shapes=[pltpu.VMEM((tt + 16, c), F32), pltpu.VMEM((tt + 16, c), F32)],
        compiler_params=_cp(("arbitrary",)),
    )(dxr, dxr, dxr, p, p, p, w)


def _head_gates(xv, wr, wi, br, bi, lam):
    xb = xv.astype(BF16)
    r = _sig(_dot(xb, wr, 1, 0) + br)
    ig = _sig(_dot(xb, wi, 1, 0) + bi)
    sp = _softplus_neg(lam)
    log_a = (-RG_C) * r * sp
    return r, ig, sp, jnp.exp(log_a), jnp.sqrt(_neg_expm1(2.0 * log_a))


def _tile_scan(a, b, increasing, tt, rows):
    s = 1
    while s < tt:
        if increasing:
            a_sh, b_sh, ok = pltpu.roll(a, s, 0), pltpu.roll(b, s, 0), rows >= s
        else:
            a_sh, b_sh, ok = pltpu.roll(a, tt - s, 0), pltpu.roll(b, tt - s, 0), rows < tt - s
        b = jnp.where(ok, a * b_sh + b, b)
        a = jnp.where(ok, a * a_sh, a)
        s *= 2
    return a, b


def _scan_tile_index(step, nt, reverse):
    return jnp.where(step == 0, 0, nt - step) if reverse else step


def _scan(xr, wr, wi, br, bi, lam, reverse, tt):
    n, c = xr.shape
    nt = n // tt
    heads, hs = wr.shape[0], wr.shape[2]
    lc = min(c, LANE)

    def tile(i):
        return _scan_tile_index(i, nt, reverse)

    def body(x_ref, wr_ref, wi_ref, br_ref, bi_ref, lam_ref, h_ref, hin_ref, a_scr, b_scr, carry):
        i = pl.program_id(0)

        @pl.when(i == 0)
        def _():
            carry[...] = jnp.zeros(carry.shape, F32)

        for hd in range(heads):
            sl = slice(hd * hs, (hd + 1) * hs)
            xv = x_ref[:, sl]
            _, ig, _, a, sq = _head_gates(xv, wr_ref[hd], wi_ref[hd], br_ref[:, sl], bi_ref[:, sl], lam_ref[:, sl])
            a_scr[:, sl] = a
            b_scr[:, sl] = sq * (ig * xv)
        hin_ref[...] = jnp.broadcast_to(carry[...], (8, c))
        rows = lax.broadcasted_iota(jnp.int32, (tt, lc), 0)
        for ch in range(c // lc):
            sl = slice(ch * lc, (ch + 1) * lc)
            big_a, big_b = _tile_scan(a_scr[:, sl], b_scr[:, sl], not reverse, tt, rows)
            h_ref[:, sl] = big_a * carry[:, sl] + big_b
        carry[...] = h_ref[pl.ds(0 if reverse else tt - 1, 1), :]

    return pl.pallas_call(
        body, name="lru_scan", grid=(nt,),
        in_specs=[pl.BlockSpec((tt, c), lambda i: (tile(i), 0)), _full_spec(wr.shape), _full_spec(wi.shape),
                  _full_spec((1, c)), _full_spec((1, c)), _full_spec((1, c))],
        out_specs=(pl.BlockSpec((tt, c), lambda i: (tile(i), 0)), pl.BlockSpec((None, 8, c), lambda i: (tile(i), 0, 0))),
        out_shape=(jax.ShapeDtypeStruct((n, c), F32), jax.ShapeDtypeStruct((nt, 8, c), F32)),
        scratch_shapes=[pltpu.VMEM((tt, c), F32), pltpu.VMEM((tt, c), F32), pltpu.VMEM((1, c), F32)],
        compiler_params=_cp(("arbitrary",)),
    )(xr, wr, wi, br, bi, lam)


def _scan_bwd(xr, dh, h, hin, wr, wi, br, bi, lam, reverse, tt, dx_prev=None):
    n, c = xr.shape
    nt = n // tt
    heads, hs = wr.shape[0], wr.shape[2]
    lc = min(c, LANE)
    inc = not reverse
    first, last = (0, tt - 1) if inc else (tt - 1, 0)

    def tile(i):
        return _scan_tile_index(nt - 1 - i, nt, reverse)

    def body(*refs):
        x_ref, dh_ref, h_ref, hin_ref, wr_ref, wi_ref, br_ref, bi_ref, lam_ref = refs[:9]
        refs = refs[9:]
        if dx_prev is not None:
            dxp_ref, refs = refs[0], refs[1:]
        dx_ref, dwr_ref, dwi_ref, sum_ref, a_scr, r_scr, i_scr, sq_scr, g_scr, da_scr, u_scr, ucarry = refs
        i = pl.program_id(0)

        @pl.when(i == 0)
        def _():
            ucarry[...] = jnp.zeros(ucarry.shape, F32)
            dwr_ref[...] = jnp.zeros(dwr_ref.shape, F32)
            dwi_ref[...] = jnp.zeros(dwi_ref.shape, F32)
            sum_ref[...] = jnp.zeros(sum_ref.shape, F32)

        for hd in range(heads):
            sl = slice(hd * hs, (hd + 1) * hs)
            r, ig, _, a, sq = _head_gates(x_ref[:, sl], wr_ref[hd], wi_ref[hd], br_ref[:, sl], bi_ref[:, sl],
                                          lam_ref[:, sl])
            a_scr[:, sl], r_scr[:, sl], i_scr[:, sl], sq_scr[:, sl] = a, r, ig, sq

        rows = lax.broadcasted_iota(jnp.int32, (tt, lc), 0)
        to_prev = 1 if inc else tt - 1
        to_next = tt - 1 if inc else 1
        for ch in range(c // lc):
            sl = slice(ch * lc, (ch + 1) * lc)
            a, dhv = a_scr[:, sl], dh_ref[:, sl]
            big_a, big_b = _tile_scan(a, a * dhv, not inc, tt, rows)
            u_in = ucarry[:, sl]
            u = big_a * u_in + big_b
            u_scr[:, sl] = u
            g = dhv + jnp.where(rows == last, u_in, pltpu.roll(u, to_next, 0))
            g_scr[:, sl] = g
            h_prev = jnp.where(rows == first, hin_ref[pl.ds(0, 1), sl], pltpu.roll(h_ref[:, sl], to_prev, 0))
            da_scr[:, sl] = g * h_prev
        ucarry[...] = u_scr[pl.ds(first, 1), :]

        for hd in range(heads):
            sl = slice(hd * hs, (hd + 1) * hs)
            xv, a, r, ig, sq = x_ref[:, sl], a_scr[:, sl], r_scr[:, sl], i_scr[:, sl], sq_scr[:, sl]
            g, lam_v = g_scr[:, sl], lam_ref[:, sl]
            sp = _softplus_neg(lam_v)
            d_sq = g * ig * xv
            d_em = d_sq * 0.5 / sq
            d_log_a = (da_scr[:, sl] - 2.0 * a * d_em) * a
            dzr = (d_log_a * ((-RG_C) * sp)) * r * (1.0 - r)
            dzi = (g * sq * xv) * ig * (1.0 - ig)
            dzr_b, dzi_b, xb = dzr.astype(BF16), dzi.astype(BF16), xv.astype(BF16)
            dx = g * sq * ig + _dot(dzr_b, wr_ref[hd], 1, 1) + _dot(dzi_b, wi_ref[hd], 1, 1)
            if dx_prev is not None:
                dx = dx + dxp_ref[:, sl]
            dx_ref[:, sl] = dx
            dwr_ref[hd] += _dot(xb, dzr_b, 0, 0)
            dwi_ref[hd] += _dot(xb, dzi_b, 0, 0)
            sum_ref[pl.ds(0, 1), sl] += _sum0(dzr)
            sum_ref[pl.ds(1, 1), sl] += _sum0(dzi)
            sum_ref[pl.ds(2, 1), sl] += _sum0(d_log_a * ((-RG_C) * r)) * (-_sig(-lam_v))

    tile_spec = pl.BlockSpec((tt, c), lambda i: (tile(i), 0))
    ins = [xr, dh, h, hin, wr, wi, br, bi, lam]
    in_specs = [tile_spec, tile_spec, tile_spec, pl.BlockSpec((None, 8, c), lambda i: (tile(i), 0, 0)),
                _full_spec(wr.shape), _full_spec(wi.shape), _full_spec((1, c)), _full_spec((1, c)), _full_spec((1, c))]
    if dx_prev is not None:
        ins.append(dx_prev)
        in_specs.append(tile_spec)
    return pl.pallas_call(
        body, name="lru_scan_bwd", grid=(nt,), in_specs=in_specs,
        out_specs=(tile_spec, _full_spec(wr.shape), _full_spec(wi.shape), _full_spec((8, c))),
        out_shape=(jax.ShapeDtypeStruct((n, c), F32), jax.ShapeDtypeStruct(wr.shape, F32),
                   jax.ShapeDtypeStruct(wi.shape, F32), jax.ShapeDtypeStruct((8, c), F32)),
        scratch_shapes=[pltpu.VMEM((tt, c), F32)] * 7 + [pltpu.VMEM((1, c), F32)],
        compiler_params=_cp(("arbitrary",)),
    )(*ins)


def _conv_rows(nseg, seg):
    return nseg * (seg + CONV_PAD) + CONV_PAD


def _seg_base(s, seg):
    return CONV_PAD + s * (seg + CONV_PAD)


def _zero_gaps(scr, nseg, seg, c):
    for s in range(nseg + 1):
        scr[pl.ds(s * (seg + CONV_PAD), CONV_PAD), :] = jnp.zeros((CONV_PAD, c), F32)


def _build_shifts(scr, rot, n_rows, sl):
    for b in range(1, 8):
        rot[b - 1, pl.ds(0, n_rows - 8), :] = scr[pl.ds(b, n_rows - 8), sl]


def _tap(scr, rot, base, off, seg, sl):
    a, b = divmod(off, 8)
    if b == 0:
        return scr[pl.ds(base + 8 * a, seg), sl]
    return rot[b - 1, pl.ds(base + 8 * a, seg), :]


def _conv_chunks(nseg, seg):
    return [(s * seg + q * SEG, _seg_base(s, seg) + q * SEG) for s in range(nseg) for q in range(seg // SEG)]


def _conv_scratch(tt, c):
    rows = max(_conv_rows(1, tt), _conv_rows(tt // SEG, SEG))
    return pltpu.VMEM((rows, c), F32), pltpu.VMEM((7, rows, min(c, LANE)), F32)


def _ln_stats(v):
    mu = jnp.mean(v, axis=-1, keepdims=True)
    xc = v - mu
    rstd = lax.rsqrt(jnp.mean(xc * xc, axis=-1, keepdims=True) + LN_EPS)
    return xc * rstd, rstd


def _mix_mid(p, h_f, h_b, w31, b31, clg, clb, tt):
    n, c = h_f.shape
    taps = w31.shape[0]
    lc = min(c, LANE)

    half = taps // 2

    def body(gr_ref, cv_ref, cg_ref, hf_ref, hb_ref, w_ref, b_ref, g_ref, bb_ref, y_ref, uc_ref, scr, rot):
        i = pl.program_id(0)
        gel, _ = _gelu_and_grad(gr_ref[...])
        y_ref[:, 0:c] = ((hf_ref[...] + hb_ref[...]) * gel).astype(BF16)

        def conv(nseg, seg):
            _zero_gaps(scr, nseg, seg, c)
            for s in range(nseg):
                rows = pl.ds(s * seg, seg)
                scr[pl.ds(_seg_base(s, seg), seg), :] = cv_ref[rows, :] * _sig(cg_ref[rows, :])
            for ch in range(c // lc):
                sl = slice(ch * lc, (ch + 1) * lc)
                _build_shifts(scr, rot, _conv_rows(nseg, seg), sl)
                for out0, base in _conv_chunks(nseg, seg):
                    acc = jnp.broadcast_to(b_ref[:, sl], (SEG, lc))
                    for k in range(taps):
                        acc = acc + _tap(scr, rot, base, k - half, SEG, sl) * w_ref[pl.ds(k, 1), sl]
                    uc_ref[pl.ds(out0, SEG), sl] = acc

        @pl.when(i == 0)
        def _():
            conv(1, tt)

        @pl.when(i > 0)
        def _():
            conv(tt // SEG, SEG)

        xh, _ = _ln_stats(uc_ref[...])
        v = xh * g_ref[...] + bb_ref[...]
        y_ref[:, c:2 * c] = (v * _sig(v)).astype(BF16)

    return pl.pallas_call(
        body, name="mix_mid", grid=(n // tt,),
        in_specs=[_row_spec(tt, c, 1), _row_spec(tt, c, 2), _row_spec(tt, c, 3), _row_spec(tt, c), _row_spec(tt, c),
                  _full_spec(w31.shape), _full_spec((1, c)), _full_spec((1, c)), _full_spec((1, c))],
        out_specs=(_row_spec(tt, 2 * c), _row_spec(tt, c)),
        out_shape=(jax.ShapeDtypeStruct((n, 2 * c), BF16), jax.ShapeDtypeStruct((n, c), F32)),
        scratch_shapes=list(_conv_scratch(tt, c)),
        compiler_params=_cp(("parallel",)),
    )(p, p, p, h_f, h_b, w31, b31, clg, clb)


def _mix_mid_bwd(dymix, p, h_f, h_b, uc, w31, clg, clb, tt):
    n, c = h_f.shape
    taps = w31.shape[0]
    lc = min(c, LANE)
    sum_rows = 8 * ((taps + 3 + 7) // 8)

    half = taps // 2
    nt = n // tt

    def body(dyr_ref, dyc_ref, gr_ref, cv_ref, cg_ref, hf_ref, hb_ref, uc_ref, w_ref, g_ref, bb_ref,
             dh_ref, dp_ref, sum_ref, uscr, urot, dscr, drot, duc_scr, wacc):
        i = pl.program_id(0)

        @pl.when(i == 0)
        def _():
            sum_ref[...] = jnp.zeros(sum_ref.shape, F32)
            wacc[...] = jnp.zeros(wacc.shape, F32)

        dyr = dyr_ref[...]
        gel, dgel = _gelu_and_grad(gr_ref[...])
        dh_ref[...] = dyr * gel
        dp_ref[:, 0:c] = (dyr * (hf_ref[...] + hb_ref[...]) * dgel).astype(BF16)

        xh, rstd = _ln_stats(uc_ref[...])
        v = xh * g_ref[...] + bb_ref[...]
        sg = _sig(v)
        dv = dyc_ref[...] * (sg * (1.0 + v * (1.0 - sg)))
        dxh = dv * g_ref[...]
        m1 = jnp.mean(dxh, axis=-1, keepdims=True)
        m2 = jnp.mean(dxh * xh, axis=-1, keepdims=True)
        duc = rstd * (dxh - m1 - xh * m2)
        sum_ref[pl.ds(taps, 1), :] += _sum0(duc)
        sum_ref[pl.ds(taps + 1, 1), :] += _sum0(dv * xh)
        sum_ref[pl.ds(taps + 2, 1), :] += _sum0(dv)

        duc_scr[...] = duc

        def conv_bwd(nseg, seg):
            n_rows = _conv_rows(nseg, seg)
            _zero_gaps(dscr, nseg, seg, c)
            _zero_gaps(uscr, nseg, seg, c)
            for s in range(nseg):
                rows = pl.ds(s * seg, seg)
                dscr[pl.ds(_seg_base(s, seg), seg), :] = duc_scr[rows, :]
                uscr[pl.ds(_seg_base(s, seg), seg), :] = cv_ref[rows, :] * _sig(cg_ref[rows, :])
            for ch in range(c // lc):
                sl = slice(ch * lc, (ch + 1) * lc)
                _build_shifts(dscr, drot, n_rows, sl)
                _build_shifts(uscr, urot, n_rows, sl)
                chunks = _conv_chunks(nseg, seg)
                for out0, base in chunks:
                    rows = pl.ds(out0, SEG)
                    du = jnp.zeros((SEG, lc), F32)
                    for k in range(taps):
                        du = du + _tap(dscr, drot, base, k - half, SEG, sl) * w_ref[pl.ds(taps - 1 - k, 1), sl]
                    sgc, cv = _sig(cg_ref[rows, sl]), cv_ref[rows, sl]
                    dp_ref[rows, c + sl.start:c + sl.stop] = (du * sgc).astype(BF16)
                    dp_ref[rows, 2 * c + sl.start:2 * c + sl.stop] = (du * cv * sgc * (1.0 - sgc)).astype(BF16)
                for g0 in range(0, len(chunks), 4):
                    group = chunks[g0:g0 + 4]
                    duc_q = [duc_scr[pl.ds(out0, SEG), sl] for out0, _ in group]
                    for k in range(taps):
                        part = jnp.zeros((8, lc), F32)
                        for dq, (_, base) in zip(duc_q, group):
                            prod = dq * _tap(uscr, urot, base, k - half, SEG, sl)
                            part = part + jnp.sum(prod.reshape(SEG // 8, 8, lc), axis=0)
                        wacc[pl.ds(8 * k, 8), sl] += part

        @pl.when(i == 0)
        def _():
            conv_bwd(1, tt)

        @pl.when(i > 0)
        def _():
            conv_bwd(tt // SEG, SEG)

        @pl.when(i == nt - 1)
        def _():
            for k in range(taps):
                sum_ref[pl.ds(k, 1), :] = _sum0(wacc[pl.ds(8 * k, 8), :])

    scr_a, rot_a = _conv_scratch(tt, c)
    return pl.pallas_call(
        body, name="mix_mid_bwd", grid=(n // tt,),
        in_specs=[_row_spec(tt, c, 0), _row_spec(tt, c, 1), _row_spec(tt, c, 1), _row_spec(tt, c, 2),
                  _row_spec(tt, c, 3), _row_spec(tt, c), _row_spec(tt, c), _row_spec(tt, c),
                  _full_spec(w31.shape), _full_spec((1, c)), _full_spec((1, c))],
        out_specs=(_row_spec(tt, c), _row_spec(tt, 3 * c), _full_spec((sum_rows, c))),
        out_shape=(jax.ShapeDtypeStruct((n, c), F32), jax.ShapeDtypeStruct((n, 3 * c), BF16),
                   jax.ShapeDtypeStruct((sum_rows, c), F32)),
        scratch_shapes=[scr_a, rot_a, scr_a, rot_a, pltpu.VMEM((tt, c), F32), pltpu.VMEM((8 * taps, c), F32)],
        compiler_params=_cp(("arbitrary",), 48),
    )(dymix, dymix, p, p, p, h_f, h_b, uc, w31, clg, clb)


def _ada_fwd(cc, w_ada, b_cols):
    depth, d, wc = w_ada.shape
    tn = _pick(wc, (768, 512, 384, 256, 128))

    def body(c_ref, w_ref, b_ref, o_ref):
        cv = c_ref[...]
        o_ref[...] = _dot((cv * _sig(cv)).astype(BF16), w_ref[...].astype(BF16), 1, 0) + b_ref[...]

    return pl.pallas_call(
        body, name="ada_fwd", grid=(depth, wc // tn),
        in_specs=[_full_spec((MOD_ROWS, d)), pl.BlockSpec((None, d, tn), lambda l, j: (l, 0, j)),
                  pl.BlockSpec((None, 1, tn), lambda l, j: (l, 0, j))],
        out_specs=pl.BlockSpec((None, MOD_ROWS, tn), lambda l, j: (l, 0, j)),
        out_shape=jax.ShapeDtypeStruct((depth, MOD_ROWS, wc), F32),
        compiler_params=_cp(("parallel", "parallel")),
    )(cc, w_ada, b_cols)


def _ada_bwd(cc, dm, w_ada):
    depth, d, wc = w_ada.shape
    tn = _pick(wc, (768, 512, 384, 256, 128))

    def body(c_ref, dm_ref, w_ref, gw_ref, gc_ref):
        cv = c_ref[...]
        sg = _sig(cv)
        dmb = dm_ref[...].astype(BF16)
        gw_ref[...] = _dot((cv * sg).astype(BF16), dmb, 0, 0)

        @pl.when(jnp.logical_and(pl.program_id(0) == 0, pl.program_id(1) == 0))
        def _():
            gc_ref[...] = jnp.zeros(gc_ref.shape, F32)

        gc_ref[...] += _dot(dmb, w_ref[...].astype(BF16), 1, 1) * (sg * (1.0 + cv * (1.0 - sg)))

    return pl.pallas_call(
        body, name="ada_bwd", grid=(depth, wc // tn),
        in_specs=[_full_spec((MOD_ROWS, d)), pl.BlockSpec((None, MOD_ROWS, tn), lambda l, j: (l, 0, j)),
                  pl.BlockSpec((None, d, tn), lambda l, j: (l, 0, j))],
        out_specs=(pl.BlockSpec((None, d, tn), lambda l, j: (l, 0, j)), _full_spec((MOD_ROWS, d))),
        out_shape=(jax.ShapeDtypeStruct((depth, d, wc), F32), jax.ShapeDtypeStruct((MOD_ROWS, d), F32)),
        compiler_params=_cp(("arbitrary", "arbitrary")),
    )(cc, dm, w_ada)


def _adam(grads, w, m, v):
    depth, rows, cols = w.shape
    assert len(grads) == depth
    n_slab = grads[0].shape[0]
    tr = _pick(rows, (128, 64, 32, 16, 8))
    c1, c2 = 1.0 - ADAM_B1 ** ADAM_STEP, 1.0 - ADAM_B2 ** ADAM_STEP

    def body(*refs):
        g_refs = refs[:depth]
        w_ref, m_ref, v_ref, go_ref, do_ref, mo_ref, vo_ref = refs[depth:]
        for q in range(depth):
            @pl.when(pl.program_id(0) == q)
            def _(q=q):
                g = g_refs[q][0].astype(F32)
                for k in range(1, n_slab):
                    g = g + g_refs[q][k].astype(F32)
                go_ref[...] = g

        g = go_ref[...]
        m_new = ADAM_B1 * m_ref[...] + (1.0 - ADAM_B1) * g
        v_new = ADAM_B2 * v_ref[...] + (1.0 - ADAM_B2) * (g * g)
        mo_ref[...] = m_new
        vo_ref[...] = v_new
        do_ref[...] = -ADAM_LR * ((m_new / c1) / (jnp.sqrt(v_new / c2) + ADAM_EPS) + ADAM_WD * w_ref[...])

    blk = pl.BlockSpec((None, tr, cols), lambda l, i: (l, i, 0))
    g_specs = [pl.BlockSpec((n_slab, tr, cols), lambda l, i, q=q: (0, jnp.where(l == q, i, 0), 0))
               for q in range(depth)]
    out = jax.ShapeDtypeStruct((depth, rows, cols), F32)
    return pl.pallas_call(
        body, name="adamw", grid=(depth, rows // tr),
        in_specs=g_specs + [blk, blk, blk],
        out_specs=(blk, blk, blk, blk), out_shape=(out, out, out, out),
        compiler_params=_cp(("parallel", "parallel"), 48),
    )(*grads, w, m, v)


_WEIGHTS = ['c_ctx', 'w_ada', 'b_ada', 'ln_g', 'ln_b', 'ff1_in', 'ff1_out', 'ff2_in', 'ff2_out', 'w_in', 'conv4_w',
            'conv4_b', 'w_rg', 'b_rg', 'w_ig', 'b_ig', 'lam', 'conv31_w', 'conv31_b', 'cln_g', 'cln_b', 'w_out', 'b_out']
_SMALL_SHARDED = {'ln_g': 2, 'ln_b': 2, 'conv4_w': 2, 'w_rg': 3, 'b_rg': 2, 'w_ig': 3, 'b_ig': 2, 'lam': 2,
                  'conv31_w': 2}
_BIG_SHARDED = {'ff1_in': 2, 'ff1_out': 1, 'ff2_in': 2, 'ff2_out': 1, 'w_in': 2, 'w_out': 1}
_SMALL = [n for n in _WEIGHTS if n not in _BIG_SHARDED and n != 'w_ada']


def kernel(x, c, ctx, c_ctx, w_ada, b_ada, ln_g, ln_b, ff1_in, ff1_out, ff2_in, ff2_out, w_in, conv4_w, conv4_b, w_rg, b_rg, w_ig, b_ig, lam, conv31_w, conv31_b, cln_g, cln_b, w_out, b_out, loss_target, m_c_ctx, m_w_ada, m_b_ada, m_ln_g, m_ln_b, m_ff1_in, m_ff1_out, m_ff2_in, m_ff2_out, m_w_in, m_conv4_w, m_conv4_b, m_w_rg, m_b_rg, m_w_ig, m_b_ig, m_lam, m_conv31_w, m_conv31_b, m_cln_g, m_cln_b, m_w_out, m_b_out, v_c_ctx, v_w_ada, v_b_ada, v_ln_g, v_ln_b, v_ff1_in, v_ff1_out, v_ff2_in, v_ff2_out, v_w_in, v_conv4_w, v_conv4_b, v_w_rg, v_b_rg, v_w_ig, v_b_ig, v_lam, v_conv31_w, v_conv31_b, v_cln_g, v_cln_b, v_w_out, v_b_out):
    wts = dict(c_ctx=c_ctx, w_ada=w_ada, b_ada=b_ada, ln_g=ln_g, ln_b=ln_b, ff1_in=ff1_in, ff1_out=ff1_out,
               ff2_in=ff2_in, ff2_out=ff2_out, w_in=w_in, conv4_w=conv4_w, conv4_b=conv4_b, w_rg=w_rg, b_rg=b_rg,
               w_ig=w_ig, b_ig=b_ig, lam=lam, conv31_w=conv31_w, conv31_b=conv31_b, cln_g=cln_g, cln_b=cln_b,
               w_out=w_out, b_out=b_out)
    mom = dict(zip(_WEIGHTS, (m_c_ctx, m_w_ada, m_b_ada, m_ln_g, m_ln_b, m_ff1_in, m_ff1_out, m_ff2_in, m_ff2_out,
                              m_w_in, m_conv4_w, m_conv4_b, m_w_rg, m_b_rg, m_w_ig, m_b_ig, m_lam, m_conv31_w,
                              m_conv31_b, m_cln_g, m_cln_b, m_w_out, m_b_out)))
    var = dict(zip(_WEIGHTS, (v_c_ctx, v_w_ada, v_b_ada, v_ln_g, v_ln_b, v_ff1_in, v_ff1_out, v_ff2_in, v_ff2_out,
                              v_w_in, v_conv4_w, v_conv4_b, v_w_rg, v_b_rg, v_w_ig, v_b_ig, v_lam, v_conv31_w,
                              v_conv31_b, v_cln_g, v_cln_b, v_w_out, v_b_out)))

    depth, d, wc = w_ada.shape
    t_lat, t_ctx = x.shape[1], ctx.shape[1]
    tt, n_tok = t_ctx, t_ctx + x.shape[1]
    assert t_lat % tt == 0 and tt % SEG == 0 and tt & (tt - 1) == 0 and d % 2 == 0
    ch = d // 2
    alpha = (2.0 * depth) ** 0.25
    me = 4 * lax.axis_index("x") + 2 * lax.axis_index("y") + lax.axis_index("c")

    names = list(_SMALL_SHARDED)
    got = _gather_list([wts[k] for k in names] + [c], F32)
    full = {k: _unshard(g, _SMALL_SHARDED[k]) for k, g in zip(names, got[:-1])}
    c_all = got[-1].reshape(N_DEV, d)
    cc = jnp.concatenate([c_all, jnp.pad(c_ctx[None, :], ((0, MOD_ROWS - N_DEV - 1), (0, 0)))], axis=0)

    b_cols = _my_block(b_ada, me, 1)[:, None, :]
    (m_cols,) = _gather_list([_ada_fwd(cc, w_ada, b_cols)], F32)
    m_full = jnp.moveaxis(m_cols, 0, 2).reshape(depth, MOD_ROWS, N_DEV * wc)
    m_lat = lax.dynamic_index_in_dim(m_full, me, axis=1, keepdims=False).reshape(depth, 9, d)
    m_ctx = m_full[:, N_DEV].reshape(depth, 9, d)
    modtab = jnp.pad(jnp.stack([m_ctx, m_lat], axis=1), ((0, 0), (0, 0), (0, MOD_ROWS - 9), (0, 0)))

    big = {k: [None] * depth for k in _BIG_SHARDED}
    half = {}

    def shard(k, l):
        return wts[k][l].astype(BF16)

    def landed(k, l, g):
        big[k][l] = _unshard(g, _BIG_SHARDED[k] - 1)

    def jobs(*specs):
        return [(stage, k, l) for stage, k, l in specs if l < depth]

    def comm_of(todo):
        return [(stage, shard(k, l) if stage == "spread" else half[k, l]) for stage, k, l in todo]

    def settle(todo, got):
        for (stage, k, l), g in zip(todo, got):
            if stage == "spread":
                half[k, l] = g
            else:
                landed(k, l, g)

    got = _all_gather_arrays([shard('ff1_in', 0), shard('ff1_out', 0)])
    landed('ff1_in', 0, got[0])
    landed('ff1_out', 0, got[1])
    wr_b, wi_b = full['w_rg'].astype(BF16), full['w_ig'].astype(BF16)

    def row(v):
        return v.reshape(1, -1)

    s = jnp.concatenate([ctx[0], x[0]], axis=0)
    saved = []
    h_next = _modulate(s, modtab[0], 0, 1, tt)
    for l in range(depth):
        mt, sv = modtab[l], {}
        sv['s0'] = s
        sv['h1'] = h_next
        todo = jobs(("spread", 'w_in', l), ("spread", 'w_out', l), ("spread", 'ff2_in', l))
        (sv['z1'], sv['a1']), got = _mm_swiglu(sv['h1'], big['ff1_in'][l], comm=comm_of(todo))
        settle(todo, got)
        todo = jobs(("relay", 'w_in', l), ("relay", 'w_out', l), ("relay", 'ff2_in', l), ("spread", 'ff2_out', l))
        sv['f1'], got = _mm_nn("ffn_out", sv['a1'], big['ff1_out'][l], comm=comm_of(todo))
        settle(todo, got)
        sv['r1'], sv['s1'], sv['h2'] = _res_ln(s, sv['f1'], mt, 2, 0.5, row(full['ln_g'][l, 0]), row(full['ln_b'][l, 0]),
                                               alpha, tt, nxt=(mt, 3, 4))
        todo = jobs(("relay", 'ff2_out', l))
        sv['p'], got = _mm_nn("mix_in", sv['h2'], big['w_in'][l], comm=comm_of(todo))
        settle(todo, got)
        sv['xr'] = _conv4(sv['p'], full['conv4_w'][l], row(conv4_b[l]), tt, ch)
        for dr_, rev in ((0, False), (1, True)):
            sv['h', dr_], sv['hin', dr_] = _scan(
                sv['xr'], wr_b[l, dr_], wi_b[l, dr_], row(full['b_rg'][l, dr_]), row(full['b_ig'][l, dr_]),
                row(full['lam'][l, dr_]), rev, tt)
        sv['ymix'], sv['uc'] = _mix_mid(sv['p'], sv['h', 0], sv['h', 1], full['conv31_w'][l], row(conv31_b[l]),
                                        row(cln_g[l]), row(cln_b[l]), tt)
        sv['y'], _ = _mm_nn("mix_out", sv['ymix'], big['w_out'][l], bias=row(b_out[l]))
        sv['r2'], sv['s2'], sv['h3'] = _res_ln(sv['s1'], sv['y'], mt, 5, 1.0, row(full['ln_g'][l, 1]),
                                               row(full['ln_b'][l, 1]), alpha, tt, nxt=(mt, 6, 7))
        todo = jobs(("spread", 'ff1_in', l + 1), ("spread", 'ff1_out', l + 1))
        (sv['z3'], sv['a3']), got = _mm_swiglu(sv['h3'], big['ff2_in'][l], comm=comm_of(todo))
        settle(todo, got)
        todo = jobs(("relay", 'ff1_in', l + 1), ("relay", 'ff1_out', l + 1))
        sv['f3'], got = _mm_nn("ffn_out", sv['a3'], big['ff2_out'][l], comm=comm_of(todo))
        settle(todo, got)
        outs = _res_ln(sv['s2'], sv['f3'], mt, 8, 0.5, row(full['ln_g'][l, 2]), row(full['ln_b'][l, 2]), alpha, tt,
                       nxt=(modtab[l + 1], 0, 1) if l + 1 < depth else None)
        sv['r3'], s = outs[0], outs[1]
        h_next = outs[2] if l + 1 < depth else None
        saved.append(sv)

    cot, loss_blk = _loss_head(s, loss_target[0], tt)
    loss = lax.psum(loss_blk[0, 0], ("x", "y", "c"))

    gx = {k: [None] * depth for k in list(_BIG_SHARDED) + ['w_rg', 'w_ig']}
    gsm = {k: [None] * depth for k in ('ln_g', 'ln_b', 'conv4_w', 'conv4_b', 'b_rg', 'b_ig', 'lam',
                                       'conv31_w', 'conv31_b', 'cln_g', 'cln_b', 'b_out', 'dm')}
    g_gate = {'w_rg': [None] * depth, 'w_ig': [None] * depth}
    taps4, taps31 = conv4_w.shape[1], conv31_w.shape[1]
    heads, hs = w_rg.shape[2], w_rg.shape[4]

    mods, pers = {}, {}

    def ln_bwd(cot, r, f, mt, i_gate, coef, gam):
        if isinstance(cot, tuple):
            return _ln_bwd(cot, r, f, mt, i_gate, coef, gam, tt, alpha=alpha)
        return _ln_bwd(cot, r, f, mt, i_gate, coef, gam, tt) + (None,)

    def scatter(pend):
        return [("scatter", g) for _, _, g in pend]

    def exchanged(pend, got):
        for (k, l, _), g in zip(pend, got):
            gx[k][l] = g

    def ffn_bwd(cot, l, mt, r, f, z, a, h, s_in, w_in_name, w_out_name, ln_idx, i_gate, i_scale, pend_short, pend_long,
                final=False):
        dr, dfb, tot, per, up = ln_bwd(cot, r, f, mt, i_gate, 0.5, row(full['ln_g'][l, ln_idx]))
        dz, got = _mm_nt_swiglu_bwd(dfb, big[w_out_name][l], z, comm=scatter(pend_short))
        exchanged(pend_short, got)
        g_out, _ = _mm_tn("ffn_out_dw", a, dfb)
        pend_out = [(w_out_name, l, g_out.reshape(N_DEV, -1, d))]
        if final:
            g_in, got = _mm_tn("ffn_in_dw", h, dz, b_halves=True, shard_cols=True, comm=scatter(pend_long))
            exchanged(pend_long, got)
            pend_out.append((w_in_name, l, g_in))
            dh, got = _mm_nt("ffn_in_dx", dz, big[w_in_name][l], a_halves=True, comm=scatter(pend_out))
            exchanged(pend_out, got)
            pend_in = []
        else:
            dh, got = _mm_nt("ffn_in_dx", dz, big[w_in_name][l], a_halves=True, comm=scatter(pend_long))
            exchanged(pend_long, got)
            g_in, got = _mm_tn("ffn_in_dw", h, dz, b_halves=True, shard_cols=True, comm=scatter(pend_out))
            exchanged(pend_out, got)
            pend_in = [(w_in_name, l, g_in)]
        return (dr, dh, s_in, mt, i_scale), tot, per, up, pend_in

    pend = []
    for l in reversed(range(depth)):
        mt, sv = modtab[l], saved[l]
        cot, tot3, per3, up, pend_ff2_in = ffn_bwd(cot, l, mt, sv['r3'], sv['f3'], sv['z3'], sv['a3'], sv['h3'],
                                                   sv['s2'], 'ff2_in', 'ff2_out', 2, 8, 7, [], pend)
        if up is not None:
            mods[l + 1, 1] = up
        dr, dyb, tot2, per2, mods[l, 3] = ln_bwd(cot, sv['r2'], sv['y'], mt, 5, 1.0, row(full['ln_g'][l, 1]))
        dymix, _ = _mm_nt("mix_out_dx", dyb, big['w_out'][l])
        g_w_out, _ = _mm_tn("mix_out_dw", sv['ymix'], dyb)
        pend = [('w_out', l, g_w_out.reshape(N_DEV, -1, d))]
        dh_rec, dp_rest, csum = _mix_mid_bwd(dymix, sv['p'], sv['h', 0], sv['h', 1], sv['uc'], full['conv31_w'][l],
                                             row(cln_g[l]), row(cln_b[l]), tt)
        dxr, gw_r, gw_i, gsum = None, [], [], []
        for dr_, rev in ((0, False), (1, True)):
            dxr, gwr, gwi, gs = _scan_bwd(
                sv['xr'], dh_rec, sv['h', dr_], sv['hin', dr_], wr_b[l, dr_], wi_b[l, dr_],
                row(full['b_rg'][l, dr_]), row(full['b_ig'][l, dr_]), row(full['lam'][l, dr_]), rev, tt, dxr)
            gw_r.append(gwr)
            gw_i.append(gwi)
            gsum.append(gs)
        dp_xr, c4sum = _conv4_bwd(dxr, sv['p'], full['conv4_w'][l], tt, ch)
        dp = jnp.concatenate([dp_xr, dp_rest], axis=1)
        dh, got = _mm_nt("mix_in_dx", dp, big['w_in'][l], comm=scatter(pend))
        exchanged(pend, got)
        g_w_in, _ = _mm_tn("mix_in_dw", sv['h2'], dp, shard_cols=True)
        cot = (dr, dh, sv['s1'], mt, 4)
        for k, gws in (('w_rg', gw_r), ('w_ig', gw_i)):
            slabs = jnp.stack(gws).reshape(2, heads, N_DEV, hs // N_DEV, hs)
            g_gate[k][l] = jnp.moveaxis(slabs, 2, 0).reshape(N_DEV, -1, hs).astype(BF16)
        short = [('w_in', l, g_w_in)]
        if l == 0:
            short += [(k, q, g_gate[k][q]) for k in g_gate for q in range(depth)]
        cot, tot1, per1, mods[l, 2], pend = ffn_bwd(cot, l, mt, sv['r1'], sv['f1'], sv['z1'], sv['a1'], sv['h1'],
                                                    sv['s0'], 'ff1_in', 'ff1_out', 0, 2, 1, short,
                                                    pend_ff2_in, final=(l == 0))
        pers[l] = (per1, per2, per3)

        gsm['ln_g'][l] = jnp.stack([tot1[0], tot2[0], tot3[0]])
        gsm['ln_b'][l] = jnp.stack([tot1[1], tot2[1], tot3[1]])
        gsm['b_out'][l] = tot2[2]
        gsm['conv4_w'][l], gsm['conv4_b'][l] = c4sum[:taps4], c4sum[taps4]
        gsm['b_rg'][l] = jnp.stack([g[0] for g in gsum])
        gsm['b_ig'][l] = jnp.stack([g[1] for g in gsum])
        gsm['lam'][l] = jnp.stack([g[2] for g in gsum])
        gsm['conv31_w'][l], gsm['conv31_b'][l] = csum[:taps31], csum[taps31]
        gsm['cln_g'][l], gsm['cln_b'][l] = csum[taps31 + 1], csum[taps31 + 2]
    cot, mods[0, 1] = _mod_bwd(*cot[:5], alpha, tt)
    grad_x = cot[t_ctx:][None]
    assert not pend
    for l in range(depth):
        rows = lambda st: [v[8 * st + q] for k in (1, 2, 3) for v, q in ((mods[l, k], 1), (mods[l, k], 0), (pers[l][k - 1], 0))]
        gsm['dm'][l] = jnp.stack([jnp.stack(rows(st)) for st in (0, 1)])

    keys = list(gsm)
    partial = [jnp.stack(gsm[k]) for k in keys]
    shapes = [a.shape for a in partial]
    gathered = _all_gather(_pack(partial, F32))
    totals = dict(zip(keys, _unpack(_sum_slabs(gathered), shapes)))
    dm_all = _unpack(gathered, shapes, lead=(N_DEV,))[keys.index('dm')]
    dm_tot = totals['dm'].reshape(depth, 2, 9 * d)
    dm_sum = _sum_slabs(jnp.stack([_pack([dm_tot[:, 0]], F32), _pack([dm_tot[:, 1]], F32)]))
    grad = {'b_ada': _unpack(dm_sum, [(depth, 9 * d)])[0]}
    for k in ('ln_g', 'ln_b', 'conv4_w', 'b_rg', 'b_ig', 'lam', 'conv31_w'):
        grad[k] = _my_block(totals[k], me, 2)
    for k in ('conv4_b', 'conv31_b', 'cln_g', 'cln_b', 'b_out'):
        grad[k] = totals[k]

    dm_rows = jnp.concatenate([jnp.moveaxis(dm_all[:, :, 1].reshape(N_DEV, depth, 9 * d), 0, 1),
                               jnp.pad(dm_tot[:, 0][:, None, :], ((0, 0), (0, MOD_ROWS - N_DEV - 1), (0, 0)))], axis=1)
    dm_cols = lax.dynamic_slice_in_dim(dm_rows, me * wc, wc, axis=2)
    grad_w_ada, gcc = _ada_bwd(cc, dm_cols, w_ada)
    gcc_all = _all_gather(_pack([gcc], F32))
    grad['c_ctx'] = _unpack(_sum_slabs(gcc_all), [(MOD_ROWS, d)])[0][N_DEV]

    res = {}
    small = [k for k in _SMALL if k not in g_gate]
    gs, ws, ms, vs = (_pack([src[k] for k in small], F32).reshape(1, -1, 8 * LANE) for src in (grad, wts, mom, var))
    outs = _adam([gs], ws, ms, vs)
    small_shapes = [wts[k].shape for k in small]
    for k, vals in zip(small, zip(*(_unpack(o[0], small_shapes) for o in outs))):
        res[k] = vals
    res['w_ada'] = _adam([grad_w_ada[l][None] for l in range(depth)], w_ada, m_w_ada, v_w_ada)
    for j, k in enumerate(g_gate):
        flat = (depth, -1, hs)
        outs = _adam(gx[k], wts[k].reshape(flat), mom[k].reshape(flat), var[k].reshape(flat))
        res[k] = tuple(o.reshape(wts[k].shape) for o in outs)
    for k in _BIG_SHARDED:
        res[k] = _adam(gx[k], wts[k], mom[k], var[k])

    out = [loss, grad_x]
    for j in range(4):
        out += [res[k][j] for k in _WEIGHTS]
    return tuple(out)
```
